```python
import math
import jax
import jax.numpy as jnp
from jax import lax
import numpy as np

D_MODEL = 2048
BATCH = 16
SEQ = 256
DEPTH = 2
DEC_BATCH = 4
DEC_SEQ = 4096
PAST_LEN = 256

GRID_W = 64
ROPE_DIM = 64
ROPE_BASE = 10000.0
RMS_EPS = 1e-6
ATTN_BLOCK = 128
ML_HEADS = 4
ML_DH = 256
ML_W = ML_HEADS * ML_DH
ML_CHUNK = 64
MLA_HEADS = 8
MLA_NOPE = 128
MLA_ROPE = ROPE_DIM
MLA_V = 128
MLA_KV_RANK = 512
MLA_W = MLA_HEADS * MLA_V
S5_GROUP = 16
S5_GROUPS = 64
S5_W = S5_GROUPS * S5_GROUP
S5_P = 64
S5_CHUNK = 128
DF_HEADS = 8
DF_DQK = ROPE_DIM
DF_DV = 2 * DF_DQK
DF_W = DF_HEADS * DF_DV
N_BRANCH = 4
BRANCH_W = 1024
FFN_HIDDEN = (8 * D_MODEL + 3 * 256 - 1) // (3 * 256) * 256
IN_SIZES = (ML_W, ML_W, ML_W, ML_W, 4 * ML_HEADS,
            MLA_HEADS * (MLA_NOPE + MLA_ROPE), MLA_KV_RANK + MLA_ROPE,
            S5_W,
            DF_HEADS * 2 * DF_DQK, DF_HEADS * 2 * DF_DQK, DF_HEADS * DF_DV,
            N_BRANCH * D_MODEL)
IN_SPLITS = tuple(int(s) for s in np.cumsum(IN_SIZES)[:-1])
IN_COLS = int(sum(IN_SIZES))

kernel_name = 'hybrid_mlstm_mla_s5_diffattn_prefix_dit_step'


def _rmsnorm(x, g):
    xf = x.astype(jnp.float32)
    y = xf * lax.rsqrt(jnp.mean(xf * xf, axis=-1, keepdims=True) + RMS_EPS)
    return (y * g.astype(jnp.float32)).astype(x.dtype)


def _map_query_blocks(fn, q):
    bsz, n = q.shape[0], q.shape[1]
    nb = n // ATTN_BLOCK
    qb = jnp.moveaxis(q.reshape((bsz, nb, ATTN_BLOCK) + q.shape[2:]), 1, 0)
    out = jnp.moveaxis(lax.map(fn, qb), 0, 1)
    return out.reshape((bsz, n) + out.shape[3:])


def _softmax_attend(q, k, v, scale):
    def blk(qb):
        s = jnp.einsum('bqhd,bkhd->bhqk', qb, k).astype(jnp.float32) * scale
        p = jax.nn.softmax(s, axis=-1).astype(v.dtype)
        return jnp.einsum('bhqk,bkhd->bqhd', p, v)
    return _map_query_blocks(blk, q)


def _diff_attend(q, k, v, lam, scale):
    def blk(qb):
        s = jnp.einsum('bqhcd,bkhcd->bchqk', qb, k).astype(jnp.float32) * scale
        p = jax.nn.softmax(s, axis=-1)
        pd = (p[:, 0] - lam * p[:, 1]).astype(v.dtype)
        return jnp.einsum('bhqk,bkhd->bqhd', pd, v)
    return _map_query_blocks(blk, q)


def _axial_rope_tables(n_tok):
    grid_rows = n_tok // GRID_W
    rows, cols = jnp.meshgrid(jnp.arange(grid_rows, dtype=jnp.float32),
                              jnp.arange(GRID_W, dtype=jnp.float32), indexing='ij')
    quarter = ROPE_DIM // 4
    inv = ROPE_BASE ** (-jnp.arange(quarter, dtype=jnp.float32) / quarter)
    ang_r = rows.reshape(-1, 1) * inv
    ang_c = cols.reshape(-1, 1) * inv
    return (jnp.cos(ang_r), jnp.sin(ang_r), jnp.cos(ang_c), jnp.sin(ang_c))


def _rotate(z, cos, sin):
    z1, z2 = jnp.split(z, 2, axis=-1)
    return jnp.concatenate([z1 * cos - z2 * sin, z2 * cos + z1 * sin], axis=-1)


def _apply_axial_rope(x, tables):
    cos_r, sin_r, cos_c, sin_c = (t[:, None, :] for t in tables)
    x_row, x_col = jnp.split(x.astype(jnp.float32), 2, axis=-1)
    out = jnp.concatenate([_rotate(x_row, cos_r, sin_r), _rotate(x_col, cos_c, sin_c)], axis=-1)
    return out.astype(x.dtype)


def _mlstm_dir(q, k, v, log_i, log_f, c0, n0, m0):
    bsz, n = q.shape[0], q.shape[1]
    nc = n // ML_CHUNK
    tril = jnp.tril(jnp.ones((ML_CHUNK, ML_CHUNK), dtype=bool))

    def chunks(a):
        return jnp.moveaxis(a.reshape((bsz, nc, ML_CHUNK) + a.shape[2:]), 1, 0)

    def body(carry, xs):
        c_st, n_st, m_st = carry
        qc, kc, vc, li, lf = xs
        b = jnp.cumsum(lf, axis=1)
        log_d = b[:, :, None, :] - b[:, None, :, :] + li[:, None, :, :]
        log_d = jnp.where(tril[None, :, :, None], log_d, -jnp.inf)
        log_inter = b + m_st[:, None, :]
        m_t = jnp.maximum(log_inter, jnp.max(log_d, axis=2))
        w_d = jnp.exp(log_d - m_t[:, :, None, :])
        w_inter = jnp.exp(log_inter - m_t)
        s = jnp.einsum('bthd,bshd->btsh', qc, kc) * w_d
        num = (jnp.einsum('btsh,bshd->bthd', s, vc)
               + w_inter[..., None] * jnp.einsum('bhvk,bthk->bthv', c_st, qc))
        den = jnp.sum(s, axis=2) + w_inter * jnp.einsum('bhk,bthk->bth', n_st, qc)
        h = num / jnp.maximum(jnp.abs(den), jnp.exp(-m_t))[..., None]
        b_last = b[:, -1, :]
        log_w = b_last[:, None, :] - b + li
        m_new = jnp.maximum(b_last + m_st, jnp.max(log_w, axis=1))
        w_s = jnp.exp(log_w - m_new[:, None, :])
        w_c = jnp.exp(b_last + m_st - m_new)
        c_new = w_c[..., None, None] * c_st + jnp.einsum('bshv,bshk->bhvk', vc * w_s[..., None], kc)
        n_new = w_c[..., None] * n_st + jnp.einsum('bsh,bshk->bhk', w_s, kc)
        return (c_new, n_new, m_new), h

    (c_f, n_f, m_f), h = lax.scan(body, (c0, n0, m0),
                                  (chunks(q), chunks(k), chunks(v), chunks(log_i), chunks(log_f)))
    h = jnp.moveaxis(h, 0, 1).reshape(q.shape)
    return h, c_f, n_f, m_f


def _mlstm_bidir(q, k, v, log_i, log_f, c0, n0, m0):
    h_f, c_f, n_f, m_f = _mlstm_dir(q, k, v, log_i[:, :, 0], log_f[:, :, 0], c0[:, 0], n0[:, 0], m0[:, 0])
    h_b, c_b, n_b, m_b = _mlstm_dir(jnp.flip(q, 1), jnp.flip(k, 1), jnp.flip(v, 1),
                                    jnp.flip(log_i[:, :, 1], 1), jnp.flip(log_f[:, :, 1], 1),
                                    c0[:, 1], n0[:, 1], m0[:, 1])
    return (h_f + jnp.flip(h_b, 1), jnp.stack([c_f, c_b], axis=1),
            jnp.stack([n_f, n_b], axis=1), jnp.stack([m_f, m_b], axis=1))


def _complex_affine_combine(e1, e2):
    a1r, a1i, b1r, b1i = e1
    a2r, a2i, b2r, b2i = e2
    return (a2r * a1r - a2i * a1i, a2r * a1i + a2i * a1r,
            a2r * b1r - a2i * b1i + b2r, a2r * b1i + a2i * b1r + b2i)


def _s5_discretise(a_re, a_im, log_dt, b_re, b_im):
    lr = jnp.minimum(a_re.astype(jnp.float32), -1e-4)
    li = a_im.astype(jnp.float32)
    dt = jnp.exp(log_dt.astype(jnp.float32))[:, None]
    mag = jnp.exp(dt * lr)
    ab_re, ab_im = mag * jnp.cos(dt * li), mag * jnp.sin(dt * li)
    den = lr * lr + li * li
    nr, ni = ab_re - 1.0, ab_im
    qr = (nr * lr + ni * li) / den
    qi = (ni * lr - nr * li) / den
    b_re, b_im = b_re.astype(jnp.float32), b_im.astype(jnp.float32)
    bb_re = qr[..., None] * b_re - qi[..., None] * b_im
    bb_im = qr[..., None] * b_im + qi[..., None] * b_re
    return ab_re, ab_im, bb_re, bb_im


def _s5_dir(u, ab_re, ab_im, bb_re, bb_im, c_re, c_im, h_re, h_im):
    bsz, n = u.shape[0], u.shape[1]
    nc = n // S5_CHUNK
    uc = jnp.moveaxis(u.reshape((bsz, nc, S5_CHUNK) + u.shape[2:]), 1, 0)

    def body(carry, u_blk):
        hr, hi = carry
        bu_re = jnp.einsum('gpi,blgi->blgp', bb_re, u_blk)
        bu_im = jnp.einsum('gpi,blgi->blgp', bb_im, u_blk)
        bu_re = bu_re.at[:, 0].add(ab_re * hr - ab_im * hi)
        bu_im = bu_im.at[:, 0].add(ab_re * hi + ab_im * hr)
        a_re = jnp.broadcast_to(ab_re, bu_re.shape)
        a_im = jnp.broadcast_to(ab_im, bu_im.shape)
        _, _, sr, si = lax.associative_scan(_complex_affine_combine, (a_re, a_im, bu_re, bu_im), axis=1)
        y = jnp.einsum('gip,blgp->blgi', c_re, sr) - jnp.einsum('gip,blgp->blgi', c_im, si)
        return (sr[:, -1], si[:, -1]), y

    (hr_f, hi_f), y = lax.scan(body, (h_re, h_im), uc)
    return jnp.moveaxis(y, 0, 1).reshape(u.shape), hr_f, hi_f


def _s5_bidir(u, a_re, a_im, log_dt, b_re, b_im, c_re, c_im, h0_re, h0_im):
    ab_re, ab_im, bb_re, bb_im = _s5_discretise(a_re[0], a_im[0], log_dt[0], b_re[0], b_im[0])
    y_f, hr_f, hi_f = _s5_dir(u, ab_re, ab_im, bb_re, bb_im, c_re[0].astype(jnp.float32),
                              c_im[0].astype(jnp.float32), h0_re[:, 0], h0_im[:, 0])
    ab_re, ab_im, bb_re, bb_im = _s5_discretise(a_re[1], a_im[1], log_dt[1], b_re[1], b_im[1])
    y_b, hr_b, hi_b = _s5_dir(jnp.flip(u, 1), ab_re, ab_im, bb_re, bb_im, c_re[1].astype(jnp.float32),
                              c_im[1].astype(jnp.float32), h0_re[:, 1], h0_im[:, 1])
    return (y_f + jnp.flip(y_b, 1), jnp.stack([hr_f, hr_b], axis=1), jnp.stack([hi_f, hi_b], axis=1))


def _mixer(h, lp, rope, ctx):
    f32 = jnp.float32
    bsz, n, _ = h.shape
    (ml_q, ml_k, ml_v, ml_o, ml_if, mla_q, mla_kva, s5_u,
     df_q, df_k, df_v, gate_pre) = jnp.split(h @ lp['w_in'], IN_SPLITS, axis=-1)
    latent = ctx is not None
    if latent:
        (ckv_c, krope_c, dk_c, dv_c, ml_c0, ml_n0, ml_m0, s5_h0r, s5_h0i) = ctx
        ml_c0, ml_n0, ml_m0 = ml_c0.astype(f32), ml_n0.astype(f32), ml_m0.astype(f32)
        s5_h0r, s5_h0i = s5_h0r.astype(f32), s5_h0i.astype(f32)
    else:
        ml_c0 = jnp.zeros((bsz, 2, ML_HEADS, ML_DH, ML_DH), f32)
        ml_n0 = jnp.zeros((bsz, 2, ML_HEADS, ML_DH), f32)
        ml_m0 = jnp.zeros((bsz, 2, ML_HEADS), f32)
        s5_h0r = jnp.zeros((bsz, 2, S5_GROUPS, S5_P), f32)
        s5_h0i = jnp.zeros((bsz, 2, S5_GROUPS, S5_P), f32)

    q = ml_q.reshape(bsz, n, ML_HEADS, ML_DH).astype(f32)
    k = ml_k.reshape(bsz, n, ML_HEADS, ML_DH).astype(f32) * (ML_DH ** -0.5)
    v = ml_v.reshape(bsz, n, ML_HEADS, ML_DH).astype(f32)
    gpre = ml_if.reshape(bsz, n, 2, 2, ML_HEADS).astype(f32) + lp['ml_if_bias'].astype(f32)
    log_i = gpre[:, :, :, 0]
    log_f = jax.nn.log_sigmoid(gpre[:, :, :, 1])
    h_ml, ml_c, ml_n, ml_m = _mlstm_bidir(q, k, v, log_i, log_f, ml_c0, ml_n0, ml_m0)
    h_ml = _rmsnorm(h_ml, lp['ml_norm'].reshape(ML_HEADS, ML_DH)).astype(h.dtype)
    y_ml = (h_ml * jax.nn.sigmoid(ml_o.reshape(bsz, n, ML_HEADS, ML_DH))).reshape(bsz, n, ML_W)

    qm = mla_q.reshape(bsz, n, MLA_HEADS, MLA_NOPE + MLA_ROPE)
    q_nope, q_rope = qm[..., :MLA_NOPE], qm[..., MLA_NOPE:]
    ckv = _rmsnorm(mla_kva[..., :MLA_KV_RANK], lp['mla_kv_norm'])
    krope = mla_kva[..., MLA_KV_RANK:]
    if latent:
        q_rope = _apply_axial_rope(q_rope, rope)
        krope_lat = _apply_axial_rope(krope[:, :, None, :], rope)[:, :, 0, :]
        ckv_all = jnp.concatenate([ckv, ckv_c.astype(ckv.dtype)], axis=1)
        krope_all = jnp.concatenate([krope_lat, krope_c.astype(krope.dtype)], axis=1)
    else:
        ckv_all, krope_all = ckv, krope
    kv = (ckv_all @ lp['mla_w_kvb']).reshape(bsz, -1, MLA_HEADS, MLA_NOPE + MLA_V)
    k_nope, v_mla = kv[..., :MLA_NOPE], kv[..., MLA_NOPE:]
    k_mla = jnp.concatenate(
        [k_nope, jnp.broadcast_to(krope_all[:, :, None, :], k_nope.shape[:3] + (MLA_ROPE,))], axis=-1)
    q_mla = jnp.concatenate([q_nope, q_rope], axis=-1)
    y_mla = _softmax_attend(q_mla, k_mla, v_mla, (MLA_NOPE + MLA_ROPE) ** -0.5).reshape(bsz, n, MLA_W)

    u = s5_u.reshape(bsz, n, S5_GROUPS, S5_GROUP).astype(f32)
    y_ss, s5_hr, s5_hi = _s5_bidir(u, lp['s5_a_re'], lp['s5_a_im'], lp['s5_log_dt'], lp['s5_b_re'],
                                   lp['s5_b_im'], lp['s5_c_re'], lp['s5_c_im'], s5_h0r, s5_h0i)
    y_ss = y_ss.reshape(bsz, n, S5_W).astype(h.dtype) + lp['s5_d'] * s5_u
    g_ss = jax.nn.gelu(y_ss)
    y_s5 = g_ss * jax.nn.sigmoid(g_ss @ lp['s5_w_glu'])

    dq = df_q.reshape(bsz, n, 2 * DF_HEADS, DF_DQK)
    dk = df_k.reshape(bsz, n, 2 * DF_HEADS, DF_DQK)
    dv = df_v.reshape(bsz, n, DF_HEADS, DF_DV)
    dk_ctx_layout = dk.reshape(bsz, n, DF_HEADS, 2 * DF_DQK)
    if latent:
        dq = _apply_axial_rope(dq, rope)
        dk_lat = _apply_axial_rope(dk, rope).reshape(bsz, n, DF_HEADS, 2, DF_DQK)
        dk_all = jnp.concatenate(
            [dk_lat, dk_c.astype(dk.dtype).reshape(bsz, -1, DF_HEADS, 2, DF_DQK)], axis=1)
        dv_all = jnp.concatenate([dv, dv_c.astype(dv.dtype)], axis=1)
    else:
        dk_all = dk.reshape(bsz, n, DF_HEADS, 2, DF_DQK)
        dv_all = dv
    lam_p = lp['df_lambda'].astype(f32)
    lam = (jnp.exp(jnp.sum(lam_p[0] * lam_p[1])) - jnp.exp(jnp.sum(lam_p[2] * lam_p[3]))
           + lp['lam_init'])
    o = _diff_attend(dq.reshape(bsz, n, DF_HEADS, 2, DF_DQK), dk_all, dv_all, lam, DF_DQK ** -0.5)
    y_df = (_rmsnorm(o, lp['df_norm']) * (1.0 - lp['lam_init'])).reshape(bsz, n, DF_W)

    gates = jax.nn.sigmoid(gate_pre.reshape(bsz, n, N_BRANCH, D_MODEL))
    wb = lp['w_branch']
    merged = (gates[:, :, 0] * (y_ml @ wb[0]) + gates[:, :, 1] * (y_mla @ wb[1])
              + gates[:, :, 2] * (y_s5 @ wb[2]) + gates[:, :, 3] * (y_df @ wb[3]))
    out = merged @ lp['w_o']
    new_ctx = None if latent else (ckv, krope, dk_ctx_layout, dv, ml_c, ml_n, ml_m, s5_hr, s5_hi)
    return out, new_ctx


def _layer(x, cond, lp, rope, ctx):
    mod = jax.nn.silu(cond) @ lp['w_ada'] + lp['b_ada']
    sh1, sc1, g1, sh2, sc2, g2 = jnp.split(mod[:, None, :], 6, axis=-1)
    h = _rmsnorm(x, lp['norm_mix']) * (1.0 + sc1) + sh1
    mix, new_ctx = _mixer(h, lp, rope, ctx)
    x = x + g1 * mix
    h = _rmsnorm(x, lp['norm_ffn']) * (1.0 + sc2) + sh2
    a, b = jnp.split(h @ lp['w_ffn_in'], 2, axis=-1)
    x = x + g2 * ((jax.nn.silu(a) * b) @ lp['w_ffn_out'])
    return x, new_ctx


def setup_inputs(seed: int = 0) -> dict:
    key = jax.random.key(seed)
    ks = iter(jax.random.split(key, 48))
    f32 = jnp.float32

    def nrm(shape, scale):
        return scale * jax.random.normal(next(ks), shape, f32)

    def gain(shape):
        return 1.0 + nrm(shape, 0.01)

    inp = {}
    inp['x_prompt'] = nrm((BATCH, SEQ, D_MODEL), 1.0)
    inp['x_sample'] = nrm((DEC_BATCH, DEC_SEQ, D_MODEL), 1.0)
    inp['cache_mla_ckv'] = nrm((DEC_BATCH, DEPTH, PAST_LEN, MLA_KV_RANK), 1.0)
    inp['cache_mla_krope'] = nrm((DEC_BATCH, DEPTH, PAST_LEN, MLA_ROPE), 1.0)
    inp['cache_diff_k'] = nrm((DEC_BATCH, DEPTH, PAST_LEN, DF_HEADS, 2 * DF_DQK), 1.0)
    inp['cache_diff_v'] = nrm((DEC_BATCH, DEPTH, PAST_LEN, DF_HEADS, DF_DV), 1.0)
    inp['state_mlstm_c'] = nrm((DEC_BATCH, DEPTH, 2, ML_HEADS, ML_DH, ML_DH), 0.1)
    inp['state_mlstm_n'] = nrm((DEC_BATCH, DEPTH, 2, ML_HEADS, ML_DH), 0.1)
    inp['state_mlstm_m'] = nrm((DEC_BATCH, DEPTH, 2, ML_HEADS), 1.0)
    inp['state_s5_re'] = nrm((DEC_BATCH, DEPTH, 2, S5_GROUPS, S5_P), 0.1)
    inp['state_s5_im'] = nrm((DEC_BATCH, DEPTH, 2, S5_GROUPS, S5_P), 0.1)
    inp['c'] = nrm((DEC_BATCH, D_MODEL), 1.0)
    inp['c_ctx'] = nrm((D_MODEL,), 1.0)
    inp['w_ada'] = nrm((DEPTH, D_MODEL, 6 * D_MODEL), 0.5 * D_MODEL ** -0.5)
    inp['b_ada'] = nrm((DEPTH, 6 * D_MODEL), 0.01)
    inp['norm_mix'] = gain((DEPTH, D_MODEL))
    inp['norm_ffn'] = gain((DEPTH, D_MODEL))
    inp['w_in'] = nrm((DEPTH, D_MODEL, IN_COLS), D_MODEL ** -0.5)
    i_bias = nrm((DEPTH, 2, ML_HEADS), 0.1)
    f_bias = jnp.linspace(3.0, 6.0, ML_HEADS, dtype=f32) + nrm((DEPTH, 2, ML_HEADS), 0.1)
    inp['ml_if_bias'] = jnp.stack([i_bias, f_bias], axis=2)
    inp['ml_norm'] = gain((DEPTH, ML_W))
    inp['mla_kv_norm'] = gain((DEPTH, MLA_KV_RANK))
    inp['mla_w_kvb'] = nrm((DEPTH, MLA_KV_RANK, MLA_HEADS * (MLA_NOPE + MLA_V)), MLA_KV_RANK ** -0.5)
    n_idx = jnp.arange(S5_P, dtype=f32)
    inp['s5_a_re'] = -0.5 + nrm((DEPTH, 2, S5_GROUPS, S5_P), 0.01)
    inp['s5_a_im'] = math.pi * n_idx + nrm((DEPTH, 2, S5_GROUPS, S5_P), 0.01)
    inp['s5_log_dt'] = jax.random.uniform(next(ks), (DEPTH, 2, S5_GROUPS), f32,
                                          math.log(1e-3), math.log(1e-1))
    inp['s5_b_re'] = nrm((DEPTH, 2, S5_GROUPS, S5_P, S5_GROUP), (2 * S5_GROUP) ** -0.5)
    inp['s5_b_im'] = nrm((DEPTH, 2, S5_GROUPS, S5_P, S5_GROUP), (2 * S5_GROUP) ** -0.5)
    inp['s5_c_re'] = nrm((DEPTH, 2, S5_GROUPS, S5_GROUP, S5_P), S5_P ** -0.5)
    inp['s5_c_im'] = nrm((DEPTH, 2, S5_GROUPS, S5_GROUP, S5_P), S5_P ** -0.5)
    inp['s5_d'] = nrm((DEPTH, S5_W), 1.0)
    inp['s5_w_glu'] = nrm((DEPTH, S5_W, S5_W), S5_W ** -0.5)
    inp['df_lambda'] = nrm((DEPTH, 4, DF_DQK), 0.1)
    inp['df_norm'] = gain((DEPTH, DF_DV))
    inp['w_branch'] = nrm((DEPTH, N_BRANCH, BRANCH_W, D_MODEL), BRANCH_W ** -0.5)
    inp['w_o'] = nrm((DEPTH, D_MODEL, D_MODEL), D_MODEL ** -0.5)
    inp['w_ffn_in'] = nrm((DEPTH, D_MODEL, 2 * FFN_HIDDEN), D_MODEL ** -0.5)
    inp['w_ffn_out'] = nrm((DEPTH, FFN_HIDDEN, D_MODEL), FFN_HIDDEN ** -0.5)
    inp['final_norm'] = gain((D_MODEL,))
    return inp


def reference(x_prompt, x_sample, cache_mla_ckv, cache_mla_krope, cache_diff_k, cache_diff_v,
              state_mlstm_c, state_mlstm_n, state_mlstm_m, state_s5_re, state_s5_im, c,
              c_ctx, w_ada, b_ada, norm_mix, norm_ffn, w_in, ml_if_bias, ml_norm, mla_kv_norm,
              mla_w_kvb, s5_a_re, s5_a_im, s5_log_dt, s5_b_re, s5_b_im, s5_c_re, s5_c_im, s5_d,
              s5_w_glu, df_lambda, df_norm, w_branch, w_o, w_ffn_in, w_ffn_out, final_norm):
    rope = _axial_rope_tables(x_sample.shape[1])
    cond_ctx = jnp.broadcast_to(c_ctx, (x_prompt.shape[0], D_MODEL))
    xp, xs = x_prompt, x_sample
    ctx_out = []
    for l in range(DEPTH):
        lp = {'w_ada': w_ada[l], 'b_ada': b_ada[l], 'norm_mix': norm_mix[l], 'norm_ffn': norm_ffn[l],
              'w_in': w_in[l], 'ml_if_bias': ml_if_bias[l], 'ml_norm': ml_norm[l],
              'mla_kv_norm': mla_kv_norm[l], 'mla_w_kvb': mla_w_kvb[l],
              's5_a_re': s5_a_re[l], 's5_a_im': s5_a_im[l], 's5_log_dt': s5_log_dt[l],
              's5_b_re': s5_b_re[l], 's5_b_im': s5_b_im[l], 's5_c_re': s5_c_re[l], 's5_c_im': s5_c_im[l],
              's5_d': s5_d[l], 's5_w_glu': s5_w_glu[l], 'df_lambda': df_lambda[l], 'df_norm': df_norm[l],
              'lam_init': 0.8 - 0.6 * math.exp(-0.3 * l),
              'w_branch': w_branch[l], 'w_o': w_o[l], 'w_ffn_in': w_ffn_in[l], 'w_ffn_out': w_ffn_out[l]}
        xp, new_ctx = _layer(xp, cond_ctx, lp, None, None)
        ctx_out.append(new_ctx)
        cached = (cache_mla_ckv[:, l], cache_mla_krope[:, l], cache_diff_k[:, l], cache_diff_v[:, l],
                  state_mlstm_c[:, l], state_mlstm_n[:, l], state_mlstm_m[:, l],
                  state_s5_re[:, l], state_s5_im[:, l])
        xs, _ = _layer(xs, c, lp, rope, cached)
    y_prompt = _rmsnorm(xp, final_norm)
    y_sample = _rmsnorm(xs, final_norm)
    new_mla_ckv = jnp.stack([t[0] for t in ctx_out], axis=1)
    new_mla_krope = jnp.stack([t[1] for t in ctx_out], axis=1)
    new_diff_k = jnp.stack([t[2] for t in ctx_out], axis=1)
    new_diff_v = jnp.stack([t[3] for t in ctx_out], axis=1)
    new_mlstm_c = jnp.stack([t[4] for t in ctx_out], axis=1)
    new_mlstm_n = jnp.stack([t[5] for t in ctx_out], axis=1)
    new_mlstm_m = jnp.stack([t[6] for t in ctx_out], axis=1)
    new_s5_re = jnp.stack([t[7] for t in ctx_out], axis=1)
    new_s5_im = jnp.stack([t[8] for t in ctx_out], axis=1)
    return (y_prompt, y_sample, new_mla_ckv, new_mla_krope, new_diff_k, new_diff_v,
            new_mlstm_c, new_mlstm_n, new_mlstm_m, new_s5_re, new_s5_im)
```

```python
import functools
import math

import jax
import jax.numpy as jnp
import numpy as np
from jax import lax
from jax.experimental import pallas as pl
from jax.experimental.pallas import tpu as pltpu

F32 = jnp.float32
BF16 = jnp.bfloat16

D_MODEL = 2048
DEPTH = 2
GRID_W = 64
ROPE_DIM = 64
ROPE_BASE = 10000.0
RMS_EPS = 1e-6
ML_HEADS = 4
ML_DH = 256
ML_W = ML_HEADS * ML_DH
MLA_HEADS = 8
MLA_NOPE = 128
MLA_ROPE = ROPE_DIM
MLA_V = 128
MLA_KV_RANK = 512
MLA_QK = MLA_NOPE + MLA_ROPE
S5_GROUP = 16
S5_GROUPS = 64
S5_W = S5_GROUPS * S5_GROUP
S5_P = 64
S5_STATE = S5_GROUPS * S5_P
DF_HEADS = 8
DF_DQK = ROPE_DIM
DF_DV = 2 * DF_DQK
DF_W = DF_HEADS * DF_DV
N_BRANCH = 4
BRANCH_W = 1024
FFN_HIDDEN = (8 * D_MODEL + 3 * 256 - 1) // (3 * 256) * 256

LANES = 128
SUBLANES = 8
VMEM_LIMIT_BYTES = 56 * 1024 * 1024

COL_ML_Q = 0
COL_ML_K = 1024
COL_ML_V = 2048
COL_ML_O = 3072
COL_S5_U = 4096
COL_DF_Q = 5120
COL_DF_K = 6144
COL_DF_V = 7168
COL_GATE = 8192
COL_MLA_QN = 16384
COL_MLA_QR = 17408
COL_MLA_CKV = 17920
COL_MLA_KR = 18432
COL_ML_IF = 18560
PROJ_TN = 512
PROJ_COLS = 18944

TM_DENSE = 512
ML_CHUNK = 128
S5_CHUNK = 64
S5_SEQS = 4
S5_LANE_BLK = 512
S5_GROUPS_PER_TILE = 16
ATTN_TQ = 256


def _cparams(*sem):
    return pltpu.CompilerParams(dimension_semantics=sem, vmem_limit_bytes=VMEM_LIMIT_BYTES)


def _dot(a, b):
    return jnp.dot(a, b, preferred_element_type=F32)


def _dot_nt(a, b):
    return lax.dot_general(a, b, (((1,), (1,)), ((), ())), preferred_element_type=F32)


def _dot_exact(a, b):
    return jnp.dot(a, b, preferred_element_type=F32, precision=lax.Precision.HIGHEST)


def _rms(x):
    return x * lax.rsqrt(jnp.mean(x * x, axis=-1, keepdims=True) + RMS_EPS)


def _rope_slab(x, cos, sin):
    lane = lax.broadcasted_iota(jnp.int32, x.shape, 1)
    partner = jnp.where((lane % 32) < 16, pltpu.roll(x, LANES - 16, 1), pltpu.roll(x, 16, 1))
    return x * cos + partner * sin


def _ada_kernel(c_ref, w_ref, b_ref, o_ref):
    c = c_ref[...]
    s = c * jax.nn.sigmoid(c)
    o_ref[...] = _dot(s.astype(BF16), w_ref[...].astype(BF16)) + b_ref[...]


def _ada(cond, w_ada, b_ada):
    rows = cond.shape[0]
    tn = 1024
    return pl.pallas_call(
        _ada_kernel,
        grid=(DEPTH, 6 * D_MODEL // tn),
        in_specs=[pl.BlockSpec((rows, D_MODEL), lambda l, j: (0, 0)),
                  pl.BlockSpec((None, D_MODEL, tn), lambda l, j: (l, 0, j)),
                  pl.BlockSpec((None, 1, tn), lambda l, j: (l, 0, j))],
        out_specs=pl.BlockSpec((None, rows, tn), lambda l, j: (l, 0, j)),
        out_shape=jax.ShapeDtypeStruct((DEPTH, rows, 6 * D_MODEL), F32),
        compiler_params=_cparams("parallel", "parallel"),
    )(cond, w_ada, b_ada.reshape(DEPTH, 1, 6 * D_MODEL))


def _mod_spec(which, row_of_tile):
    return pl.BlockSpec((None, None, 1, D_MODEL), lambda i, j: (row_of_tile(i), which, 0, 0))


def _norm_mod(x_ref, g_ref, sc_ref, sh_ref):
    return (_rms(x_ref[...]) * g_ref[...]) * (1.0 + sc_ref[...]) + sh_ref[...]


def _proj_in_kernel(x_ref, g_ref, sc_ref, sh_ref, w_ref, o_ref, h_ref):
    @pl.when(pl.program_id(1) == 0)
    def _():
        h_ref[...] = _norm_mod(x_ref, g_ref, sc_ref, sh_ref).astype(BF16)

    o_ref[...] = _dot(h_ref[...], w_ref[...])


def _proj_in(x, g, mod, row_of_tile, w, tm):
    n = x.shape[0]
    ncol = w.shape[1]
    return pl.pallas_call(
        _proj_in_kernel,
        grid=(n // tm, ncol // PROJ_TN),
        in_specs=[pl.BlockSpec((tm, D_MODEL), lambda i, j: (i, 0)),
                  pl.BlockSpec((1, D_MODEL), lambda i, j: (0, 0)),
                  _mod_spec(1, row_of_tile), _mod_spec(0, row_of_tile),
                  pl.BlockSpec((D_MODEL, PROJ_TN), lambda i, j: (0, j))],
        out_specs=pl.BlockSpec((tm, PROJ_TN), lambda i, j: (i, j)),
        out_shape=jax.ShapeDtypeStruct((n, ncol), F32),
        scratch_shapes=[pltpu.VMEM((tm, D_MODEL), BF16)],
        compiler_params=_cparams("parallel", "arbitrary"),
    )(x, g, mod, mod, w)


def _ffn_in_kernel(x_ref, g_ref, sc_ref, sh_ref, wa_ref, wb_ref, o_ref, h_ref):
    @pl.when(pl.program_id(1) == 0)
    def _():
        h_ref[...] = _norm_mod(x_ref, g_ref, sc_ref, sh_ref).astype(BF16)

    h = h_ref[...]
    a = _dot(h, wa_ref[...])
    b = _dot(h, wb_ref[...])
    o_ref[...] = (a * jax.nn.sigmoid(a) * b).astype(BF16)


def _ffn_in(x, g, mod, row_of_tile, w, tm):
    n = x.shape[0]
    tn = 512
    nj = FFN_HIDDEN // tn
    return pl.pallas_call(
        _ffn_in_kernel,
        grid=(n // tm, nj),
        in_specs=[pl.BlockSpec((tm, D_MODEL), lambda i, j: (i, 0)),
                  pl.BlockSpec((1, D_MODEL), lambda i, j: (0, 0)),
                  _mod_spec(4, row_of_tile), _mod_spec(3, row_of_tile),
                  pl.BlockSpec((D_MODEL, tn), lambda i, j: (0, j)),
                  pl.BlockSpec((D_MODEL, tn), lambda i, j: (0, nj + j))],
        out_specs=pl.BlockSpec((tm, tn), lambda i, j: (i, j)),
        out_shape=jax.ShapeDtypeStruct((n, FFN_HIDDEN), BF16),
        scratch_shapes=[pltpu.VMEM((tm, D_MODEL), BF16)],
        compiler_params=_cparams("parallel", "arbitrary"),
    )(x, g, mod, mod, w, w)


def _resid_kernel(a_ref, w_ref, x_ref, gate_ref, o_ref):
    o_ref[...] = x_ref[...] + gate_ref[...] * _dot(a_ref[...], w_ref[...])


def _matmul_resid(a, w, x, mod, which, row_of_tile, tm):
    n, kdim = a.shape
    tn = 512
    return pl.pallas_call(
        _resid_kernel,
        grid=(n // tm, D_MODEL // tn),
        in_specs=[pl.BlockSpec((tm, kdim), lambda i, j: (i, 0)),
                  pl.BlockSpec((kdim, tn), lambda i, j: (0, j)),
                  pl.BlockSpec((tm, tn), lambda i, j: (i, j)),
                  pl.BlockSpec((None, None, 1, tn), lambda i, j: (row_of_tile(i), which, 0, j))],
        out_specs=pl.BlockSpec((tm, tn), lambda i, j: (i, j)),
        out_shape=jax.ShapeDtypeStruct((n, D_MODEL), F32),
        compiler_params=_cparams("parallel", "parallel"),
    )(a, w, x, mod)


def _final_norm_kernel(x_ref, g_ref, o_ref):
    o_ref[...] = _rms(x_ref[...]) * g_ref[...]


def _final_norm(x, g, tm):
    n = x.shape[0]
    return pl.pallas_call(
        _final_norm_kernel,
        grid=(n // tm,),
        in_specs=[pl.BlockSpec((tm, D_MODEL), lambda i: (i, 0)),
                  pl.BlockSpec((1, D_MODEL), lambda i: (0, 0))],
        out_specs=pl.BlockSpec((tm, D_MODEL), lambda i: (i, 0)),
        out_shape=jax.ShapeDtypeStruct((n, D_MODEL), F32),
        compiler_params=_cparams("parallel"),
    )(x, g)


def _mlstm_kernel(qf_ref, kf_ref, vf_ref, gf_ref, qb_ref, kb_ref, vb_ref, gb_ref, bias_ref,
                  c0_ref, n0_ref, m0_ref,
                  hf_ref, hb_ref, c_out_ref, n_out_ref, m_out_ref,
                  c_scr, n_scr, m_scr):
    ci = pl.program_id(1)
    t = qf_ref.shape[0]

    @pl.when(ci == 0)
    def _():
        c_scr[...] = c0_ref[...]
        n_scr[...] = n0_ref[...]
        m_scr[...] = m0_ref[...]

    row = lax.broadcasted_iota(jnp.int32, (t, t), 0)
    col = lax.broadcasted_iota(jnp.int32, (t, t), 1)
    refs = ((qf_ref, kf_ref, vf_ref, gf_ref, hf_ref), (qb_ref, kb_ref, vb_ref, gb_ref, hb_ref))
    for d in range(2):
        q_ref, k_ref, v_ref, g_ref, h_ref = refs[d]
        keep = (col <= row) if d == 0 else (col >= row)
        cum = jnp.where(keep, 1.0, 0.0).astype(F32)
        gates = g_ref[...] + bias_ref[...]
        csum = _dot_exact(cum, jax.nn.log_sigmoid(gates))
        gates_t = gates.T
        csum_t = csum.T
        last = t - 1 if d == 0 else 0
        for hd in range(ML_HEADS):
            r = d * ML_HEADS + hd
            i_col = d * 2 * ML_HEADS + hd
            f_col = i_col + ML_HEADS
            b_c = csum[:, f_col:f_col + 1]
            b_r = csum_t[f_col:f_col + 1, :]
            li_c = gates[:, i_col:i_col + 1]
            li_r = gates_t[i_col:i_col + 1, :]
            m_st = m_scr[r:r + 1, 0:1]
            c_st = c_scr[r]
            n_st = n_scr[r:r + 1, :]
            sl = slice(hd * ML_DH, (hd + 1) * ML_DH)
            q = q_ref[:, sl]
            k = k_ref[:, sl] * (ML_DH ** -0.5)
            v = v_ref[:, sl]
            qb16 = q.astype(BF16)
            kb16 = k.astype(BF16)

            log_d = jnp.where(keep, b_c - b_r + li_r, -jnp.inf)
            log_inter = b_c + m_st
            m_t = jnp.maximum(log_inter, jnp.max(log_d, axis=1, keepdims=True))
            w_d = jnp.exp(log_d - m_t)
            w_inter = jnp.exp(log_inter - m_t)
            s = _dot_nt(qb16, kb16) * w_d
            num = _dot(s.astype(BF16), v.astype(BF16)) + w_inter * _dot_nt(qb16, c_st.astype(BF16))
            den = jnp.sum(s, axis=1, keepdims=True) + w_inter * jnp.sum(q * n_st, axis=1, keepdims=True)
            h_ref[:, sl] = num / jnp.maximum(jnp.abs(den), jnp.exp(-m_t))

            b_last = b_c[last:last + 1, :]
            log_w = b_last - b_c + li_c
            m_new = jnp.maximum(b_last + m_st, jnp.max(log_w, axis=0, keepdims=True))
            w_s = jnp.exp(log_w - m_new)
            w_c = jnp.exp(b_last + m_st - m_new)
            vw_t = (v * w_s).T.astype(BF16)
            c_scr[r] = w_c * c_st + _dot(vw_t, kb16)
            n_scr[r:r + 1, :] = w_c * n_st + jnp.sum(k * w_s, axis=0, keepdims=True)
            m_scr[r:r + 1, :] = jnp.broadcast_to(m_new, (1, LANES))

    @pl.when(ci == pl.num_programs(1) - 1)
    def _():
        c_out_ref[...] = c_scr[...]
        n_out_ref[...] = n_scr[...]
        m_out_ref[...] = m_scr[...]


def _mlstm(proj, row0, bsz, seq, bias, c0, n0, m0):
    tc = min(ML_CHUNK, seq)
    nc = seq // tc
    blk0 = row0 // tc
    r8 = 2 * ML_HEADS

    def fwd(colblk):
        return lambda b, c: (blk0 + b * nc + c, colblk)

    def bwd(colblk):
        return lambda b, c: (blk0 + b * nc + nc - 1 - c, colblk)

    def seqspecs(mk):
        return [pl.BlockSpec((tc, ML_W), mk(COL_ML_Q // ML_W)),
                pl.BlockSpec((tc, ML_W), mk(COL_ML_K // ML_W)),
                pl.BlockSpec((tc, ML_W), mk(COL_ML_V // ML_W)),
                pl.BlockSpec((tc, LANES), mk(COL_ML_IF // LANES))]

    state_specs = [pl.BlockSpec((None, r8, ML_DH, ML_DH), lambda b, c: (b, 0, 0, 0)),
                   pl.BlockSpec((None, r8, ML_DH), lambda b, c: (b, 0, 0)),
                   pl.BlockSpec((None, r8, LANES), lambda b, c: (b, 0, 0))]
    return pl.pallas_call(
        _mlstm_kernel,
        grid=(bsz, nc),
        in_specs=seqspecs(fwd) + seqspecs(bwd) + [pl.BlockSpec((1, LANES), lambda b, c: (0, 0))] + state_specs,
        out_specs=[pl.BlockSpec((tc, ML_W), lambda b, c: (b * nc + c, 0)),
                   pl.BlockSpec((tc, ML_W), lambda b, c: (b * nc + nc - 1 - c, 0))] + state_specs,
        out_shape=[jax.ShapeDtypeStruct((bsz * seq, ML_W), F32),
                   jax.ShapeDtypeStruct((bsz * seq, ML_W), F32),
                   jax.ShapeDtypeStruct((bsz, r8, ML_DH, ML_DH), F32),
                   jax.ShapeDtypeStruct((bsz, r8, ML_DH), F32),
                   jax.ShapeDtypeStruct((bsz, r8, LANES), F32)],
        scratch_shapes=[pltpu.VMEM((r8, ML_DH, ML_DH), F32),
                        pltpu.VMEM((r8, ML_DH), F32),
                        pltpu.VMEM((r8, LANES), F32)],
        compiler_params=_cparams("parallel", "arbitrary"),
    )(proj, proj, proj, proj, proj, proj, proj, proj, bias, c0, n0, m0)


def _ml_post_kernel(hf_ref, hb_ref, o_ref, g_ref, y_ref):
    h = hf_ref[...] + hb_ref[...]
    for hd in range(ML_HEADS):
        sl = slice(hd * ML_DH, (hd + 1) * ML_DH)
        y_ref[:, sl] = (_rms(h[:, sl]) * g_ref[:, sl] * jax.nn.sigmoid(o_ref[:, sl])).astype(BF16)


def _ml_post(hf, hb, proj, row0, g, tm):
    n = hf.shape[0]
    blk0 = row0 // tm
    return pl.pallas_call(
        _ml_post_kernel,
        grid=(n // tm,),
        in_specs=[pl.BlockSpec((tm, ML_W), lambda i: (i, 0)),
                  pl.BlockSpec((tm, ML_W), lambda i: (i, 0)),
                  pl.BlockSpec((tm, ML_W), lambda i: (blk0 + i, COL_ML_O // ML_W)),
                  pl.BlockSpec((1, ML_W), lambda i: (0, 0))],
        out_specs=pl.BlockSpec((tm, ML_W), lambda i: (i, 0)),
        out_shape=jax.ShapeDtypeStruct((n, ML_W), BF16),
        compiler_params=_cparams("parallel"),
    )(hf, hb, proj, g)


def _mla_prep_kernel(*refs, norm, rope, with_q):
    it = iter(refs)
    ckv_ref, kr_ref, g_ref, wkvb_ref = next(it), next(it), next(it), next(it)
    if with_q:
        qn_ref, qr_ref = next(it), next(it)
    if rope:
        cos_ref, sin_ref = next(it), next(it)
    k_ref, v_ref = next(it), next(it)
    if with_q:
        q_ref = next(it)
    if norm:
        ckv_out_ref = next(it)

    ckv = ckv_ref[...]
    if norm:
        ckv = _rms(ckv) * g_ref[...]
        ckv_out_ref[...] = ckv
    kv = _dot(ckv.astype(BF16), wkvb_ref[...])
    kr = kr_ref[...]
    if rope:
        kr = _rope_slab(kr, cos_ref[...], sin_ref[...])
    kr16 = kr[:, :MLA_ROPE].astype(BF16)
    hw = MLA_NOPE + MLA_V
    for hd in range(MLA_HEADS):
        k_ref[hd, :, 0:MLA_NOPE] = kv[:, hd * hw:hd * hw + MLA_NOPE].astype(BF16)
        k_ref[hd, :, MLA_NOPE:MLA_QK] = kr16
        v_ref[hd] = kv[:, hd * hw + MLA_NOPE:(hd + 1) * hw].astype(BF16)
    if with_q:
        scale = MLA_QK ** -0.5
        qn = qn_ref[...]
        for sb in range(MLA_HEADS * MLA_ROPE // LANES):
            qr = qr_ref[:, sb * LANES:(sb + 1) * LANES]
            if rope:
                qr = _rope_slab(qr, cos_ref[...], sin_ref[...])
            for half in range(LANES // MLA_ROPE):
                hd = sb * (LANES // MLA_ROPE) + half
                q_ref[hd, :, MLA_NOPE:MLA_QK] = (qr[:, half * MLA_ROPE:(half + 1) * MLA_ROPE] * scale).astype(BF16)
        for hd in range(MLA_HEADS):
            q_ref[hd, :, 0:MLA_NOPE] = (qn[:, hd * MLA_NOPE:(hd + 1) * MLA_NOPE] * scale).astype(BF16)


def _mla_prep(ckv_src, kr_src, ckv_col, kr_col, row0, n, g, wkvb, tm, q_src=None, cos=None, sin=None, norm=True):
    rope = cos is not None
    with_q = q_src is not None
    blk0 = row0 // tm
    ins = [ckv_src, kr_src, g, wkvb]
    in_specs = [pl.BlockSpec((tm, MLA_KV_RANK), lambda i: (blk0 + i, ckv_col)),
                pl.BlockSpec((tm, LANES), lambda i: (blk0 + i, kr_col)),
                pl.BlockSpec((1, MLA_KV_RANK), lambda i: (0, 0)),
                pl.BlockSpec(wkvb.shape, lambda i: (0, 0))]
    if with_q:
        ins += [q_src, q_src]
        in_specs += [pl.BlockSpec((tm, MLA_HEADS * MLA_NOPE), lambda i: (blk0 + i, COL_MLA_QN // (MLA_HEADS * MLA_NOPE))),
                     pl.BlockSpec((tm, MLA_HEADS * MLA_ROPE), lambda i: (blk0 + i, COL_MLA_QR // (MLA_HEADS * MLA_ROPE)))]
    if rope:
        nt = cos.shape[0] // tm
        ins += [cos, sin]
        in_specs += [pl.BlockSpec((tm, LANES), lambda i: (i % nt, 0))] * 2
    out_shape = [jax.ShapeDtypeStruct((MLA_HEADS, n, MLA_QK), BF16),
                 jax.ShapeDtypeStruct((MLA_HEADS, n, MLA_V), BF16)]
    out_specs = [pl.BlockSpec((MLA_HEADS, tm, MLA_QK), lambda i: (0, i, 0)),
                 pl.BlockSpec((MLA_HEADS, tm, MLA_V), lambda i: (0, i, 0))]
    if with_q:
        out_shape.append(jax.ShapeDtypeStruct((MLA_HEADS, n, MLA_QK), BF16))
        out_specs.append(pl.BlockSpec((MLA_HEADS, tm, MLA_QK), lambda i: (0, i, 0)))
    if norm:
        out_shape.append(jax.ShapeDtypeStruct((n, MLA_KV_RANK), F32))
        out_specs.append(pl.BlockSpec((tm, MLA_KV_RANK), lambda i: (i, 0)))
    return pl.pallas_call(
        functools.partial(_mla_prep_kernel, norm=norm, rope=rope, with_q=with_q),
        grid=(n // tm,),
        in_specs=in_specs, out_specs=out_specs, out_shape=out_shape,
        compiler_params=_cparams("parallel"),
    )(*ins)


def _mla_attn_kernel(q_ref, k_ref, v_ref, o_ref):
    s = _dot_nt(q_ref[...], k_ref[...])
    p = jnp.exp(s - jnp.max(s, axis=-1, keepdims=True))
    l = jnp.sum(p, axis=-1, keepdims=True)
    o_ref[...] = (_dot(p.astype(BF16), v_ref[...]) / l).astype(BF16)


def _mla_attn(q, k, v, bsz, seq):
    tq = min(ATTN_TQ, seq)
    nq = seq // tq
    keys = k.shape[2]
    return pl.pallas_call(
        _mla_attn_kernel,
        grid=(MLA_HEADS, bsz, nq),
        in_specs=[pl.BlockSpec((None, tq, MLA_QK), lambda h, b, i: (h, b * nq + i, 0)),
                  pl.BlockSpec((None, None, keys, MLA_QK), lambda h, b, i: (h, b, 0, 0)),
                  pl.BlockSpec((None, None, keys, MLA_V), lambda h, b, i: (h, b, 0, 0))],
        out_specs=pl.BlockSpec((tq, MLA_V), lambda h, b, i: (b * nq + i, h)),
        out_shape=jax.ShapeDtypeStruct((bsz * seq, MLA_HEADS * MLA_V), BF16),
        compiler_params=_cparams("parallel", "parallel", "parallel"),
    )(q, k, v)


def _rope_cast_kernel(*refs, rope, scale):
    if rope:
        x_ref, cos_ref, sin_ref, o_ref = refs
    else:
        x_ref, o_ref = refs
    for sb in range(x_ref.shape[1] // LANES):
        sl = slice(sb * LANES, (sb + 1) * LANES)
        x = x_ref[:, sl]
        if rope:
            x = _rope_slab(x, cos_ref[...], sin_ref[...])
        o_ref[:, sl] = (x * scale).astype(BF16)


def _rope_cast(src, colblk, row0, n, tm, scale=1.0, cos=None, sin=None):
    rope = cos is not None
    blk0 = row0 // tm
    ins = [src]
    in_specs = [pl.BlockSpec((tm, DF_W), lambda i: (blk0 + i, colblk))]
    if rope:
        nt = cos.shape[0] // tm
        ins += [cos, sin]
        in_specs += [pl.BlockSpec((tm, LANES), lambda i: (i % nt, 0))] * 2
    return pl.pallas_call(
        functools.partial(_rope_cast_kernel, rope=rope, scale=scale),
        grid=(n // tm,),
        in_specs=in_specs,
        out_specs=pl.BlockSpec((tm, DF_W), lambda i: (i, 0)),
        out_shape=jax.ShapeDtypeStruct((n, DF_W), BF16),
        compiler_params=_cparams("parallel"),
    )(*ins)


def _diff_attn_kernel(q_ref, k_ref, v_ref, lam_ref, g_ref, o_ref, *, lam_init):
    lp = lam_ref[...]
    lam = (jnp.exp(jnp.sum(lp[0:1] * lp[1:2], axis=-1, keepdims=True))
           - jnp.exp(jnp.sum(lp[2:3] * lp[3:4], axis=-1, keepdims=True)) + lam_init)
    q = q_ref[...]
    k = k_ref[...]
    lane = lax.broadcasted_iota(jnp.int32, q.shape, 1)
    zero = jnp.zeros_like(q)
    s1 = _dot_nt(jnp.where(lane < DF_DQK, q, zero), k)
    s2 = _dot_nt(jnp.where(lane >= DF_DQK, q, zero), k)
    p1 = jnp.exp(s1 - jnp.max(s1, axis=-1, keepdims=True))
    p2 = jnp.exp(s2 - jnp.max(s2, axis=-1, keepdims=True))
    r1 = 1.0 / jnp.sum(p1, axis=-1, keepdims=True)
    r2 = lam / jnp.sum(p2, axis=-1, keepdims=True)
    pd = (p1 * r1 - p2 * r2).astype(BF16)
    o = _dot(pd, v_ref[...])
    o_ref[...] = (_rms(o) * g_ref[...] * (1.0 - lam_init)).astype(BF16)


def _diff_attn(q, k, v, lam_p, g, lam_init, bsz, seq):
    tq = min(ATTN_TQ, seq)
    nq = seq // tq
    keys = k.shape[1]
    return pl.pallas_call(
        functools.partial(_diff_attn_kernel, lam_init=lam_init),
        grid=(bsz, DF_HEADS, nq),
        in_specs=[pl.BlockSpec((tq, DF_DV), lambda b, h, i: (b * nq + i, h)),
                  pl.BlockSpec((None, keys, DF_DV), lambda b, h, i: (b, 0, h)),
                  pl.BlockSpec((None, keys, DF_DV), lambda b, h, i: (b, 0, h)),
                  pl.BlockSpec((4, DF_DQK), lambda b, h, i: (0, 0)),
                  pl.BlockSpec((1, DF_DV), lambda b, h, i: (0, 0))],
        out_specs=pl.BlockSpec((tq, DF_DV), lambda b, h, i: (b * nq + i, h)),
        out_shape=jax.ShapeDtypeStruct((bsz * seq, DF_W), BF16),
        compiler_params=_cparams("parallel", "parallel", "parallel"),
    )(q, k, v, lam_p, g)


def _s5_kernel(uf_ref, ub_ref, a_re_ref, a_im_ref, h0_re_ref, h0_im_ref,
               wb_re_ref, wb_im_ref, wc_re_ref, wc_im_ref,
               yf_ref, yb_ref, hr_out_ref, hi_out_ref,
               bu_re, bu_im, h_re, h_im):
    ci = pl.program_id(1)
    tc = uf_ref.shape[1]
    nseq = 2 * S5_SEQS
    gblk, sblk = wb_re_ref.shape[2:]
    ngb = S5_W // gblk

    @pl.when(ci == 0)
    def _():
        h_re[...] = h0_re_ref[...]
        h_im[...] = h0_im_ref[...]

    ri = lax.broadcasted_iota(jnp.int32, (tc, tc), 0)
    cj = lax.broadcasted_iota(jnp.int32, (tc, tc), 1)
    rev = jnp.where(ri + cj == tc - 1, 1.0, 0.0).astype(BF16)

    for d, u_ref in enumerate((uf_ref, ub_ref)):
        us = []
        for s in range(S5_SEQS):
            u = u_ref[s].astype(BF16)
            if d == 1:
                u = _dot(rev, u).astype(BF16)
            us.append(u)
        u_all = jnp.concatenate(us, axis=0)
        for gb in range(ngb):
            ug = u_all[:, gb * gblk:(gb + 1) * gblk]
            for w_ref, dst in ((wb_re_ref, bu_re), (wb_im_ref, bu_im)):
                bu = _dot(ug, w_ref[d, gb])
                for s in range(S5_SEQS):
                    for lk in range(sblk // LANES):
                        dst[gb * (sblk // LANES) + lk, pl.ds(d * S5_SEQS + s, tc, stride=nseq), :] = (
                            bu[s * tc:(s + 1) * tc, lk * LANES:(lk + 1) * LANES])

    nlk = S5_LANE_BLK // LANES
    for lb in range(S5_STATE // S5_LANE_BLK):
        lks = tuple(range(lb * nlk, (lb + 1) * nlk))
        ar = [a_re_ref[:, k * LANES:(k + 1) * LANES] for k in lks]
        ai = [a_im_ref[:, k * LANES:(k + 1) * LANES] for k in lks]

        def step(j, carry):
            r0 = pl.multiple_of(j * nseq, nseq)
            out = []
            for i, k in enumerate(lks):
                hr, hi = carry[2 * i], carry[2 * i + 1]
                nhr = ar[i] * hr - ai[i] * hi + bu_re[k, pl.ds(r0, nseq), :]
                nhi = ar[i] * hi + ai[i] * hr + bu_im[k, pl.ds(r0, nseq), :]
                bu_re[k, pl.ds(r0, nseq), :] = nhr
                bu_im[k, pl.ds(r0, nseq), :] = nhi
                out += [nhr, nhi]
            return tuple(out)

        init = []
        for k in lks:
            init += [h_re[:, k * LANES:(k + 1) * LANES], h_im[:, k * LANES:(k + 1) * LANES]]
        fin = lax.fori_loop(0, tc, step, tuple(init), unroll=8)
        for i, k in enumerate(lks):
            h_re[:, k * LANES:(k + 1) * LANES] = fin[2 * i]
            h_im[:, k * LANES:(k + 1) * LANES] = fin[2 * i + 1]

    def seq_states(src, s):
        parts = [src[k, pl.ds(s, tc, stride=nseq), :] for k in range(S5_STATE // LANES)]
        return jnp.concatenate(parts, axis=1).astype(BF16)

    for d, y_ref in enumerate((yf_ref, yb_ref)):
        hs_re, hs_im = [], []
        for s in range(S5_SEQS):
            hr = seq_states(bu_re, d * S5_SEQS + s)
            hi = seq_states(bu_im, d * S5_SEQS + s)
            if d == 1:
                hr = _dot(rev, hr).astype(BF16)
                hi = _dot(rev, hi).astype(BF16)
            hs_re.append(hr)
            hs_im.append(hi)
        hr_all = jnp.concatenate(hs_re, axis=0)
        hi_all = jnp.concatenate(hs_im, axis=0)
        for gb in range(ngb):
            ssl = slice(gb * sblk, (gb + 1) * sblk)
            y = _dot(hr_all[:, ssl], wc_re_ref[d, gb]) - _dot(hi_all[:, ssl], wc_im_ref[d, gb])
            for s in range(S5_SEQS):
                y_ref[s, :, gb * gblk:(gb + 1) * gblk] = y[s * tc:(s + 1) * tc]

    @pl.when(ci == pl.num_programs(1) - 1)
    def _():
        hr_out_ref[...] = h_re[...]
        hi_out_ref[...] = h_im[...]


def _s5(proj3, g0, bsz, seq, a_re, a_im, h0_re, h0_im, wb_re, wb_im, wc_re, wc_im):
    tc = min(S5_CHUNK, seq)
    nc = seq // tc
    ng = bsz // S5_SEQS
    nseq = 2 * S5_SEQS
    ucol = COL_S5_U // S5_W
    const4 = lambda g, c: (0, 0, 0, 0)
    wspec = lambda w: pl.BlockSpec(w.shape, const4, pipeline_mode=pl.Buffered(1))
    hspec = pl.BlockSpec((None, nseq, S5_STATE), lambda g, c: (g, 0, 0))
    return pl.pallas_call(
        _s5_kernel,
        grid=(ng, nc),
        in_specs=[pl.BlockSpec((S5_SEQS, tc, S5_W), lambda g, c: (g0 + g, c, ucol)),
                  pl.BlockSpec((S5_SEQS, tc, S5_W), lambda g, c: (g0 + g, nc - 1 - c, ucol)),
                  pl.BlockSpec((nseq, S5_STATE), lambda g, c: (0, 0)),
                  pl.BlockSpec((nseq, S5_STATE), lambda g, c: (0, 0)),
                  hspec, hspec, wspec(wb_re), wspec(wb_im), wspec(wc_re), wspec(wc_im)],
        out_specs=[pl.BlockSpec((S5_SEQS, tc, S5_W), lambda g, c: (g, c, 0)),
                   pl.BlockSpec((S5_SEQS, tc, S5_W), lambda g, c: (g, nc - 1 - c, 0)),
                   hspec, hspec],
        out_shape=[jax.ShapeDtypeStruct((bsz, seq, S5_W), F32),
                   jax.ShapeDtypeStruct((bsz, seq, S5_W), F32),
                   jax.ShapeDtypeStruct((ng, nseq, S5_STATE), F32),
                   jax.ShapeDtypeStruct((ng, nseq, S5_STATE), F32)],
        scratch_shapes=[pltpu.VMEM((S5_STATE // LANES, tc * nseq, LANES), F32),
                        pltpu.VMEM((S5_STATE // LANES, tc * nseq, LANES), F32),
                        pltpu.VMEM((nseq, S5_STATE), F32),
                        pltpu.VMEM((nseq, S5_STATE), F32)],
        compiler_params=_cparams("parallel", "arbitrary"),
    )(proj3, proj3, a_re, a_im, h0_re, h0_im, wb_re, wb_im, wc_re, wc_im)


def _s5_post_kernel(yf_ref, yb_ref, u_ref, d_ref, w_ref, o_ref):
    y = (yf_ref[...] + yb_ref[...]) + d_ref[...] * u_ref[...]
    g = jax.nn.gelu(y)
    o_ref[...] = (g * jax.nn.sigmoid(_dot(g.astype(BF16), w_ref[...]))).astype(BF16)


def _s5_post(yf, yb, proj, row0, d, w, tm):
    n = yf.shape[0]
    blk0 = row0 // tm
    return pl.pallas_call(
        _s5_post_kernel,
        grid=(n // tm,),
        in_specs=[pl.BlockSpec((tm, S5_W), lambda i: (i, 0)),
                  pl.BlockSpec((tm, S5_W), lambda i: (i, 0)),
                  pl.BlockSpec((tm, S5_W), lambda i: (blk0 + i, COL_S5_U // S5_W)),
                  pl.BlockSpec((1, S5_W), lambda i: (0, 0)),
                  pl.BlockSpec((S5_W, S5_W), lambda i: (0, 0))],
        out_specs=pl.BlockSpec((tm, S5_W), lambda i: (i, 0)),
        out_shape=jax.ShapeDtypeStruct((n, S5_W), BF16),
        compiler_params=_cparams("parallel"),
    )(yf, yb, proj, d, w)


def _s5_discretise(a_re, a_im, log_dt, b_re, b_im):
    lr = jnp.minimum(a_re, -1e-4)
    li = a_im
    dt = jnp.exp(log_dt)[..., None]
    mag = jnp.exp(dt * lr)
    ab_re, ab_im = mag * jnp.cos(dt * li), mag * jnp.sin(dt * li)
    den = lr * lr + li * li
    nr, ni = ab_re - 1.0, ab_im
    qr = (nr * lr + ni * li) / den
    qi = (ni * lr - nr * li) / den
    bb_re = qr[..., None] * b_re - qi[..., None] * b_im
    bb_im = qr[..., None] * b_im + qi[..., None] * b_re
    return ab_re, ab_im, bb_re, bb_im


def _block_diag(w):
    ngrp = S5_GROUPS_PER_TILE
    d, g, r, c = w.shape
    wg = w.reshape(d, g // ngrp, ngrp, r, c)
    eye = jnp.eye(ngrp, dtype=w.dtype)
    out = jnp.einsum('dbgrc,gh->dbgrhc', wg, eye)
    return out.reshape(d, g // ngrp, ngrp * r, ngrp * c)


def _merge_kernel(y0_ref, y1_ref, y2_ref, y3_ref, g0_ref, g1_ref, g2_ref, g3_ref, w_ref, o_ref):
    acc = None
    for b, (y_ref, g_ref) in enumerate(((y0_ref, g0_ref), (y1_ref, g1_ref), (y2_ref, g2_ref), (y3_ref, g3_ref))):
        term = jax.nn.sigmoid(g_ref[...]) * _dot(y_ref[...], w_ref[b])
        acc = term if acc is None else acc + term
    o_ref[...] = acc.astype(BF16)


def _merge(ys, proj, wb, tm):
    n = ys[0].shape[0]
    tn = 512
    gate0 = COL_GATE // tn
    per = D_MODEL // tn

    def gspec(b):
        return pl.BlockSpec((tm, tn), lambda i, j: (i, gate0 + b * per + j))

    return pl.pallas_call(
        _merge_kernel,
        grid=(n // tm, per),
        in_specs=[pl.BlockSpec((tm, BRANCH_W), lambda i, j: (i, 0))] * N_BRANCH
                 + [gspec(b) for b in range(N_BRANCH)]
                 + [pl.BlockSpec((N_BRANCH, BRANCH_W, tn), lambda i, j: (0, 0, j))],
        out_specs=pl.BlockSpec((tm, tn), lambda i, j: (i, j)),
        out_shape=jax.ShapeDtypeStruct((n, D_MODEL), BF16),
        compiler_params=_cparams("parallel", "parallel"),
    )(*ys, proj, proj, proj, proj, wb)


def _rope_tables(n_tok):
    grid_rows = n_tok // GRID_W
    rows, cols = jnp.meshgrid(jnp.arange(grid_rows, dtype=F32), jnp.arange(GRID_W, dtype=F32), indexing='ij')
    quarter = ROPE_DIM // 4
    inv = ROPE_BASE ** (-jnp.arange(quarter, dtype=F32) / quarter)
    ang_r = rows.reshape(-1, 1) * inv
    ang_c = cols.reshape(-1, 1) * inv
    cos = jnp.concatenate([jnp.cos(ang_r)] * 2 + [jnp.cos(ang_c)] * 2, axis=-1)
    sin = jnp.concatenate([-jnp.sin(ang_r), jnp.sin(ang_r), -jnp.sin(ang_c), jnp.sin(ang_c)], axis=-1)
    return jnp.tile(cos, (1, LANES // ROPE_DIM)), jnp.tile(sin, (1, LANES // ROPE_DIM))


def _permute_w_in(w):
    sizes = (ML_W, ML_W, ML_W, ML_W, 4 * ML_HEADS, MLA_HEADS * MLA_QK, MLA_KV_RANK + MLA_ROPE, S5_W,
             DF_W, DF_W, DF_W, N_BRANCH * D_MODEL)
    splits = tuple(int(s) for s in np.cumsum(sizes)[:-1])
    (ml_q, ml_k, ml_v, ml_o, ml_if, mla_q, mla_kva, s5_u, df_q, df_k, df_v, gate) = jnp.split(w, splits, axis=1)
    mq = mla_q.reshape(D_MODEL, MLA_HEADS, MLA_QK)
    qn = mq[:, :, :MLA_NOPE].reshape(D_MODEL, MLA_HEADS * MLA_NOPE)
    qr = mq[:, :, MLA_NOPE:].reshape(D_MODEL, MLA_HEADS * MLA_ROPE)
    pad = lambda a, width: jnp.pad(a, ((0, 0), (0, width - a.shape[1])))
    cols = [ml_q, ml_k, ml_v, ml_o, s5_u, df_q, df_k, df_v, gate, qn, qr,
            mla_kva[:, :MLA_KV_RANK], pad(mla_kva[:, MLA_KV_RANK:], LANES), pad(ml_if, LANES)]
    out = jnp.concatenate(cols, axis=1)
    return pad(out, PROJ_COLS).astype(BF16)


def _seq_mixers(proj, row0, bsz, seq, lw, lam_init, state, cache, rope):
    n = bsz * seq
    tm = min(TM_DENSE, seq)
    ml_c0, ml_n0, ml_m0, s5_h0r, s5_h0i = state
    r8 = 2 * ML_HEADS

    m0 = jnp.broadcast_to(ml_m0.reshape(bsz, r8, 1), (bsz, r8, LANES))
    hf, hb, c_new, n_new, m_new = _mlstm(proj, row0, bsz, seq, lw['ml_bias'],
                                         ml_c0.reshape(bsz, r8, ML_DH, ML_DH), ml_n0.reshape(bsz, r8, ML_DH), m0)
    y_ml = _ml_post(hf, hb, proj, row0, lw['ml_norm'], tm)

    cos, sin = rope if rope is not None else (None, None)
    k_new, v_new, q_mla, ckv = _mla_prep(proj, proj, COL_MLA_CKV // MLA_KV_RANK, COL_MLA_KR // LANES, row0, n,
                                         lw['mla_kv_norm'], lw['mla_w_kvb'], tm, q_src=proj, cos=cos, sin=sin)
    k_all = k_new.reshape(MLA_HEADS, bsz, seq, MLA_QK)
    v_all = v_new.reshape(MLA_HEADS, bsz, seq, MLA_V)
    if cache is not None:
        ckv_c, krope_c, dk_c, dv_c = cache
        past = ckv_c.shape[1]
        kr_pad = jnp.pad(krope_c.reshape(bsz * past, MLA_ROPE), ((0, 0), (0, LANES - MLA_ROPE)))
        k_c, v_c = _mla_prep(ckv_c.reshape(bsz * past, MLA_KV_RANK), kr_pad, 0, 0, 0, bsz * past,
                             lw['mla_kv_norm'], lw['mla_w_kvb'], min(tm, bsz * past), norm=False)
        k_all = jnp.concatenate([k_all, k_c.reshape(MLA_HEADS, bsz, past, MLA_QK)], axis=2)
        v_all = jnp.concatenate([v_all, v_c.reshape(MLA_HEADS, bsz, past, MLA_V)], axis=2)
    y_mla = _mla_attn(q_mla, k_all, v_all, bsz, seq)

    proj3 = proj.reshape(proj.shape[0] // seq, seq, PROJ_COLS)
    ng = bsz // S5_SEQS

    def pack_state(hs):
        return hs.reshape(ng, S5_SEQS, 2, S5_STATE).transpose(0, 2, 1, 3).reshape(ng, 2 * S5_SEQS, S5_STATE)

    def unpack_state(hs):
        return hs.reshape(ng, 2, S5_SEQS, S5_STATE).transpose(0, 2, 1, 3).reshape(bsz, 2, S5_GROUPS, S5_P)

    yf, yb, hr_new, hi_new = _s5(proj3, row0 // (seq * S5_SEQS), bsz, seq, lw['s5_a_re8'], lw['s5_a_im8'],
                                 pack_state(s5_h0r), pack_state(s5_h0i),
                                 lw['s5_wb_re'], lw['s5_wb_im'], lw['s5_wc_re'], lw['s5_wc_im'])
    y_s5 = _s5_post(yf.reshape(n, S5_W), yb.reshape(n, S5_W), proj, row0, lw['s5_d'], lw['s5_w_glu'], tm)

    dq = _rope_cast(proj, COL_DF_Q // DF_W, row0, n, tm, scale=DF_DQK ** -0.5, cos=cos, sin=sin)
    dk = _rope_cast(proj, COL_DF_K // DF_W, row0, n, tm, cos=cos, sin=sin).reshape(bsz, seq, DF_W)
    dv = _rope_cast(proj, COL_DF_V // DF_W, row0, n, tm).reshape(bsz, seq, DF_W)
    if cache is not None:
        dk = jnp.concatenate([dk, dk_c.reshape(bsz, -1, DF_W).astype(BF16)], axis=1)
        dv = jnp.concatenate([dv, dv_c.reshape(bsz, -1, DF_W).astype(BF16)], axis=1)
    y_df = _diff_attn(dq, dk, dv, lw['df_lambda'], lw['df_norm'], lam_init, bsz, seq)

    cols = lambda c0, w: lax.slice(proj, (row0, c0), (row0 + n, c0 + w))
    new_ctx = (ckv.reshape(bsz, seq, MLA_KV_RANK),
               cols(COL_MLA_KR, MLA_ROPE).reshape(bsz, seq, MLA_ROPE),
               cols(COL_DF_K, DF_W).reshape(bsz, seq, DF_HEADS, 2 * DF_DQK),
               cols(COL_DF_V, DF_W).reshape(bsz, seq, DF_HEADS, DF_DV),
               c_new.reshape(bsz, 2, ML_HEADS, ML_DH, ML_DH), n_new.reshape(bsz, 2, ML_HEADS, ML_DH),
               m_new[:, :, 0].reshape(bsz, 2, ML_HEADS), unpack_state(hr_new), unpack_state(hi_new))
    return (y_ml, y_mla, y_s5, y_df), new_ctx


def kernel(x_prompt, x_sample, cache_mla_ckv, cache_mla_krope, cache_diff_k, cache_diff_v,
           state_mlstm_c, state_mlstm_n, state_mlstm_m, state_s5_re, state_s5_im, c,
           c_ctx, w_ada, b_ada, norm_mix, norm_ffn, w_in, ml_if_bias, ml_norm, mla_kv_norm,
           mla_w_kvb, s5_a_re, s5_a_im, s5_log_dt, s5_b_re, s5_b_im, s5_c_re, s5_c_im, s5_d,
           s5_w_glu, df_lambda, df_norm, w_branch, w_o, w_ffn_in, w_ffn_out, final_norm):
    bp, sp, _ = x_prompt.shape
    bs, ss, _ = x_sample.shape
    n_p, n_s = bp * sp, bs * ss
    n = n_p + n_s
    tm = TM_DENSE

    def row_of_tile(i):
        return jnp.minimum((i * tm) // ss, bs)

    cond = jnp.concatenate([c, c_ctx[None, :], jnp.zeros((SUBLANES - 1 - bs, D_MODEL), F32)], axis=0)
    mods = _ada(cond, w_ada, b_ada).reshape(DEPTH, SUBLANES, 6, 1, D_MODEL)
    rope = _rope_tables(ss)

    x = jnp.concatenate([x_sample.reshape(n_s, D_MODEL), x_prompt.reshape(n_p, D_MODEL)], axis=0)
    ctx_out = []
    for l in range(DEPTH):
        lam_init = 0.8 - 0.6 * math.exp(-0.3 * l)
        ab_re, ab_im, bb_re, bb_im = _s5_discretise(s5_a_re[l], s5_a_im[l], s5_log_dt[l], s5_b_re[l], s5_b_im[l])
        rep = lambda a: jnp.repeat(a.reshape(2, S5_STATE), S5_SEQS, axis=0)
        lw = {
            'ml_bias': jnp.pad(ml_if_bias[l].reshape(1, 4 * ML_HEADS), ((0, 0), (0, LANES - 4 * ML_HEADS))),
            'ml_norm': ml_norm[l].reshape(1, ML_W),
            'mla_kv_norm': mla_kv_norm[l].reshape(1, MLA_KV_RANK),
            'mla_w_kvb': mla_w_kvb[l].astype(BF16),
            's5_a_re8': rep(ab_re), 's5_a_im8': rep(ab_im),
            's5_wb_re': _block_diag(bb_re.transpose(0, 1, 3, 2)).astype(BF16),
            's5_wb_im': _block_diag(bb_im.transpose(0, 1, 3, 2)).astype(BF16),
            's5_wc_re': _block_diag(s5_c_re[l].transpose(0, 1, 3, 2)).astype(BF16),
            's5_wc_im': _block_diag(s5_c_im[l].transpose(0, 1, 3, 2)).astype(BF16),
            's5_d': s5_d[l].reshape(1, S5_W),
            's5_w_glu': s5_w_glu[l].astype(BF16),
            'df_lambda': df_lambda[l],
            'df_norm': df_norm[l].reshape(1, DF_DV),
        }
        mod = mods[l]
        proj = _proj_in(x, norm_mix[l].reshape(1, D_MODEL), mod, row_of_tile, _permute_w_in(w_in[l]), tm)

        zeros_state = (jnp.zeros((bp, 2, ML_HEADS, ML_DH, ML_DH), F32), jnp.zeros((bp, 2, ML_HEADS, ML_DH), F32),
                       jnp.zeros((bp, 2, ML_HEADS), F32), jnp.zeros((bp, 2, S5_GROUPS, S5_P), F32),
                       jnp.zeros((bp, 2, S5_GROUPS, S5_P), F32))
        ys_p, new_ctx = _seq_mixers(proj, n_s, bp, sp, lw, lam_init, zeros_state, None, None)
        ctx_out.append(new_ctx)
        state = (state_mlstm_c[:, l], state_mlstm_n[:, l], state_mlstm_m[:, l], state_s5_re[:, l], state_s5_im[:, l])
        cache = (cache_mla_ckv[:, l], cache_mla_krope[:, l], cache_diff_k[:, l], cache_diff_v[:, l])
        ys_s, _ = _seq_mixers(proj, 0, bs, ss, lw, lam_init, state, cache, rope)

        ys = [jnp.concatenate([a, b], axis=0) for a, b in zip(ys_s, ys_p)]
        merged = _merge(ys, proj, w_branch[l].astype(BF16), tm)
        x = _matmul_resid(merged, w_o[l].astype(BF16), x, mod, 2, row_of_tile, tm)
        act = _ffn_in(x, norm_ffn[l].reshape(1, D_MODEL), mod, row_of_tile, w_ffn_in[l].astype(BF16), tm)
        x = _matmul_resid(act, w_ffn_out[l].astype(BF16), x, mod, 5, row_of_tile, tm)

    y = _final_norm(x, final_norm.reshape(1, D_MODEL), tm)
    y_sample = y[:n_s].reshape(bs, ss, D_MODEL)
    y_prompt = y[n_s:].reshape(bp, sp, D_MODEL)
    stacked = tuple(jnp.stack([ctx[k] for ctx in ctx_out], axis=1) for k in range(9))
    return (y_prompt, y_sample) + stacked
```

```python
import functools
import math

import jax
import jax.numpy as jnp
import numpy as np
from jax import lax
from jax.experimental import pallas as pl
from jax.experimental.pallas import tpu as pltpu

F32 = jnp.float32
BF16 = jnp.bfloat16

D_MODEL = 2048
DEPTH = 2
GRID_W = 64
ROPE_DIM = 64
ROPE_BASE = 10000.0
RMS_EPS = 1e-6
ML_HEADS = 4
ML_DH = 256
ML_W = ML_HEADS * ML_DH
MLA_HEADS = 8
MLA_NOPE = 128
MLA_ROPE = ROPE_DIM
MLA_V = 128
MLA_KV_RANK = 512
MLA_QK = MLA_NOPE + MLA_ROPE
S5_GROUP = 16
S5_GROUPS = 64
S5_W = S5_GROUPS * S5_GROUP
S5_P = 64
S5_STATE = S5_GROUPS * S5_P
DF_HEADS = 8
DF_DQK = ROPE_DIM
DF_DV = 2 * DF_DQK
DF_W = DF_HEADS * DF_DV
N_BRANCH = 4
BRANCH_W = 1024
FFN_HIDDEN = (8 * D_MODEL + 3 * 256 - 1) // (3 * 256) * 256

LANES = 128
SUBLANES = 8
VMEM_LIMIT_BYTES = 56 * 1024 * 1024

COL_ML_Q = 0
COL_ML_K = 1024
COL_ML_V = 2048
COL_ML_O = 3072
COL_S5_U = 4096
COL_DF_Q = 5120
COL_DF_K = 6144
COL_DF_V = 7168
COL_GATE = 8192
COL_MLA_QN = 16384
COL_MLA_QR = 17408
COL_MLA_CKV = 17920
COL_MLA_KR = 18432
COL_ML_IF = 18560
PROJ_TN = 512
PROJ_COLS = 18944

LOG2E = math.log2(math.e)

TM_DENSE = 1024
TM_SEQ = 512
ATTN_KC = 512
ML_CHUNK = 128
S5_CHUNK = 64
S5_SEQS = 4
S5_LANE_BLK = 512
S5_GROUPS_PER_TILE = 16
ATTN_TQ = 256


def _cparams(*sem):
    return pltpu.CompilerParams(dimension_semantics=sem, vmem_limit_bytes=VMEM_LIMIT_BYTES)


def _dot(a, b):
    return jnp.dot(a, b, preferred_element_type=F32)


def _dot_nt(a, b):
    return lax.dot_general(a, b, (((1,), (1,)), ((), ())), preferred_element_type=F32)


def _dot_exact(a, b):
    return jnp.dot(a, b, preferred_element_type=F32, precision=lax.Precision.HIGHEST)


def _rms(x):
    return x * lax.rsqrt(jnp.mean(x * x, axis=-1, keepdims=True) + RMS_EPS)


def _rope_slab(x, cos, sin):
    lane = lax.broadcasted_iota(jnp.int32, x.shape, 1)
    partner = jnp.where((lane % 32) < 16, pltpu.roll(x, LANES - 16, 1), pltpu.roll(x, 16, 1))
    return x * cos + partner * sin


def _ada_kernel(c_ref, w_ref, b_ref, o_ref):
    c = c_ref[...]
    s = c * jax.nn.sigmoid(c)
    o_ref[...] = _dot(s.astype(BF16), w_ref[...].astype(BF16)) + b_ref[...]


def _ada(cond, w_ada, b_ada):
    rows = cond.shape[0]
    tn = 1024
    return pl.pallas_call(
        _ada_kernel,
        grid=(DEPTH, 6 * D_MODEL // tn),
        in_specs=[pl.BlockSpec((rows, D_MODEL), lambda l, j: (0, 0)),
                  pl.BlockSpec((None, D_MODEL, tn), lambda l, j: (l, 0, j)),
                  pl.BlockSpec((None, 1, tn), lambda l, j: (l, 0, j))],
        out_specs=pl.BlockSpec((None, rows, tn), lambda l, j: (l, 0, j)),
        out_shape=jax.ShapeDtypeStruct((DEPTH, rows, 6 * D_MODEL), F32),
        compiler_params=_cparams("parallel", "parallel"),
        name="ada",
    )(cond, w_ada, b_ada.reshape(DEPTH, 1, 6 * D_MODEL))


def _mod_spec(which, row_of_tile):
    return pl.BlockSpec((None, None, 1, D_MODEL), lambda i, j: (row_of_tile(i), which, 0, 0))


def _norm_mod(x_ref, g_ref, sc_ref, sh_ref):
    return (_rms(x_ref[...]) * g_ref[...]) * (1.0 + sc_ref[...]) + sh_ref[...]


def _proj_in_kernel(x_ref, g_ref, sc_ref, sh_ref, w_ref, o_ref, h_ref):
    @pl.when(pl.program_id(1) == 0)
    def _():
        h_ref[...] = _norm_mod(x_ref, g_ref, sc_ref, sh_ref).astype(BF16)

    o_ref[...] = _dot(h_ref[...], w_ref[...])


def _proj_in(x, g, mod, row_of_tile, w, tm):
    n = x.shape[0]
    ncol = w.shape[1]
    return pl.pallas_call(
        _proj_in_kernel,
        grid=(n // tm, ncol // PROJ_TN),
        in_specs=[pl.BlockSpec((tm, D_MODEL), lambda i, j: (i, 0)),
                  pl.BlockSpec((1, D_MODEL), lambda i, j: (0, 0)),
                  _mod_spec(1, row_of_tile), _mod_spec(0, row_of_tile),
                  pl.BlockSpec((D_MODEL, PROJ_TN), lambda i, j: (0, j))],
        out_specs=pl.BlockSpec((tm, PROJ_TN), lambda i, j: (i, j)),
        out_shape=jax.ShapeDtypeStruct((n, ncol), F32),
        scratch_shapes=[pltpu.VMEM((tm, D_MODEL), BF16)],
        compiler_params=_cparams("parallel", "arbitrary"),
        name="proj_in",
    )(x, g, mod, mod, w)


def _ffn_in_kernel(x_ref, g_ref, sc_ref, sh_ref, wa_ref, wb_ref, o_ref, h_ref):
    @pl.when(pl.program_id(1) == 0)
    def _():
        h_ref[...] = _norm_mod(x_ref, g_ref, sc_ref, sh_ref).astype(BF16)

    h = h_ref[...]
    a = _dot(h, wa_ref[...])
    b = _dot(h, wb_ref[...])
    o_ref[...] = (a * jax.nn.sigmoid(a) * b).astype(BF16)


def _ffn_in(x, g, mod, row_of_tile, w, tm):
    n = x.shape[0]
    tn = 512
    nj = FFN_HIDDEN // tn
    return pl.pallas_call(
        _ffn_in_kernel,
        grid=(n // tm, nj),
        in_specs=[pl.BlockSpec((tm, D_MODEL), lambda i, j: (i, 0)),
                  pl.BlockSpec((1, D_MODEL), lambda i, j: (0, 0)),
                  _mod_spec(4, row_of_tile), _mod_spec(3, row_of_tile),
                  pl.BlockSpec((D_MODEL, tn), lambda i, j: (0, j)),
                  pl.BlockSpec((D_MODEL, tn), lambda i, j: (0, nj + j))],
        out_specs=pl.BlockSpec((tm, tn), lambda i, j: (i, j)),
        out_shape=jax.ShapeDtypeStruct((n, FFN_HIDDEN), BF16),
        scratch_shapes=[pltpu.VMEM((tm, D_MODEL), BF16)],
        compiler_params=_cparams("parallel", "arbitrary"),
        name="ffn_in",
    )(x, g, mod, mod, w, w)


def _resid_kernel(a_ref, w_ref, x_ref, gate_ref, o_ref):
    o_ref[...] = x_ref[...] + gate_ref[...] * _dot(a_ref[...], w_ref[...])


def _matmul_resid(a, w, x, mod, which, row_of_tile, tm):
    n, kdim = a.shape
    tn = 512
    return pl.pallas_call(
        _resid_kernel,
        grid=(n // tm, D_MODEL // tn),
        in_specs=[pl.BlockSpec((tm, kdim), lambda i, j: (i, 0)),
                  pl.BlockSpec((kdim, tn), lambda i, j: (0, j)),
                  pl.BlockSpec((tm, tn), lambda i, j: (i, j)),
                  pl.BlockSpec((None, None, 1, tn), lambda i, j: (row_of_tile(i), which, 0, j))],
        out_specs=pl.BlockSpec((tm, tn), lambda i, j: (i, j)),
        out_shape=jax.ShapeDtypeStruct((n, D_MODEL), F32),
        compiler_params=_cparams("parallel", "parallel"),
        name="matmul_resid",
    )(a, w, x, mod)


def _final_norm_kernel(x_ref, g_ref, o_ref):
    o_ref[...] = _rms(x_ref[...]) * g_ref[...]


def _final_norm(x, g, tm):
    n = x.shape[0]
    return pl.pallas_call(
        _final_norm_kernel,
        grid=(n // tm,),
        in_specs=[pl.BlockSpec((tm, D_MODEL), lambda i: (i, 0)),
                  pl.BlockSpec((1, D_MODEL), lambda i: (0, 0))],
        out_specs=pl.BlockSpec((tm, D_MODEL), lambda i: (i, 0)),
        out_shape=jax.ShapeDtypeStruct((n, D_MODEL), F32),
        compiler_params=_cparams("parallel"),
        name="final_norm",
    )(x, g)


def _mlstm_kernel(qf_ref, kf_ref, vf_ref, gf_ref, qb_ref, kb_ref, vb_ref, gb_ref, bias_ref,
                  c0_ref, n0_ref, m0_ref,
                  hf_ref, hb_ref, c_out_ref, n_out_ref, m_out_ref,
                  c_scr, n_scr, m_scr):
    ci = pl.program_id(1)
    t = qf_ref.shape[0]

    @pl.when(ci == 0)
    def _():
        c_scr[...] = c0_ref[...]
        n_scr[...] = n0_ref[...]
        m_scr[...] = m0_ref[...]

    row = lax.broadcasted_iota(jnp.int32, (t, t), 0)
    col = lax.broadcasted_iota(jnp.int32, (t, t), 1)
    refs = ((qf_ref, kf_ref, vf_ref, gf_ref, hf_ref), (qb_ref, kb_ref, vb_ref, gb_ref, hb_ref))
    for d in range(2):
        q_ref, k_ref, v_ref, g_ref, h_ref = refs[d]
        keep = (col <= row) if d == 0 else (col >= row)
        cum = jnp.where(keep, 1.0, 0.0).astype(F32)
        gates = g_ref[...] + bias_ref[...]
        csum = _dot_exact(cum, jax.nn.log_sigmoid(gates))
        gates_t = gates.T
        csum_t = csum.T
        last = t - 1 if d == 0 else 0
        for hd in range(ML_HEADS):
            r = d * ML_HEADS + hd
            i_col = d * 2 * ML_HEADS + hd
            f_col = i_col + ML_HEADS
            b_c = csum[:, f_col:f_col + 1]
            b_r = csum_t[f_col:f_col + 1, :]
            li_c = gates[:, i_col:i_col + 1]
            li_r = gates_t[i_col:i_col + 1, :]
            m_st = m_scr[r:r + 1, 0:1]
            c_st = c_scr[r]
            n_st = n_scr[r:r + 1, :]
            sl = slice(hd * ML_DH, (hd + 1) * ML_DH)
            q = q_ref[:, sl]
            k = k_ref[:, sl] * (ML_DH ** -0.5)
            v = v_ref[:, sl]
            qb16 = q.astype(BF16)
            kb16 = k.astype(BF16)

            log_d = jnp.where(keep, b_c - b_r + li_r, -jnp.inf)
            log_inter = b_c + m_st
            m_t = jnp.maximum(log_inter, jnp.max(log_d, axis=1, keepdims=True))
            w_d = jnp.exp(log_d - m_t)
            w_inter = jnp.exp(log_inter - m_t)
            s = _dot_nt(qb16, kb16) * w_d
            num = _dot(s.astype(BF16), v.astype(BF16)) + w_inter * _dot_nt(qb16, c_st.astype(BF16))
            den = jnp.sum(s, axis=1, keepdims=True) + w_inter * jnp.sum(q * n_st, axis=1, keepdims=True)
            h_ref[:, sl] = num / jnp.maximum(jnp.abs(den), jnp.exp(-m_t))

            b_last = b_c[last:last + 1, :]
            log_w = b_last - b_c + li_c
            m_new = jnp.maximum(b_last + m_st, jnp.max(log_w, axis=0, keepdims=True))
            w_s = jnp.exp(log_w - m_new)
            w_c = jnp.exp(b_last + m_st - m_new)
            vw_t = (v * w_s).T.astype(BF16)
            c_scr[r] = w_c * c_st + _dot(vw_t, kb16)
            n_scr[r:r + 1, :] = w_c * n_st + jnp.sum(k * w_s, axis=0, keepdims=True)
            m_scr[r:r + 1, :] = jnp.broadcast_to(m_new, (1, LANES))

    @pl.when(ci == pl.num_programs(1) - 1)
    def _():
        c_out_ref[...] = c_scr[...]
        n_out_ref[...] = n_scr[...]
        m_out_ref[...] = m_scr[...]


def _mlstm(proj, row0, bsz, seq, bias, c0, n0, m0):
    tc = min(ML_CHUNK, seq)
    nc = seq // tc
    blk0 = row0 // tc
    r8 = 2 * ML_HEADS

    def fwd(colblk):
        return lambda b, c: (blk0 + b * nc + c, colblk)

    def bwd(colblk):
        return lambda b, c: (blk0 + b * nc + nc - 1 - c, colblk)

    def seqspecs(mk):
        return [pl.BlockSpec((tc, ML_W), mk(COL_ML_Q // ML_W)),
                pl.BlockSpec((tc, ML_W), mk(COL_ML_K // ML_W)),
                pl.BlockSpec((tc, ML_W), mk(COL_ML_V // ML_W)),
                pl.BlockSpec((tc, LANES), mk(COL_ML_IF // LANES))]

    state_specs = [pl.BlockSpec((None, r8, ML_DH, ML_DH), lambda b, c: (b, 0, 0, 0)),
                   pl.BlockSpec((None, r8, ML_DH), lambda b, c: (b, 0, 0)),
                   pl.BlockSpec((None, r8, LANES), lambda b, c: (b, 0, 0))]
    return pl.pallas_call(
        _mlstm_kernel,
        grid=(bsz, nc),
        in_specs=seqspecs(fwd) + seqspecs(bwd) + [pl.BlockSpec((1, LANES), lambda b, c: (0, 0))] + state_specs,
        out_specs=[pl.BlockSpec((tc, ML_W), lambda b, c: (b * nc + c, 0)),
                   pl.BlockSpec((tc, ML_W), lambda b, c: (b * nc + nc - 1 - c, 0))] + state_specs,
        out_shape=[jax.ShapeDtypeStruct((bsz * seq, ML_W), F32),
                   jax.ShapeDtypeStruct((bsz * seq, ML_W), F32),
                   jax.ShapeDtypeStruct((bsz, r8, ML_DH, ML_DH), F32),
                   jax.ShapeDtypeStruct((bsz, r8, ML_DH), F32),
                   jax.ShapeDtypeStruct((bsz, r8, LANES), F32)],
        scratch_shapes=[pltpu.VMEM((r8, ML_DH, ML_DH), F32),
                        pltpu.VMEM((r8, ML_DH), F32),
                        pltpu.VMEM((r8, LANES), F32)],
        compiler_params=_cparams("parallel", "arbitrary"),
        name="mlstm",
    )(proj, proj, proj, proj, proj, proj, proj, proj, bias, c0, n0, m0)


def _ml_post_kernel(hf_ref, hb_ref, o_ref, g_ref, y_ref):
    h = hf_ref[...] + hb_ref[...]
    for hd in range(ML_HEADS):
        sl = slice(hd * ML_DH, (hd + 1) * ML_DH)
        y_ref[:, sl] = (_rms(h[:, sl]) * g_ref[:, sl] * jax.nn.sigmoid(o_ref[:, sl])).astype(BF16)


def _ml_post(hf, hb, proj, row0, g, tm):
    n = hf.shape[0]
    blk0 = row0 // tm
    return pl.pallas_call(
        _ml_post_kernel,
        grid=(n // tm,),
        in_specs=[pl.BlockSpec((tm, ML_W), lambda i: (i, 0)),
                  pl.BlockSpec((tm, ML_W), lambda i: (i, 0)),
                  pl.BlockSpec((tm, ML_W), lambda i: (blk0 + i, COL_ML_O // ML_W)),
                  pl.BlockSpec((1, ML_W), lambda i: (0, 0))],
        out_specs=pl.BlockSpec((tm, ML_W), lambda i: (i, 0)),
        out_shape=jax.ShapeDtypeStruct((n, ML_W), BF16),
        compiler_params=_cparams("parallel"),
        name="ml_post",
    )(hf, hb, proj, g)


def _mla_prep_kernel(*refs, norm, rope, with_q):
    it = iter(refs)
    ckv_ref, kr_ref, g_ref, wkvb_ref = next(it), next(it), next(it), next(it)
    if with_q:
        qn_ref, qr_ref = next(it), next(it)
    if rope:
        cos_ref, sin_ref = next(it), next(it)
    k_ref, v_ref = next(it), next(it)
    if with_q:
        q_ref = next(it)
    if norm:
        ckv_out_ref = next(it)

    ckv = ckv_ref[...]
    if norm:
        ckv = _rms(ckv) * g_ref[...]
        ckv_out_ref[...] = ckv
    kv = _dot(ckv.astype(BF16), wkvb_ref[...])
    kr = kr_ref[...]
    if rope:
        kr = _rope_slab(kr, cos_ref[...], sin_ref[...])
    kr16 = kr[:, :MLA_ROPE].astype(BF16)
    hw = MLA_NOPE + MLA_V
    for hd in range(MLA_HEADS):
        k_ref[hd, :, 0:MLA_NOPE] = kv[:, hd * hw:hd * hw + MLA_NOPE].astype(BF16)
        k_ref[hd, :, MLA_NOPE:MLA_QK] = kr16
        v_ref[hd] = kv[:, hd * hw + MLA_NOPE:(hd + 1) * hw].T.astype(BF16)
    if with_q:
        scale = MLA_QK ** -0.5 * LOG2E
        qn = qn_ref[...]
        for sb in range(MLA_HEADS * MLA_ROPE // LANES):
            qr = qr_ref[:, sb * LANES:(sb + 1) * LANES]
            if rope:
                qr = _rope_slab(qr, cos_ref[...], sin_ref[...])
            for half in range(LANES // MLA_ROPE):
                hd = sb * (LANES // MLA_ROPE) + half
                q_ref[hd, :, MLA_NOPE:MLA_QK] = (qr[:, half * MLA_ROPE:(half + 1) * MLA_ROPE] * scale).astype(BF16)
        for hd in range(MLA_HEADS):
            q_ref[hd, :, 0:MLA_NOPE] = (qn[:, hd * MLA_NOPE:(hd + 1) * MLA_NOPE] * scale).astype(BF16)


def _mla_prep(ckv_src, kr_src, ckv_col, kr_col, row0, n, g, wkvb, tm, q_src=None, cos=None, sin=None, norm=True):
    rope = cos is not None
    with_q = q_src is not None
    blk0 = row0 // tm
    ins = [ckv_src, kr_src, g, wkvb]
    in_specs = [pl.BlockSpec((tm, MLA_KV_RANK), lambda i: (blk0 + i, ckv_col)),
                pl.BlockSpec((tm, LANES), lambda i: (blk0 + i, kr_col)),
                pl.BlockSpec((1, MLA_KV_RANK), lambda i: (0, 0)),
                pl.BlockSpec(wkvb.shape, lambda i: (0, 0))]
    if with_q:
        ins += [q_src, q_src]
        in_specs += [pl.BlockSpec((tm, MLA_HEADS * MLA_NOPE), lambda i: (blk0 + i, COL_MLA_QN // (MLA_HEADS * MLA_NOPE))),
                     pl.BlockSpec((tm, MLA_HEADS * MLA_ROPE), lambda i: (blk0 + i, COL_MLA_QR // (MLA_HEADS * MLA_ROPE)))]
    if rope:
        nt = cos.shape[0] // tm
        ins += [cos, sin]
        in_specs += [pl.BlockSpec((tm, LANES), lambda i: (i % nt, 0))] * 2
    out_shape = [jax.ShapeDtypeStruct((MLA_HEADS, n, MLA_QK), BF16),
                 jax.ShapeDtypeStruct((MLA_HEADS, MLA_V, n), BF16)]
    out_specs = [pl.BlockSpec((MLA_HEADS, tm, MLA_QK), lambda i: (0, i, 0)),
                 pl.BlockSpec((MLA_HEADS, MLA_V, tm), lambda i: (0, 0, i))]
    if with_q:
        out_shape.append(jax.ShapeDtypeStruct((MLA_HEADS, n, MLA_QK), BF16))
        out_specs.append(pl.BlockSpec((MLA_HEADS, tm, MLA_QK), lambda i: (0, i, 0)))
    if norm:
        out_shape.append(jax.ShapeDtypeStruct((n, MLA_KV_RANK), F32))
        out_specs.append(pl.BlockSpec((tm, MLA_KV_RANK), lambda i: (i, 0)))
    return pl.pallas_call(
        functools.partial(_mla_prep_kernel, norm=norm, rope=rope, with_q=with_q),
        grid=(n // tm,),
        in_specs=in_specs, out_specs=out_specs, out_shape=out_shape,
        compiler_params=_cparams("parallel"),
        name="mla_prep",
    )(*ins)


class _OnlineSoftmax:
    def __init__(self, tq, dv):
        self.m = jnp.full((1, tq), -jnp.inf, F32)
        self.l = jnp.zeros((1, tq), F32)
        self.acc = jnp.zeros((dv, tq), F32)

    def update(self, s, v_t):
        m_new = jnp.maximum(self.m, jnp.max(s, axis=0, keepdims=True))
        alpha = jnp.exp2(self.m - m_new)
        p = jnp.exp2(s - m_new)
        self.l = alpha * self.l + jnp.sum(p, axis=0, keepdims=True)
        self.acc = alpha * self.acc + _dot(v_t, p.astype(BF16))
        self.m = m_new

    def result(self):
        return self.acc / self.l


def _key_chunks(n):
    kc = min(ATTN_KC, n)
    return [(c * kc, kc) for c in range(n // kc)]


def _mla_attn_kernel(*refs, cached):
    if cached:
        q_ref, k_ref, v_ref, kc_ref, vc_ref, o_ref = refs
        sources = ((k_ref, v_ref), (kc_ref, vc_ref))
    else:
        q_ref, k_ref, v_ref, o_ref = refs
        sources = ((k_ref, v_ref),)
    q = q_ref[...]
    sm = _OnlineSoftmax(q.shape[0], MLA_V)
    for kk_ref, vv_ref in sources:
        for c0, kc in _key_chunks(kk_ref.shape[0]):
            sm.update(_dot_nt(kk_ref[c0:c0 + kc, :], q), vv_ref[:, c0:c0 + kc])
    o_ref[...] = sm.result().T.astype(BF16)


def _mla_attn(q, k, v_t, bsz, seq, k_c=None, v_c_t=None):
    tq = min(ATTN_TQ, seq)
    nq = seq // tq
    cached = k_c is not None
    ins = [q, k, v_t]
    in_specs = [pl.BlockSpec((None, tq, MLA_QK), lambda h, b, i: (h, b * nq + i, 0)),
                pl.BlockSpec((None, seq, MLA_QK), lambda h, b, i: (h, b, 0)),
                pl.BlockSpec((None, MLA_V, seq), lambda h, b, i: (h, 0, b))]
    if cached:
        past = k_c.shape[1] // bsz
        ins += [k_c, v_c_t]
        in_specs += [pl.BlockSpec((None, past, MLA_QK), lambda h, b, i: (h, b, 0)),
                     pl.BlockSpec((None, MLA_V, past), lambda h, b, i: (h, 0, b))]
    return pl.pallas_call(
        functools.partial(_mla_attn_kernel, cached=cached),
        grid=(MLA_HEADS, bsz, nq),
        in_specs=in_specs,
        out_specs=pl.BlockSpec((tq, MLA_V), lambda h, b, i: (b * nq + i, h)),
        out_shape=jax.ShapeDtypeStruct((bsz * seq, MLA_HEADS * MLA_V), BF16),
        compiler_params=_cparams("parallel", "parallel", "parallel"),
        name="mla_attn",
    )(*ins)


def _rope_cast_kernel(*refs, rope, scale, transpose):
    if rope:
        x_ref, cos_ref, sin_ref, o_ref = refs
    else:
        x_ref, o_ref = refs
    for sb in range(x_ref.shape[1] // LANES):
        sl = slice(sb * LANES, (sb + 1) * LANES)
        x = x_ref[:, sl]
        if rope:
            x = _rope_slab(x, cos_ref[...], sin_ref[...])
        if transpose:
            o_ref[sl, :] = x.T.astype(BF16)
        else:
            o_ref[:, sl] = (x * scale).astype(BF16)


def _rope_cast(src, colblk, n, tm, scale=1.0, cos=None, sin=None, transpose=False):
    rope = cos is not None
    ins = [src]
    in_specs = [pl.BlockSpec((tm, DF_W), lambda i: (i, colblk))]
    if rope:
        nt = cos.shape[0] // tm
        ins += [cos, sin]
        in_specs += [pl.BlockSpec((tm, LANES), lambda i: (i % nt, 0))] * 2
    if transpose:
        out_spec, out_shape = pl.BlockSpec((DF_W, tm), lambda i: (0, i)), (DF_W, n)
    else:
        out_spec, out_shape = pl.BlockSpec((tm, DF_W), lambda i: (i, 0)), (n, DF_W)
    return pl.pallas_call(
        functools.partial(_rope_cast_kernel, rope=rope, scale=scale, transpose=transpose),
        grid=(n // tm,),
        in_specs=in_specs,
        out_specs=out_spec,
        out_shape=jax.ShapeDtypeStruct(out_shape, BF16),
        compiler_params=_cparams("parallel"),
        name="rope_cast",
    )(*ins)


def _diff_attn_kernel(*refs, lam_init, cached):
    if cached:
        q_ref, k_ref, v_ref, kc_ref, vc_ref, lam_ref, g_ref, o_ref = refs
        sources = ((k_ref, v_ref), (kc_ref, vc_ref))
    else:
        q_ref, k_ref, v_ref, lam_ref, g_ref, o_ref = refs
        sources = ((k_ref, v_ref),)
    lp = lam_ref[...]
    lam = (jnp.exp(jnp.sum(lp[0:1] * lp[1:2], axis=-1, keepdims=True))
           - jnp.exp(jnp.sum(lp[2:3] * lp[3:4], axis=-1, keepdims=True)) + lam_init)
    q = q_ref[...]
    lane = lax.broadcasted_iota(jnp.int32, q.shape, 1)
    zero = jnp.zeros_like(q)
    q1 = jnp.where(lane < DF_DQK, q, zero)
    q2 = jnp.where(lane >= DF_DQK, q, zero)
    sm1 = _OnlineSoftmax(q.shape[0], DF_DV)
    sm2 = _OnlineSoftmax(q.shape[0], DF_DV)
    for kk_ref, vv_ref in sources:
        for c0, kc in _key_chunks(kk_ref.shape[0]):
            k = kk_ref[c0:c0 + kc, :].astype(BF16)
            v_t = vv_ref[:, c0:c0 + kc]
            sm1.update(_dot_nt(k, q1), v_t)
            sm2.update(_dot_nt(k, q2), v_t)
    o = (sm1.result() - lam * sm2.result()).T
    o_ref[...] = (_rms(o) * g_ref[...] * (1.0 - lam_init)).astype(BF16)


def _diff_attn(q, k, v_t, lam_p, g, lam_init, bsz, seq, k_c=None, v_c_t=None):
    tq = min(ATTN_TQ, seq)
    nq = seq // tq
    cached = k_c is not None
    ins = [q, k, v_t]
    in_specs = [pl.BlockSpec((tq, DF_DV), lambda b, h, i: (b * nq + i, h)),
                pl.BlockSpec((seq, DF_DV), lambda b, h, i: (b, h)),
                pl.BlockSpec((DF_DV, seq), lambda b, h, i: (h, b))]
    if cached:
        past = k_c.shape[0] // bsz
        ins += [k_c, v_c_t]
        in_specs += [pl.BlockSpec((past, DF_DV), lambda b, h, i: (b, h)),
                     pl.BlockSpec((DF_DV, past), lambda b, h, i: (h, b))]
    ins += [lam_p, g]
    in_specs += [pl.BlockSpec((4, DF_DQK), lambda b, h, i: (0, 0)),
                 pl.BlockSpec((1, DF_DV), lambda b, h, i: (0, 0))]
    return pl.pallas_call(
        functools.partial(_diff_attn_kernel, lam_init=lam_init, cached=cached),
        grid=(bsz, DF_HEADS, nq),
        in_specs=in_specs,
        out_specs=pl.BlockSpec((tq, DF_DV), lambda b, h, i: (b * nq + i, h)),
        out_shape=jax.ShapeDtypeStruct((bsz * seq, DF_W), BF16),
        compiler_params=_cparams("parallel", "parallel", "parallel"),
        name="diff_attn",
    )(*ins)


def _s5_kernel(uf_ref, ub_ref, a_re_ref, a_im_ref, h0_re_ref, h0_im_ref,
               wb_re_ref, wb_im_ref, wc_re_ref, wc_im_ref,
               yf_ref, yb_ref, hr_out_ref, hi_out_ref,
               bu_re, bu_im, h_re, h_im):
    ci = pl.program_id(1)
    tc = uf_ref.shape[1]
    nseq = 2 * S5_SEQS
    gblk, sblk = wb_re_ref.shape[2:]
    ngb = S5_W // gblk

    @pl.when(ci == 0)
    def _():
        h_re[...] = h0_re_ref[...]
        h_im[...] = h0_im_ref[...]

    ri = lax.broadcasted_iota(jnp.int32, (tc, tc), 0)
    cj = lax.broadcasted_iota(jnp.int32, (tc, tc), 1)
    rev = jnp.where(ri + cj == tc - 1, 1.0, 0.0).astype(BF16)

    for d, u_ref in enumerate((uf_ref, ub_ref)):
        us = []
        for s in range(S5_SEQS):
            u = u_ref[s].astype(BF16)
            if d == 1:
                u = _dot(rev, u).astype(BF16)
            us.append(u)
        u_all = jnp.concatenate(us, axis=0)
        for gb in range(ngb):
            ug = u_all[:, gb * gblk:(gb + 1) * gblk]
            for w_ref, dst in ((wb_re_ref, bu_re), (wb_im_ref, bu_im)):
                bu = _dot(ug, w_ref[d, gb])
                for s in range(S5_SEQS):
                    for lk in range(sblk // LANES):
                        dst[gb * (sblk // LANES) + lk, pl.ds(d * S5_SEQS + s, tc, stride=nseq), :] = (
                            bu[s * tc:(s + 1) * tc, lk * LANES:(lk + 1) * LANES])

    nlk = S5_LANE_BLK // LANES
    for lb in range(S5_STATE // S5_LANE_BLK):
        lks = tuple(range(lb * nlk, (lb + 1) * nlk))
        ar = [a_re_ref[:, k * LANES:(k + 1) * LANES] for k in lks]
        ai = [a_im_ref[:, k * LANES:(k + 1) * LANES] for k in lks]

        def step(j, carry):
            r0 = pl.multiple_of(j * nseq, nseq)
            out = []
            for i, k in enumerate(lks):
                hr, hi = carry[2 * i], carry[2 * i + 1]
                nhr = ar[i] * hr - ai[i] * hi + bu_re[k, pl.ds(r0, nseq), :]
                nhi = ar[i] * hi + ai[i] * hr + bu_im[k, pl.ds(r0, nseq), :]
                bu_re[k, pl.ds(r0, nseq), :] = nhr
                bu_im[k, pl.ds(r0, nseq), :] = nhi
                out += [nhr, nhi]
            return tuple(out)

        init = []
        for k in lks:
            init += [h_re[:, k * LANES:(k + 1) * LANES], h_im[:, k * LANES:(k + 1) * LANES]]
        fin = lax.fori_loop(0, tc, step, tuple(init), unroll=8)
        for i, k in enumerate(lks):
            h_re[:, k * LANES:(k + 1) * LANES] = fin[2 * i]
            h_im[:, k * LANES:(k + 1) * LANES] = fin[2 * i + 1]

    def seq_states(src, s):
        parts = [src[k, pl.ds(s, tc, stride=nseq), :] for k in range(S5_STATE // LANES)]
        return jnp.concatenate(parts, axis=1).astype(BF16)

    for d, y_ref in enumerate((yf_ref, yb_ref)):
        hs_re, hs_im = [], []
        for s in range(S5_SEQS):
            hr = seq_states(bu_re, d * S5_SEQS + s)
            hi = seq_states(bu_im, d * S5_SEQS + s)
            if d == 1:
                hr = _dot(rev, hr).astype(BF16)
                hi = _dot(rev, hi).astype(BF16)
            hs_re.append(hr)
            hs_im.append(hi)
        hr_all = jnp.concatenate(hs_re, axis=0)
        hi_all = jnp.concatenate(hs_im, axis=0)
        for gb in range(ngb):
            ssl = slice(gb * sblk, (gb + 1) * sblk)
            y = _dot(hr_all[:, ssl], wc_re_ref[d, gb]) - _dot(hi_all[:, ssl], wc_im_ref[d, gb])
            for s in range(S5_SEQS):
                y_ref[s, :, gb * gblk:(gb + 1) * gblk] = y[s * tc:(s + 1) * tc]

    @pl.when(ci == pl.num_programs(1) - 1)
    def _():
        hr_out_ref[...] = h_re[...]
        hi_out_ref[...] = h_im[...]


def _s5(proj3, g0, bsz, seq, a_re, a_im, h0_re, h0_im, wb_re, wb_im, wc_re, wc_im):
    tc = min(S5_CHUNK, seq)
    nc = seq // tc
    ng = bsz // S5_SEQS
    nseq = 2 * S5_SEQS
    ucol = COL_S5_U // S5_W
    const4 = lambda g, c: (0, 0, 0, 0)
    wspec = lambda w: pl.BlockSpec(w.shape, const4, pipeline_mode=pl.Buffered(1))
    hspec = pl.BlockSpec((None, nseq, S5_STATE), lambda g, c: (g, 0, 0))
    return pl.pallas_call(
        _s5_kernel,
        grid=(ng, nc),
        in_specs=[pl.BlockSpec((S5_SEQS, tc, S5_W), lambda g, c: (g0 + g, c, ucol)),
                  pl.BlockSpec((S5_SEQS, tc, S5_W), lambda g, c: (g0 + g, nc - 1 - c, ucol)),
                  pl.BlockSpec((nseq, S5_STATE), lambda g, c: (0, 0)),
                  pl.BlockSpec((nseq, S5_STATE), lambda g, c: (0, 0)),
                  hspec, hspec, wspec(wb_re), wspec(wb_im), wspec(wc_re), wspec(wc_im)],
        out_specs=[pl.BlockSpec((S5_SEQS, tc, S5_W), lambda g, c: (g, c, 0)),
                   pl.BlockSpec((S5_SEQS, tc, S5_W), lambda g, c: (g, nc - 1 - c, 0)),
                   hspec, hspec],
        out_shape=[jax.ShapeDtypeStruct((bsz, seq, S5_W), F32),
                   jax.ShapeDtypeStruct((bsz, seq, S5_W), F32),
                   jax.ShapeDtypeStruct((ng, nseq, S5_STATE), F32),
                   jax.ShapeDtypeStruct((ng, nseq, S5_STATE), F32)],
        scratch_shapes=[pltpu.VMEM((S5_STATE // LANES, tc * nseq, LANES), F32),
                        pltpu.VMEM((S5_STATE // LANES, tc * nseq, LANES), F32),
                        pltpu.VMEM((nseq, S5_STATE), F32),
                        pltpu.VMEM((nseq, S5_STATE), F32)],
        compiler_params=_cparams("parallel", "arbitrary"),
        name="s5",
    )(proj3, proj3, a_re, a_im, h0_re, h0_im, wb_re, wb_im, wc_re, wc_im)


def _s5_post_kernel(yf_ref, yb_ref, u_ref, d_ref, w_ref, o_ref):
    y = (yf_ref[...] + yb_ref[...]) + d_ref[...] * u_ref[...]
    g = jax.nn.gelu(y)
    o_ref[...] = (g * jax.nn.sigmoid(_dot(g.astype(BF16), w_ref[...]))).astype(BF16)


def _s5_post(yf, yb, proj, row0, d, w, tm):
    n = yf.shape[0]
    blk0 = row0 // tm
    return pl.pallas_call(
        _s5_post_kernel,
        grid=(n // tm,),
        in_specs=[pl.BlockSpec((tm, S5_W), lambda i: (i, 0)),
                  pl.BlockSpec((tm, S5_W), lambda i: (i, 0)),
                  pl.BlockSpec((tm, S5_W), lambda i: (blk0 + i, COL_S5_U // S5_W)),
                  pl.BlockSpec((1, S5_W), lambda i: (0, 0)),
                  pl.BlockSpec((S5_W, S5_W), lambda i: (0, 0))],
        out_specs=pl.BlockSpec((tm, S5_W), lambda i: (i, 0)),
        out_shape=jax.ShapeDtypeStruct((n, S5_W), BF16),
        compiler_params=_cparams("parallel"),
        name="s5_post",
    )(yf, yb, proj, d, w)


def _s5_discretise(a_re, a_im, log_dt, b_re, b_im):
    lr = jnp.minimum(a_re, -1e-4)
    li = a_im
    dt = jnp.exp(log_dt)[..., None]
    mag = jnp.exp(dt * lr)
    ab_re, ab_im = mag * jnp.cos(dt * li), mag * jnp.sin(dt * li)
    den = lr * lr + li * li
    nr, ni = ab_re - 1.0, ab_im
    qr = (nr * lr + ni * li) / den
    qi = (ni * lr - nr * li) / den
    bb_re = qr[..., None] * b_re - qi[..., None] * b_im
    bb_im = qr[..., None] * b_im + qi[..., None] * b_re
    return ab_re, ab_im, bb_re, bb_im


def _block_diag(w):
    ngrp = S5_GROUPS_PER_TILE
    d, g, r, c = w.shape
    wg = w.reshape(d, g // ngrp, ngrp, r, c)
    eye = jnp.eye(ngrp, dtype=w.dtype)
    out = jnp.einsum('dbgrc,gh->dbgrhc', wg, eye)
    return out.reshape(d, g // ngrp, ngrp * r, ngrp * c)


def _merge_kernel(y0_ref, y1_ref, y2_ref, y3_ref, g0_ref, g1_ref, g2_ref, g3_ref, w_ref, o_ref):
    acc = None
    for b, (y_ref, g_ref) in enumerate(((y0_ref, g0_ref), (y1_ref, g1_ref), (y2_ref, g2_ref), (y3_ref, g3_ref))):
        term = jax.nn.sigmoid(g_ref[...]) * _dot(y_ref[...], w_ref[b])
        acc = term if acc is None else acc + term
    o_ref[...] = acc.astype(BF16)


def _merge(ys, proj, wb, tm):
    n = ys[0].shape[0]
    tn = 512
    gate0 = COL_GATE // tn
    per = D_MODEL // tn

    def gspec(b):
        return pl.BlockSpec((tm, tn), lambda i, j: (i, gate0 + b * per + j))

    return pl.pallas_call(
        _merge_kernel,
        grid=(n // tm, per),
        in_specs=[pl.BlockSpec((tm, BRANCH_W), lambda i, j: (i, 0))] * N_BRANCH
                 + [gspec(b) for b in range(N_BRANCH)]
                 + [pl.BlockSpec((N_BRANCH, BRANCH_W, tn), lambda i, j: (0, 0, j))],
        out_specs=pl.BlockSpec((tm, tn), lambda i, j: (i, j)),
        out_shape=jax.ShapeDtypeStruct((n, D_MODEL), BF16),
        compiler_params=_cparams("parallel", "parallel"),
        name="merge",
    )(*ys, proj, proj, proj, proj, wb)


def _rope_tables(n_tok):
    grid_rows = n_tok // GRID_W
    rows, cols = jnp.meshgrid(jnp.arange(grid_rows, dtype=F32), jnp.arange(GRID_W, dtype=F32), indexing='ij')
    quarter = ROPE_DIM // 4
    inv = ROPE_BASE ** (-jnp.arange(quarter, dtype=F32) / quarter)
    ang_r = rows.reshape(-1, 1) * inv
    ang_c = cols.reshape(-1, 1) * inv
    cos = jnp.concatenate([jnp.cos(ang_r)] * 2 + [jnp.cos(ang_c)] * 2, axis=-1)
    sin = jnp.concatenate([-jnp.sin(ang_r), jnp.sin(ang_r), -jnp.sin(ang_c), jnp.sin(ang_c)], axis=-1)
    return jnp.tile(cos, (1, LANES // ROPE_DIM)), jnp.tile(sin, (1, LANES // ROPE_DIM))


def _permute_w_in(w):
    sizes = (ML_W, ML_W, ML_W, ML_W, 4 * ML_HEADS, MLA_HEADS * MLA_QK, MLA_KV_RANK + MLA_ROPE, S5_W,
             DF_W, DF_W, DF_W, N_BRANCH * D_MODEL)
    splits = tuple(int(s) for s in np.cumsum(sizes)[:-1])
    (ml_q, ml_k, ml_v, ml_o, ml_if, mla_q, mla_kva, s5_u, df_q, df_k, df_v, gate) = jnp.split(w, splits, axis=1)
    mq = mla_q.reshape(D_MODEL, MLA_HEADS, MLA_QK)
    qn = mq[:, :, :MLA_NOPE].reshape(D_MODEL, MLA_HEADS * MLA_NOPE)
    qr = mq[:, :, MLA_NOPE:].reshape(D_MODEL, MLA_HEADS * MLA_ROPE)
    pad = lambda a, width: jnp.pad(a, ((0, 0), (0, width - a.shape[1])))
    cols = [ml_q, ml_k, ml_v, ml_o, s5_u, df_q, df_k, df_v, gate, qn, qr,
            mla_kva[:, :MLA_KV_RANK], pad(mla_kva[:, MLA_KV_RANK:], LANES), pad(ml_if, LANES)]
    out = jnp.concatenate(cols, axis=1)
    return pad(out, PROJ_COLS).astype(BF16)


def _seq_mixers(proj, row0, bsz, seq, lw, lam_init, state, cache, rope):
    n = bsz * seq
    tm = min(TM_SEQ, seq)
    ml_c0, ml_n0, ml_m0, s5_h0r, s5_h0i = state
    r8 = 2 * ML_HEADS

    m0 = jnp.broadcast_to(ml_m0.reshape(bsz, r8, 1), (bsz, r8, LANES))
    hf, hb, c_new, n_new, m_new = _mlstm(proj, row0, bsz, seq, lw['ml_bias'],
                                         ml_c0.reshape(bsz, r8, ML_DH, ML_DH), ml_n0.reshape(bsz, r8, ML_DH), m0)
    y_ml = _ml_post(hf, hb, proj, row0, lw['ml_norm'], tm)

    cos, sin = rope if rope is not None else (None, None)
    k_new, v_new, q_mla, ckv = _mla_prep(proj, proj, COL_MLA_CKV // MLA_KV_RANK, COL_MLA_KR // LANES, row0, n,
                                         lw['mla_kv_norm'], lw['mla_w_kvb'], tm, q_src=proj, cos=cos, sin=sin)
    k_c = v_c = None
    if cache is not None:
        ckv_c, krope_c, dk_c, dv_c = cache
        past = ckv_c.shape[1]
        kr_pad = jnp.pad(krope_c.reshape(bsz * past, MLA_ROPE), ((0, 0), (0, LANES - MLA_ROPE)))
        k_c, v_c = _mla_prep(ckv_c.reshape(bsz * past, MLA_KV_RANK), kr_pad, 0, 0, 0, bsz * past,
                             lw['mla_kv_norm'], lw['mla_w_kvb'], min(tm, bsz * past), norm=False)
    y_mla = _mla_attn(q_mla, k_new, v_new, bsz, seq, k_c, v_c)

    proj3 = proj.reshape(proj.shape[0] // seq, seq, PROJ_COLS)
    ng = bsz // S5_SEQS

    def pack_state(hs):
        return hs.reshape(ng, S5_SEQS, 2, S5_STATE).transpose(0, 2, 1, 3).reshape(ng, 2 * S5_SEQS, S5_STATE)

    def unpack_state(hs):
        return hs.reshape(ng, 2, S5_SEQS, S5_STATE).transpose(0, 2, 1, 3).reshape(bsz, 2, S5_GROUPS, S5_P)

    yf, yb, hr_new, hi_new = _s5(proj3, row0 // (seq * S5_SEQS), bsz, seq, lw['s5_a_re8'], lw['s5_a_im8'],
                                 pack_state(s5_h0r), pack_state(s5_h0i),
                                 lw['s5_wb_re'], lw['s5_wb_im'], lw['s5_wc_re'], lw['s5_wc_im'])
    y_s5 = _s5_post(yf.reshape(n, S5_W), yb.reshape(n, S5_W), proj, row0, lw['s5_d'], lw['s5_w_glu'], tm)

    dq = _rope_cast(proj, COL_DF_Q // DF_W, n, tm, scale=DF_DQK ** -0.5 * LOG2E, cos=cos, sin=sin)
    dk = _rope_cast(proj, COL_DF_K // DF_W, n, tm, cos=cos, sin=sin)
    dv_t = _rope_cast(proj, COL_DF_V // DF_W, n, tm, transpose=True)
    dk_cache = dv_cache_t = None
    if cache is not None:
        dk_cache = dk_c.reshape(bsz * past, DF_W)
        dv_cache_t = dv_c.reshape(bsz * past, DF_W).T.astype(BF16)
    y_df = _diff_attn(dq, dk, dv_t, lw['df_lambda'], lw['df_norm'], lam_init, bsz, seq, dk_cache, dv_cache_t)

    cols = lambda c0, w: lax.slice(proj, (row0, c0), (row0 + n, c0 + w))
    new_ctx = (ckv.reshape(bsz, seq, MLA_KV_RANK),
               cols(COL_MLA_KR, MLA_ROPE).reshape(bsz, seq, MLA_ROPE),
               cols(COL_DF_K, DF_W).reshape(bsz, seq, DF_HEADS, 2 * DF_DQK),
               cols(COL_DF_V, DF_W).reshape(bsz, seq, DF_HEADS, DF_DV),
               c_new.reshape(bsz, 2, ML_HEADS, ML_DH, ML_DH), n_new.reshape(bsz, 2, ML_HEADS, ML_DH),
               m_new[:, :, 0].reshape(bsz, 2, ML_HEADS), unpack_state(hr_new), unpack_state(hi_new))
    return (y_ml, y_mla, y_s5, y_df), new_ctx


def kernel(x_prompt, x_sample, cache_mla_ckv, cache_mla_krope, cache_diff_k, cache_diff_v,
           state_mlstm_c, state_mlstm_n, state_mlstm_m, state_s5_re, state_s5_im, c,
           c_ctx, w_ada, b_ada, norm_mix, norm_ffn, w_in, ml_if_bias, ml_norm, mla_kv_norm,
           mla_w_kvb, s5_a_re, s5_a_im, s5_log_dt, s5_b_re, s5_b_im, s5_c_re, s5_c_im, s5_d,
           s5_w_glu, df_lambda, df_norm, w_branch, w_o, w_ffn_in, w_ffn_out, final_norm):
    bp, sp, _ = x_prompt.shape
    bs, ss, _ = x_sample.shape
    n_p, n_s = bp * sp, bs * ss
    tm = TM_DENSE

    cond = jnp.concatenate([c, c_ctx[None, :], jnp.zeros((SUBLANES - 1 - bs, D_MODEL), F32)], axis=0)
    mods = _ada(cond, w_ada, b_ada).reshape(DEPTH, SUBLANES, 6, 1, D_MODEL)
    rope = _rope_tables(ss)
    latent_row = lambda i: (i * tm) // ss
    prompt_row = lambda i: bs

    xs = x_sample.reshape(n_s, D_MODEL)
    xp = x_prompt.reshape(n_p, D_MODEL)
    zeros_state = (jnp.zeros((bp, 2, ML_HEADS, ML_DH, ML_DH), F32), jnp.zeros((bp, 2, ML_HEADS, ML_DH), F32),
                   jnp.zeros((bp, 2, ML_HEADS), F32), jnp.zeros((bp, 2, S5_GROUPS, S5_P), F32),
                   jnp.zeros((bp, 2, S5_GROUPS, S5_P), F32))
    ctx_out = []
    for l in range(DEPTH):
        lam_init = 0.8 - 0.6 * math.exp(-0.3 * l)
        ab_re, ab_im, bb_re, bb_im = _s5_discretise(s5_a_re[l], s5_a_im[l], s5_log_dt[l], s5_b_re[l], s5_b_im[l])
        rep = lambda a: jnp.repeat(a.reshape(2, S5_STATE), S5_SEQS, axis=0)
        lw = {
            'ml_bias': jnp.pad(ml_if_bias[l].reshape(1, 4 * ML_HEADS), ((0, 0), (0, LANES - 4 * ML_HEADS))),
            'ml_norm': ml_norm[l].reshape(1, ML_W),
            'mla_kv_norm': mla_kv_norm[l].reshape(1, MLA_KV_RANK),
            'mla_w_kvb': mla_w_kvb[l].astype(BF16),
            's5_a_re8': rep(ab_re), 's5_a_im8': rep(ab_im),
            's5_wb_re': _block_diag(bb_re.transpose(0, 1, 3, 2)).astype(BF16),
            's5_wb_im': _block_diag(bb_im.transpose(0, 1, 3, 2)).astype(BF16),
            's5_wc_re': _block_diag(s5_c_re[l].transpose(0, 1, 3, 2)).astype(BF16),
            's5_wc_im': _block_diag(s5_c_im[l].transpose(0, 1, 3, 2)).astype(BF16),
            's5_d': s5_d[l].reshape(1, S5_W),
            's5_w_glu': s5_w_glu[l].astype(BF16),
            'df_lambda': df_lambda[l],
            'df_norm': df_norm[l].reshape(1, DF_DV),
        }
        mod = mods[l]
        w_in_l = _permute_w_in(w_in[l])
        w_branch_l, w_o_l = w_branch[l].astype(BF16), w_o[l].astype(BF16)
        w_ffn_in_l, w_ffn_out_l = w_ffn_in[l].astype(BF16), w_ffn_out[l].astype(BF16)
        g_mix, g_ffn = norm_mix[l].reshape(1, D_MODEL), norm_ffn[l].reshape(1, D_MODEL)

        def layer(x, row_of_tile, bsz, seq, state, cache, rope_tabs):
            proj = _proj_in(x, g_mix, mod, row_of_tile, w_in_l, tm)
            ys, new_ctx = _seq_mixers(proj, 0, bsz, seq, lw, lam_init, state, cache, rope_tabs)
            merged = _merge(list(ys), proj, w_branch_l, tm)
            x = _matmul_resid(merged, w_o_l, x, mod, 2, row_of_tile, tm)
            act = _ffn_in(x, g_ffn, mod, row_of_tile, w_ffn_in_l, tm)
            return _matmul_resid(act, w_ffn_out_l, x, mod, 5, row_of_tile, tm), new_ctx

        xp, new_ctx = layer(xp, prompt_row, bp, sp, zeros_state, None, None)
        ctx_out.append(new_ctx)
        state = (state_mlstm_c[:, l], state_mlstm_n[:, l], state_mlstm_m[:, l], state_s5_re[:, l], state_s5_im[:, l])
        cache = (cache_mla_ckv[:, l], cache_mla_krope[:, l], cache_diff_k[:, l], cache_diff_v[:, l])
        xs, _ = layer(xs, latent_row, bs, ss, state, cache, rope)

    g_fin = final_norm.reshape(1, D_MODEL)
    y_prompt = _final_norm(xp, g_fin, tm).reshape(bp, sp, D_MODEL)
    y_sample = _final_norm(xs, g_fin, tm).reshape(bs, ss, D_MODEL)
    stacked = tuple(jnp.stack([ctx[k] for ctx in ctx_out], axis=1) for k in range(9))
    return (y_prompt, y_sample) + stacked
```

```python
import functools
import math

import jax
import jax.numpy as jnp
import numpy as np
from jax import lax
from jax.experimental import pallas as pl
from jax.experimental.pallas import tpu as pltpu

F32 = jnp.float32
BF16 = jnp.bfloat16

D_MODEL = 2048
DEPTH = 2
GRID_W = 64
ROPE_DIM = 64
ROPE_BASE = 10000.0
RMS_EPS = 1e-6
ML_HEADS = 4
ML_DH = 256
ML_W = ML_HEADS * ML_DH
MLA_HEADS = 8
MLA_NOPE = 128
MLA_ROPE = ROPE_DIM
MLA_V = 128
MLA_KV_RANK = 512
MLA_QK = MLA_NOPE + MLA_ROPE
S5_GROUP = 16
S5_GROUPS = 64
S5_W = S5_GROUPS * S5_GROUP
S5_P = 64
S5_STATE = S5_GROUPS * S5_P
DF_HEADS = 8
DF_DQK = ROPE_DIM
DF_DV = 2 * DF_DQK
DF_W = DF_HEADS * DF_DV
N_BRANCH = 4
BRANCH_W = 1024
FFN_HIDDEN = (8 * D_MODEL + 3 * 256 - 1) // (3 * 256) * 256

LANES = 128
SUBLANES = 8
VMEM_LIMIT_BYTES = 56 * 1024 * 1024

COL_ML_Q = 0
COL_ML_K = 1024
COL_ML_V = 2048
COL_ML_O = 3072
COL_S5_U = 4096
COL_DF_Q = 5120
COL_DF_K = 6144
COL_DF_V = 7168
COL_GATE = 8192
COL_MLA_QN = 16384
COL_MLA_QR = 17408
COL_MLA_CKV = 17920
COL_MLA_KR = 18432
COL_ML_IF = 18560
PROJ_TN = 512
PROJ_COLS = 18944

LOG2E = math.log2(math.e)

TM_DENSE = 1024
TM_SEQ = 512
ATTN_STEP_Q = 1024
ML_CHUNK = 128
S5_CHUNK = 64
S5_SEQS = 4
S5_LANE_BLK = 512
S5_GROUPS_PER_TILE = 16
ATTN_TQ = 256


def _cparams(*sem):
    return pltpu.CompilerParams(dimension_semantics=sem, vmem_limit_bytes=VMEM_LIMIT_BYTES)


def _dot(a, b):
    return jnp.dot(a, b, preferred_element_type=F32)


def _dot_nt(a, b):
    return lax.dot_general(a, b, (((1,), (1,)), ((), ())), preferred_element_type=F32)


def _dot_exact(a, b):
    return jnp.dot(a, b, preferred_element_type=F32, precision=lax.Precision.HIGHEST)


def _rms(x):
    return x * lax.rsqrt(jnp.mean(x * x, axis=-1, keepdims=True) + RMS_EPS)


def _rope_slab(x, cos, sin):
    lane = lax.broadcasted_iota(jnp.int32, x.shape, 1)
    partner = jnp.where((lane % 32) < 16, pltpu.roll(x, LANES - 16, 1), pltpu.roll(x, 16, 1))
    return x * cos + partner * sin


def _ada_kernel(c_ref, w_ref, b_ref, o_ref):
    c = c_ref[...]
    s = c * jax.nn.sigmoid(c)
    o_ref[...] = _dot(s.astype(BF16), w_ref[...].astype(BF16)) + b_ref[...]


def _ada(cond, w_ada, b_ada):
    rows = cond.shape[0]
    tn = 1024
    return pl.pallas_call(
        _ada_kernel,
        grid=(DEPTH, 6 * D_MODEL // tn),
        in_specs=[pl.BlockSpec((rows, D_MODEL), lambda l, j: (0, 0)),
                  pl.BlockSpec((None, D_MODEL, tn), lambda l, j: (l, 0, j)),
                  pl.BlockSpec((None, 1, tn), lambda l, j: (l, 0, j))],
        out_specs=pl.BlockSpec((None, rows, tn), lambda l, j: (l, 0, j)),
        out_shape=jax.ShapeDtypeStruct((DEPTH, rows, 6 * D_MODEL), F32),
        compiler_params=_cparams("parallel", "parallel"),
        name="ada",
    )(cond, w_ada, b_ada.reshape(DEPTH, 1, 6 * D_MODEL))


def _mod_spec(which, row_of_tile):
    return pl.BlockSpec((None, None, 1, D_MODEL), lambda i, j: (row_of_tile(i), which, 0, 0))


def _norm_mod(x_ref, g_ref, sc_ref, sh_ref):
    return (_rms(x_ref[...]) * g_ref[...]) * (1.0 + sc_ref[...]) + sh_ref[...]


def _proj_in_kernel(x_ref, g_ref, sc_ref, sh_ref, w_ref, o_ref, h_ref):
    @pl.when(pl.program_id(1) == 0)
    def _():
        h_ref[...] = _norm_mod(x_ref, g_ref, sc_ref, sh_ref).astype(BF16)

    o_ref[...] = _dot(h_ref[...], w_ref[...])


def _proj_in(x, g, mod, row_of_tile, w, tm):
    n = x.shape[0]
    ncol = w.shape[1]
    return pl.pallas_call(
        _proj_in_kernel,
        grid=(n // tm, ncol // PROJ_TN),
        in_specs=[pl.BlockSpec((tm, D_MODEL), lambda i, j: (i, 0)),
                  pl.BlockSpec((1, D_MODEL), lambda i, j: (0, 0)),
                  _mod_spec(1, row_of_tile), _mod_spec(0, row_of_tile),
                  pl.BlockSpec((D_MODEL, PROJ_TN), lambda i, j: (0, j))],
        out_specs=pl.BlockSpec((tm, PROJ_TN), lambda i, j: (i, j)),
        out_shape=jax.ShapeDtypeStruct((n, ncol), F32),
        scratch_shapes=[pltpu.VMEM((tm, D_MODEL), BF16)],
        compiler_params=_cparams("parallel", "arbitrary"),
        name="proj_in",
    )(x, g, mod, mod, w)


def _ffn_in_kernel(x_ref, g_ref, sc_ref, sh_ref, wa_ref, wb_ref, o_ref, h_ref):
    @pl.when(pl.program_id(1) == 0)
    def _():
        h_ref[...] = _norm_mod(x_ref, g_ref, sc_ref, sh_ref).astype(BF16)

    h = h_ref[...]
    a = _dot(h, wa_ref[...])
    b = _dot(h, wb_ref[...])
    o_ref[...] = (a * jax.nn.sigmoid(a) * b).astype(BF16)


def _ffn_in(x, g, mod, row_of_tile, w, tm):
    n = x.shape[0]
    tn = 512
    nj = FFN_HIDDEN // tn
    return pl.pallas_call(
        _ffn_in_kernel,
        grid=(n // tm, nj),
        in_specs=[pl.BlockSpec((tm, D_MODEL), lambda i, j: (i, 0)),
                  pl.BlockSpec((1, D_MODEL), lambda i, j: (0, 0)),
                  _mod_spec(4, row_of_tile), _mod_spec(3, row_of_tile),
                  pl.BlockSpec((D_MODEL, tn), lambda i, j: (0, j)),
                  pl.BlockSpec((D_MODEL, tn), lambda i, j: (0, nj + j))],
        out_specs=pl.BlockSpec((tm, tn), lambda i, j: (i, j)),
        out_shape=jax.ShapeDtypeStruct((n, FFN_HIDDEN), BF16),
        scratch_shapes=[pltpu.VMEM((tm, D_MODEL), BF16)],
        compiler_params=_cparams("parallel", "arbitrary"),
        name="ffn_in",
    )(x, g, mod, mod, w, w)


def _resid_kernel(a_ref, w_ref, x_ref, gate_ref, o_ref):
    o_ref[...] = x_ref[...] + gate_ref[...] * _dot(a_ref[...], w_ref[...])


def _matmul_resid(a, w, x, mod, which, row_of_tile, tm):
    n, kdim = a.shape
    tn = 512
    return pl.pallas_call(
        _resid_kernel,
        grid=(n // tm, D_MODEL // tn),
        in_specs=[pl.BlockSpec((tm, kdim), lambda i, j: (i, 0)),
                  pl.BlockSpec((kdim, tn), lambda i, j: (0, j)),
                  pl.BlockSpec((tm, tn), lambda i, j: (i, j)),
                  pl.BlockSpec((None, None, 1, tn), lambda i, j: (row_of_tile(i), which, 0, j))],
        out_specs=pl.BlockSpec((tm, tn), lambda i, j: (i, j)),
        out_shape=jax.ShapeDtypeStruct((n, D_MODEL), F32),
        compiler_params=_cparams("parallel", "parallel"),
        name="matmul_resid",
    )(a, w, x, mod)


def _final_norm_kernel(x_ref, g_ref, o_ref):
    o_ref[...] = _rms(x_ref[...]) * g_ref[...]


def _final_norm(x, g, tm):
    n = x.shape[0]
    return pl.pallas_call(
        _final_norm_kernel,
        grid=(n // tm,),
        in_specs=[pl.BlockSpec((tm, D_MODEL), lambda i: (i, 0)),
                  pl.BlockSpec((1, D_MODEL), lambda i: (0, 0))],
        out_specs=pl.BlockSpec((tm, D_MODEL), lambda i: (i, 0)),
        out_shape=jax.ShapeDtypeStruct((n, D_MODEL), F32),
        compiler_params=_cparams("parallel"),
        name="final_norm",
    )(x, g)


def _mlstm_kernel(qf_ref, kf_ref, vf_ref, gf_ref, qb_ref, kb_ref, vb_ref, gb_ref, bias_ref,
                  c0_ref, n0_ref, m0_ref,
                  hf_ref, hb_ref, c_out_ref, n_out_ref, m_out_ref,
                  c_scr, n_scr, m_scr):
    ci = pl.program_id(1)
    t = qf_ref.shape[0]

    @pl.when(ci == 0)
    def _():
        c_scr[...] = c0_ref[...]
        n_scr[...] = n0_ref[...]
        m_scr[...] = m0_ref[...]

    row = lax.broadcasted_iota(jnp.int32, (t, t), 0)
    col = lax.broadcasted_iota(jnp.int32, (t, t), 1)
    refs = ((qf_ref, kf_ref, vf_ref, gf_ref, hf_ref), (qb_ref, kb_ref, vb_ref, gb_ref, hb_ref))
    for d in range(2):
        q_ref, k_ref, v_ref, g_ref, h_ref = refs[d]
        keep = (col <= row) if d == 0 else (col >= row)
        cum = jnp.where(keep, 1.0, 0.0).astype(F32)
        gates = g_ref[...] + bias_ref[...]
        csum = _dot_exact(cum, jax.nn.log_sigmoid(gates))
        gates_t = gates.T
        csum_t = csum.T
        last = t - 1 if d == 0 else 0
        for hd in range(ML_HEADS):
            r = d * ML_HEADS + hd
            i_col = d * 2 * ML_HEADS + hd
            f_col = i_col + ML_HEADS
            b_c = csum[:, f_col:f_col + 1]
            b_r = csum_t[f_col:f_col + 1, :]
            li_c = gates[:, i_col:i_col + 1]
            li_r = gates_t[i_col:i_col + 1, :]
            m_st = m_scr[r:r + 1, 0:1]
            c_st = c_scr[r]
            n_st = n_scr[r:r + 1, :]
            sl = slice(hd * ML_DH, (hd + 1) * ML_DH)
            q = q_ref[:, sl]
            k = k_ref[:, sl] * (ML_DH ** -0.5)
            v = v_ref[:, sl]
            qb16 = q.astype(BF16)
            kb16 = k.astype(BF16)

            log_d = jnp.where(keep, b_c - b_r + li_r, -jnp.inf)
            log_inter = b_c + m_st
            m_t = jnp.maximum(log_inter, jnp.max(log_d, axis=1, keepdims=True))
            w_d = jnp.exp(log_d - m_t)
            w_inter = jnp.exp(log_inter - m_t)
            s = _dot_nt(qb16, kb16) * w_d
            num = _dot(s.astype(BF16), v.astype(BF16)) + w_inter * _dot_nt(qb16, c_st.astype(BF16))
            den = jnp.sum(s, axis=1, keepdims=True) + w_inter * jnp.sum(q * n_st, axis=1, keepdims=True)
            h_ref[:, sl] = num / jnp.maximum(jnp.abs(den), jnp.exp(-m_t))

            b_last = b_c[last:last + 1, :]
            log_w = b_last - b_c + li_c
            m_new = jnp.maximum(b_last + m_st, jnp.max(log_w, axis=0, keepdims=True))
            w_s = jnp.exp(log_w - m_new)
            w_c = jnp.exp(b_last + m_st - m_new)
            vw_t = (v * w_s).T.astype(BF16)
            c_scr[r] = w_c * c_st + _dot(vw_t, kb16)
            n_scr[r:r + 1, :] = w_c * n_st + jnp.sum(k * w_s, axis=0, keepdims=True)
            m_scr[r:r + 1, :] = jnp.broadcast_to(m_new, (1, LANES))

    @pl.when(ci == pl.num_programs(1) - 1)
    def _():
        c_out_ref[...] = c_scr[...]
        n_out_ref[...] = n_scr[...]
        m_out_ref[...] = m_scr[...]


def _mlstm(proj, row0, bsz, seq, bias, c0, n0, m0):
    tc = min(ML_CHUNK, seq)
    nc = seq // tc
    blk0 = row0 // tc
    r8 = 2 * ML_HEADS

    def fwd(colblk):
        return lambda b, c: (blk0 + b * nc + c, colblk)

    def bwd(colblk):
        return lambda b, c: (blk0 + b * nc + nc - 1 - c, colblk)

    def seqspecs(mk):
        return [pl.BlockSpec((tc, ML_W), mk(COL_ML_Q // ML_W)),
                pl.BlockSpec((tc, ML_W), mk(COL_ML_K // ML_W)),
                pl.BlockSpec((tc, ML_W), mk(COL_ML_V // ML_W)),
                pl.BlockSpec((tc, LANES), mk(COL_ML_IF // LANES))]

    state_specs = [pl.BlockSpec((None, r8, ML_DH, ML_DH), lambda b, c: (b, 0, 0, 0)),
                   pl.BlockSpec((None, r8, ML_DH), lambda b, c: (b, 0, 0)),
                   pl.BlockSpec((None, r8, LANES), lambda b, c: (b, 0, 0))]
    return pl.pallas_call(
        _mlstm_kernel,
        grid=(bsz, nc),
        in_specs=seqspecs(fwd) + seqspecs(bwd) + [pl.BlockSpec((1, LANES), lambda b, c: (0, 0))] + state_specs,
        out_specs=[pl.BlockSpec((tc, ML_W), lambda b, c: (b * nc + c, 0)),
                   pl.BlockSpec((tc, ML_W), lambda b, c: (b * nc + nc - 1 - c, 0))] + state_specs,
        out_shape=[jax.ShapeDtypeStruct((bsz * seq, ML_W), F32),
                   jax.ShapeDtypeStruct((bsz * seq, ML_W), F32),
                   jax.ShapeDtypeStruct((bsz, r8, ML_DH, ML_DH), F32),
                   jax.ShapeDtypeStruct((bsz, r8, ML_DH), F32),
                   jax.ShapeDtypeStruct((bsz, r8, LANES), F32)],
        scratch_shapes=[pltpu.VMEM((r8, ML_DH, ML_DH), F32),
                        pltpu.VMEM((r8, ML_DH), F32),
                        pltpu.VMEM((r8, LANES), F32)],
        compiler_params=_cparams("parallel", "arbitrary"),
        name="mlstm",
    )(proj, proj, proj, proj, proj, proj, proj, proj, bias, c0, n0, m0)


def _ml_post_kernel(hf_ref, hb_ref, o_ref, g_ref, y_ref):
    h = hf_ref[...] + hb_ref[...]
    for hd in range(ML_HEADS):
        sl = slice(hd * ML_DH, (hd + 1) * ML_DH)
        y_ref[:, sl] = (_rms(h[:, sl]) * g_ref[:, sl] * jax.nn.sigmoid(o_ref[:, sl])).astype(BF16)


def _ml_post(hf, hb, proj, row0, g, tm):
    n = hf.shape[0]
    blk0 = row0 // tm
    return pl.pallas_call(
        _ml_post_kernel,
        grid=(n // tm,),
        in_specs=[pl.BlockSpec((tm, ML_W), lambda i: (i, 0)),
                  pl.BlockSpec((tm, ML_W), lambda i: (i, 0)),
                  pl.BlockSpec((tm, ML_W), lambda i: (blk0 + i, COL_ML_O // ML_W)),
                  pl.BlockSpec((1, ML_W), lambda i: (0, 0))],
        out_specs=pl.BlockSpec((tm, ML_W), lambda i: (i, 0)),
        out_shape=jax.ShapeDtypeStruct((n, ML_W), BF16),
        compiler_params=_cparams("parallel"),
        name="ml_post",
    )(hf, hb, proj, g)


def _mla_prep_kernel(*refs, norm, rope, with_q):
    it = iter(refs)
    ckv_ref, kr_ref, g_ref, wkvb_ref = next(it), next(it), next(it), next(it)
    if with_q:
        qn_ref, qr_ref = next(it), next(it)
    if rope:
        cos_ref, sin_ref = next(it), next(it)
    k_ref, v_ref = next(it), next(it)
    if with_q:
        q_ref = next(it)
    if norm:
        ckv_out_ref = next(it)

    ckv = ckv_ref[...]
    if norm:
        ckv = _rms(ckv) * g_ref[...]
        ckv_out_ref[...] = ckv
    kv = _dot(ckv.astype(BF16), wkvb_ref[...])
    kr = kr_ref[...]
    if rope:
        kr = _rope_slab(kr, cos_ref[...], sin_ref[...])
    kr16 = kr[:, :MLA_ROPE].astype(BF16)
    hw = MLA_NOPE + MLA_V
    for hd in range(MLA_HEADS):
        k_ref[hd, :, 0:MLA_NOPE] = kv[:, hd * hw:hd * hw + MLA_NOPE].astype(BF16)
        k_ref[hd, :, MLA_NOPE:MLA_QK] = kr16
        v_ref[hd] = kv[:, hd * hw + MLA_NOPE:(hd + 1) * hw].astype(BF16)
    if with_q:
        scale = MLA_QK ** -0.5 * LOG2E
        qn = qn_ref[...]
        for sb in range(MLA_HEADS * MLA_ROPE // LANES):
            qr = qr_ref[:, sb * LANES:(sb + 1) * LANES]
            if rope:
                qr = _rope_slab(qr, cos_ref[...], sin_ref[...])
            for half in range(LANES // MLA_ROPE):
                hd = sb * (LANES // MLA_ROPE) + half
                q_ref[hd, :, MLA_NOPE:MLA_QK] = (qr[:, half * MLA_ROPE:(half + 1) * MLA_ROPE] * scale).astype(BF16)
        for hd in range(MLA_HEADS):
            q_ref[hd, :, 0:MLA_NOPE] = (qn[:, hd * MLA_NOPE:(hd + 1) * MLA_NOPE] * scale).astype(BF16)


def _mla_prep(ckv_src, kr_src, ckv_col, kr_col, row0, n, g, wkvb, tm, q_src=None, cos=None, sin=None, norm=True):
    rope = cos is not None
    with_q = q_src is not None
    blk0 = row0 // tm
    ins = [ckv_src, kr_src, g, wkvb]
    in_specs = [pl.BlockSpec((tm, MLA_KV_RANK), lambda i: (blk0 + i, ckv_col)),
                pl.BlockSpec((tm, LANES), lambda i: (blk0 + i, kr_col)),
                pl.BlockSpec((1, MLA_KV_RANK), lambda i: (0, 0)),
                pl.BlockSpec(wkvb.shape, lambda i: (0, 0))]
    if with_q:
        ins += [q_src, q_src]
        in_specs += [pl.BlockSpec((tm, MLA_HEADS * MLA_NOPE), lambda i: (blk0 + i, COL_MLA_QN // (MLA_HEADS * MLA_NOPE))),
                     pl.BlockSpec((tm, MLA_HEADS * MLA_ROPE), lambda i: (blk0 + i, COL_MLA_QR // (MLA_HEADS * MLA_ROPE)))]
    if rope:
        nt = cos.shape[0] // tm
        ins += [cos, sin]
        in_specs += [pl.BlockSpec((tm, LANES), lambda i: (i % nt, 0))] * 2
    out_shape = [jax.ShapeDtypeStruct((MLA_HEADS, n, MLA_QK), BF16),
                 jax.ShapeDtypeStruct((MLA_HEADS, n, MLA_V), BF16)]
    out_specs = [pl.BlockSpec((MLA_HEADS, tm, MLA_QK), lambda i: (0, i, 0)),
                 pl.BlockSpec((MLA_HEADS, tm, MLA_V), lambda i: (0, i, 0))]
    if with_q:
        out_shape.append(jax.ShapeDtypeStruct((MLA_HEADS, n, MLA_QK), BF16))
        out_specs.append(pl.BlockSpec((MLA_HEADS, tm, MLA_QK), lambda i: (0, i, 0)))
    if norm:
        out_shape.append(jax.ShapeDtypeStruct((n, MLA_KV_RANK), F32))
        out_specs.append(pl.BlockSpec((tm, MLA_KV_RANK), lambda i: (i, 0)))
    return pl.pallas_call(
        functools.partial(_mla_prep_kernel, norm=norm, rope=rope, with_q=with_q),
        grid=(n // tm,),
        in_specs=in_specs, out_specs=out_specs, out_shape=out_shape,
        compiler_params=_cparams("parallel"),
        name="mla_prep",
    )(*ins)


def _softmax_parts(scores):
    m = functools.reduce(jnp.maximum, [jnp.max(s, axis=-1, keepdims=True) for s in scores])
    ps = [jnp.exp2(s - m) for s in scores]
    l = functools.reduce(jnp.add, [jnp.sum(p, axis=-1, keepdims=True) for p in ps])
    return ps, 1.0 / l


def _query_halves(q_ref):
    tq = min(ATTN_TQ, q_ref.shape[0])
    return [slice(i * tq, (i + 1) * tq) for i in range(q_ref.shape[0] // tq)]


def _mla_attn_kernel(*refs, cached):
    if cached:
        q_ref, k_ref, v_ref, kc_ref, vc_ref, o_ref = refs
        sources = ((k_ref, v_ref), (kc_ref, vc_ref))
    else:
        q_ref, k_ref, v_ref, o_ref = refs
        sources = ((k_ref, v_ref),)
    halves = _query_halves(q_ref)
    scores = [[_dot_nt(q_ref[rows, :], kk_ref[...]) for kk_ref, _ in sources] for rows in halves]
    for rows, sc in zip(halves, scores):
        ps, r = _softmax_parts(sc)
        o = functools.reduce(jnp.add, [_dot(p.astype(BF16), vv_ref[...]) for p, (_, vv_ref) in zip(ps, sources)])
        o_ref[rows, :] = (o * r).astype(BF16)


def _mla_attn(q, k, v, bsz, seq, k_c=None, v_c=None):
    tq = min(ATTN_STEP_Q, seq)
    nq = seq // tq
    cached = k_c is not None
    ins = [q, k, v]
    in_specs = [pl.BlockSpec((None, tq, MLA_QK), lambda h, b, i: (h, b * nq + i, 0)),
                pl.BlockSpec((None, seq, MLA_QK), lambda h, b, i: (h, b, 0)),
                pl.BlockSpec((None, seq, MLA_V), lambda h, b, i: (h, b, 0))]
    if cached:
        past = k_c.shape[1] // bsz
        ins += [k_c, v_c]
        in_specs += [pl.BlockSpec((None, past, MLA_QK), lambda h, b, i: (h, b, 0)),
                     pl.BlockSpec((None, past, MLA_V), lambda h, b, i: (h, b, 0))]
    return pl.pallas_call(
        functools.partial(_mla_attn_kernel, cached=cached),
        grid=(MLA_HEADS, bsz, nq),
        in_specs=in_specs,
        out_specs=pl.BlockSpec((tq, MLA_V), lambda h, b, i: (b * nq + i, h)),
        out_shape=jax.ShapeDtypeStruct((bsz * seq, MLA_HEADS * MLA_V), BF16),
        compiler_params=_cparams("parallel", "parallel", "parallel"),
        name="mla_attn",
    )(*ins)


def _rope_cast_kernel(*refs, rope, scale):
    if rope:
        x_ref, cos_ref, sin_ref, o_ref = refs
    else:
        x_ref, o_ref = refs
    for sb in range(x_ref.shape[1] // LANES):
        sl = slice(sb * LANES, (sb + 1) * LANES)
        x = x_ref[:, sl]
        if rope:
            x = _rope_slab(x, cos_ref[...], sin_ref[...])
        o_ref[:, sl] = (x * scale).astype(BF16)


def _rope_cast(src, colblk, n, tm, scale=1.0, cos=None, sin=None):
    rope = cos is not None
    ins = [src]
    in_specs = [pl.BlockSpec((tm, DF_W), lambda i: (i, colblk))]
    if rope:
        nt = cos.shape[0] // tm
        ins += [cos, sin]
        in_specs += [pl.BlockSpec((tm, LANES), lambda i: (i % nt, 0))] * 2
    return pl.pallas_call(
        functools.partial(_rope_cast_kernel, rope=rope, scale=scale),
        grid=(n // tm,),
        in_specs=in_specs,
        out_specs=pl.BlockSpec((tm, DF_W), lambda i: (i, 0)),
        out_shape=jax.ShapeDtypeStruct((n, DF_W), BF16),
        compiler_params=_cparams("parallel"),
        name="rope_cast",
    )(*ins)


def _diff_attn_kernel(*refs, lam_init, cached):
    if cached:
        q_ref, k_ref, v_ref, kc_ref, vc_ref, lam_ref, g_ref, o_ref = refs
        sources = ((k_ref, v_ref), (kc_ref, vc_ref))
    else:
        q_ref, k_ref, v_ref, lam_ref, g_ref, o_ref = refs
        sources = ((k_ref, v_ref),)
    lp = lam_ref[...]
    lam = (jnp.exp(jnp.sum(lp[0:1] * lp[1:2], axis=-1, keepdims=True))
           - jnp.exp(jnp.sum(lp[2:3] * lp[3:4], axis=-1, keepdims=True)) + lam_init)
    halves = _query_halves(q_ref)
    ks = [kk_ref[...].astype(BF16) for kk_ref, _ in sources]
    scores = []
    for rows in halves:
        q = q_ref[rows, :]
        lane = lax.broadcasted_iota(jnp.int32, q.shape, 1)
        zero = jnp.zeros_like(q)
        q1 = jnp.where(lane < DF_DQK, q, zero)
        q2 = jnp.where(lane >= DF_DQK, q, zero)
        scores.append(([_dot_nt(q1, k) for k in ks], [_dot_nt(q2, k) for k in ks]))
    for rows, (s1, s2) in zip(halves, scores):
        p1, r1 = _softmax_parts(s1)
        p2, r2 = _softmax_parts(s2)
        c = lam * r2 / r1
        o = functools.reduce(jnp.add, [_dot((a - c * b).astype(BF16), vv_ref[...].astype(BF16))
                                       for a, b, (_, vv_ref) in zip(p1, p2, sources)]) * r1
        o_ref[rows, :] = (_rms(o) * g_ref[...] * (1.0 - lam_init)).astype(BF16)


def _diff_attn(q, k, v, lam_p, g, lam_init, bsz, seq, k_c=None, v_c=None):
    tq = min(ATTN_STEP_Q, seq)
    nq = seq // tq
    cached = k_c is not None
    ins = [q, k, v]
    in_specs = [pl.BlockSpec((tq, DF_DV), lambda b, h, i: (b * nq + i, h)),
                pl.BlockSpec((seq, DF_DV), lambda b, h, i: (b, h)),
                pl.BlockSpec((seq, DF_DV), lambda b, h, i: (b, h))]
    if cached:
        past = k_c.shape[0] // bsz
        ins += [k_c, v_c]
        in_specs += [pl.BlockSpec((past, DF_DV), lambda b, h, i: (b, h)),
                     pl.BlockSpec((past, DF_DV), lambda b, h, i: (b, h))]
    ins += [lam_p, g]
    in_specs += [pl.BlockSpec((4, DF_DQK), lambda b, h, i: (0, 0)),
                 pl.BlockSpec((1, DF_DV), lambda b, h, i: (0, 0))]
    return pl.pallas_call(
        functools.partial(_diff_attn_kernel, lam_init=lam_init, cached=cached),
        grid=(bsz, DF_HEADS, nq),
        in_specs=in_specs,
        out_specs=pl.BlockSpec((tq, DF_DV), lambda b, h, i: (b * nq + i, h)),
        out_shape=jax.ShapeDtypeStruct((bsz * seq, DF_W), BF16),
        compiler_params=_cparams("parallel", "parallel", "parallel"),
        name="diff_attn",
    )(*ins)


def _s5_kernel(uf_ref, ub_ref, a_re_ref, a_im_ref, h0_re_ref, h0_im_ref,
               wb_re_ref, wb_im_ref, wc_re_ref, wc_im_ref,
               yf_ref, yb_ref, hr_out_ref, hi_out_ref,
               bu_re, bu_im, h_re, h_im):
    ci = pl.program_id(1)
    tc = uf_ref.shape[1]
    nseq = 2 * S5_SEQS
    gblk, sblk = wb_re_ref.shape[2:]
    ngb = S5_W // gblk

    @pl.when(ci == 0)
    def _():
        h_re[...] = h0_re_ref[...]
        h_im[...] = h0_im_ref[...]

    ri = lax.broadcasted_iota(jnp.int32, (tc, tc), 0)
    cj = lax.broadcasted_iota(jnp.int32, (tc, tc), 1)
    rev = jnp.where(ri + cj == tc - 1, 1.0, 0.0).astype(BF16)

    for d, u_ref in enumerate((uf_ref, ub_ref)):
        us = []
        for s in range(S5_SEQS):
            u = u_ref[s].astype(BF16)
            if d == 1:
                u = _dot(rev, u).astype(BF16)
            us.append(u)
        u_all = jnp.concatenate(us, axis=0)
        for gb in range(ngb):
            ug = u_all[:, gb * gblk:(gb + 1) * gblk]
            for w_ref, dst in ((wb_re_ref, bu_re), (wb_im_ref, bu_im)):
                bu = _dot(ug, w_ref[d, gb])
                for s in range(S5_SEQS):
                    for lk in range(sblk // LANES):
                        dst[gb * (sblk // LANES) + lk, pl.ds(d * S5_SEQS + s, tc, stride=nseq), :] = (
                            bu[s * tc:(s + 1) * tc, lk * LANES:(lk + 1) * LANES])

    nlk = S5_LANE_BLK // LANES
    for lb in range(S5_STATE // S5_LANE_BLK):
        lks = tuple(range(lb * nlk, (lb + 1) * nlk))
        ar = [a_re_ref[:, k * LANES:(k + 1) * LANES] for k in lks]
        ai = [a_im_ref[:, k * LANES:(k + 1) * LANES] for k in lks]

        def step(j, carry):
            r0 = pl.multiple_of(j * nseq, nseq)
            out = []
            for i, k in enumerate(lks):
                hr, hi = carry[2 * i], carry[2 * i + 1]
                nhr = ar[i] * hr - ai[i] * hi + bu_re[k, pl.ds(r0, nseq), :]
                nhi = ar[i] * hi + ai[i] * hr + bu_im[k, pl.ds(r0, nseq), :]
                bu_re[k, pl.ds(r0, nseq), :] = nhr
                bu_im[k, pl.ds(r0, nseq), :] = nhi
                out += [nhr, nhi]
            return tuple(out)

        init = []
        for k in lks:
            init += [h_re[:, k * LANES:(k + 1) * LANES], h_im[:, k * LANES:(k + 1) * LANES]]
        fin = lax.fori_loop(0, tc, step, tuple(init), unroll=8)
        for i, k in enumerate(lks):
            h_re[:, k * LANES:(k + 1) * LANES] = fin[2 * i]
            h_im[:, k * LANES:(k + 1) * LANES] = fin[2 * i + 1]

    def seq_states(src, s):
        parts = [src[k, pl.ds(s, tc, stride=nseq), :] for k in range(S5_STATE // LANES)]
        return jnp.concatenate(parts, axis=1).astype(BF16)

    for d, y_ref in enumerate((yf_ref, yb_ref)):
        hs_re, hs_im = [], []
        for s in range(S5_SEQS):
            hr = seq_states(bu_re, d * S5_SEQS + s)
            hi = seq_states(bu_im, d * S5_SEQS + s)
            if d == 1:
                hr = _dot(rev, hr).astype(BF16)
                hi = _dot(rev, hi).astype(BF16)
            hs_re.append(hr)
            hs_im.append(hi)
        hr_all = jnp.concatenate(hs_re, axis=0)
        hi_all = jnp.concatenate(hs_im, axis=0)
        for gb in range(ngb):
            ssl = slice(gb * sblk, (gb + 1) * sblk)
            y = _dot(hr_all[:, ssl], wc_re_ref[d, gb]) - _dot(hi_all[:, ssl], wc_im_ref[d, gb])
            for s in range(S5_SEQS):
                y_ref[s, :, gb * gblk:(gb + 1) * gblk] = y[s * tc:(s + 1) * tc]

    @pl.when(ci == pl.num_programs(1) - 1)
    def _():
        hr_out_ref[...] = h_re[...]
        hi_out_ref[...] = h_im[...]


def _s5(proj3, g0, bsz, seq, a_re, a_im, h0_re, h0_im, wb_re, wb_im, wc_re, wc_im):
    tc = min(S5_CHUNK, seq)
    nc = seq // tc
    ng = bsz // S5_SEQS
    nseq = 2 * S5_SEQS
    ucol = COL_S5_U // S5_W
    const4 = lambda g, c: (0, 0, 0, 0)
    wspec = lambda w: pl.BlockSpec(w.shape, const4, pipeline_mode=pl.Buffered(1))
    hspec = pl.BlockSpec((None, nseq, S5_STATE), lambda g, c: (g, 0, 0))
    return pl.pallas_call(
        _s5_kernel,
        grid=(ng, nc),
        in_specs=[pl.BlockSpec((S5_SEQS, tc, S5_W), lambda g, c: (g0 + g, c, ucol)),
                  pl.BlockSpec((S5_SEQS, tc, S5_W), lambda g, c: (g0 + g, nc - 1 - c, ucol)),
                  pl.BlockSpec((nseq, S5_STATE), lambda g, c: (0, 0)),
                  pl.BlockSpec((nseq, S5_STATE), lambda g, c: (0, 0)),
                  hspec, hspec, wspec(wb_re), wspec(wb_im), wspec(wc_re), wspec(wc_im)],
        out_specs=[pl.BlockSpec((S5_SEQS, tc, S5_W), lambda g, c: (g, c, 0)),
                   pl.BlockSpec((S5_SEQS, tc, S5_W), lambda g, c: (g, nc - 1 - c, 0)),
                   hspec, hspec],
        out_shape=[jax.ShapeDtypeStruct((bsz, seq, S5_W), F32),
                   jax.ShapeDtypeStruct((bsz, seq, S5_W), F32),
                   jax.ShapeDtypeStruct((ng, nseq, S5_STATE), F32),
                   jax.ShapeDtypeStruct((ng, nseq, S5_STATE), F32)],
        scratch_shapes=[pltpu.VMEM((S5_STATE // LANES, tc * nseq, LANES), F32),
                        pltpu.VMEM((S5_STATE // LANES, tc * nseq, LANES), F32),
                        pltpu.VMEM((nseq, S5_STATE), F32),
                        pltpu.VMEM((nseq, S5_STATE), F32)],
        compiler_params=_cparams("parallel", "arbitrary"),
        name="s5",
    )(proj3, proj3, a_re, a_im, h0_re, h0_im, wb_re, wb_im, wc_re, wc_im)


def _s5_post_kernel(yf_ref, yb_ref, u_ref, d_ref, w_ref, o_ref):
    y = (yf_ref[...] + yb_ref[...]) + d_ref[...] * u_ref[...]
    g = jax.nn.gelu(y)
    o_ref[...] = (g * jax.nn.sigmoid(_dot(g.astype(BF16), w_ref[...]))).astype(BF16)


def _s5_post(yf, yb, proj, row0, d, w, tm):
    n = yf.shape[0]
    blk0 = row0 // tm
    return pl.pallas_call(
        _s5_post_kernel,
        grid=(n // tm,),
        in_specs=[pl.BlockSpec((tm, S5_W), lambda i: (i, 0)),
                  pl.BlockSpec((tm, S5_W), lambda i: (i, 0)),
                  pl.BlockSpec((tm, S5_W), lambda i: (blk0 + i, COL_S5_U // S5_W)),
                  pl.BlockSpec((1, S5_W), lambda i: (0, 0)),
                  pl.BlockSpec((S5_W, S5_W), lambda i: (0, 0))],
        out_specs=pl.BlockSpec((tm, S5_W), lambda i: (i, 0)),
        out_shape=jax.ShapeDtypeStruct((n, S5_W), BF16),
        compiler_params=_cparams("parallel"),
        name="s5_post",
    )(yf, yb, proj, d, w)


def _s5_discretise(a_re, a_im, log_dt, b_re, b_im):
    lr = jnp.minimum(a_re, -1e-4)
    li = a_im
    dt = jnp.exp(log_dt)[..., None]
    mag = jnp.exp(dt * lr)
    ab_re, ab_im = mag * jnp.cos(dt * li), mag * jnp.sin(dt * li)
    den = lr * lr + li * li
    nr, ni = ab_re - 1.0, ab_im
    qr = (nr * lr + ni * li) / den
    qi = (ni * lr - nr * li) / den
    bb_re = qr[..., None] * b_re - qi[..., None] * b_im
    bb_im = qr[..., None] * b_im + qi[..., None] * b_re
    return ab_re, ab_im, bb_re, bb_im


def _block_diag(w):
    ngrp = S5_GROUPS_PER_TILE
    d, g, r, c = w.shape
    wg = w.reshape(d, g // ngrp, ngrp, r, c)
    eye = jnp.eye(ngrp, dtype=w.dtype)
    out = jnp.einsum('dbgrc,gh->dbgrhc', wg, eye)
    return out.reshape(d, g // ngrp, ngrp * r, ngrp * c)


def _merge_kernel(y0_ref, y1_ref, y2_ref, y3_ref, g0_ref, g1_ref, g2_ref, g3_ref, w_ref, o_ref):
    acc = None
    for b, (y_ref, g_ref) in enumerate(((y0_ref, g0_ref), (y1_ref, g1_ref), (y2_ref, g2_ref), (y3_ref, g3_ref))):
        term = jax.nn.sigmoid(g_ref[...]) * _dot(y_ref[...], w_ref[b])
        acc = term if acc is None else acc + term
    o_ref[...] = acc.astype(BF16)


def _merge(ys, proj, wb, tm):
    n = ys[0].shape[0]
    tn = 512
    gate0 = COL_GATE // tn
    per = D_MODEL // tn

    def gspec(b):
        return pl.BlockSpec((tm, tn), lambda i, j: (i, gate0 + b * per + j))

    return pl.pallas_call(
        _merge_kernel,
        grid=(n // tm, per),
        in_specs=[pl.BlockSpec((tm, BRANCH_W), lambda i, j: (i, 0))] * N_BRANCH
                 + [gspec(b) for b in range(N_BRANCH)]
                 + [pl.BlockSpec((N_BRANCH, BRANCH_W, tn), lambda i, j: (0, 0, j))],
        out_specs=pl.BlockSpec((tm, tn), lambda i, j: (i, j)),
        out_shape=jax.ShapeDtypeStruct((n, D_MODEL), BF16),
        compiler_params=_cparams("parallel", "parallel"),
        name="merge",
    )(*ys, proj, proj, proj, proj, wb)


def _rope_tables(n_tok):
    grid_rows = n_tok // GRID_W
    rows, cols = jnp.meshgrid(jnp.arange(grid_rows, dtype=F32), jnp.arange(GRID_W, dtype=F32), indexing='ij')
    quarter = ROPE_DIM // 4
    inv = ROPE_BASE ** (-jnp.arange(quarter, dtype=F32) / quarter)
    ang_r = rows.reshape(-1, 1) * inv
    ang_c = cols.reshape(-1, 1) * inv
    cos = jnp.concatenate([jnp.cos(ang_r)] * 2 + [jnp.cos(ang_c)] * 2, axis=-1)
    sin = jnp.concatenate([-jnp.sin(ang_r), jnp.sin(ang_r), -jnp.sin(ang_c), jnp.sin(ang_c)], axis=-1)
    return jnp.tile(cos, (1, LANES // ROPE_DIM)), jnp.tile(sin, (1, LANES // ROPE_DIM))


def _permute_w_in(w):
    sizes = (ML_W, ML_W, ML_W, ML_W, 4 * ML_HEADS, MLA_HEADS * MLA_QK, MLA_KV_RANK + MLA_ROPE, S5_W,
             DF_W, DF_W, DF_W, N_BRANCH * D_MODEL)
    splits = tuple(int(s) for s in np.cumsum(sizes)[:-1])
    (ml_q, ml_k, ml_v, ml_o, ml_if, mla_q, mla_kva, s5_u, df_q, df_k, df_v, gate) = jnp.split(w, splits, axis=1)
    mq = mla_q.reshape(D_MODEL, MLA_HEADS, MLA_QK)
    qn = mq[:, :, :MLA_NOPE].reshape(D_MODEL, MLA_HEADS * MLA_NOPE)
    qr = mq[:, :, MLA_NOPE:].reshape(D_MODEL, MLA_HEADS * MLA_ROPE)
    pad = lambda a, width: jnp.pad(a, ((0, 0), (0, width - a.shape[1])))
    cols = [ml_q, ml_k, ml_v, ml_o, s5_u, df_q, df_k, df_v, gate, qn, qr,
            mla_kva[:, :MLA_KV_RANK], pad(mla_kva[:, MLA_KV_RANK:], LANES), pad(ml_if, LANES)]
    out = jnp.concatenate(cols, axis=1)
    return pad(out, PROJ_COLS).astype(BF16)


def _seq_mixers(proj, row0, bsz, seq, lw, lam_init, state, cache, rope):
    n = bsz * seq
    tm = min(TM_SEQ, seq)
    ml_c0, ml_n0, ml_m0, s5_h0r, s5_h0i = state
    r8 = 2 * ML_HEADS

    m0 = jnp.broadcast_to(ml_m0.reshape(bsz, r8, 1), (bsz, r8, LANES))
    hf, hb, c_new, n_new, m_new = _mlstm(proj, row0, bsz, seq, lw['ml_bias'],
                                         ml_c0.reshape(bsz, r8, ML_DH, ML_DH), ml_n0.reshape(bsz, r8, ML_DH), m0)
    y_ml = _ml_post(hf, hb, proj, row0, lw['ml_norm'], tm)

    cos, sin = rope if rope is not None else (None, None)
    k_new, v_new, q_mla, ckv = _mla_prep(proj, proj, COL_MLA_CKV // MLA_KV_RANK, COL_MLA_KR // LANES, row0, n,
                                         lw['mla_kv_norm'], lw['mla_w_kvb'], tm, q_src=proj, cos=cos, sin=sin)
    k_c = v_c = None
    if cache is not None:
        ckv_c, krope_c, dk_c, dv_c = cache
        past = ckv_c.shape[1]
        kr_pad = jnp.pad(krope_c.reshape(bsz * past, MLA_ROPE), ((0, 0), (0, LANES - MLA_ROPE)))
        k_c, v_c = _mla_prep(ckv_c.reshape(bsz * past, MLA_KV_RANK), kr_pad, 0, 0, 0, bsz * past,
                             lw['mla_kv_norm'], lw['mla_w_kvb'], min(tm, bsz * past), norm=False)
    y_mla = _mla_attn(q_mla, k_new, v_new, bsz, seq, k_c, v_c)

    proj3 = proj.reshape(proj.shape[0] // seq, seq, PROJ_COLS)
    ng = bsz // S5_SEQS

    def pack_state(hs):
        return hs.reshape(ng, S5_SEQS, 2, S5_STATE).transpose(0, 2, 1, 3).reshape(ng, 2 * S5_SEQS, S5_STATE)

    def unpack_state(hs):
        return hs.reshape(ng, 2, S5_SEQS, S5_STATE).transpose(0, 2, 1, 3).reshape(bsz, 2, S5_GROUPS, S5_P)

    yf, yb, hr_new, hi_new = _s5(proj3, row0 // (seq * S5_SEQS), bsz, seq, lw['s5_a_re8'], lw['s5_a_im8'],
                                 pack_state(s5_h0r), pack_state(s5_h0i),
                                 lw['s5_wb_re'], lw['s5_wb_im'], lw['s5_wc_re'], lw['s5_wc_im'])
    y_s5 = _s5_post(yf.reshape(n, S5_W), yb.reshape(n, S5_W), proj, row0, lw['s5_d'], lw['s5_w_glu'], tm)

    dq = _rope_cast(proj, COL_DF_Q // DF_W, n, tm, scale=DF_DQK ** -0.5 * LOG2E, cos=cos, sin=sin)
    dk = _rope_cast(proj, COL_DF_K // DF_W, n, tm, cos=cos, sin=sin)
    dv = _rope_cast(proj, COL_DF_V // DF_W, n, tm)
    dk_cache = dv_cache = None
    if cache is not None:
        dk_cache = dk_c.reshape(bsz * past, DF_W)
        dv_cache = dv_c.reshape(bsz * past, DF_W)
    y_df = _diff_attn(dq, dk, dv, lw['df_lambda'], lw['df_norm'], lam_init, bsz, seq, dk_cache, dv_cache)

    cols = lambda c0, w: lax.slice(proj, (row0, c0), (row0 + n, c0 + w))
    new_ctx = (ckv.reshape(bsz, seq, MLA_KV_RANK),
               cols(COL_MLA_KR, MLA_ROPE).reshape(bsz, seq, MLA_ROPE),
               cols(COL_DF_K, DF_W).reshape(bsz, seq, DF_HEADS, 2 * DF_DQK),
               cols(COL_DF_V, DF_W).reshape(bsz, seq, DF_HEADS, DF_DV),
               c_new.reshape(bsz, 2, ML_HEADS, ML_DH, ML_DH), n_new.reshape(bsz, 2, ML_HEADS, ML_DH),
               m_new[:, :, 0].reshape(bsz, 2, ML_HEADS), unpack_state(hr_new), unpack_state(hi_new))
    return (y_ml, y_mla, y_s5, y_df), new_ctx


def kernel(x_prompt, x_sample, cache_mla_ckv, cache_mla_krope, cache_diff_k, cache_diff_v,
           state_mlstm_c, state_mlstm_n, state_mlstm_m, state_s5_re, state_s5_im, c,
           c_ctx, w_ada, b_ada, norm_mix, norm_ffn, w_in, ml_if_bias, ml_norm, mla_kv_norm,
           mla_w_kvb, s5_a_re, s5_a_im, s5_log_dt, s5_b_re, s5_b_im, s5_c_re, s5_c_im, s5_d,
           s5_w_glu, df_lambda, df_norm, w_branch, w_o, w_ffn_in, w_ffn_out, final_norm):
    bp, sp, _ = x_prompt.shape
    bs, ss, _ = x_sample.shape
    n_p, n_s = bp * sp, bs * ss
    tm = TM_DENSE

    cond = jnp.concatenate([c, c_ctx[None, :], jnp.zeros((SUBLANES - 1 - bs, D_MODEL), F32)], axis=0)
    mods = _ada(cond, w_ada, b_ada).reshape(DEPTH, SUBLANES, 6, 1, D_MODEL)
    rope = _rope_tables(ss)
    latent_row = lambda i: (i * tm) // ss
    prompt_row = lambda i: bs

    xs = x_sample.reshape(n_s, D_MODEL)
    xp = x_prompt.reshape(n_p, D_MODEL)
    zeros_state = (jnp.zeros((bp, 2, ML_HEADS, ML_DH, ML_DH), F32), jnp.zeros((bp, 2, ML_HEADS, ML_DH), F32),
                   jnp.zeros((bp, 2, ML_HEADS), F32), jnp.zeros((bp, 2, S5_GROUPS, S5_P), F32),
                   jnp.zeros((bp, 2, S5_GROUPS, S5_P), F32))
    ctx_out = []
    for l in range(DEPTH):
        lam_init = 0.8 - 0.6 * math.exp(-0.3 * l)
        ab_re, ab_im, bb_re, bb_im = _s5_discretise(s5_a_re[l], s5_a_im[l], s5_log_dt[l], s5_b_re[l], s5_b_im[l])
        rep = lambda a: jnp.repeat(a.reshape(2, S5_STATE), S5_SEQS, axis=0)
        lw = {
            'ml_bias': jnp.pad(ml_if_bias[l].reshape(1, 4 * ML_HEADS), ((0, 0), (0, LANES - 4 * ML_HEADS))),
            'ml_norm': ml_norm[l].reshape(1, ML_W),
            'mla_kv_norm': mla_kv_norm[l].reshape(1, MLA_KV_RANK),
            'mla_w_kvb': mla_w_kvb[l].astype(BF16),
            's5_a_re8': rep(ab_re), 's5_a_im8': rep(ab_im),
            's5_wb_re': _block_diag(bb_re.transpose(0, 1, 3, 2)).astype(BF16),
            's5_wb_im': _block_diag(bb_im.transpose(0, 1, 3, 2)).astype(BF16),
            's5_wc_re': _block_diag(s5_c_re[l].transpose(0, 1, 3, 2)).astype(BF16),
            's5_wc_im': _block_diag(s5_c_im[l].transpose(0, 1, 3, 2)).astype(BF16),
            's5_d': s5_d[l].reshape(1, S5_W),
            's5_w_glu': s5_w_glu[l].astype(BF16),
            'df_lambda': df_lambda[l],
            'df_norm': df_norm[l].reshape(1, DF_DV),
        }
        mod = mods[l]
        w_in_l = _permute_w_in(w_in[l])
        w_branch_l, w_o_l = w_branch[l].astype(BF16), w_o[l].astype(BF16)
        w_ffn_in_l, w_ffn_out_l = w_ffn_in[l].astype(BF16), w_ffn_out[l].astype(BF16)
        g_mix, g_ffn = norm_mix[l].reshape(1, D_MODEL), norm_ffn[l].reshape(1, D_MODEL)

        def layer(x, row_of_tile, bsz, seq, state, cache, rope_tabs):
            proj = _proj_in(x, g_mix, mod, row_of_tile, w_in_l, tm)
            ys, new_ctx = _seq_mixers(proj, 0, bsz, seq, lw, lam_init, state, cache, rope_tabs)
            merged = _merge(list(ys), proj, w_branch_l, tm)
            x = _matmul_resid(merged, w_o_l, x, mod, 2, row_of_tile, tm)
            act = _ffn_in(x, g_ffn, mod, row_of_tile, w_ffn_in_l, tm)
            return _matmul_resid(act, w_ffn_out_l, x, mod, 5, row_of_tile, tm), new_ctx

        xp, new_ctx = layer(xp, prompt_row, bp, sp, zeros_state, None, None)
        ctx_out.append(new_ctx)
        state = (state_mlstm_c[:, l], state_mlstm_n[:, l], state_mlstm_m[:, l], state_s5_re[:, l], state_s5_im[:, l])
        cache = (cache_mla_ckv[:, l], cache_mla_krope[:, l], cache_diff_k[:, l], cache_diff_v[:, l])
        xs, _ = layer(xs, latent_row, bs, ss, state, cache, rope)

    g_fin = final_norm.reshape(1, D_MODEL)
    y_prompt = _final_norm(xp, g_fin, tm).reshape(bp, sp, D_MODEL)
    y_sample = _final_norm(xs, g_fin, tm).reshape(bs, ss, D_MODEL)
    stacked = tuple(jnp.stack([ctx[k] for ctx in ctx_out], axis=1) for k in range(9))
    return (y_prompt, y_sample) + stacked
```

```python
import functools
import math

import jax
import jax.numpy as jnp
import numpy as np
from jax import lax
from jax.experimental import pallas as pl
from jax.experimental.pallas import tpu as pltpu

F32 = jnp.float32
BF16 = jnp.bfloat16

D_MODEL = 2048
DEPTH = 2
GRID_W = 64
ROPE_DIM = 64
ROPE_BASE = 10000.0
RMS_EPS = 1e-6
ML_HEADS = 4
ML_DH = 256
ML_W = ML_HEADS * ML_DH
MLA_HEADS = 8
MLA_NOPE = 128
MLA_ROPE = ROPE_DIM
MLA_V = 128
MLA_KV_RANK = 512
MLA_QK = MLA_NOPE + MLA_ROPE
S5_GROUP = 16
S5_GROUPS = 64
S5_W = S5_GROUPS * S5_GROUP
S5_P = 64
S5_STATE = S5_GROUPS * S5_P
DF_HEADS = 8
DF_DQK = ROPE_DIM
DF_DV = 2 * DF_DQK
DF_W = DF_HEADS * DF_DV
N_BRANCH = 4
BRANCH_W = 1024
FFN_HIDDEN = (8 * D_MODEL + 3 * 256 - 1) // (3 * 256) * 256

LANES = 128
SUBLANES = 8
VMEM_LIMIT_BYTES = 56 * 1024 * 1024

COL_ML_Q = 0
COL_ML_K = 1024
COL_ML_V = 2048
COL_ML_O = 3072
COL_S5_U = 4096
COL_DF_Q = 5120
COL_DF_K = 6144
COL_DF_V = 7168
COL_GATE = 8192
COL_MLA_QN = 16384
COL_MLA_QR = 17408
COL_MLA_CKV = 17920
COL_MLA_KR = 18432
COL_ML_IF = 18560
PROJ_TN = 1024
PROJ_COLS = 19456

LOG2E = math.log2(math.e)

TM_DENSE = 1024
TM_SEQ = 512
ATTN_STEP_Q = 1024
ML_CHUNK = 256
S5_CHUNK = 64
S5_SEQS = 4
S5_LANE_BLK = 512
S5_GROUPS_PER_TILE = 16
ATTN_TQ = 256


def _cparams(*sem):
    return pltpu.CompilerParams(dimension_semantics=sem, vmem_limit_bytes=VMEM_LIMIT_BYTES)


def _dot(a, b):
    return jnp.dot(a, b, preferred_element_type=F32)


def _dot_nt(a, b):
    return lax.dot_general(a, b, (((1,), (1,)), ((), ())), preferred_element_type=F32)


def _dot_exact(a, b):
    return jnp.dot(a, b, preferred_element_type=F32, precision=lax.Precision.HIGHEST)


def _rms(x):
    return x * lax.rsqrt(jnp.mean(x * x, axis=-1, keepdims=True) + RMS_EPS)


def _rope_slab(x, cos, sin):
    lane = lax.broadcasted_iota(jnp.int32, x.shape, 1)
    partner = jnp.where((lane % 32) < 16, pltpu.roll(x, LANES - 16, 1), pltpu.roll(x, 16, 1))
    return x * cos + partner * sin


def _ada_kernel(c_ref, w_ref, b_ref, o_ref):
    c = c_ref[...]
    s = c * jax.nn.sigmoid(c)
    o_ref[...] = _dot(s.astype(BF16), w_ref[...].astype(BF16)) + b_ref[...]


def _ada(cond, w_ada, b_ada):
    rows = cond.shape[0]
    tn = 1024
    return pl.pallas_call(
        _ada_kernel,
        grid=(DEPTH, 6 * D_MODEL // tn),
        in_specs=[pl.BlockSpec((rows, D_MODEL), lambda l, j: (0, 0)),
                  pl.BlockSpec((None, D_MODEL, tn), lambda l, j: (l, 0, j)),
                  pl.BlockSpec((None, 1, tn), lambda l, j: (l, 0, j))],
        out_specs=pl.BlockSpec((None, rows, tn), lambda l, j: (l, 0, j)),
        out_shape=jax.ShapeDtypeStruct((DEPTH, rows, 6 * D_MODEL), F32),
        compiler_params=_cparams("parallel", "parallel"),
        name="ada",
    )(cond, w_ada, b_ada.reshape(DEPTH, 1, 6 * D_MODEL))


def _mod_spec(which, row_of_tile):
    return pl.BlockSpec((None, None, 1, D_MODEL), lambda i, j: (row_of_tile(i), which, 0, 0))


def _norm_mod(x_ref, g_ref, sc_ref, sh_ref):
    return (_rms(x_ref[...]) * g_ref[...]) * (1.0 + sc_ref[...]) + sh_ref[...]


def _proj_in_kernel(x_ref, g_ref, sc_ref, sh_ref, w_ref, o_ref, h_ref):
    @pl.when(pl.program_id(1) == 0)
    def _():
        h_ref[...] = _norm_mod(x_ref, g_ref, sc_ref, sh_ref).astype(BF16)

    o_ref[...] = _dot(h_ref[...], w_ref[...])


def _proj_in(x, g, mod, row_of_tile, w, tm):
    n = x.shape[0]
    ncol = w.shape[1]
    return pl.pallas_call(
        _proj_in_kernel,
        grid=(n // tm, ncol // PROJ_TN),
        in_specs=[pl.BlockSpec((tm, D_MODEL), lambda i, j: (i, 0), pipeline_mode=pl.Buffered(1)),
                  pl.BlockSpec((1, D_MODEL), lambda i, j: (0, 0)),
                  _mod_spec(1, row_of_tile), _mod_spec(0, row_of_tile),
                  pl.BlockSpec((D_MODEL, PROJ_TN), lambda i, j: (0, j))],
        out_specs=pl.BlockSpec((tm, PROJ_TN), lambda i, j: (i, j)),
        out_shape=jax.ShapeDtypeStruct((n, ncol), F32),
        scratch_shapes=[pltpu.VMEM((tm, D_MODEL), BF16)],
        compiler_params=_cparams("parallel", "arbitrary"),
        name="proj_in",
    )(x, g, mod, mod, w)


def _ffn_in_kernel(x_ref, g_ref, sc_ref, sh_ref, wa_ref, wb_ref, o_ref, h_ref):
    @pl.when(pl.program_id(1) == 0)
    def _():
        h_ref[...] = _norm_mod(x_ref, g_ref, sc_ref, sh_ref).astype(BF16)

    h = h_ref[...]
    a = _dot(h, wa_ref[...])
    b = _dot(h, wb_ref[...])
    o_ref[...] = (a * jax.nn.sigmoid(a) * b).astype(BF16)


def _ffn_in(x, g, mod, row_of_tile, w, tm):
    n = x.shape[0]
    tn = 512
    nj = FFN_HIDDEN // tn
    return pl.pallas_call(
        _ffn_in_kernel,
        grid=(n // tm, nj),
        in_specs=[pl.BlockSpec((tm, D_MODEL), lambda i, j: (i, 0)),
                  pl.BlockSpec((1, D_MODEL), lambda i, j: (0, 0)),
                  _mod_spec(4, row_of_tile), _mod_spec(3, row_of_tile),
                  pl.BlockSpec((D_MODEL, tn), lambda i, j: (0, j)),
                  pl.BlockSpec((D_MODEL, tn), lambda i, j: (0, nj + j))],
        out_specs=pl.BlockSpec((tm, tn), lambda i, j: (i, j)),
        out_shape=jax.ShapeDtypeStruct((n, FFN_HIDDEN), BF16),
        scratch_shapes=[pltpu.VMEM((tm, D_MODEL), BF16)],
        compiler_params=_cparams("parallel", "arbitrary"),
        name="ffn_in",
    )(x, g, mod, mod, w, w)


def _resid_kernel(a_ref, w_ref, x_ref, gate_ref, o_ref):
    o_ref[...] = x_ref[...] + gate_ref[...] * _dot(a_ref[...], w_ref[...])


def _matmul_resid(a, w, x, mod, which, row_of_tile, tm):
    n, kdim = a.shape
    tn = 512
    return pl.pallas_call(
        _resid_kernel,
        grid=(n // tm, D_MODEL // tn),
        in_specs=[pl.BlockSpec((tm, kdim), lambda i, j: (i, 0)),
                  pl.BlockSpec((kdim, tn), lambda i, j: (0, j)),
                  pl.BlockSpec((tm, tn), lambda i, j: (i, j)),
                  pl.BlockSpec((None, None, 1, tn), lambda i, j: (row_of_tile(i), which, 0, j))],
        out_specs=pl.BlockSpec((tm, tn), lambda i, j: (i, j)),
        out_shape=jax.ShapeDtypeStruct((n, D_MODEL), F32),
        compiler_params=_cparams("parallel", "parallel"),
        name="matmul_resid",
    )(a, w, x, mod)


def _final_norm_kernel(x_ref, g_ref, o_ref):
    o_ref[...] = _rms(x_ref[...]) * g_ref[...]


def _final_norm(x, g, tm):
    n = x.shape[0]
    return pl.pallas_call(
        _final_norm_kernel,
        grid=(n // tm,),
        in_specs=[pl.BlockSpec((tm, D_MODEL), lambda i: (i, 0)),
                  pl.BlockSpec((1, D_MODEL), lambda i: (0, 0))],
        out_specs=pl.BlockSpec((tm, D_MODEL), lambda i: (i, 0)),
        out_shape=jax.ShapeDtypeStruct((n, D_MODEL), F32),
        compiler_params=_cparams("parallel"),
        name="final_norm",
    )(x, g)


def _mlstm_kernel(qf_ref, kf_ref, vf_ref, gf_ref, qb_ref, kb_ref, vb_ref, gb_ref, bias_ref,
                  c0_ref, n0_ref, m0_ref,
                  hf_ref, hb_ref, c_out_ref, n_out_ref, m_out_ref,
                  c_scr, n_scr, m_scr):
    ci = pl.program_id(1)
    t = qf_ref.shape[0]

    @pl.when(ci == 0)
    def _():
        c_scr[...] = c0_ref[...]
        n_scr[...] = n0_ref[...]
        m_scr[...] = m0_ref[...]

    row = lax.broadcasted_iota(jnp.int32, (t, t), 0)
    col = lax.broadcasted_iota(jnp.int32, (t, t), 1)
    refs = ((qf_ref, kf_ref, vf_ref, gf_ref, hf_ref), (qb_ref, kb_ref, vb_ref, gb_ref, hb_ref))
    for d in range(2):
        q_ref, k_ref, v_ref, g_ref, h_ref = refs[d]
        keep = (col <= row) if d == 0 else (col >= row)
        cum = jnp.where(keep, 1.0, 0.0).astype(F32)
        gates = g_ref[...] + bias_ref[...]
        csum = _dot_exact(cum, jax.nn.log_sigmoid(gates))
        gates_t = gates.T
        csum_t = csum.T
        last = t - 1 if d == 0 else 0
        for hd in range(ML_HEADS):
            r = d * ML_HEADS + hd
            i_col = d * 2 * ML_HEADS + hd
            f_col = i_col + ML_HEADS
            b_c = csum[:, f_col:f_col + 1]
            b_r = csum_t[f_col:f_col + 1, :]
            li_c = gates[:, i_col:i_col + 1]
            li_r = gates_t[i_col:i_col + 1, :]
            m_st = m_scr[r:r + 1, 0:1]
            c_st = c_scr[r]
            n_st = n_scr[r:r + 1, :]
            sl = slice(hd * ML_DH, (hd + 1) * ML_DH)
            q = q_ref[:, sl]
            k = k_ref[:, sl] * (ML_DH ** -0.5)
            v = v_ref[:, sl]
            qb16 = q.astype(BF16)
            kb16 = k.astype(BF16)

            log_d = jnp.where(keep, b_c - b_r + li_r, -jnp.inf)
            log_inter = b_c + m_st
            m_t = jnp.maximum(log_inter, jnp.max(log_d, axis=1, keepdims=True))
            w_d = jnp.exp(log_d - m_t)
            w_inter = jnp.exp(log_inter - m_t)
            s = _dot_nt(qb16, kb16) * w_d
            num = _dot(s.astype(BF16), v.astype(BF16)) + w_inter * _dot_nt(qb16, c_st.astype(BF16))
            den = jnp.sum(s, axis=1, keepdims=True) + w_inter * jnp.sum(q * n_st, axis=1, keepdims=True)
            h_ref[:, sl] = num / jnp.maximum(jnp.abs(den), jnp.exp(-m_t))

            b_last = b_c[last:last + 1, :]
            log_w = b_last - b_c + li_c
            m_new = jnp.maximum(b_last + m_st, jnp.max(log_w, axis=0, keepdims=True))
            w_s = jnp.exp(log_w - m_new)
            w_c = jnp.exp(b_last + m_st - m_new)
            vw_t = (v * w_s).T.astype(BF16)
            c_scr[r] = w_c * c_st + _dot(vw_t, kb16)
            n_scr[r:r + 1, :] = w_c * n_st + jnp.sum(k * w_s, axis=0, keepdims=True)
            m_scr[r:r + 1, :] = jnp.broadcast_to(m_new, (1, LANES))

    @pl.when(ci == pl.num_programs(1) - 1)
    def _():
        c_out_ref[...] = c_scr[...]
        n_out_ref[...] = n_scr[...]
        m_out_ref[...] = m_scr[...]


def _mlstm(proj, row0, bsz, seq, bias, c0, n0, m0):
    tc = min(ML_CHUNK, seq)
    nc = seq // tc
    blk0 = row0 // tc
    r8 = 2 * ML_HEADS

    def fwd(colblk):
        return lambda b, c: (blk0 + b * nc + c, colblk)

    def bwd(colblk):
        return lambda b, c: (blk0 + b * nc + nc - 1 - c, colblk)

    def seqspecs(mk):
        return [pl.BlockSpec((tc, ML_W), mk(COL_ML_Q // ML_W)),
                pl.BlockSpec((tc, ML_W), mk(COL_ML_K // ML_W)),
                pl.BlockSpec((tc, ML_W), mk(COL_ML_V // ML_W)),
                pl.BlockSpec((tc, LANES), mk(COL_ML_IF // LANES))]

    state_specs = [pl.BlockSpec((None, r8, ML_DH, ML_DH), lambda b, c: (b, 0, 0, 0)),
                   pl.BlockSpec((None, r8, ML_DH), lambda b, c: (b, 0, 0)),
                   pl.BlockSpec((None, r8, LANES), lambda b, c: (b, 0, 0))]
    return pl.pallas_call(
        _mlstm_kernel,
        grid=(bsz, nc),
        in_specs=seqspecs(fwd) + seqspecs(bwd) + [pl.BlockSpec((1, LANES), lambda b, c: (0, 0))] + state_specs,
        out_specs=[pl.BlockSpec((tc, ML_W), lambda b, c: (b * nc + c, 0)),
                   pl.BlockSpec((tc, ML_W), lambda b, c: (b * nc + nc - 1 - c, 0))] + state_specs,
        out_shape=[jax.ShapeDtypeStruct((bsz * seq, ML_W), F32),
                   jax.ShapeDtypeStruct((bsz * seq, ML_W), F32),
                   jax.ShapeDtypeStruct((bsz, r8, ML_DH, ML_DH), F32),
                   jax.ShapeDtypeStruct((bsz, r8, ML_DH), F32),
                   jax.ShapeDtypeStruct((bsz, r8, LANES), F32)],
        scratch_shapes=[pltpu.VMEM((r8, ML_DH, ML_DH), F32),
                        pltpu.VMEM((r8, ML_DH), F32),
                        pltpu.VMEM((r8, LANES), F32)],
        compiler_params=_cparams("parallel", "arbitrary"),
        name="mlstm",
    )(proj, proj, proj, proj, proj, proj, proj, proj, bias, c0, n0, m0)


def _ml_post_kernel(hf_ref, hb_ref, o_ref, g_ref, y_ref):
    h = hf_ref[...] + hb_ref[...]
    for hd in range(ML_HEADS):
        sl = slice(hd * ML_DH, (hd + 1) * ML_DH)
        y_ref[:, sl] = (_rms(h[:, sl]) * g_ref[:, sl] * jax.nn.sigmoid(o_ref[:, sl])).astype(BF16)


def _ml_post(hf, hb, proj, row0, g, tm):
    n = hf.shape[0]
    blk0 = row0 // tm
    return pl.pallas_call(
        _ml_post_kernel,
        grid=(n // tm,),
        in_specs=[pl.BlockSpec((tm, ML_W), lambda i: (i, 0)),
                  pl.BlockSpec((tm, ML_W), lambda i: (i, 0)),
                  pl.BlockSpec((tm, ML_W), lambda i: (blk0 + i, COL_ML_O // ML_W)),
                  pl.BlockSpec((1, ML_W), lambda i: (0, 0))],
        out_specs=pl.BlockSpec((tm, ML_W), lambda i: (i, 0)),
        out_shape=jax.ShapeDtypeStruct((n, ML_W), BF16),
        compiler_params=_cparams("parallel"),
        name="ml_post",
    )(hf, hb, proj, g)


def _mla_prep_kernel(*refs, norm, rope, with_q):
    it = iter(refs)
    ckv_ref, kr_ref, g_ref, wkvb_ref = next(it), next(it), next(it), next(it)
    if with_q:
        qn_ref, qr_ref = next(it), next(it)
    if rope:
        cos_ref, sin_ref = next(it), next(it)
    k_ref, v_ref = next(it), next(it)
    if with_q:
        q_ref = next(it)
    if norm:
        ckv_out_ref = next(it)

    ckv = ckv_ref[...]
    if norm:
        ckv = _rms(ckv) * g_ref[...]
        ckv_out_ref[...] = ckv
    kv = _dot(ckv.astype(BF16), wkvb_ref[...])
    kr = kr_ref[...]
    if rope:
        kr = _rope_slab(kr, cos_ref[...], sin_ref[...])
    kr16 = kr[:, :MLA_ROPE].astype(BF16)
    hw = MLA_NOPE + MLA_V
    for hd in range(MLA_HEADS):
        k_ref[hd, :, 0:MLA_NOPE] = kv[:, hd * hw:hd * hw + MLA_NOPE].astype(BF16)
        k_ref[hd, :, MLA_NOPE:MLA_QK] = kr16
        v_ref[hd] = kv[:, hd * hw + MLA_NOPE:(hd + 1) * hw].astype(BF16)
    if with_q:
        scale = MLA_QK ** -0.5 * LOG2E
        qn = qn_ref[...]
        for sb in range(MLA_HEADS * MLA_ROPE // LANES):
            qr = qr_ref[:, sb * LANES:(sb + 1) * LANES]
            if rope:
                qr = _rope_slab(qr, cos_ref[...], sin_ref[...])
            for half in range(LANES // MLA_ROPE):
                hd = sb * (LANES // MLA_ROPE) + half
                q_ref[hd, :, MLA_NOPE:MLA_QK] = (qr[:, half * MLA_ROPE:(half + 1) * MLA_ROPE] * scale).astype(BF16)
        for hd in range(MLA_HEADS):
            q_ref[hd, :, 0:MLA_NOPE] = (qn[:, hd * MLA_NOPE:(hd + 1) * MLA_NOPE] * scale).astype(BF16)


def _mla_prep(ckv_src, kr_src, ckv_col, kr_col, row0, n, g, wkvb, tm, q_src=None, cos=None, sin=None, norm=True):
    rope = cos is not None
    with_q = q_src is not None
    blk0 = row0 // tm
    ins = [ckv_src, kr_src, g, wkvb]
    in_specs = [pl.BlockSpec((tm, MLA_KV_RANK), lambda i: (blk0 + i, ckv_col)),
                pl.BlockSpec((tm, LANES), lambda i: (blk0 + i, kr_col)),
                pl.BlockSpec((1, MLA_KV_RANK), lambda i: (0, 0)),
                pl.BlockSpec(wkvb.shape, lambda i: (0, 0))]
    if with_q:
        ins += [q_src, q_src]
        in_specs += [pl.BlockSpec((tm, MLA_HEADS * MLA_NOPE), lambda i: (blk0 + i, COL_MLA_QN // (MLA_HEADS * MLA_NOPE))),
                     pl.BlockSpec((tm, MLA_HEADS * MLA_ROPE), lambda i: (blk0 + i, COL_MLA_QR // (MLA_HEADS * MLA_ROPE)))]
    if rope:
        nt = cos.shape[0] // tm
        ins += [cos, sin]
        in_specs += [pl.BlockSpec((tm, LANES), lambda i: (i % nt, 0))] * 2
    out_shape = [jax.ShapeDtypeStruct((MLA_HEADS, n, MLA_QK), BF16),
                 jax.ShapeDtypeStruct((MLA_HEADS, n, MLA_V), BF16)]
    out_specs = [pl.BlockSpec((MLA_HEADS, tm, MLA_QK), lambda i: (0, i, 0)),
                 pl.BlockSpec((MLA_HEADS, tm, MLA_V), lambda i: (0, i, 0))]
    if with_q:
        out_shape.append(jax.ShapeDtypeStruct((MLA_HEADS, n, MLA_QK), BF16))
        out_specs.append(pl.BlockSpec((MLA_HEADS, tm, MLA_QK), lambda i: (0, i, 0)))
    if norm:
        out_shape.append(jax.ShapeDtypeStruct((n, MLA_KV_RANK), F32))
        out_specs.append(pl.BlockSpec((tm, MLA_KV_RANK), lambda i: (i, 0)))
    return pl.pallas_call(
        functools.partial(_mla_prep_kernel, norm=norm, rope=rope, with_q=with_q),
        grid=(n // tm,),
        in_specs=in_specs, out_specs=out_specs, out_shape=out_shape,
        compiler_params=_cparams("parallel"),
        name="mla_prep",
    )(*ins)


def _softmax_parts(scores):
    m = functools.reduce(jnp.maximum, [jnp.max(s, axis=-1, keepdims=True) for s in scores])
    ps = [jnp.exp2(s - m) for s in scores]
    l = functools.reduce(jnp.add, [jnp.sum(p, axis=-1, keepdims=True) for p in ps])
    return ps, 1.0 / l


def _query_halves(q_ref):
    tq = min(ATTN_TQ, q_ref.shape[0])
    return [slice(i * tq, (i + 1) * tq) for i in range(q_ref.shape[0] // tq)]


def _mla_attn_kernel(*refs, cached):
    if cached:
        q_ref, k_ref, v_ref, kc_ref, vc_ref, o_ref = refs
        sources = ((k_ref, v_ref), (kc_ref, vc_ref))
    else:
        q_ref, k_ref, v_ref, o_ref = refs
        sources = ((k_ref, v_ref),)
    halves = _query_halves(q_ref)
    scores = [[_dot_nt(q_ref[rows, :], kk_ref[...]) for kk_ref, _ in sources] for rows in halves]
    for rows, sc in zip(halves, scores):
        ps, r = _softmax_parts(sc)
        o = functools.reduce(jnp.add, [_dot(p.astype(BF16), vv_ref[...]) for p, (_, vv_ref) in zip(ps, sources)])
        o_ref[rows, :] = (o * r).astype(BF16)


def _mla_attn(q, k, v, bsz, seq, k_c=None, v_c=None):
    tq = min(ATTN_STEP_Q, seq)
    nq = seq // tq
    cached = k_c is not None
    ins = [q, k, v]
    in_specs = [pl.BlockSpec((None, tq, MLA_QK), lambda h, b, i: (h, b * nq + i, 0)),
                pl.BlockSpec((None, seq, MLA_QK), lambda h, b, i: (h, b, 0)),
                pl.BlockSpec((None, seq, MLA_V), lambda h, b, i: (h, b, 0))]
    if cached:
        past = k_c.shape[1] // bsz
        ins += [k_c, v_c]
        in_specs += [pl.BlockSpec((None, past, MLA_QK), lambda h, b, i: (h, b, 0)),
                     pl.BlockSpec((None, past, MLA_V), lambda h, b, i: (h, b, 0))]
    return pl.pallas_call(
        functools.partial(_mla_attn_kernel, cached=cached),
        grid=(MLA_HEADS, bsz, nq),
        in_specs=in_specs,
        out_specs=pl.BlockSpec((tq, MLA_V), lambda h, b, i: (b * nq + i, h)),
        out_shape=jax.ShapeDtypeStruct((bsz * seq, MLA_HEADS * MLA_V), BF16),
        compiler_params=_cparams("parallel", "parallel", "parallel"),
        name="mla_attn",
    )(*ins)


def _rope_cast_kernel(x_ref, cos_ref, sin_ref, o_ref):
    for sb in range(x_ref.shape[1] // LANES):
        sl = slice(sb * LANES, (sb + 1) * LANES)
        o_ref[:, sl] = _rope_slab(x_ref[:, sl], cos_ref[...], sin_ref[...]).astype(BF16)


def _rope_cast(src, colblk, n, tm, cos, sin):
    nt = cos.shape[0] // tm
    return pl.pallas_call(
        _rope_cast_kernel,
        grid=(n // tm,),
        in_specs=[pl.BlockSpec((tm, DF_W), lambda i: (i, colblk))]
                 + [pl.BlockSpec((tm, LANES), lambda i: (i % nt, 0))] * 2,
        out_specs=pl.BlockSpec((tm, DF_W), lambda i: (i, 0)),
        out_shape=jax.ShapeDtypeStruct((n, DF_W), BF16),
        compiler_params=_cparams("parallel"),
        name="rope_cast",
    )(src, cos, sin)


def _diff_attn_kernel(*refs, lam_init, cached, rope):
    it = iter(refs)
    q_ref, k_ref, v_ref = next(it), next(it), next(it)
    sources = [(k_ref, v_ref)]
    if cached:
        sources.append((next(it), next(it)))
    if rope:
        cos_ref, sin_ref = next(it), next(it)
    lam_ref, g_ref, o_ref = next(it), next(it), next(it)
    lp = lam_ref[...]
    lam = (jnp.exp(jnp.sum(lp[0:1] * lp[1:2], axis=-1, keepdims=True))
           - jnp.exp(jnp.sum(lp[2:3] * lp[3:4], axis=-1, keepdims=True)) + lam_init)
    halves = _query_halves(q_ref)
    ks = [kk_ref[...].astype(BF16) for kk_ref, _ in sources]
    vs = [vv_ref[...].astype(BF16) for _, vv_ref in sources]
    scores = []
    for rows in halves:
        q = q_ref[rows, :]
        if rope:
            q = _rope_slab(q, cos_ref[rows, :], sin_ref[rows, :])
        q = (q * (DF_DQK ** -0.5 * LOG2E)).astype(BF16)
        lane = lax.broadcasted_iota(jnp.int32, q.shape, 1)
        zero = jnp.zeros_like(q)
        q1 = jnp.where(lane < DF_DQK, q, zero)
        q2 = jnp.where(lane >= DF_DQK, q, zero)
        scores.append(([_dot_nt(q1, k) for k in ks], [_dot_nt(q2, k) for k in ks]))
    for rows, (s1, s2) in zip(halves, scores):
        p1, r1 = _softmax_parts(s1)
        p2, r2 = _softmax_parts(s2)
        c = lam * r2 / r1
        o = functools.reduce(jnp.add, [_dot((a - c * b).astype(BF16), v) for a, b, v in zip(p1, p2, vs)]) * r1
        o_ref[rows, :] = (_rms(o) * g_ref[...] * (1.0 - lam_init)).astype(BF16)


def _diff_attn(proj, k, kcol, lam_p, g, lam_init, bsz, seq, k_c=None, v_c=None, cos=None, sin=None):
    tq = min(ATTN_STEP_Q, seq)
    nq = seq // tq
    cached = k_c is not None
    rope = cos is not None
    qcol, vcol = COL_DF_Q // DF_DV, COL_DF_V // DF_DV
    ins = [proj, k, proj]
    in_specs = [pl.BlockSpec((tq, DF_DV), lambda b, h, i: (b * nq + i, qcol + h)),
                pl.BlockSpec((seq, DF_DV), lambda b, h, i: (b, kcol + h)),
                pl.BlockSpec((seq, DF_DV), lambda b, h, i: (b, vcol + h))]
    if cached:
        past = k_c.shape[0] // bsz
        ins += [k_c, v_c]
        in_specs += [pl.BlockSpec((past, DF_DV), lambda b, h, i: (b, h)),
                     pl.BlockSpec((past, DF_DV), lambda b, h, i: (b, h))]
    if rope:
        ins += [cos, sin]
        in_specs += [pl.BlockSpec((tq, LANES), lambda b, h, i: (i, 0))] * 2
    ins += [lam_p, g]
    in_specs += [pl.BlockSpec((4, DF_DQK), lambda b, h, i: (0, 0)),
                 pl.BlockSpec((1, DF_DV), lambda b, h, i: (0, 0))]
    return pl.pallas_call(
        functools.partial(_diff_attn_kernel, lam_init=lam_init, cached=cached, rope=rope),
        grid=(bsz, DF_HEADS, nq),
        in_specs=in_specs,
        out_specs=pl.BlockSpec((tq, DF_DV), lambda b, h, i: (b * nq + i, h)),
        out_shape=jax.ShapeDtypeStruct((bsz * seq, DF_W), BF16),
        compiler_params=_cparams("parallel", "parallel", "parallel"),
        name="diff_attn",
    )(*ins)


def _s5_kernel(uf_ref, ub_ref, a_re_ref, a_im_ref, h0_re_ref, h0_im_ref,
               wb_re_ref, wb_im_ref, wc_re_ref, wc_im_ref,
               yf_ref, yb_ref, hr_out_ref, hi_out_ref,
               bu_re, bu_im, h_re, h_im):
    ci = pl.program_id(1)
    tc = uf_ref.shape[1]
    nseq = 2 * S5_SEQS
    gblk, sblk = wb_re_ref.shape[2:]
    ngb = S5_W // gblk

    @pl.when(ci == 0)
    def _():
        h_re[...] = h0_re_ref[...]
        h_im[...] = h0_im_ref[...]

    ri = lax.broadcasted_iota(jnp.int32, (tc, tc), 0)
    cj = lax.broadcasted_iota(jnp.int32, (tc, tc), 1)
    rev = jnp.where(ri + cj == tc - 1, 1.0, 0.0).astype(BF16)

    for d, u_ref in enumerate((uf_ref, ub_ref)):
        us = []
        for s in range(S5_SEQS):
            u = u_ref[s].astype(BF16)
            if d == 1:
                u = _dot(rev, u).astype(BF16)
            us.append(u)
        u_all = jnp.concatenate(us, axis=0)
        for gb in range(ngb):
            ug = u_all[:, gb * gblk:(gb + 1) * gblk]
            for w_ref, dst in ((wb_re_ref, bu_re), (wb_im_ref, bu_im)):
                bu = _dot(ug, w_ref[d, gb])
                for s in range(S5_SEQS):
                    for lk in range(sblk // LANES):
                        dst[gb * (sblk // LANES) + lk, pl.ds(d * S5_SEQS + s, tc, stride=nseq), :] = (
                            bu[s * tc:(s + 1) * tc, lk * LANES:(lk + 1) * LANES])

    nlk = S5_LANE_BLK // LANES
    for lb in range(S5_STATE // S5_LANE_BLK):
        lks = tuple(range(lb * nlk, (lb + 1) * nlk))
        ar = [a_re_ref[:, k * LANES:(k + 1) * LANES] for k in lks]
        ai = [a_im_ref[:, k * LANES:(k + 1) * LANES] for k in lks]

        def step(j, carry):
            r0 = pl.multiple_of(j * nseq, nseq)
            out = []
            for i, k in enumerate(lks):
                hr, hi = carry[2 * i], carry[2 * i + 1]
                nhr = ar[i] * hr - ai[i] * hi + bu_re[k, pl.ds(r0, nseq), :]
                nhi = ar[i] * hi + ai[i] * hr + bu_im[k, pl.ds(r0, nseq), :]
                bu_re[k, pl.ds(r0, nseq), :] = nhr
                bu_im[k, pl.ds(r0, nseq), :] = nhi
                out += [nhr, nhi]
            return tuple(out)

        init = []
        for k in lks:
            init += [h_re[:, k * LANES:(k + 1) * LANES], h_im[:, k * LANES:(k + 1) * LANES]]
        fin = lax.fori_loop(0, tc, step, tuple(init), unroll=8)
        for i, k in enumerate(lks):
            h_re[:, k * LANES:(k + 1) * LANES] = fin[2 * i]
            h_im[:, k * LANES:(k + 1) * LANES] = fin[2 * i + 1]

    def seq_states(src, s):
        parts = [src[k, pl.ds(s, tc, stride=nseq), :] for k in range(S5_STATE // LANES)]
        return jnp.concatenate(parts, axis=1).astype(BF16)

    def reverse_rows(y):
        hi = y.astype(BF16)
        r1 = y - hi.astype(F32)
        mid = r1.astype(BF16)
        lo = (r1 - mid.astype(F32)).astype(BF16)
        return _dot(rev, hi) + _dot(rev, mid) + _dot(rev, lo)

    for d, y_ref in enumerate((yf_ref, yb_ref)):
        hr_all = jnp.concatenate([seq_states(bu_re, d * S5_SEQS + s) for s in range(S5_SEQS)], axis=0)
        hi_all = jnp.concatenate([seq_states(bu_im, d * S5_SEQS + s) for s in range(S5_SEQS)], axis=0)
        for gb in range(ngb):
            ssl = slice(gb * sblk, (gb + 1) * sblk)
            y = _dot(hr_all[:, ssl], wc_re_ref[d, gb]) - _dot(hi_all[:, ssl], wc_im_ref[d, gb])
            for s in range(S5_SEQS):
                ys = y[s * tc:(s + 1) * tc]
                y_ref[s, :, gb * gblk:(gb + 1) * gblk] = reverse_rows(ys) if d == 1 else ys

    @pl.when(ci == pl.num_programs(1) - 1)
    def _():
        hr_out_ref[...] = h_re[...]
        hi_out_ref[...] = h_im[...]


def _s5(proj3, g0, bsz, seq, a_re, a_im, h0_re, h0_im, wb_re, wb_im, wc_re, wc_im):
    tc = min(S5_CHUNK, seq)
    nc = seq // tc
    ng = bsz // S5_SEQS
    nseq = 2 * S5_SEQS
    ucol = COL_S5_U // S5_W
    const4 = lambda g, c: (0, 0, 0, 0)
    wspec = lambda w: pl.BlockSpec(w.shape, const4, pipeline_mode=pl.Buffered(1))
    hspec = pl.BlockSpec((None, nseq, S5_STATE), lambda g, c: (g, 0, 0))
    return pl.pallas_call(
        _s5_kernel,
        grid=(ng, nc),
        in_specs=[pl.BlockSpec((S5_SEQS, tc, S5_W), lambda g, c: (g0 + g, c, ucol)),
                  pl.BlockSpec((S5_SEQS, tc, S5_W), lambda g, c: (g0 + g, nc - 1 - c, ucol)),
                  pl.BlockSpec((nseq, S5_STATE), lambda g, c: (0, 0)),
                  pl.BlockSpec((nseq, S5_STATE), lambda g, c: (0, 0)),
                  hspec, hspec, wspec(wb_re), wspec(wb_im), wspec(wc_re), wspec(wc_im)],
        out_specs=[pl.BlockSpec((S5_SEQS, tc, S5_W), lambda g, c: (g, c, 0)),
                   pl.BlockSpec((S5_SEQS, tc, S5_W), lambda g, c: (g, nc - 1 - c, 0)),
                   hspec, hspec],
        out_shape=[jax.ShapeDtypeStruct((bsz, seq, S5_W), F32),
                   jax.ShapeDtypeStruct((bsz, seq, S5_W), F32),
                   jax.ShapeDtypeStruct((ng, nseq, S5_STATE), F32),
                   jax.ShapeDtypeStruct((ng, nseq, S5_STATE), F32)],
        scratch_shapes=[pltpu.VMEM((S5_STATE // LANES, tc * nseq, LANES), F32),
                        pltpu.VMEM((S5_STATE // LANES, tc * nseq, LANES), F32),
                        pltpu.VMEM((nseq, S5_STATE), F32),
                        pltpu.VMEM((nseq, S5_STATE), F32)],
        compiler_params=_cparams("parallel", "arbitrary"),
        name="s5",
    )(proj3, proj3, a_re, a_im, h0_re, h0_im, wb_re, wb_im, wc_re, wc_im)


def _s5_post_kernel(yf_ref, yb_ref, u_ref, d_ref, w_ref, o_ref):
    y = (yf_ref[...] + yb_ref[...]) + d_ref[...] * u_ref[...]
    g = jax.nn.gelu(y)
    o_ref[...] = (g * jax.nn.sigmoid(_dot(g.astype(BF16), w_ref[...]))).astype(BF16)


def _s5_post(yf, yb, proj, row0, d, w, tm):
    n = yf.shape[0]
    blk0 = row0 // tm
    return pl.pallas_call(
        _s5_post_kernel,
        grid=(n // tm,),
        in_specs=[pl.BlockSpec((tm, S5_W), lambda i: (i, 0)),
                  pl.BlockSpec((tm, S5_W), lambda i: (i, 0)),
                  pl.BlockSpec((tm, S5_W), lambda i: (blk0 + i, COL_S5_U // S5_W)),
                  pl.BlockSpec((1, S5_W), lambda i: (0, 0)),
                  pl.BlockSpec((S5_W, S5_W), lambda i: (0, 0))],
        out_specs=pl.BlockSpec((tm, S5_W), lambda i: (i, 0)),
        out_shape=jax.ShapeDtypeStruct((n, S5_W), BF16),
        compiler_params=_cparams("parallel"),
        name="s5_post",
    )(yf, yb, proj, d, w)


def _s5_discretise(a_re, a_im, log_dt, b_re, b_im):
    lr = jnp.minimum(a_re, -1e-4)
    li = a_im
    dt = jnp.exp(log_dt)[..., None]
    mag = jnp.exp(dt * lr)
    ab_re, ab_im = mag * jnp.cos(dt * li), mag * jnp.sin(dt * li)
    den = lr * lr + li * li
    nr, ni = ab_re - 1.0, ab_im
    qr = (nr * lr + ni * li) / den
    qi = (ni * lr - nr * li) / den
    bb_re = qr[..., None] * b_re - qi[..., None] * b_im
    bb_im = qr[..., None] * b_im + qi[..., None] * b_re
    return ab_re, ab_im, bb_re, bb_im


def _block_diag(w):
    ngrp = S5_GROUPS_PER_TILE
    d, g, r, c = w.shape
    wg = w.reshape(d, g // ngrp, ngrp, r, c)
    eye = jnp.eye(ngrp, dtype=w.dtype)
    out = jnp.einsum('dbgrc,gh->dbgrhc', wg, eye)
    return out.reshape(d, g // ngrp, ngrp * r, ngrp * c)


def _merge_kernel(y0_ref, y1_ref, y2_ref, y3_ref, g0_ref, g1_ref, g2_ref, g3_ref, w_ref, o_ref):
    acc = None
    for b, (y_ref, g_ref) in enumerate(((y0_ref, g0_ref), (y1_ref, g1_ref), (y2_ref, g2_ref), (y3_ref, g3_ref))):
        term = jax.nn.sigmoid(g_ref[...]) * _dot(y_ref[...], w_ref[b])
        acc = term if acc is None else acc + term
    o_ref[...] = acc.astype(BF16)


def _merge(ys, proj, wb, tm):
    n = ys[0].shape[0]
    tn = 512
    gate0 = COL_GATE // tn
    per = D_MODEL // tn

    def gspec(b):
        return pl.BlockSpec((tm, tn), lambda i, j: (i, gate0 + b * per + j))

    return pl.pallas_call(
        _merge_kernel,
        grid=(n // tm, per),
        in_specs=[pl.BlockSpec((tm, BRANCH_W), lambda i, j: (i, 0))] * N_BRANCH
                 + [gspec(b) for b in range(N_BRANCH)]
                 + [pl.BlockSpec((N_BRANCH, BRANCH_W, tn), lambda i, j: (0, 0, j))],
        out_specs=pl.BlockSpec((tm, tn), lambda i, j: (i, j)),
        out_shape=jax.ShapeDtypeStruct((n, D_MODEL), BF16),
        compiler_params=_cparams("parallel", "parallel"),
        name="merge",
    )(*ys, proj, proj, proj, proj, wb)


def _rope_tables(n_tok):
    grid_rows = n_tok // GRID_W
    rows, cols = jnp.meshgrid(jnp.arange(grid_rows, dtype=F32), jnp.arange(GRID_W, dtype=F32), indexing='ij')
    quarter = ROPE_DIM // 4
    inv = ROPE_BASE ** (-jnp.arange(quarter, dtype=F32) / quarter)
    ang_r = rows.reshape(-1, 1) * inv
    ang_c = cols.reshape(-1, 1) * inv
    cos = jnp.concatenate([jnp.cos(ang_r)] * 2 + [jnp.cos(ang_c)] * 2, axis=-1)
    sin = jnp.concatenate([-jnp.sin(ang_r), jnp.sin(ang_r), -jnp.sin(ang_c), jnp.sin(ang_c)], axis=-1)
    return jnp.tile(cos, (1, LANES // ROPE_DIM)), jnp.tile(sin, (1, LANES // ROPE_DIM))


def _permute_w_in(w):
    sizes = (ML_W, ML_W, ML_W, ML_W, 4 * ML_HEADS, MLA_HEADS * MLA_QK, MLA_KV_RANK + MLA_ROPE, S5_W,
             DF_W, DF_W, DF_W, N_BRANCH * D_MODEL)
    splits = tuple(int(s) for s in np.cumsum(sizes)[:-1])
    (ml_q, ml_k, ml_v, ml_o, ml_if, mla_q, mla_kva, s5_u, df_q, df_k, df_v, gate) = jnp.split(w, splits, axis=1)
    mq = mla_q.reshape(D_MODEL, MLA_HEADS, MLA_QK)
    qn = mq[:, :, :MLA_NOPE].reshape(D_MODEL, MLA_HEADS * MLA_NOPE)
    qr = mq[:, :, MLA_NOPE:].reshape(D_MODEL, MLA_HEADS * MLA_ROPE)
    pad = lambda a, width: jnp.pad(a, ((0, 0), (0, width - a.shape[1])))
    cols = [ml_q, ml_k, ml_v, ml_o, s5_u, df_q, df_k, df_v, gate, qn, qr,
            mla_kva[:, :MLA_KV_RANK], pad(mla_kva[:, MLA_KV_RANK:], LANES), pad(ml_if, LANES)]
    out = jnp.concatenate(cols, axis=1)
    return pad(out, PROJ_COLS).astype(BF16)


def _seq_mixers(proj, row0, bsz, seq, lw, lam_init, state, cache, rope):
    n = bsz * seq
    tm = min(TM_SEQ, seq)
    ml_c0, ml_n0, ml_m0, s5_h0r, s5_h0i = state
    r8 = 2 * ML_HEADS

    m0 = jnp.broadcast_to(ml_m0.reshape(bsz, r8, 1), (bsz, r8, LANES))
    hf, hb, c_new, n_new, m_new = _mlstm(proj, row0, bsz, seq, lw['ml_bias'],
                                         ml_c0.reshape(bsz, r8, ML_DH, ML_DH), ml_n0.reshape(bsz, r8, ML_DH), m0)
    y_ml = _ml_post(hf, hb, proj, row0, lw['ml_norm'], tm)

    cos, sin = rope if rope is not None else (None, None)
    k_new, v_new, q_mla, ckv = _mla_prep(proj, proj, COL_MLA_CKV // MLA_KV_RANK, COL_MLA_KR // LANES, row0, n,
                                         lw['mla_kv_norm'], lw['mla_w_kvb'], tm, q_src=proj, cos=cos, sin=sin)
    k_c = v_c = None
    if cache is not None:
        ckv_c, krope_c, dk_c, dv_c = cache
        past = ckv_c.shape[1]
        kr_pad = jnp.pad(krope_c.reshape(bsz * past, MLA_ROPE), ((0, 0), (0, LANES - MLA_ROPE)))
        k_c, v_c = _mla_prep(ckv_c.reshape(bsz * past, MLA_KV_RANK), kr_pad, 0, 0, 0, bsz * past,
                             lw['mla_kv_norm'], lw['mla_w_kvb'], min(tm, bsz * past), norm=False)
    y_mla = _mla_attn(q_mla, k_new, v_new, bsz, seq, k_c, v_c)

    proj3 = proj.reshape(proj.shape[0] // seq, seq, PROJ_COLS)
    ng = bsz // S5_SEQS

    def pack_state(hs):
        return hs.reshape(ng, S5_SEQS, 2, S5_STATE).transpose(0, 2, 1, 3).reshape(ng, 2 * S5_SEQS, S5_STATE)

    def unpack_state(hs):
        return hs.reshape(ng, 2, S5_SEQS, S5_STATE).transpose(0, 2, 1, 3).reshape(bsz, 2, S5_GROUPS, S5_P)

    yf, yb, hr_new, hi_new = _s5(proj3, row0 // (seq * S5_SEQS), bsz, seq, lw['s5_a_re8'], lw['s5_a_im8'],
                                 pack_state(s5_h0r), pack_state(s5_h0i),
                                 lw['s5_wb_re'], lw['s5_wb_im'], lw['s5_wc_re'], lw['s5_wc_im'])
    y_s5 = _s5_post(yf.reshape(n, S5_W), yb.reshape(n, S5_W), proj, row0, lw['s5_d'], lw['s5_w_glu'], tm)

    if cos is not None:
        dk, kcol = _rope_cast(proj, COL_DF_K // DF_W, n, tm, cos, sin), 0
    else:
        dk, kcol = proj, COL_DF_K // DF_DV
    dk_cache = dv_cache = None
    if cache is not None:
        dk_cache = dk_c.reshape(bsz * past, DF_W)
        dv_cache = dv_c.reshape(bsz * past, DF_W)
    y_df = _diff_attn(proj, dk, kcol, lw['df_lambda'], lw['df_norm'], lam_init, bsz, seq,
                      dk_cache, dv_cache, cos, sin)

    cols = lambda c0, w: lax.slice(proj, (row0, c0), (row0 + n, c0 + w))
    new_ctx = (ckv.reshape(bsz, seq, MLA_KV_RANK),
               cols(COL_MLA_KR, MLA_ROPE).reshape(bsz, seq, MLA_ROPE),
               cols(COL_DF_K, DF_W).reshape(bsz, seq, DF_HEADS, 2 * DF_DQK),
               cols(COL_DF_V, DF_W).reshape(bsz, seq, DF_HEADS, DF_DV),
               c_new.reshape(bsz, 2, ML_HEADS, ML_DH, ML_DH), n_new.reshape(bsz, 2, ML_HEADS, ML_DH),
               m_new[:, :, 0].reshape(bsz, 2, ML_HEADS), unpack_state(hr_new), unpack_state(hi_new))
    return (y_ml, y_mla, y_s5, y_df), new_ctx


def kernel(x_prompt, x_sample, cache_mla_ckv, cache_mla_krope, cache_diff_k, cache_diff_v,
           state_mlstm_c, state_mlstm_n, state_mlstm_m, state_s5_re, state_s5_im, c,
           c_ctx, w_ada, b_ada, norm_mix, norm_ffn, w_in, ml_if_bias, ml_norm, mla_kv_norm,
           mla_w_kvb, s5_a_re, s5_a_im, s5_log_dt, s5_b_re, s5_b_im, s5_c_re, s5_c_im, s5_d,
           s5_w_glu, df_lambda, df_norm, w_branch, w_o, w_ffn_in, w_ffn_out, final_norm):
    bp, sp, _ = x_prompt.shape
    bs, ss, _ = x_sample.shape
    n_p, n_s = bp * sp, bs * ss
    tm = TM_DENSE

    cond = jnp.concatenate([c, c_ctx[None, :], jnp.zeros((SUBLANES - 1 - bs, D_MODEL), F32)], axis=0)
    mods = _ada(cond, w_ada, b_ada).reshape(DEPTH, SUBLANES, 6, 1, D_MODEL)
    rope = _rope_tables(ss)
    latent_row = lambda i: (i * tm) // ss
    prompt_row = lambda i: bs

    xs = x_sample.reshape(n_s, D_MODEL)
    xp = x_prompt.reshape(n_p, D_MODEL)
    zeros_state = (jnp.zeros((bp, 2, ML_HEADS, ML_DH, ML_DH), F32), jnp.zeros((bp, 2, ML_HEADS, ML_DH), F32),
                   jnp.zeros((bp, 2, ML_HEADS), F32), jnp.zeros((bp, 2, S5_GROUPS, S5_P), F32),
                   jnp.zeros((bp, 2, S5_GROUPS, S5_P), F32))
    ctx_out = []
    for l in range(DEPTH):
        lam_init = 0.8 - 0.6 * math.exp(-0.3 * l)
        ab_re, ab_im, bb_re, bb_im = _s5_discretise(s5_a_re[l], s5_a_im[l], s5_log_dt[l], s5_b_re[l], s5_b_im[l])
        rep = lambda a: jnp.repeat(a.reshape(2, S5_STATE), S5_SEQS, axis=0)
        lw = {
            'ml_bias': jnp.pad(ml_if_bias[l].reshape(1, 4 * ML_HEADS), ((0, 0), (0, LANES - 4 * ML_HEADS))),
            'ml_norm': ml_norm[l].reshape(1, ML_W),
            'mla_kv_norm': mla_kv_norm[l].reshape(1, MLA_KV_RANK),
            'mla_w_kvb': mla_w_kvb[l].astype(BF16),
            's5_a_re8': rep(ab_re), 's5_a_im8': rep(ab_im),
            's5_wb_re': _block_diag(bb_re.transpose(0, 1, 3, 2)).astype(BF16),
            's5_wb_im': _block_diag(bb_im.transpose(0, 1, 3, 2)).astype(BF16),
            's5_wc_re': _block_diag(s5_c_re[l].transpose(0, 1, 3, 2)).astype(BF16),
            's5_wc_im': _block_diag(s5_c_im[l].transpose(0, 1, 3, 2)).astype(BF16),
            's5_d': s5_d[l].reshape(1, S5_W),
            's5_w_glu': s5_w_glu[l].astype(BF16),
            'df_lambda': df_lambda[l],
            'df_norm': df_norm[l].reshape(1, DF_DV),
        }
        mod = mods[l]
        w_in_l = _permute_w_in(w_in[l])
        w_branch_l, w_o_l = w_branch[l].astype(BF16), w_o[l].astype(BF16)
        w_ffn_in_l, w_ffn_out_l = w_ffn_in[l].astype(BF16), w_ffn_out[l].astype(BF16)
        g_mix, g_ffn = norm_mix[l].reshape(1, D_MODEL), norm_ffn[l].reshape(1, D_MODEL)

        def layer(x, row_of_tile, bsz, seq, state, cache, rope_tabs):
            proj = _proj_in(x, g_mix, mod, row_of_tile, w_in_l, tm)
            ys, new_ctx = _seq_mixers(proj, 0, bsz, seq, lw, lam_init, state, cache, rope_tabs)
            merged = _merge(list(ys), proj, w_branch_l, tm)
            x = _matmul_resid(merged, w_o_l, x, mod, 2, row_of_tile, tm)
            act = _ffn_in(x, g_ffn, mod, row_of_tile, w_ffn_in_l, tm)
            return _matmul_resid(act, w_ffn_out_l, x, mod, 5, row_of_tile, tm), new_ctx

        xp, new_ctx = layer(xp, prompt_row, bp, sp, zeros_state, None, None)
        ctx_out.append(new_ctx)
        state = (state_mlstm_c[:, l], state_mlstm_n[:, l], state_mlstm_m[:, l], state_s5_re[:, l], state_s5_im[:, l])
        cache = (cache_mla_ckv[:, l], cache_mla_krope[:, l], cache_diff_k[:, l], cache_diff_v[:, l])
        xs, _ = layer(xs, latent_row, bs, ss, state, cache, rope)

    g_fin = final_norm.reshape(1, D_MODEL)
    y_prompt = _final_norm(xp, g_fin, tm).reshape(bp, sp, D_MODEL)
    y_sample = _final_norm(xs, g_fin, tm).reshape(bs, ss, D_MODEL)
    stacked = tuple(jnp.stack([ctx[k] for ctx in ctx_out], axis=1) for k in range(9))
    return (y_prompt, y_sample) + stacked
```

```python
import functools
import math

import jax
import jax.numpy as jnp
import numpy as np
from jax import lax
from jax.experimental import pallas as pl
from jax.experimental.pallas import tpu as pltpu

F32 = jnp.float32
BF16 = jnp.bfloat16

D_MODEL = 2048
DEPTH = 2
GRID_W = 64
ROPE_DIM = 64
ROPE_BASE = 10000.0
RMS_EPS = 1e-6
ML_HEADS = 4
ML_DH = 256
ML_W = ML_HEADS * ML_DH
MLA_HEADS = 8
MLA_NOPE = 128
MLA_ROPE = ROPE_DIM
MLA_V = 128
MLA_KV_RANK = 512
MLA_QK = MLA_NOPE + MLA_ROPE
S5_GROUP = 16
S5_GROUPS = 64
S5_W = S5_GROUPS * S5_GROUP
S5_P = 64
S5_STATE = S5_GROUPS * S5_P
DF_HEADS = 8
DF_DQK = ROPE_DIM
DF_DV = 2 * DF_DQK
DF_W = DF_HEADS * DF_DV
N_BRANCH = 4
BRANCH_W = 1024
FFN_HIDDEN = (8 * D_MODEL + 3 * 256 - 1) // (3 * 256) * 256

LANES = 128
SUBLANES = 8
VMEM_LIMIT_BYTES = 56 * 1024 * 1024

COL_ML_Q = 0
COL_ML_K = 1024
COL_ML_V = 2048
COL_ML_O = 3072
COL_S5_U = 4096
COL_DF_Q = 5120
COL_DF_K = 6144
COL_DF_V = 7168
COL_GATE = 8192
COL_MLA_QN = 16384
COL_MLA_QR = 17408
COL_MLA_CKV = 17920
COL_MLA_KR = 18432
COL_ML_IF = 18560
PROJ_TN = 1024
PROJ_COLS = 19456

LOG2E = math.log2(math.e)

TM_DENSE = 1024
TM_SEQ = 512
DIFF_STEP_Q = 512
ATTN_STEP_Q = 1024
ML_CHUNK = 256
S5_CHUNK = 64
S5_SEQS = 4
S5_LANE_BLK = 512
S5_GROUPS_PER_TILE = 16
ATTN_TQ = 256


def _cparams(*sem):
    return pltpu.CompilerParams(dimension_semantics=sem, vmem_limit_bytes=VMEM_LIMIT_BYTES)


def _dot(a, b):
    return jnp.dot(a, b, preferred_element_type=F32)


def _dot_nt(a, b):
    return lax.dot_general(a, b, (((1,), (1,)), ((), ())), preferred_element_type=F32)


def _dot_exact(a, b):
    return jnp.dot(a, b, preferred_element_type=F32, precision=lax.Precision.HIGHEST)


def _rms(x):
    return x * lax.rsqrt(jnp.mean(x * x, axis=-1, keepdims=True) + RMS_EPS)


def _rope_slab(x, cos, sin):
    lane = lax.broadcasted_iota(jnp.int32, x.shape, 1)
    partner = jnp.where((lane % 32) < 16, pltpu.roll(x, LANES - 16, 1), pltpu.roll(x, 16, 1))
    return x * cos + partner * sin


def _ada_kernel(c_ref, w_ref, b_ref, o_ref):
    c = c_ref[...]
    s = c * jax.nn.sigmoid(c)
    o_ref[...] = _dot(s.astype(BF16), w_ref[...].astype(BF16)) + b_ref[...]


def _ada(cond, w_ada, b_ada):
    rows = cond.shape[0]
    tn = 1024
    return pl.pallas_call(
        _ada_kernel,
        grid=(DEPTH, 6 * D_MODEL // tn),
        in_specs=[pl.BlockSpec((rows, D_MODEL), lambda l, j: (0, 0)),
                  pl.BlockSpec((None, D_MODEL, tn), lambda l, j: (l, 0, j)),
                  pl.BlockSpec((None, 1, tn), lambda l, j: (l, 0, j))],
        out_specs=pl.BlockSpec((None, rows, tn), lambda l, j: (l, 0, j)),
        out_shape=jax.ShapeDtypeStruct((DEPTH, rows, 6 * D_MODEL), F32),
        compiler_params=_cparams("parallel", "parallel"),
        name="ada",
    )(cond, w_ada, b_ada.reshape(DEPTH, 1, 6 * D_MODEL))


def _mod_spec(which, row_of_tile):
    return pl.BlockSpec((None, None, 1, D_MODEL), lambda i, j: (row_of_tile(i), which, 0, 0))


def _norm_mod(x_ref, g_ref, sc_ref, sh_ref):
    return (_rms(x_ref[...]) * g_ref[...]) * (1.0 + sc_ref[...]) + sh_ref[...]


def _proj_in_kernel(x_ref, g_ref, sc_ref, sh_ref, w_ref, o_ref, h_ref):
    @pl.when(pl.program_id(1) == 0)
    def _():
        h_ref[...] = _norm_mod(x_ref, g_ref, sc_ref, sh_ref).astype(BF16)

    o_ref[...] = _dot(h_ref[...], w_ref[...])


def _proj_in(x, g, mod, row_of_tile, w, tm):
    n = x.shape[0]
    ncol = w.shape[1]
    return pl.pallas_call(
        _proj_in_kernel,
        grid=(n // tm, ncol // PROJ_TN),
        in_specs=[pl.BlockSpec((tm, D_MODEL), lambda i, j: (i, 0), pipeline_mode=pl.Buffered(1)),
                  pl.BlockSpec((1, D_MODEL), lambda i, j: (0, 0)),
                  _mod_spec(1, row_of_tile), _mod_spec(0, row_of_tile),
                  pl.BlockSpec((D_MODEL, PROJ_TN), lambda i, j: (0, j))],
        out_specs=pl.BlockSpec((tm, PROJ_TN), lambda i, j: (i, j)),
        out_shape=jax.ShapeDtypeStruct((n, ncol), F32),
        scratch_shapes=[pltpu.VMEM((tm, D_MODEL), BF16)],
        compiler_params=_cparams("parallel", "arbitrary"),
        name="proj_in",
    )(x, g, mod, mod, w)


def _ffn_in_kernel(x_ref, g_ref, sc_ref, sh_ref, wa_ref, wb_ref, o_ref, h_ref):
    @pl.when(pl.program_id(1) == 0)
    def _():
        h_ref[...] = _norm_mod(x_ref, g_ref, sc_ref, sh_ref).astype(BF16)

    h = h_ref[...]
    a = _dot(h, wa_ref[...])
    b = _dot(h, wb_ref[...])
    o_ref[...] = (a * jax.nn.sigmoid(a) * b).astype(BF16)


def _ffn_in(x, g, mod, row_of_tile, w, tm):
    n = x.shape[0]
    tn = 512
    nj = FFN_HIDDEN // tn
    return pl.pallas_call(
        _ffn_in_kernel,
        grid=(n // tm, nj),
        in_specs=[pl.BlockSpec((tm, D_MODEL), lambda i, j: (i, 0)),
                  pl.BlockSpec((1, D_MODEL), lambda i, j: (0, 0)),
                  _mod_spec(4, row_of_tile), _mod_spec(3, row_of_tile),
                  pl.BlockSpec((D_MODEL, tn), lambda i, j: (0, j)),
                  pl.BlockSpec((D_MODEL, tn), lambda i, j: (0, nj + j))],
        out_specs=pl.BlockSpec((tm, tn), lambda i, j: (i, j)),
        out_shape=jax.ShapeDtypeStruct((n, FFN_HIDDEN), BF16),
        scratch_shapes=[pltpu.VMEM((tm, D_MODEL), BF16)],
        compiler_params=_cparams("parallel", "arbitrary"),
        name="ffn_in",
    )(x, g, mod, mod, w, w)


def _resid_kernel(a_ref, w_ref, x_ref, gate_ref, o_ref):
    o_ref[...] = x_ref[...] + gate_ref[...] * _dot(a_ref[...], w_ref[...])


def _matmul_resid(a, w, x, mod, which, row_of_tile, tm):
    n, kdim = a.shape
    tn = 512
    return pl.pallas_call(
        _resid_kernel,
        grid=(n // tm, D_MODEL // tn),
        in_specs=[pl.BlockSpec((tm, kdim), lambda i, j: (i, 0)),
                  pl.BlockSpec((kdim, tn), lambda i, j: (0, j)),
                  pl.BlockSpec((tm, tn), lambda i, j: (i, j)),
                  pl.BlockSpec((None, None, 1, tn), lambda i, j: (row_of_tile(i), which, 0, j))],
        out_specs=pl.BlockSpec((tm, tn), lambda i, j: (i, j)),
        out_shape=jax.ShapeDtypeStruct((n, D_MODEL), F32),
        compiler_params=_cparams("parallel", "parallel"),
        name="matmul_resid",
    )(a, w, x, mod)


def _final_norm_kernel(x_ref, g_ref, o_ref):
    o_ref[...] = _rms(x_ref[...]) * g_ref[...]


def _final_norm(x, g, tm):
    n = x.shape[0]
    return pl.pallas_call(
        _final_norm_kernel,
        grid=(n // tm,),
        in_specs=[pl.BlockSpec((tm, D_MODEL), lambda i: (i, 0)),
                  pl.BlockSpec((1, D_MODEL), lambda i: (0, 0))],
        out_specs=pl.BlockSpec((tm, D_MODEL), lambda i: (i, 0)),
        out_shape=jax.ShapeDtypeStruct((n, D_MODEL), F32),
        compiler_params=_cparams("parallel"),
        name="final_norm",
    )(x, g)


def _mlstm_kernel(qf_ref, kf_ref, vf_ref, gf_ref, qb_ref, kb_ref, vb_ref, gb_ref, bias_ref,
                  c0_ref, n0_ref, m0_ref,
                  hf_ref, hb_ref, c_out_ref, n_out_ref, m_out_ref,
                  c_scr, n_scr, m_scr):
    ci = pl.program_id(1)
    t = qf_ref.shape[0]

    @pl.when(ci == 0)
    def _():
        c_scr[...] = c0_ref[...]
        n_scr[...] = n0_ref[...]
        m_scr[...] = m0_ref[...]

    row = lax.broadcasted_iota(jnp.int32, (t, t), 0)
    col = lax.broadcasted_iota(jnp.int32, (t, t), 1)
    refs = ((qf_ref, kf_ref, vf_ref, gf_ref, hf_ref), (qb_ref, kb_ref, vb_ref, gb_ref, hb_ref))
    for d in range(2):
        q_ref, k_ref, v_ref, g_ref, h_ref = refs[d]
        keep = (col <= row) if d == 0 else (col >= row)
        cum = jnp.where(keep, 1.0, 0.0).astype(F32)
        gates = g_ref[...] + bias_ref[...]
        csum = _dot_exact(cum, jax.nn.log_sigmoid(gates))
        gates_t = gates.T
        csum_t = csum.T
        last = t - 1 if d == 0 else 0
        for hd in range(ML_HEADS):
            r = d * ML_HEADS + hd
            i_col = d * 2 * ML_HEADS + hd
            f_col = i_col + ML_HEADS
            b_c = csum[:, f_col:f_col + 1]
            b_r = csum_t[f_col:f_col + 1, :]
            li_c = gates[:, i_col:i_col + 1]
            li_r = gates_t[i_col:i_col + 1, :]
            m_st = m_scr[r:r + 1, 0:1]
            c_st = c_scr[r]
            n_st = n_scr[r:r + 1, :]
            sl = slice(hd * ML_DH, (hd + 1) * ML_DH)
            q = q_ref[:, sl]
            k = k_ref[:, sl] * (ML_DH ** -0.5)
            v = v_ref[:, sl]
            qb16 = q.astype(BF16)
            kb16 = k.astype(BF16)

            log_d = jnp.where(keep, b_c - b_r + li_r, -jnp.inf)
            log_inter = b_c + m_st
            m_t = jnp.maximum(log_inter, jnp.max(log_d, axis=1, keepdims=True))
            w_d = jnp.exp(log_d - m_t)
            w_inter = jnp.exp(log_inter - m_t)
            s = _dot_nt(qb16, kb16) * w_d
            num = _dot(s.astype(BF16), v.astype(BF16)) + w_inter * _dot_nt(qb16, c_st.astype(BF16))
            den = jnp.sum(s, axis=1, keepdims=True) + w_inter * jnp.sum(q * n_st, axis=1, keepdims=True)
            h_ref[:, sl] = num / jnp.maximum(jnp.abs(den), jnp.exp(-m_t))

            b_last = b_c[last:last + 1, :]
            log_w = b_last - b_c + li_c
            m_new = jnp.maximum(b_last + m_st, jnp.max(log_w, axis=0, keepdims=True))
            w_s = jnp.exp(log_w - m_new)
            w_c = jnp.exp(b_last + m_st - m_new)
            vw_t = (v * w_s).T.astype(BF16)
            c_scr[r] = w_c * c_st + _dot(vw_t, kb16)
            n_scr[r:r + 1, :] = w_c * n_st + jnp.sum(k * w_s, axis=0, keepdims=True)
            m_scr[r:r + 1, :] = jnp.broadcast_to(m_new, (1, LANES))

    @pl.when(ci == pl.num_programs(1) - 1)
    def _():
        c_out_ref[...] = c_scr[...]
        n_out_ref[...] = n_scr[...]
        m_out_ref[...] = m_scr[...]


def _mlstm(proj, row0, bsz, seq, bias, c0, n0, m0):
    tc = min(ML_CHUNK, seq)
    nc = seq // tc
    blk0 = row0 // tc
    r8 = 2 * ML_HEADS

    def fwd(colblk):
        return lambda b, c: (blk0 + b * nc + c, colblk)

    def bwd(colblk):
        return lambda b, c: (blk0 + b * nc + nc - 1 - c, colblk)

    def seqspecs(mk):
        return [pl.BlockSpec((tc, ML_W), mk(COL_ML_Q // ML_W)),
                pl.BlockSpec((tc, ML_W), mk(COL_ML_K // ML_W)),
                pl.BlockSpec((tc, ML_W), mk(COL_ML_V // ML_W)),
                pl.BlockSpec((tc, LANES), mk(COL_ML_IF // LANES))]

    state_specs = [pl.BlockSpec((None, r8, ML_DH, ML_DH), lambda b, c: (b, 0, 0, 0)),
                   pl.BlockSpec((None, r8, ML_DH), lambda b, c: (b, 0, 0)),
                   pl.BlockSpec((None, r8, LANES), lambda b, c: (b, 0, 0))]
    return pl.pallas_call(
        _mlstm_kernel,
        grid=(bsz, nc),
        in_specs=seqspecs(fwd) + seqspecs(bwd) + [pl.BlockSpec((1, LANES), lambda b, c: (0, 0))] + state_specs,
        out_specs=[pl.BlockSpec((tc, ML_W), lambda b, c: (b * nc + c, 0)),
                   pl.BlockSpec((tc, ML_W), lambda b, c: (b * nc + nc - 1 - c, 0))] + state_specs,
        out_shape=[jax.ShapeDtypeStruct((bsz * seq, ML_W), F32),
                   jax.ShapeDtypeStruct((bsz * seq, ML_W), F32),
                   jax.ShapeDtypeStruct((bsz, r8, ML_DH, ML_DH), F32),
                   jax.ShapeDtypeStruct((bsz, r8, ML_DH), F32),
                   jax.ShapeDtypeStruct((bsz, r8, LANES), F32)],
        scratch_shapes=[pltpu.VMEM((r8, ML_DH, ML_DH), F32),
                        pltpu.VMEM((r8, ML_DH), F32),
                        pltpu.VMEM((r8, LANES), F32)],
        compiler_params=_cparams("parallel", "arbitrary"),
        name="mlstm",
    )(proj, proj, proj, proj, proj, proj, proj, proj, bias, c0, n0, m0)


def _ml_post_kernel(hf_ref, hb_ref, o_ref, g_ref, y_ref):
    h = hf_ref[...] + hb_ref[...]
    for hd in range(ML_HEADS):
        sl = slice(hd * ML_DH, (hd + 1) * ML_DH)
        y_ref[:, sl] = (_rms(h[:, sl]) * g_ref[:, sl] * jax.nn.sigmoid(o_ref[:, sl])).astype(BF16)


def _ml_post(hf, hb, proj, row0, g, tm):
    n = hf.shape[0]
    blk0 = row0 // tm
    return pl.pallas_call(
        _ml_post_kernel,
        grid=(n // tm,),
        in_specs=[pl.BlockSpec((tm, ML_W), lambda i: (i, 0)),
                  pl.BlockSpec((tm, ML_W), lambda i: (i, 0)),
                  pl.BlockSpec((tm, ML_W), lambda i: (blk0 + i, COL_ML_O // ML_W)),
                  pl.BlockSpec((1, ML_W), lambda i: (0, 0))],
        out_specs=pl.BlockSpec((tm, ML_W), lambda i: (i, 0)),
        out_shape=jax.ShapeDtypeStruct((n, ML_W), BF16),
        compiler_params=_cparams("parallel"),
        name="ml_post",
    )(hf, hb, proj, g)


def _mla_prep_kernel(*refs, norm, rope, with_q):
    it = iter(refs)
    ckv_ref, kr_ref, g_ref, wkvb_ref = next(it), next(it), next(it), next(it)
    if with_q:
        qn_ref, qr_ref = next(it), next(it)
    if rope:
        cos_ref, sin_ref = next(it), next(it)
    k_ref, v_ref = next(it), next(it)
    if with_q:
        q_ref = next(it)
    if norm:
        ckv_out_ref = next(it)

    ckv = ckv_ref[...]
    if norm:
        ckv = _rms(ckv) * g_ref[...]
        ckv_out_ref[...] = ckv
    kv = _dot(ckv.astype(BF16), wkvb_ref[...])
    kr = kr_ref[...]
    if rope:
        kr = _rope_slab(kr, cos_ref[...], sin_ref[...])
    kr16 = kr[:, :MLA_ROPE].astype(BF16)
    hw = MLA_NOPE + MLA_V
    for hd in range(MLA_HEADS):
        k_ref[hd, :, 0:MLA_NOPE] = kv[:, hd * hw:hd * hw + MLA_NOPE].astype(BF16)
        k_ref[hd, :, MLA_NOPE:MLA_QK] = kr16
        v_ref[hd] = kv[:, hd * hw + MLA_NOPE:(hd + 1) * hw].astype(BF16)
    if with_q:
        scale = MLA_QK ** -0.5 * LOG2E
        qn = qn_ref[...]
        for sb in range(MLA_HEADS * MLA_ROPE // LANES):
            qr = qr_ref[:, sb * LANES:(sb + 1) * LANES]
            if rope:
                qr = _rope_slab(qr, cos_ref[...], sin_ref[...])
            for half in range(LANES // MLA_ROPE):
                hd = sb * (LANES // MLA_ROPE) + half
                q_ref[hd, :, MLA_NOPE:MLA_QK] = (qr[:, half * MLA_ROPE:(half + 1) * MLA_ROPE] * scale).astype(BF16)
        for hd in range(MLA_HEADS):
            q_ref[hd, :, 0:MLA_NOPE] = (qn[:, hd * MLA_NOPE:(hd + 1) * MLA_NOPE] * scale).astype(BF16)


def _mla_prep(ckv_src, kr_src, ckv_col, kr_col, row0, n, g, wkvb, tm, q_src=None, cos=None, sin=None, norm=True):
    rope = cos is not None
    with_q = q_src is not None
    blk0 = row0 // tm
    ins = [ckv_src, kr_src, g, wkvb]
    in_specs = [pl.BlockSpec((tm, MLA_KV_RANK), lambda i: (blk0 + i, ckv_col)),
                pl.BlockSpec((tm, LANES), lambda i: (blk0 + i, kr_col)),
                pl.BlockSpec((1, MLA_KV_RANK), lambda i: (0, 0)),
                pl.BlockSpec(wkvb.shape, lambda i: (0, 0))]
    if with_q:
        ins += [q_src, q_src]
        in_specs += [pl.BlockSpec((tm, MLA_HEADS * MLA_NOPE), lambda i: (blk0 + i, COL_MLA_QN // (MLA_HEADS * MLA_NOPE))),
                     pl.BlockSpec((tm, MLA_HEADS * MLA_ROPE), lambda i: (blk0 + i, COL_MLA_QR // (MLA_HEADS * MLA_ROPE)))]
    if rope:
        nt = cos.shape[0] // tm
        ins += [cos, sin]
        in_specs += [pl.BlockSpec((tm, LANES), lambda i: (i % nt, 0))] * 2
    out_shape = [jax.ShapeDtypeStruct((MLA_HEADS, n, MLA_QK), BF16),
                 jax.ShapeDtypeStruct((MLA_HEADS, n, MLA_V), BF16)]
    out_specs = [pl.BlockSpec((MLA_HEADS, tm, MLA_QK), lambda i: (0, i, 0)),
                 pl.BlockSpec((MLA_HEADS, tm, MLA_V), lambda i: (0, i, 0))]
    if with_q:
        out_shape.append(jax.ShapeDtypeStruct((MLA_HEADS, n, MLA_QK), BF16))
        out_specs.append(pl.BlockSpec((MLA_HEADS, tm, MLA_QK), lambda i: (0, i, 0)))
    if norm:
        out_shape.append(jax.ShapeDtypeStruct((n, MLA_KV_RANK), F32))
        out_specs.append(pl.BlockSpec((tm, MLA_KV_RANK), lambda i: (i, 0)))
    return pl.pallas_call(
        functools.partial(_mla_prep_kernel, norm=norm, rope=rope, with_q=with_q),
        grid=(n // tm,),
        in_specs=in_specs, out_specs=out_specs, out_shape=out_shape,
        compiler_params=_cparams("parallel"),
        name="mla_prep",
    )(*ins)


def _query_halves(q_ref):
    tq = min(ATTN_TQ, q_ref.shape[0])
    return [slice(i * tq, (i + 1) * tq) for i in range(q_ref.shape[0] // tq)]


def _mla_attn_kernel(*refs, cached):
    if cached:
        q_ref, k_ref, v_ref, kc_ref, vc_ref, o_ref, s_scr, p_scr = refs
        sources = ((k_ref, v_ref), (kc_ref, vc_ref))
    else:
        q_ref, k_ref, v_ref, o_ref, s_scr, p_scr = refs
        sources = ((k_ref, v_ref),)
    halves = _query_halves(q_ref)
    spans, off = [], 0
    for kk_ref, _ in sources:
        spans.append(slice(off, off + kk_ref.shape[0]))
        off += kk_ref.shape[0]
    for i, rows in enumerate(halves):
        for (kk_ref, _), span in zip(sources, spans):
            s_scr[i, :, span] = _dot_nt(q_ref[rows, :], kk_ref[...])
    for i, rows in enumerate(halves):
        s = s_scr[i]
        p = jnp.exp2(s - jnp.max(s, axis=-1, keepdims=True))
        r = 1.0 / jnp.sum(p, axis=-1, keepdims=True)
        p_scr[i] = p.astype(BF16)
        o = functools.reduce(jnp.add, [_dot(p_scr[i, :, span], vv_ref[...]) for (_, vv_ref), span in zip(sources, spans)])
        o_ref[rows, :] = (o * r).astype(BF16)


def _mla_attn(q, k, v, bsz, seq, k_c=None, v_c=None):
    tq = min(ATTN_STEP_Q, seq)
    nq = seq // tq
    cached = k_c is not None
    keys = seq
    ins = [q, k, v]
    in_specs = [pl.BlockSpec((None, tq, MLA_QK), lambda h, b, i: (h, b * nq + i, 0)),
                pl.BlockSpec((None, seq, MLA_QK), lambda h, b, i: (h, b, 0)),
                pl.BlockSpec((None, seq, MLA_V), lambda h, b, i: (h, b, 0))]
    if cached:
        past = k_c.shape[1] // bsz
        keys += past
        ins += [k_c, v_c]
        in_specs += [pl.BlockSpec((None, past, MLA_QK), lambda h, b, i: (h, b, 0)),
                     pl.BlockSpec((None, past, MLA_V), lambda h, b, i: (h, b, 0))]
    nsub = tq // min(ATTN_TQ, tq)
    return pl.pallas_call(
        functools.partial(_mla_attn_kernel, cached=cached),
        grid=(MLA_HEADS, bsz, nq),
        in_specs=in_specs,
        out_specs=pl.BlockSpec((tq, MLA_V), lambda h, b, i: (b * nq + i, h)),
        out_shape=jax.ShapeDtypeStruct((bsz * seq, MLA_HEADS * MLA_V), BF16),
        scratch_shapes=[pltpu.VMEM((nsub, tq // nsub, keys), F32),
                        pltpu.VMEM((nsub, tq // nsub, keys), BF16)],
        compiler_params=_cparams("parallel", "parallel", "parallel"),
        name="mla_attn",
    )(*ins)


def _rope_cast_kernel(x_ref, cos_ref, sin_ref, o_ref):
    for sb in range(x_ref.shape[1] // LANES):
        sl = slice(sb * LANES, (sb + 1) * LANES)
        o_ref[:, sl] = _rope_slab(x_ref[:, sl], cos_ref[...], sin_ref[...]).astype(BF16)


def _rope_cast(src, colblk, n, tm, cos, sin):
    nt = cos.shape[0] // tm
    return pl.pallas_call(
        _rope_cast_kernel,
        grid=(n // tm,),
        in_specs=[pl.BlockSpec((tm, DF_W), lambda i: (i, colblk))]
                 + [pl.BlockSpec((tm, LANES), lambda i: (i % nt, 0))] * 2,
        out_specs=pl.BlockSpec((tm, DF_W), lambda i: (i, 0)),
        out_shape=jax.ShapeDtypeStruct((n, DF_W), BF16),
        compiler_params=_cparams("parallel"),
        name="rope_cast",
    )(src, cos, sin)


def _diff_attn_kernel(*refs, lam_init, cached, rope):
    it = iter(refs)
    q_ref, k_ref, v_ref = next(it), next(it), next(it)
    sources = [(k_ref, v_ref)]
    if cached:
        sources.append((next(it), next(it)))
    if rope:
        cos_ref, sin_ref = next(it), next(it)
    lam_ref, g_ref, o_ref, s_scr, p_scr = next(it), next(it), next(it), next(it), next(it)
    lp = lam_ref[...]
    lam = (jnp.exp(jnp.sum(lp[0:1] * lp[1:2], axis=-1, keepdims=True))
           - jnp.exp(jnp.sum(lp[2:3] * lp[3:4], axis=-1, keepdims=True)) + lam_init)
    halves = _query_halves(q_ref)
    ks = [kk_ref[...].astype(BF16) for kk_ref, _ in sources]
    vs = [vv_ref[...].astype(BF16) for _, vv_ref in sources]
    spans, off = [], 0
    for kk_ref, _ in sources:
        spans.append(slice(off, off + kk_ref.shape[0]))
        off += kk_ref.shape[0]
    for i, rows in enumerate(halves):
        q = q_ref[rows, :]
        if rope:
            q = _rope_slab(q, cos_ref[rows, :], sin_ref[rows, :])
        q = (q * (DF_DQK ** -0.5 * LOG2E)).astype(BF16)
        lane = lax.broadcasted_iota(jnp.int32, q.shape, 1)
        zero = jnp.zeros_like(q)
        for comp, qc in enumerate((jnp.where(lane < DF_DQK, q, zero), jnp.where(lane >= DF_DQK, q, zero))):
            for k, span in zip(ks, spans):
                s_scr[comp, i, :, span] = _dot_nt(qc, k)
    for i, rows in enumerate(halves):
        r = []
        for comp in range(2):
            s = s_scr[comp, i]
            p = jnp.exp2(s - jnp.max(s, axis=-1, keepdims=True))
            r.append(1.0 / jnp.sum(p, axis=-1, keepdims=True))
            s_scr[comp, i] = p
        c = lam * r[1] / r[0]
        p_scr[i] = (s_scr[0, i] - c * s_scr[1, i]).astype(BF16)
        o = functools.reduce(jnp.add, [_dot(p_scr[i, :, span], v) for v, span in zip(vs, spans)]) * r[0]
        o_ref[rows, :] = (_rms(o) * g_ref[...] * (1.0 - lam_init)).astype(BF16)


def _diff_attn(proj, k, kcol, lam_p, g, lam_init, bsz, seq, k_c=None, v_c=None, cos=None, sin=None):
    tq = min(DIFF_STEP_Q, seq)
    nq = seq // tq
    nsub = tq // min(ATTN_TQ, tq)
    cached = k_c is not None
    rope = cos is not None
    keys = seq
    qcol, vcol = COL_DF_Q // DF_DV, COL_DF_V // DF_DV
    ins = [proj, k, proj]
    in_specs = [pl.BlockSpec((tq, DF_DV), lambda b, h, i: (b * nq + i, qcol + h)),
                pl.BlockSpec((seq, DF_DV), lambda b, h, i: (b, kcol + h)),
                pl.BlockSpec((seq, DF_DV), lambda b, h, i: (b, vcol + h))]
    if cached:
        past = k_c.shape[0] // bsz
        keys += past
        ins += [k_c, v_c]
        in_specs += [pl.BlockSpec((past, DF_DV), lambda b, h, i: (b, h)),
                     pl.BlockSpec((past, DF_DV), lambda b, h, i: (b, h))]
    if rope:
        ins += [cos, sin]
        in_specs += [pl.BlockSpec((tq, LANES), lambda b, h, i: (i, 0))] * 2
    ins += [lam_p, g]
    in_specs += [pl.BlockSpec((4, DF_DQK), lambda b, h, i: (0, 0)),
                 pl.BlockSpec((1, DF_DV), lambda b, h, i: (0, 0))]
    return pl.pallas_call(
        functools.partial(_diff_attn_kernel, lam_init=lam_init, cached=cached, rope=rope),
        grid=(bsz, DF_HEADS, nq),
        in_specs=in_specs,
        out_specs=pl.BlockSpec((tq, DF_DV), lambda b, h, i: (b * nq + i, h)),
        out_shape=jax.ShapeDtypeStruct((bsz * seq, DF_W), BF16),
        scratch_shapes=[pltpu.VMEM((2, nsub, tq // nsub, keys), F32),
                        pltpu.VMEM((nsub, tq // nsub, keys), BF16)],
        compiler_params=_cparams("parallel", "parallel", "parallel"),
        name="diff_attn",
    )(*ins)


def _s5_kernel(uf_ref, ub_ref, a_re_ref, a_im_ref, h0_re_ref, h0_im_ref,
               wb_re_ref, wb_im_ref, wc_re_ref, wc_im_ref,
               yf_ref, yb_ref, hr_out_ref, hi_out_ref,
               bu_re, bu_im, h_re, h_im):
    ci = pl.program_id(1)
    tc = uf_ref.shape[1]
    nseq = 2 * S5_SEQS
    gblk, sblk = wb_re_ref.shape[2:]
    ngb = S5_W // gblk

    @pl.when(ci == 0)
    def _():
        h_re[...] = h0_re_ref[...]
        h_im[...] = h0_im_ref[...]

    ri = lax.broadcasted_iota(jnp.int32, (tc, tc), 0)
    cj = lax.broadcasted_iota(jnp.int32, (tc, tc), 1)
    rev = jnp.where(ri + cj == tc - 1, 1.0, 0.0).astype(BF16)

    for d, u_ref in enumerate((uf_ref, ub_ref)):
        us = []
        for s in range(S5_SEQS):
            u = u_ref[s].astype(BF16)
            if d == 1:
                u = _dot(rev, u).astype(BF16)
            us.append(u)
        u_all = jnp.concatenate(us, axis=0)
        for gb in range(ngb):
            ug = u_all[:, gb * gblk:(gb + 1) * gblk]
            for w_ref, dst in ((wb_re_ref, bu_re), (wb_im_ref, bu_im)):
                bu = _dot(ug, w_ref[d, gb])
                for s in range(S5_SEQS):
                    for lk in range(sblk // LANES):
                        dst[gb * (sblk // LANES) + lk, pl.ds(d * S5_SEQS + s, tc, stride=nseq), :] = (
                            bu[s * tc:(s + 1) * tc, lk * LANES:(lk + 1) * LANES])

    nlk = S5_LANE_BLK // LANES
    for lb in range(S5_STATE // S5_LANE_BLK):
        lks = tuple(range(lb * nlk, (lb + 1) * nlk))
        ar = [a_re_ref[:, k * LANES:(k + 1) * LANES] for k in lks]
        ai = [a_im_ref[:, k * LANES:(k + 1) * LANES] for k in lks]

        def step(j, carry):
            r0 = pl.multiple_of(j * nseq, nseq)
            out = []
            for i, k in enumerate(lks):
                hr, hi = carry[2 * i], carry[2 * i + 1]
                nhr = ar[i] * hr - ai[i] * hi + bu_re[k, pl.ds(r0, nseq), :]
                nhi = ar[i] * hi + ai[i] * hr + bu_im[k, pl.ds(r0, nseq), :]
                bu_re[k, pl.ds(r0, nseq), :] = nhr
                bu_im[k, pl.ds(r0, nseq), :] = nhi
                out += [nhr, nhi]
            return tuple(out)

        init = []
        for k in lks:
            init += [h_re[:, k * LANES:(k + 1) * LANES], h_im[:, k * LANES:(k + 1) * LANES]]
        fin = lax.fori_loop(0, tc, step, tuple(init), unroll=8)
        for i, k in enumerate(lks):
            h_re[:, k * LANES:(k + 1) * LANES] = fin[2 * i]
            h_im[:, k * LANES:(k + 1) * LANES] = fin[2 * i + 1]

    def seq_states(src, s):
        parts = [src[k, pl.ds(s, tc, stride=nseq), :] for k in range(S5_STATE // LANES)]
        return jnp.concatenate(parts, axis=1).astype(BF16)

    def reverse_rows(y):
        hi = y.astype(BF16)
        r1 = y - hi.astype(F32)
        mid = r1.astype(BF16)
        lo = (r1 - mid.astype(F32)).astype(BF16)
        return _dot(rev, hi) + _dot(rev, mid) + _dot(rev, lo)

    for d, y_ref in enumerate((yf_ref, yb_ref)):
        hr_all = jnp.concatenate([seq_states(bu_re, d * S5_SEQS + s) for s in range(S5_SEQS)], axis=0)
        hi_all = jnp.concatenate([seq_states(bu_im, d * S5_SEQS + s) for s in range(S5_SEQS)], axis=0)
        for gb in range(ngb):
            ssl = slice(gb * sblk, (gb + 1) * sblk)
            y = _dot(hr_all[:, ssl], wc_re_ref[d, gb]) - _dot(hi_all[:, ssl], wc_im_ref[d, gb])
            for s in range(S5_SEQS):
                ys = y[s * tc:(s + 1) * tc]
                y_ref[s, :, gb * gblk:(gb + 1) * gblk] = reverse_rows(ys) if d == 1 else ys

    @pl.when(ci == pl.num_programs(1) - 1)
    def _():
        hr_out_ref[...] = h_re[...]
        hi_out_ref[...] = h_im[...]


def _s5(proj3, g0, bsz, seq, a_re, a_im, h0_re, h0_im, wb_re, wb_im, wc_re, wc_im):
    tc = min(S5_CHUNK, seq)
    nc = seq // tc
    ng = bsz // S5_SEQS
    nseq = 2 * S5_SEQS
    ucol = COL_S5_U // S5_W
    const4 = lambda g, c: (0, 0, 0, 0)
    wspec = lambda w: pl.BlockSpec(w.shape, const4, pipeline_mode=pl.Buffered(1))
    hspec = pl.BlockSpec((None, nseq, S5_STATE), lambda g, c: (g, 0, 0))
    return pl.pallas_call(
        _s5_kernel,
        grid=(ng, nc),
        in_specs=[pl.BlockSpec((S5_SEQS, tc, S5_W), lambda g, c: (g0 + g, c, ucol)),
                  pl.BlockSpec((S5_SEQS, tc, S5_W), lambda g, c: (g0 + g, nc - 1 - c, ucol)),
                  pl.BlockSpec((nseq, S5_STATE), lambda g, c: (0, 0)),
                  pl.BlockSpec((nseq, S5_STATE), lambda g, c: (0, 0)),
                  hspec, hspec, wspec(wb_re), wspec(wb_im), wspec(wc_re), wspec(wc_im)],
        out_specs=[pl.BlockSpec((S5_SEQS, tc, S5_W), lambda g, c: (g, c, 0)),
                   pl.BlockSpec((S5_SEQS, tc, S5_W), lambda g, c: (g, nc - 1 - c, 0)),
                   hspec, hspec],
        out_shape=[jax.ShapeDtypeStruct((bsz, seq, S5_W), F32),
                   jax.ShapeDtypeStruct((bsz, seq, S5_W), F32),
                   jax.ShapeDtypeStruct((ng, nseq, S5_STATE), F32),
                   jax.ShapeDtypeStruct((ng, nseq, S5_STATE), F32)],
        scratch_shapes=[pltpu.VMEM((S5_STATE // LANES, tc * nseq, LANES), F32),
                        pltpu.VMEM((S5_STATE // LANES, tc * nseq, LANES), F32),
                        pltpu.VMEM((nseq, S5_STATE), F32),
                        pltpu.VMEM((nseq, S5_STATE), F32)],
        compiler_params=_cparams("parallel", "arbitrary"),
        name="s5",
    )(proj3, proj3, a_re, a_im, h0_re, h0_im, wb_re, wb_im, wc_re, wc_im)


def _s5_post_kernel(yf_ref, yb_ref, u_ref, d_ref, w_ref, o_ref):
    y = (yf_ref[...] + yb_ref[...]) + d_ref[...] * u_ref[...]
    g = jax.nn.gelu(y)
    o_ref[...] = (g * jax.nn.sigmoid(_dot(g.astype(BF16), w_ref[...]))).astype(BF16)


def _s5_post(yf, yb, proj, row0, d, w, tm):
    n = yf.shape[0]
    blk0 = row0 // tm
    return pl.pallas_call(
        _s5_post_kernel,
        grid=(n // tm,),
        in_specs=[pl.BlockSpec((tm, S5_W), lambda i: (i, 0)),
                  pl.BlockSpec((tm, S5_W), lambda i: (i, 0)),
                  pl.BlockSpec((tm, S5_W), lambda i: (blk0 + i, COL_S5_U // S5_W)),
                  pl.BlockSpec((1, S5_W), lambda i: (0, 0)),
                  pl.BlockSpec((S5_W, S5_W), lambda i: (0, 0))],
        out_specs=pl.BlockSpec((tm, S5_W), lambda i: (i, 0)),
        out_shape=jax.ShapeDtypeStruct((n, S5_W), BF16),
        compiler_params=_cparams("parallel"),
        name="s5_post",
    )(yf, yb, proj, d, w)


def _s5_discretise(a_re, a_im, log_dt, b_re, b_im):
    lr = jnp.minimum(a_re, -1e-4)
    li = a_im
    dt = jnp.exp(log_dt)[..., None]
    mag = jnp.exp(dt * lr)
    ab_re, ab_im = mag * jnp.cos(dt * li), mag * jnp.sin(dt * li)
    den = lr * lr + li * li
    nr, ni = ab_re - 1.0, ab_im
    qr = (nr * lr + ni * li) / den
    qi = (ni * lr - nr * li) / den
    bb_re = qr[..., None] * b_re - qi[..., None] * b_im
    bb_im = qr[..., None] * b_im + qi[..., None] * b_re
    return ab_re, ab_im, bb_re, bb_im


def _block_diag(w):
    ngrp = S5_GROUPS_PER_TILE
    d, g, r, c = w.shape
    wg = w.reshape(d, g // ngrp, ngrp, r, c)
    eye = jnp.eye(ngrp, dtype=w.dtype)
    out = jnp.einsum('dbgrc,gh->dbgrhc', wg, eye)
    return out.reshape(d, g // ngrp, ngrp * r, ngrp * c)


def _merge_kernel(y0_ref, y1_ref, y2_ref, y3_ref, g0_ref, g1_ref, g2_ref, g3_ref, w_ref, o_ref):
    acc = None
    for b, (y_ref, g_ref) in enumerate(((y0_ref, g0_ref), (y1_ref, g1_ref), (y2_ref, g2_ref), (y3_ref, g3_ref))):
        term = jax.nn.sigmoid(g_ref[...]) * _dot(y_ref[...], w_ref[b])
        acc = term if acc is None else acc + term
    o_ref[...] = acc.astype(BF16)


def _merge(ys, proj, wb, tm):
    n = ys[0].shape[0]
    tn = 512
    gate0 = COL_GATE // tn
    per = D_MODEL // tn

    def gspec(b):
        return pl.BlockSpec((tm, tn), lambda i, j: (i, gate0 + b * per + j))

    return pl.pallas_call(
        _merge_kernel,
        grid=(n // tm, per),
        in_specs=[pl.BlockSpec((tm, BRANCH_W), lambda i, j: (i, 0))] * N_BRANCH
                 + [gspec(b) for b in range(N_BRANCH)]
                 + [pl.BlockSpec((N_BRANCH, BRANCH_W, tn), lambda i, j: (0, 0, j))],
        out_specs=pl.BlockSpec((tm, tn), lambda i, j: (i, j)),
        out_shape=jax.ShapeDtypeStruct((n, D_MODEL), BF16),
        compiler_params=_cparams("parallel", "parallel"),
        name="merge",
    )(*ys, proj, proj, proj, proj, wb)


def _rope_tables(n_tok):
    grid_rows = n_tok // GRID_W
    rows, cols = jnp.meshgrid(jnp.arange(grid_rows, dtype=F32), jnp.arange(GRID_W, dtype=F32), indexing='ij')
    quarter = ROPE_DIM // 4
    inv = ROPE_BASE ** (-jnp.arange(quarter, dtype=F32) / quarter)
    ang_r = rows.reshape(-1, 1) * inv
    ang_c = cols.reshape(-1, 1) * inv
    cos = jnp.concatenate([jnp.cos(ang_r)] * 2 + [jnp.cos(ang_c)] * 2, axis=-1)
    sin = jnp.concatenate([-jnp.sin(ang_r), jnp.sin(ang_r), -jnp.sin(ang_c), jnp.sin(ang_c)], axis=-1)
    return jnp.tile(cos, (1, LANES // ROPE_DIM)), jnp.tile(sin, (1, LANES // ROPE_DIM))


def _permute_w_in(w):
    sizes = (ML_W, ML_W, ML_W, ML_W, 4 * ML_HEADS, MLA_HEADS * MLA_QK, MLA_KV_RANK + MLA_ROPE, S5_W,
             DF_W, DF_W, DF_W, N_BRANCH * D_MODEL)
    splits = tuple(int(s) for s in np.cumsum(sizes)[:-1])
    (ml_q, ml_k, ml_v, ml_o, ml_if, mla_q, mla_kva, s5_u, df_q, df_k, df_v, gate) = jnp.split(w, splits, axis=1)
    mq = mla_q.reshape(D_MODEL, MLA_HEADS, MLA_QK)
    qn = mq[:, :, :MLA_NOPE].reshape(D_MODEL, MLA_HEADS * MLA_NOPE)
    qr = mq[:, :, MLA_NOPE:].reshape(D_MODEL, MLA_HEADS * MLA_ROPE)
    pad = lambda a, width: jnp.pad(a, ((0, 0), (0, width - a.shape[1])))
    cols = [ml_q, ml_k, ml_v, ml_o, s5_u, df_q, df_k, df_v, gate, qn, qr,
            mla_kva[:, :MLA_KV_RANK], pad(mla_kva[:, MLA_KV_RANK:], LANES), pad(ml_if, LANES)]
    out = jnp.concatenate(cols, axis=1)
    return pad(out, PROJ_COLS).astype(BF16)


def _seq_mixers(proj, row0, bsz, seq, lw, lam_init, state, cache, rope):
    n = bsz * seq
    tm = min(TM_SEQ, seq)
    ml_c0, ml_n0, ml_m0, s5_h0r, s5_h0i = state
    r8 = 2 * ML_HEADS

    m0 = jnp.broadcast_to(ml_m0.reshape(bsz, r8, 1), (bsz, r8, LANES))
    hf, hb, c_new, n_new, m_new = _mlstm(proj, row0, bsz, seq, lw['ml_bias'],
                                         ml_c0.reshape(bsz, r8, ML_DH, ML_DH), ml_n0.reshape(bsz, r8, ML_DH), m0)
    y_ml = _ml_post(hf, hb, proj, row0, lw['ml_norm'], tm)

    cos, sin = rope if rope is not None else (None, None)
    k_new, v_new, q_mla, ckv = _mla_prep(proj, proj, COL_MLA_CKV // MLA_KV_RANK, COL_MLA_KR // LANES, row0, n,
                                         lw['mla_kv_norm'], lw['mla_w_kvb'], tm, q_src=proj, cos=cos, sin=sin)
    k_c = v_c = None
    if cache is not None:
        ckv_c, krope_c, dk_c, dv_c = cache
        past = ckv_c.shape[1]
        kr_pad = jnp.pad(krope_c.reshape(bsz * past, MLA_ROPE), ((0, 0), (0, LANES - MLA_ROPE)))
        k_c, v_c = _mla_prep(ckv_c.reshape(bsz * past, MLA_KV_RANK), kr_pad, 0, 0, 0, bsz * past,
                             lw['mla_kv_norm'], lw['mla_w_kvb'], min(tm, bsz * past), norm=False)
    y_mla = _mla_attn(q_mla, k_new, v_new, bsz, seq, k_c, v_c)

    proj3 = proj.reshape(proj.shape[0] // seq, seq, PROJ_COLS)
    ng = bsz // S5_SEQS

    def pack_state(hs):
        return hs.reshape(ng, S5_SEQS, 2, S5_STATE).transpose(0, 2, 1, 3).reshape(ng, 2 * S5_SEQS, S5_STATE)

    def unpack_state(hs):
        return hs.reshape(ng, 2, S5_SEQS, S5_STATE).transpose(0, 2, 1, 3).reshape(bsz, 2, S5_GROUPS, S5_P)

    yf, yb, hr_new, hi_new = _s5(proj3, row0 // (seq * S5_SEQS), bsz, seq, lw['s5_a_re8'], lw['s5_a_im8'],
                                 pack_state(s5_h0r), pack_state(s5_h0i),
                                 lw['s5_wb_re'], lw['s5_wb_im'], lw['s5_wc_re'], lw['s5_wc_im'])
    y_s5 = _s5_post(yf.reshape(n, S5_W), yb.reshape(n, S5_W), proj, row0, lw['s5_d'], lw['s5_w_glu'], tm)

    if cos is not None:
        dk, kcol = _rope_cast(proj, COL_DF_K // DF_W, n, tm, cos, sin), 0
    else:
        dk, kcol = proj, COL_DF_K // DF_DV
    dk_cache = dv_cache = None
    if cache is not None:
        dk_cache = dk_c.reshape(bsz * past, DF_W)
        dv_cache = dv_c.reshape(bsz * past, DF_W)
    y_df = _diff_attn(proj, dk, kcol, lw['df_lambda'], lw['df_norm'], lam_init, bsz, seq,
                      dk_cache, dv_cache, cos, sin)

    cols = lambda c0, w: lax.slice(proj, (row0, c0), (row0 + n, c0 + w))
    new_ctx = (ckv.reshape(bsz, seq, MLA_KV_RANK),
               cols(COL_MLA_KR, MLA_ROPE).reshape(bsz, seq, MLA_ROPE),
               cols(COL_DF_K, DF_W).reshape(bsz, seq, DF_HEADS, 2 * DF_DQK),
               cols(COL_DF_V, DF_W).reshape(bsz, seq, DF_HEADS, DF_DV),
               c_new.reshape(bsz, 2, ML_HEADS, ML_DH, ML_DH), n_new.reshape(bsz, 2, ML_HEADS, ML_DH),
               m_new[:, :, 0].reshape(bsz, 2, ML_HEADS), unpack_state(hr_new), unpack_state(hi_new))
    return (y_ml, y_mla, y_s5, y_df), new_ctx


def kernel(x_prompt, x_sample, cache_mla_ckv, cache_mla_krope, cache_diff_k, cache_diff_v,
           state_mlstm_c, state_mlstm_n, state_mlstm_m, state_s5_re, state_s5_im, c,
           c_ctx, w_ada, b_ada, norm_mix, norm_ffn, w_in, ml_if_bias, ml_norm, mla_kv_norm,
           mla_w_kvb, s5_a_re, s5_a_im, s5_log_dt, s5_b_re, s5_b_im, s5_c_re, s5_c_im, s5_d,
           s5_w_glu, df_lambda, df_norm, w_branch, w_o, w_ffn_in, w_ffn_out, final_norm):
    bp, sp, _ = x_prompt.shape
    bs, ss, _ = x_sample.shape
    n_p, n_s = bp * sp, bs * ss
    tm = TM_DENSE

    cond = jnp.concatenate([c, c_ctx[None, :], jnp.zeros((SUBLANES - 1 - bs, D_MODEL), F32)], axis=0)
    mods = _ada(cond, w_ada, b_ada).reshape(DEPTH, SUBLANES, 6, 1, D_MODEL)
    rope = _rope_tables(ss)
    latent_row = lambda i: (i * tm) // ss
    prompt_row = lambda i: bs

    xs = x_sample.reshape(n_s, D_MODEL)
    xp = x_prompt.reshape(n_p, D_MODEL)
    zeros_state = (jnp.zeros((bp, 2, ML_HEADS, ML_DH, ML_DH), F32), jnp.zeros((bp, 2, ML_HEADS, ML_DH), F32),
                   jnp.zeros((bp, 2, ML_HEADS), F32), jnp.zeros((bp, 2, S5_GROUPS, S5_P), F32),
                   jnp.zeros((bp, 2, S5_GROUPS, S5_P), F32))
    ctx_out = []
    for l in range(DEPTH):
        lam_init = 0.8 - 0.6 * math.exp(-0.3 * l)
        ab_re, ab_im, bb_re, bb_im = _s5_discretise(s5_a_re[l], s5_a_im[l], s5_log_dt[l], s5_b_re[l], s5_b_im[l])
        rep = lambda a: jnp.repeat(a.reshape(2, S5_STATE), S5_SEQS, axis=0)
        lw = {
            'ml_bias': jnp.pad(ml_if_bias[l].reshape(1, 4 * ML_HEADS), ((0, 0), (0, LANES - 4 * ML_HEADS))),
            'ml_norm': ml_norm[l].reshape(1, ML_W),
            'mla_kv_norm': mla_kv_norm[l].reshape(1, MLA_KV_RANK),
            'mla_w_kvb': mla_w_kvb[l].astype(BF16),
            's5_a_re8': rep(ab_re), 's5_a_im8': rep(ab_im),
            's5_wb_re': _block_diag(bb_re.transpose(0, 1, 3, 2)).astype(BF16),
            's5_wb_im': _block_diag(bb_im.transpose(0, 1, 3, 2)).astype(BF16),
            's5_wc_re': _block_diag(s5_c_re[l].transpose(0, 1, 3, 2)).astype(BF16),
            's5_wc_im': _block_diag(s5_c_im[l].transpose(0, 1, 3, 2)).astype(BF16),
            's5_d': s5_d[l].reshape(1, S5_W),
            's5_w_glu': s5_w_glu[l].astype(BF16),
            'df_lambda': df_lambda[l],
            'df_norm': df_norm[l].reshape(1, DF_DV),
        }
        mod = mods[l]
        w_in_l = _permute_w_in(w_in[l])
        w_branch_l, w_o_l = w_branch[l].astype(BF16), w_o[l].astype(BF16)
        w_ffn_in_l, w_ffn_out_l = w_ffn_in[l].astype(BF16), w_ffn_out[l].astype(BF16)
        g_mix, g_ffn = norm_mix[l].reshape(1, D_MODEL), norm_ffn[l].reshape(1, D_MODEL)

        def layer(x, row_of_tile, bsz, seq, state, cache, rope_tabs):
            proj = _proj_in(x, g_mix, mod, row_of_tile, w_in_l, tm)
            ys, new_ctx = _seq_mixers(proj, 0, bsz, seq, lw, lam_init, state, cache, rope_tabs)
            merged = _merge(list(ys), proj, w_branch_l, tm)
            x = _matmul_resid(merged, w_o_l, x, mod, 2, row_of_tile, tm)
            act = _ffn_in(x, g_ffn, mod, row_of_tile, w_ffn_in_l, tm)
            return _matmul_resid(act, w_ffn_out_l, x, mod, 5, row_of_tile, tm), new_ctx

        xp, new_ctx = layer(xp, prompt_row, bp, sp, zeros_state, None, None)
        ctx_out.append(new_ctx)
        state = (state_mlstm_c[:, l], state_mlstm_n[:, l], state_mlstm_m[:, l], state_s5_re[:, l], state_s5_im[:, l])
        cache = (cache_mla_ckv[:, l], cache_mla_krope[:, l], cache_diff_k[:, l], cache_diff_v[:, l])
        xs, _ = layer(xs, latent_row, bs, ss, state, cache, rope)

    g_fin = final_norm.reshape(1, D_MODEL)
    y_prompt = _final_norm(xp, g_fin, tm).reshape(bp, sp, D_MODEL)
    y_sample = _final_norm(xs, g_fin, tm).reshape(bs, ss, D_MODEL)
    stacked = tuple(jnp.stack([ctx[k] for ctx in ctx_out], axis=1) for k in range(9))
    return (y_prompt, y_sample) + stacked
```

```python
import functools
import math

import jax
import jax.numpy as jnp
import numpy as np
from jax import lax
from jax.experimental import pallas as pl
from jax.experimental.pallas import tpu as pltpu

F32 = jnp.float32
BF16 = jnp.bfloat16

D_MODEL = 2048
DEPTH = 2
GRID_W = 64
ROPE_DIM = 64
ROPE_BASE = 10000.0
RMS_EPS = 1e-6
ML_HEADS = 4
ML_DH = 256
ML_W = ML_HEADS * ML_DH
MLA_HEADS = 8
MLA_NOPE = 128
MLA_ROPE = ROPE_DIM
MLA_V = 128
MLA_KV_RANK = 512
MLA_QK = MLA_NOPE + MLA_ROPE
S5_GROUP = 16
S5_GROUPS = 64
S5_W = S5_GROUPS * S5_GROUP
S5_P = 64
S5_STATE = S5_GROUPS * S5_P
DF_HEADS = 8
DF_DQK = ROPE_DIM
DF_DV = 2 * DF_DQK
DF_W = DF_HEADS * DF_DV
N_BRANCH = 4
BRANCH_W = 1024
FFN_HIDDEN = (8 * D_MODEL + 3 * 256 - 1) // (3 * 256) * 256

LANES = 128
SUBLANES = 8
VMEM_LIMIT_BYTES = 56 * 1024 * 1024

COL_ML_Q = 0
COL_ML_K = 1024
COL_ML_V = 2048
COL_ML_O = 3072
COL_S5_U = 4096
COL_DF_Q = 5120
COL_DF_K = 6144
COL_DF_V = 7168
COL_GATE = 8192
COL_MLA_QN = 16384
COL_MLA_QR = 17408
COL_MLA_CKV = 17920
COL_MLA_KR = 18432
COL_ML_IF = 18560
PROJ_TN = 1024
PROJ_COLS = 19456

LOG2E = math.log2(math.e)
VT_ROWS = 128 + 16

TM_DENSE = 1024
TM_SEQ = 512
DIFF_STEP_Q = 512
ATTN_STEP_Q = 1024
ML_CHUNK = 256
S5_CHUNK = 64
S5_SEQS = 4
S5_LANE_BLK = 512
S5_GROUPS_PER_TILE = 16
ATTN_TQ = 256


def _cparams(*sem):
    return pltpu.CompilerParams(dimension_semantics=sem, vmem_limit_bytes=VMEM_LIMIT_BYTES)


def _dot(a, b):
    return jnp.dot(a, b, preferred_element_type=F32)


def _dot_nt(a, b):
    return lax.dot_general(a, b, (((1,), (1,)), ((), ())), preferred_element_type=F32)


def _dot_exact(a, b):
    return jnp.dot(a, b, preferred_element_type=F32, precision=lax.Precision.HIGHEST)


def _rms(x):
    return x * lax.rsqrt(jnp.mean(x * x, axis=-1, keepdims=True) + RMS_EPS)


def _ones_row_block(width):
    row = lax.broadcasted_iota(jnp.int32, (VT_ROWS - 128, width), 0)
    return jnp.where(row == 0, 1.0, 0.0).astype(BF16)


def _rope_slab(x, cos, sin):
    lane = lax.broadcasted_iota(jnp.int32, x.shape, 1)
    partner = jnp.where((lane % 32) < 16, pltpu.roll(x, LANES - 16, 1), pltpu.roll(x, 16, 1))
    return x * cos + partner * sin


def _ada_kernel(c_ref, w_ref, b_ref, o_ref):
    c = c_ref[...]
    s = c * jax.nn.sigmoid(c)
    o_ref[...] = _dot(s.astype(BF16), w_ref[...].astype(BF16)) + b_ref[...]


def _ada(cond, w_ada, b_ada):
    rows = cond.shape[0]
    tn = 1024
    return pl.pallas_call(
        _ada_kernel,
        grid=(DEPTH, 6 * D_MODEL // tn),
        in_specs=[pl.BlockSpec((rows, D_MODEL), lambda l, j: (0, 0)),
                  pl.BlockSpec((None, D_MODEL, tn), lambda l, j: (l, 0, j)),
                  pl.BlockSpec((None, 1, tn), lambda l, j: (l, 0, j))],
        out_specs=pl.BlockSpec((None, rows, tn), lambda l, j: (l, 0, j)),
        out_shape=jax.ShapeDtypeStruct((DEPTH, rows, 6 * D_MODEL), F32),
        compiler_params=_cparams("parallel", "parallel"),
        name="ada",
    )(cond, w_ada, b_ada.reshape(DEPTH, 1, 6 * D_MODEL))


def _mod_spec(which, row_of_tile):
    return pl.BlockSpec((None, None, 1, D_MODEL), lambda i, j: (row_of_tile(i), which, 0, 0))


def _norm_mod(x_ref, g_ref, sc_ref, sh_ref):
    return (_rms(x_ref[...]) * g_ref[...]) * (1.0 + sc_ref[...]) + sh_ref[...]


def _proj_in_kernel(x_ref, g_ref, sc_ref, sh_ref, w_ref, o_ref, h_ref):
    @pl.when(pl.program_id(1) == 0)
    def _():
        h_ref[...] = _norm_mod(x_ref, g_ref, sc_ref, sh_ref).astype(BF16)

    o_ref[...] = _dot(h_ref[...], w_ref[...])


def _proj_in(x, g, mod, row_of_tile, w, tm):
    n = x.shape[0]
    ncol = w.shape[1]
    return pl.pallas_call(
        _proj_in_kernel,
        grid=(n // tm, ncol // PROJ_TN),
        in_specs=[pl.BlockSpec((tm, D_MODEL), lambda i, j: (i, 0), pipeline_mode=pl.Buffered(1)),
                  pl.BlockSpec((1, D_MODEL), lambda i, j: (0, 0)),
                  _mod_spec(1, row_of_tile), _mod_spec(0, row_of_tile),
                  pl.BlockSpec((D_MODEL, PROJ_TN), lambda i, j: (0, j))],
        out_specs=pl.BlockSpec((tm, PROJ_TN), lambda i, j: (i, j)),
        out_shape=jax.ShapeDtypeStruct((n, ncol), F32),
        scratch_shapes=[pltpu.VMEM((tm, D_MODEL), BF16)],
        compiler_params=_cparams("parallel", "arbitrary"),
        name="proj_in",
    )(x, g, mod, mod, w)


def _ffn_in_kernel(x_ref, g_ref, sc_ref, sh_ref, wa_ref, wb_ref, o_ref, h_ref):
    @pl.when(pl.program_id(1) == 0)
    def _():
        h_ref[...] = _norm_mod(x_ref, g_ref, sc_ref, sh_ref).astype(BF16)

    h = h_ref[...]
    a = _dot(h, wa_ref[...])
    b = _dot(h, wb_ref[...])
    o_ref[...] = (a * jax.nn.sigmoid(a) * b).astype(BF16)


def _ffn_in(x, g, mod, row_of_tile, w, tm):
    n = x.shape[0]
    tn = 512
    nj = FFN_HIDDEN // tn
    return pl.pallas_call(
        _ffn_in_kernel,
        grid=(n // tm, nj),
        in_specs=[pl.BlockSpec((tm, D_MODEL), lambda i, j: (i, 0)),
                  pl.BlockSpec((1, D_MODEL), lambda i, j: (0, 0)),
                  _mod_spec(4, row_of_tile), _mod_spec(3, row_of_tile),
                  pl.BlockSpec((D_MODEL, tn), lambda i, j: (0, j)),
                  pl.BlockSpec((D_MODEL, tn), lambda i, j: (0, nj + j))],
        out_specs=pl.BlockSpec((tm, tn), lambda i, j: (i, j)),
        out_shape=jax.ShapeDtypeStruct((n, FFN_HIDDEN), BF16),
        scratch_shapes=[pltpu.VMEM((tm, D_MODEL), BF16)],
        compiler_params=_cparams("parallel", "arbitrary"),
        name="ffn_in",
    )(x, g, mod, mod, w, w)


def _resid_kernel(a_ref, w_ref, x_ref, gate_ref, o_ref):
    o_ref[...] = x_ref[...] + gate_ref[...] * _dot(a_ref[...], w_ref[...])


def _matmul_resid(a, w, x, mod, which, row_of_tile, tm):
    n, kdim = a.shape
    tn = 512
    return pl.pallas_call(
        _resid_kernel,
        grid=(n // tm, D_MODEL // tn),
        in_specs=[pl.BlockSpec((tm, kdim), lambda i, j: (i, 0)),
                  pl.BlockSpec((kdim, tn), lambda i, j: (0, j)),
                  pl.BlockSpec((tm, tn), lambda i, j: (i, j)),
                  pl.BlockSpec((None, None, 1, tn), lambda i, j: (row_of_tile(i), which, 0, j))],
        out_specs=pl.BlockSpec((tm, tn), lambda i, j: (i, j)),
        out_shape=jax.ShapeDtypeStruct((n, D_MODEL), F32),
        compiler_params=_cparams("parallel", "parallel"),
        name="matmul_resid",
    )(a, w, x, mod)


def _final_norm_kernel(x_ref, g_ref, o_ref):
    o_ref[...] = _rms(x_ref[...]) * g_ref[...]


def _final_norm(x, g, tm):
    n = x.shape[0]
    return pl.pallas_call(
        _final_norm_kernel,
        grid=(n // tm,),
        in_specs=[pl.BlockSpec((tm, D_MODEL), lambda i: (i, 0)),
                  pl.BlockSpec((1, D_MODEL), lambda i: (0, 0))],
        out_specs=pl.BlockSpec((tm, D_MODEL), lambda i: (i, 0)),
        out_shape=jax.ShapeDtypeStruct((n, D_MODEL), F32),
        compiler_params=_cparams("parallel"),
        name="final_norm",
    )(x, g)


def _mlstm_kernel(qf_ref, kf_ref, vf_ref, gf_ref, qb_ref, kb_ref, vb_ref, gb_ref, bias_ref,
                  c0_ref, n0_ref, m0_ref,
                  hf_ref, hb_ref, c_out_ref, n_out_ref, m_out_ref,
                  c_scr, n_scr, m_scr):
    ci = pl.program_id(1)
    t = qf_ref.shape[0]

    @pl.when(ci == 0)
    def _():
        c_scr[...] = c0_ref[...]
        n_scr[...] = n0_ref[...]
        m_scr[...] = m0_ref[...]

    row = lax.broadcasted_iota(jnp.int32, (t, t), 0)
    col = lax.broadcasted_iota(jnp.int32, (t, t), 1)
    refs = ((qf_ref, kf_ref, vf_ref, gf_ref, hf_ref), (qb_ref, kb_ref, vb_ref, gb_ref, hb_ref))
    for d in range(2):
        q_ref, k_ref, v_ref, g_ref, h_ref = refs[d]
        keep = (col <= row) if d == 0 else (col >= row)
        cum = jnp.where(keep, 1.0, 0.0).astype(F32)
        gates = g_ref[...] + bias_ref[...]
        csum = _dot_exact(cum, jax.nn.log_sigmoid(gates))
        gates_t = gates.T
        csum_t = csum.T
        last = t - 1 if d == 0 else 0
        for hd in range(ML_HEADS):
            r = d * ML_HEADS + hd
            i_col = d * 2 * ML_HEADS + hd
            f_col = i_col + ML_HEADS
            b_c = csum[:, f_col:f_col + 1]
            b_r = csum_t[f_col:f_col + 1, :]
            li_c = gates[:, i_col:i_col + 1]
            li_r = gates_t[i_col:i_col + 1, :]
            m_st = m_scr[r:r + 1, 0:1]
            c_st = c_scr[r]
            n_st = n_scr[r:r + 1, :]
            sl = slice(hd * ML_DH, (hd + 1) * ML_DH)
            q = q_ref[:, sl]
            k = k_ref[:, sl] * (ML_DH ** -0.5)
            v = v_ref[:, sl]
            qb16 = q.astype(BF16)
            kb16 = k.astype(BF16)

            log_d = jnp.where(keep, b_c - b_r + li_r, -jnp.inf)
            log_inter = b_c + m_st
            m_t = jnp.maximum(log_inter, jnp.max(log_d, axis=1, keepdims=True))
            w_d = jnp.exp(log_d - m_t)
            w_inter = jnp.exp(log_inter - m_t)
            s = _dot_nt(qb16, kb16) * w_d
            num = _dot(s.astype(BF16), v.astype(BF16)) + w_inter * _dot_nt(qb16, c_st.astype(BF16))
            den = jnp.sum(s, axis=1, keepdims=True) + w_inter * jnp.sum(q * n_st, axis=1, keepdims=True)
            h_ref[:, sl] = num / jnp.maximum(jnp.abs(den), jnp.exp(-m_t))

            b_last = b_c[last:last + 1, :]
            log_w = b_last - b_c + li_c
            m_new = jnp.maximum(b_last + m_st, jnp.max(log_w, axis=0, keepdims=True))
            w_s = jnp.exp(log_w - m_new)
            w_c = jnp.exp(b_last + m_st - m_new)
            vw_t = (v * w_s).T.astype(BF16)
            c_scr[r] = w_c * c_st + _dot(vw_t, kb16)
            n_scr[r:r + 1, :] = w_c * n_st + jnp.sum(k * w_s, axis=0, keepdims=True)
            m_scr[r:r + 1, :] = jnp.broadcast_to(m_new, (1, LANES))

    @pl.when(ci == pl.num_programs(1) - 1)
    def _():
        c_out_ref[...] = c_scr[...]
        n_out_ref[...] = n_scr[...]
        m_out_ref[...] = m_scr[...]


def _mlstm(proj, row0, bsz, seq, bias, c0, n0, m0):
    tc = min(ML_CHUNK, seq)
    nc = seq // tc
    blk0 = row0 // tc
    r8 = 2 * ML_HEADS

    def fwd(colblk):
        return lambda b, c: (blk0 + b * nc + c, colblk)

    def bwd(colblk):
        return lambda b, c: (blk0 + b * nc + nc - 1 - c, colblk)

    def seqspecs(mk):
        return [pl.BlockSpec((tc, ML_W), mk(COL_ML_Q // ML_W)),
                pl.BlockSpec((tc, ML_W), mk(COL_ML_K // ML_W)),
                pl.BlockSpec((tc, ML_W), mk(COL_ML_V // ML_W)),
                pl.BlockSpec((tc, LANES), mk(COL_ML_IF // LANES))]

    state_specs = [pl.BlockSpec((None, r8, ML_DH, ML_DH), lambda b, c: (b, 0, 0, 0)),
                   pl.BlockSpec((None, r8, ML_DH), lambda b, c: (b, 0, 0)),
                   pl.BlockSpec((None, r8, LANES), lambda b, c: (b, 0, 0))]
    return pl.pallas_call(
        _mlstm_kernel,
        grid=(bsz, nc),
        in_specs=seqspecs(fwd) + seqspecs(bwd) + [pl.BlockSpec((1, LANES), lambda b, c: (0, 0))] + state_specs,
        out_specs=[pl.BlockSpec((tc, ML_W), lambda b, c: (b * nc + c, 0)),
                   pl.BlockSpec((tc, ML_W), lambda b, c: (b * nc + nc - 1 - c, 0))] + state_specs,
        out_shape=[jax.ShapeDtypeStruct((bsz * seq, ML_W), F32),
                   jax.ShapeDtypeStruct((bsz * seq, ML_W), F32),
                   jax.ShapeDtypeStruct((bsz, r8, ML_DH, ML_DH), F32),
                   jax.ShapeDtypeStruct((bsz, r8, ML_DH), F32),
                   jax.ShapeDtypeStruct((bsz, r8, LANES), F32)],
        scratch_shapes=[pltpu.VMEM((r8, ML_DH, ML_DH), F32),
                        pltpu.VMEM((r8, ML_DH), F32),
                        pltpu.VMEM((r8, LANES), F32)],
        compiler_params=_cparams("parallel", "arbitrary"),
        name="mlstm",
    )(proj, proj, proj, proj, proj, proj, proj, proj, bias, c0, n0, m0)


def _ml_post_kernel(hf_ref, hb_ref, o_ref, g_ref, y_ref):
    h = hf_ref[...] + hb_ref[...]
    for hd in range(ML_HEADS):
        sl = slice(hd * ML_DH, (hd + 1) * ML_DH)
        y_ref[:, sl] = (_rms(h[:, sl]) * g_ref[:, sl] * jax.nn.sigmoid(o_ref[:, sl])).astype(BF16)


def _ml_post(hf, hb, proj, row0, g, tm):
    n = hf.shape[0]
    blk0 = row0 // tm
    return pl.pallas_call(
        _ml_post_kernel,
        grid=(n // tm,),
        in_specs=[pl.BlockSpec((tm, ML_W), lambda i: (i, 0)),
                  pl.BlockSpec((tm, ML_W), lambda i: (i, 0)),
                  pl.BlockSpec((tm, ML_W), lambda i: (blk0 + i, COL_ML_O // ML_W)),
                  pl.BlockSpec((1, ML_W), lambda i: (0, 0))],
        out_specs=pl.BlockSpec((tm, ML_W), lambda i: (i, 0)),
        out_shape=jax.ShapeDtypeStruct((n, ML_W), BF16),
        compiler_params=_cparams("parallel"),
        name="ml_post",
    )(hf, hb, proj, g)


def _mla_prep_kernel(*refs, norm, rope, with_q):
    it = iter(refs)
    ckv_ref, kr_ref, g_ref, wkvb_ref = next(it), next(it), next(it), next(it)
    if with_q:
        qn_ref, qr_ref = next(it), next(it)
    if rope:
        cos_ref, sin_ref = next(it), next(it)
    k_ref, v_ref = next(it), next(it)
    if with_q:
        q_ref = next(it)
    if norm:
        ckv_out_ref = next(it)

    ckv = ckv_ref[...]
    if norm:
        ckv = _rms(ckv) * g_ref[...]
        ckv_out_ref[...] = ckv
    kv = _dot(ckv.astype(BF16), wkvb_ref[...])
    kr = kr_ref[...]
    if rope:
        kr = _rope_slab(kr, cos_ref[...], sin_ref[...])
    kr16 = kr[:, :MLA_ROPE].astype(BF16)
    hw = MLA_NOPE + MLA_V
    for hd in range(MLA_HEADS):
        k_ref[hd, :, 0:MLA_NOPE] = kv[:, hd * hw:hd * hw + MLA_NOPE].astype(BF16)
        k_ref[hd, :, MLA_NOPE:MLA_QK] = kr16
        v_ref[hd, 0:MLA_V, :] = kv[:, hd * hw + MLA_NOPE:(hd + 1) * hw].T.astype(BF16)
        v_ref[hd, MLA_V:, :] = _ones_row_block(k_ref.shape[1])
    if with_q:
        scale = MLA_QK ** -0.5 * LOG2E
        qn = qn_ref[...]
        for sb in range(MLA_HEADS * MLA_ROPE // LANES):
            qr = qr_ref[:, sb * LANES:(sb + 1) * LANES]
            if rope:
                qr = _rope_slab(qr, cos_ref[...], sin_ref[...])
            for half in range(LANES // MLA_ROPE):
                hd = sb * (LANES // MLA_ROPE) + half
                q_ref[hd, :, MLA_NOPE:MLA_QK] = (qr[:, half * MLA_ROPE:(half + 1) * MLA_ROPE] * scale).astype(BF16)
        for hd in range(MLA_HEADS):
            q_ref[hd, :, 0:MLA_NOPE] = (qn[:, hd * MLA_NOPE:(hd + 1) * MLA_NOPE] * scale).astype(BF16)


def _mla_prep(ckv_src, kr_src, ckv_col, kr_col, row0, n, g, wkvb, tm, q_src=None, cos=None, sin=None, norm=True):
    rope = cos is not None
    with_q = q_src is not None
    blk0 = row0 // tm
    ins = [ckv_src, kr_src, g, wkvb]
    in_specs = [pl.BlockSpec((tm, MLA_KV_RANK), lambda i: (blk0 + i, ckv_col)),
                pl.BlockSpec((tm, LANES), lambda i: (blk0 + i, kr_col)),
                pl.BlockSpec((1, MLA_KV_RANK), lambda i: (0, 0)),
                pl.BlockSpec(wkvb.shape, lambda i: (0, 0))]
    if with_q:
        ins += [q_src, q_src]
        in_specs += [pl.BlockSpec((tm, MLA_HEADS * MLA_NOPE), lambda i: (blk0 + i, COL_MLA_QN // (MLA_HEADS * MLA_NOPE))),
                     pl.BlockSpec((tm, MLA_HEADS * MLA_ROPE), lambda i: (blk0 + i, COL_MLA_QR // (MLA_HEADS * MLA_ROPE)))]
    if rope:
        nt = cos.shape[0] // tm
        ins += [cos, sin]
        in_specs += [pl.BlockSpec((tm, LANES), lambda i: (i % nt, 0))] * 2
    out_shape = [jax.ShapeDtypeStruct((MLA_HEADS, n, MLA_QK), BF16),
                 jax.ShapeDtypeStruct((MLA_HEADS, VT_ROWS, n), BF16)]
    out_specs = [pl.BlockSpec((MLA_HEADS, tm, MLA_QK), lambda i: (0, i, 0)),
                 pl.BlockSpec((MLA_HEADS, VT_ROWS, tm), lambda i: (0, 0, i))]
    if with_q:
        out_shape.append(jax.ShapeDtypeStruct((MLA_HEADS, n, MLA_QK), BF16))
        out_specs.append(pl.BlockSpec((MLA_HEADS, tm, MLA_QK), lambda i: (0, i, 0)))
    if norm:
        out_shape.append(jax.ShapeDtypeStruct((n, MLA_KV_RANK), F32))
        out_specs.append(pl.BlockSpec((tm, MLA_KV_RANK), lambda i: (i, 0)))
    return pl.pallas_call(
        functools.partial(_mla_prep_kernel, norm=norm, rope=rope, with_q=with_q),
        grid=(n // tm,),
        in_specs=in_specs, out_specs=out_specs, out_shape=out_shape,
        compiler_params=_cparams("parallel"),
        name="mla_prep",
    )(*ins)


def _query_halves(q_ref):
    tq = min(ATTN_TQ, q_ref.shape[0])
    return [slice(i * tq, (i + 1) * tq) for i in range(q_ref.shape[0] // tq)]


def _mla_attn_kernel(*refs, cached):
    if cached:
        q_ref, k_ref, v_ref, kc_ref, vc_ref, o_ref, s_scr, p_scr = refs
        sources = ((k_ref, v_ref), (kc_ref, vc_ref))
    else:
        q_ref, k_ref, v_ref, o_ref, s_scr, p_scr = refs
        sources = ((k_ref, v_ref),)
    halves = _query_halves(q_ref)
    spans, off = [], 0
    for kk_ref, _ in sources:
        spans.append(slice(off, off + kk_ref.shape[0]))
        off += kk_ref.shape[0]
    for i, rows in enumerate(halves):
        for (kk_ref, _), span in zip(sources, spans):
            s_scr[i, span, :] = _dot_nt(kk_ref[...], q_ref[rows, :])
    for i, rows in enumerate(halves):
        s = s_scr[i]
        p_scr[i] = jnp.exp2(s - jnp.max(s, axis=0, keepdims=True)).astype(BF16)
        oe = functools.reduce(jnp.add, [_dot(vv_ref[...], p_scr[i, span, :]) for (_, vv_ref), span in zip(sources, spans)])
        o = oe[0:MLA_V] / oe[MLA_V:MLA_V + 1]
        o_ref[rows, :] = o.T.astype(BF16)


def _mla_attn(q, k, v, bsz, seq, k_c=None, v_c=None):
    tq = min(ATTN_STEP_Q, seq)
    nq = seq // tq
    cached = k_c is not None
    keys = seq
    ins = [q, k, v]
    in_specs = [pl.BlockSpec((None, tq, MLA_QK), lambda h, b, i: (h, b * nq + i, 0)),
                pl.BlockSpec((None, seq, MLA_QK), lambda h, b, i: (h, b, 0)),
                pl.BlockSpec((None, VT_ROWS, seq), lambda h, b, i: (h, 0, b))]
    if cached:
        past = k_c.shape[1] // bsz
        keys += past
        ins += [k_c, v_c]
        in_specs += [pl.BlockSpec((None, past, MLA_QK), lambda h, b, i: (h, b, 0)),
                     pl.BlockSpec((None, VT_ROWS, past), lambda h, b, i: (h, 0, b))]
    nsub = tq // min(ATTN_TQ, tq)
    return pl.pallas_call(
        functools.partial(_mla_attn_kernel, cached=cached),
        grid=(MLA_HEADS, bsz, nq),
        in_specs=in_specs,
        out_specs=pl.BlockSpec((tq, MLA_V), lambda h, b, i: (b * nq + i, h)),
        out_shape=jax.ShapeDtypeStruct((bsz * seq, MLA_HEADS * MLA_V), BF16),
        scratch_shapes=[pltpu.VMEM((nsub, keys, tq // nsub), F32),
                        pltpu.VMEM((nsub, keys, tq // nsub), BF16)],
        compiler_params=_cparams("parallel", "parallel", "parallel"),
        name="mla_attn",
    )(*ins)


def _rope_cast_kernel(x_ref, cos_ref, sin_ref, o_ref):
    for sb in range(x_ref.shape[1] // LANES):
        sl = slice(sb * LANES, (sb + 1) * LANES)
        o_ref[:, sl] = _rope_slab(x_ref[:, sl], cos_ref[...], sin_ref[...]).astype(BF16)


def _rope_cast(src, colblk, n, tm, cos, sin):
    nt = cos.shape[0] // tm
    return pl.pallas_call(
        _rope_cast_kernel,
        grid=(n // tm,),
        in_specs=[pl.BlockSpec((tm, DF_W), lambda i: (i, colblk))]
                 + [pl.BlockSpec((tm, LANES), lambda i: (i % nt, 0))] * 2,
        out_specs=pl.BlockSpec((tm, DF_W), lambda i: (i, 0)),
        out_shape=jax.ShapeDtypeStruct((n, DF_W), BF16),
        compiler_params=_cparams("parallel"),
        name="rope_cast",
    )(src, cos, sin)


def _diff_attn_kernel(*refs, lam_init, cached, rope):
    it = iter(refs)
    q_ref, k_ref, v_ref = next(it), next(it), next(it)
    sources = [(k_ref, v_ref)]
    if cached:
        sources.append((next(it), next(it)))
    if rope:
        cos_ref, sin_ref = next(it), next(it)
    lam_ref, g_ref, o_ref, s_scr, p_scr = next(it), next(it), next(it), next(it), next(it)
    lp = lam_ref[...]
    lam = (jnp.exp(jnp.sum(lp[0:1] * lp[1:2], axis=-1, keepdims=True))
           - jnp.exp(jnp.sum(lp[2:3] * lp[3:4], axis=-1, keepdims=True)) + lam_init)
    halves = _query_halves(q_ref)
    ks = [kk_ref[...].astype(BF16) for kk_ref, _ in sources]
    vs = [vv_ref[...].astype(BF16) for _, vv_ref in sources]
    spans, off = [], 0
    for kk_ref, _ in sources:
        spans.append(slice(off, off + kk_ref.shape[0]))
        off += kk_ref.shape[0]
    for i, rows in enumerate(halves):
        q = q_ref[rows, :]
        if rope:
            q = _rope_slab(q, cos_ref[rows, :], sin_ref[rows, :])
        q = (q * (DF_DQK ** -0.5 * LOG2E)).astype(BF16)
        lane = lax.broadcasted_iota(jnp.int32, q.shape, 1)
        zero = jnp.zeros_like(q)
        for comp, qc in enumerate((jnp.where(lane < DF_DQK, q, zero), jnp.where(lane >= DF_DQK, q, zero))):
            for k, span in zip(ks, spans):
                s_scr[comp, i, :, span] = _dot_nt(qc, k)
    for i, rows in enumerate(halves):
        r = []
        for comp in range(2):
            s = s_scr[comp, i]
            p = jnp.exp2(s - jnp.max(s, axis=-1, keepdims=True))
            r.append(1.0 / jnp.sum(p, axis=-1, keepdims=True))
            s_scr[comp, i] = p
        c = lam * r[1] / r[0]
        p_scr[i] = (s_scr[0, i] - c * s_scr[1, i]).astype(BF16)
        o = functools.reduce(jnp.add, [_dot(p_scr[i, :, span], v) for v, span in zip(vs, spans)]) * r[0]
        o_ref[rows, :] = (_rms(o) * g_ref[...] * (1.0 - lam_init)).astype(BF16)


def _diff_attn(proj, k, kcol, lam_p, g, lam_init, bsz, seq, k_c=None, v_c=None, cos=None, sin=None):
    tq = min(DIFF_STEP_Q, seq)
    nq = seq // tq
    nsub = tq // min(ATTN_TQ, tq)
    cached = k_c is not None
    rope = cos is not None
    keys = seq
    qcol, vcol = COL_DF_Q // DF_DV, COL_DF_V // DF_DV
    ins = [proj, k, proj]
    in_specs = [pl.BlockSpec((tq, DF_DV), lambda b, h, i: (b * nq + i, qcol + h)),
                pl.BlockSpec((seq, DF_DV), lambda b, h, i: (b, kcol + h)),
                pl.BlockSpec((seq, DF_DV), lambda b, h, i: (b, vcol + h))]
    if cached:
        past = k_c.shape[0] // bsz
        keys += past
        ins += [k_c, v_c]
        in_specs += [pl.BlockSpec((past, DF_DV), lambda b, h, i: (b, h)),
                     pl.BlockSpec((past, DF_DV), lambda b, h, i: (b, h))]
    if rope:
        ins += [cos, sin]
        in_specs += [pl.BlockSpec((tq, LANES), lambda b, h, i: (i, 0))] * 2
    ins += [lam_p, g]
    in_specs += [pl.BlockSpec((4, DF_DQK), lambda b, h, i: (0, 0)),
                 pl.BlockSpec((1, DF_DV), lambda b, h, i: (0, 0))]
    return pl.pallas_call(
        functools.partial(_diff_attn_kernel, lam_init=lam_init, cached=cached, rope=rope),
        grid=(bsz, DF_HEADS, nq),
        in_specs=in_specs,
        out_specs=pl.BlockSpec((tq, DF_DV), lambda b, h, i: (b * nq + i, h)),
        out_shape=jax.ShapeDtypeStruct((bsz * seq, DF_W), BF16),
        scratch_shapes=[pltpu.VMEM((2, nsub, tq // nsub, keys), F32),
                        pltpu.VMEM((nsub, tq // nsub, keys), BF16)],
        compiler_params=_cparams("parallel", "parallel", "parallel"),
        name="diff_attn",
    )(*ins)


def _s5_kernel(uf_ref, ub_ref, a_re_ref, a_im_ref, h0_re_ref, h0_im_ref,
               wb_re_ref, wb_im_ref, wc_re_ref, wc_im_ref,
               yf_ref, yb_ref, hr_out_ref, hi_out_ref,
               bu_re, bu_im, h_re, h_im):
    ci = pl.program_id(1)
    tc = uf_ref.shape[1]
    nseq = 2 * S5_SEQS
    gblk, sblk = wb_re_ref.shape[2:]
    ngb = S5_W // gblk

    @pl.when(ci == 0)
    def _():
        h_re[...] = h0_re_ref[...]
        h_im[...] = h0_im_ref[...]

    ri = lax.broadcasted_iota(jnp.int32, (tc, tc), 0)
    cj = lax.broadcasted_iota(jnp.int32, (tc, tc), 1)
    rev = jnp.where(ri + cj == tc - 1, 1.0, 0.0).astype(BF16)

    for d, u_ref in enumerate((uf_ref, ub_ref)):
        us = []
        for s in range(S5_SEQS):
            u = u_ref[s].astype(BF16)
            if d == 1:
                u = _dot(rev, u).astype(BF16)
            us.append(u)
        u_all = jnp.concatenate(us, axis=0)
        for gb in range(ngb):
            ug = u_all[:, gb * gblk:(gb + 1) * gblk]
            for w_ref, dst in ((wb_re_ref, bu_re), (wb_im_ref, bu_im)):
                bu = _dot(ug, w_ref[d, gb])
                for s in range(S5_SEQS):
                    for lk in range(sblk // LANES):
                        dst[gb * (sblk // LANES) + lk, pl.ds(d * S5_SEQS + s, tc, stride=nseq), :] = (
                            bu[s * tc:(s + 1) * tc, lk * LANES:(lk + 1) * LANES])

    nlk = S5_LANE_BLK // LANES
    for lb in range(S5_STATE // S5_LANE_BLK):
        lks = tuple(range(lb * nlk, (lb + 1) * nlk))
        ar = [a_re_ref[:, k * LANES:(k + 1) * LANES] for k in lks]
        ai = [a_im_ref[:, k * LANES:(k + 1) * LANES] for k in lks]

        def step(j, carry):
            r0 = pl.multiple_of(j * nseq, nseq)
            out = []
            for i, k in enumerate(lks):
                hr, hi = carry[2 * i], carry[2 * i + 1]
                nhr = ar[i] * hr - ai[i] * hi + bu_re[k, pl.ds(r0, nseq), :]
                nhi = ar[i] * hi + ai[i] * hr + bu_im[k, pl.ds(r0, nseq), :]
                bu_re[k, pl.ds(r0, nseq), :] = nhr
                bu_im[k, pl.ds(r0, nseq), :] = nhi
                out += [nhr, nhi]
            return tuple(out)

        init = []
        for k in lks:
            init += [h_re[:, k * LANES:(k + 1) * LANES], h_im[:, k * LANES:(k + 1) * LANES]]
        fin = lax.fori_loop(0, tc, step, tuple(init), unroll=8)
        for i, k in enumerate(lks):
            h_re[:, k * LANES:(k + 1) * LANES] = fin[2 * i]
            h_im[:, k * LANES:(k + 1) * LANES] = fin[2 * i + 1]

    def seq_states(src, s):
        parts = [src[k, pl.ds(s, tc, stride=nseq), :] for k in range(S5_STATE // LANES)]
        return jnp.concatenate(parts, axis=1).astype(BF16)

    def reverse_rows(y):
        hi = y.astype(BF16)
        r1 = y - hi.astype(F32)
        mid = r1.astype(BF16)
        lo = (r1 - mid.astype(F32)).astype(BF16)
        return _dot(rev, hi) + _dot(rev, mid) + _dot(rev, lo)

    for d, y_ref in enumerate((yf_ref, yb_ref)):
        hr_all = jnp.concatenate([seq_states(bu_re, d * S5_SEQS + s) for s in range(S5_SEQS)], axis=0)
        hi_all = jnp.concatenate([seq_states(bu_im, d * S5_SEQS + s) for s in range(S5_SEQS)], axis=0)
        for gb in range(ngb):
            ssl = slice(gb * sblk, (gb + 1) * sblk)
            y = _dot(hr_all[:, ssl], wc_re_ref[d, gb]) - _dot(hi_all[:, ssl], wc_im_ref[d, gb])
            for s in range(S5_SEQS):
                ys = y[s * tc:(s + 1) * tc]
                y_ref[s, :, gb * gblk:(gb + 1) * gblk] = reverse_rows(ys) if d == 1 else ys

    @pl.when(ci == pl.num_programs(1) - 1)
    def _():
        hr_out_ref[...] = h_re[...]
        hi_out_ref[...] = h_im[...]


def _s5(proj3, g0, bsz, seq, a_re, a_im, h0_re, h0_im, wb_re, wb_im, wc_re, wc_im):
    tc = min(S5_CHUNK, seq)
    nc = seq // tc
    ng = bsz // S5_SEQS
    nseq = 2 * S5_SEQS
    ucol = COL_S5_U // S5_W
    const4 = lambda g, c: (0, 0, 0, 0)
    wspec = lambda w: pl.BlockSpec(w.shape, const4, pipeline_mode=pl.Buffered(1))
    hspec = pl.BlockSpec((None, nseq, S5_STATE), lambda g, c: (g, 0, 0))
    return pl.pallas_call(
        _s5_kernel,
        grid=(ng, nc),
        in_specs=[pl.BlockSpec((S5_SEQS, tc, S5_W), lambda g, c: (g0 + g, c, ucol)),
                  pl.BlockSpec((S5_SEQS, tc, S5_W), lambda g, c: (g0 + g, nc - 1 - c, ucol)),
                  pl.BlockSpec((nseq, S5_STATE), lambda g, c: (0, 0)),
                  pl.BlockSpec((nseq, S5_STATE), lambda g, c: (0, 0)),
                  hspec, hspec, wspec(wb_re), wspec(wb_im), wspec(wc_re), wspec(wc_im)],
        out_specs=[pl.BlockSpec((S5_SEQS, tc, S5_W), lambda g, c: (g, c, 0)),
                   pl.BlockSpec((S5_SEQS, tc, S5_W), lambda g, c: (g, nc - 1 - c, 0)),
                   hspec, hspec],
        out_shape=[jax.ShapeDtypeStruct((bsz, seq, S5_W), F32),
                   jax.ShapeDtypeStruct((bsz, seq, S5_W), F32),
                   jax.ShapeDtypeStruct((ng, nseq, S5_STATE), F32),
                   jax.ShapeDtypeStruct((ng, nseq, S5_STATE), F32)],
        scratch_shapes=[pltpu.VMEM((S5_STATE // LANES, tc * nseq, LANES), F32),
                        pltpu.VMEM((S5_STATE // LANES, tc * nseq, LANES), F32),
                        pltpu.VMEM((nseq, S5_STATE), F32),
                        pltpu.VMEM((nseq, S5_STATE), F32)],
        compiler_params=_cparams("parallel", "arbitrary"),
        name="s5",
    )(proj3, proj3, a_re, a_im, h0_re, h0_im, wb_re, wb_im, wc_re, wc_im)


def _s5_post_kernel(yf_ref, yb_ref, u_ref, d_ref, w_ref, o_ref):
    y = (yf_ref[...] + yb_ref[...]) + d_ref[...] * u_ref[...]
    g = jax.nn.gelu(y)
    o_ref[...] = (g * jax.nn.sigmoid(_dot(g.astype(BF16), w_ref[...]))).astype(BF16)


def _s5_post(yf, yb, proj, row0, d, w, tm):
    n = yf.shape[0]
    blk0 = row0 // tm
    return pl.pallas_call(
        _s5_post_kernel,
        grid=(n // tm,),
        in_specs=[pl.BlockSpec((tm, S5_W), lambda i: (i, 0)),
                  pl.BlockSpec((tm, S5_W), lambda i: (i, 0)),
                  pl.BlockSpec((tm, S5_W), lambda i: (blk0 + i, COL_S5_U // S5_W)),
                  pl.BlockSpec((1, S5_W), lambda i: (0, 0)),
                  pl.BlockSpec((S5_W, S5_W), lambda i: (0, 0))],
        out_specs=pl.BlockSpec((tm, S5_W), lambda i: (i, 0)),
        out_shape=jax.ShapeDtypeStruct((n, S5_W), BF16),
        compiler_params=_cparams("parallel"),
        name="s5_post",
    )(yf, yb, proj, d, w)


def _s5_discretise(a_re, a_im, log_dt, b_re, b_im):
    lr = jnp.minimum(a_re, -1e-4)
    li = a_im
    dt = jnp.exp(log_dt)[..., None]
    mag = jnp.exp(dt * lr)
    ab_re, ab_im = mag * jnp.cos(dt * li), mag * jnp.sin(dt * li)
    den = lr * lr + li * li
    nr, ni = ab_re - 1.0, ab_im
    qr = (nr * lr + ni * li) / den
    qi = (ni * lr - nr * li) / den
    bb_re = qr[..., None] * b_re - qi[..., None] * b_im
    bb_im = qr[..., None] * b_im + qi[..., None] * b_re
    return ab_re, ab_im, bb_re, bb_im


def _block_diag(w):
    ngrp = S5_GROUPS_PER_TILE
    d, g, r, c = w.shape
    wg = w.reshape(d, g // ngrp, ngrp, r, c)
    eye = jnp.eye(ngrp, dtype=w.dtype)
    out = jnp.einsum('dbgrc,gh->dbgrhc', wg, eye)
    return out.reshape(d, g // ngrp, ngrp * r, ngrp * c)


def _merge_kernel(y0_ref, y1_ref, y2_ref, y3_ref, g0_ref, g1_ref, g2_ref, g3_ref, w_ref, o_ref):
    acc = None
    for b, (y_ref, g_ref) in enumerate(((y0_ref, g0_ref), (y1_ref, g1_ref), (y2_ref, g2_ref), (y3_ref, g3_ref))):
        term = jax.nn.sigmoid(g_ref[...]) * _dot(y_ref[...], w_ref[b])
        acc = term if acc is None else acc + term
    o_ref[...] = acc.astype(BF16)


def _merge(ys, proj, wb, tm):
    n = ys[0].shape[0]
    tn = 512
    gate0 = COL_GATE // tn
    per = D_MODEL // tn

    def gspec(b):
        return pl.BlockSpec((tm, tn), lambda i, j: (i, gate0 + b * per + j))

    return pl.pallas_call(
        _merge_kernel,
        grid=(n // tm, per),
        in_specs=[pl.BlockSpec((tm, BRANCH_W), lambda i, j: (i, 0))] * N_BRANCH
                 + [gspec(b) for b in range(N_BRANCH)]
                 + [pl.BlockSpec((N_BRANCH, BRANCH_W, tn), lambda i, j: (0, 0, j))],
        out_specs=pl.BlockSpec((tm, tn), lambda i, j: (i, j)),
        out_shape=jax.ShapeDtypeStruct((n, D_MODEL), BF16),
        compiler_params=_cparams("parallel", "parallel"),
        name="merge",
    )(*ys, proj, proj, proj, proj, wb)


def _rope_tables(n_tok):
    grid_rows = n_tok // GRID_W
    rows, cols = jnp.meshgrid(jnp.arange(grid_rows, dtype=F32), jnp.arange(GRID_W, dtype=F32), indexing='ij')
    quarter = ROPE_DIM // 4
    inv = ROPE_BASE ** (-jnp.arange(quarter, dtype=F32) / quarter)
    ang_r = rows.reshape(-1, 1) * inv
    ang_c = cols.reshape(-1, 1) * inv
    cos = jnp.concatenate([jnp.cos(ang_r)] * 2 + [jnp.cos(ang_c)] * 2, axis=-1)
    sin = jnp.concatenate([-jnp.sin(ang_r), jnp.sin(ang_r), -jnp.sin(ang_c), jnp.sin(ang_c)], axis=-1)
    return jnp.tile(cos, (1, LANES // ROPE_DIM)), jnp.tile(sin, (1, LANES // ROPE_DIM))


def _permute_w_in(w):
    sizes = (ML_W, ML_W, ML_W, ML_W, 4 * ML_HEADS, MLA_HEADS * MLA_QK, MLA_KV_RANK + MLA_ROPE, S5_W,
             DF_W, DF_W, DF_W, N_BRANCH * D_MODEL)
    splits = tuple(int(s) for s in np.cumsum(sizes)[:-1])
    (ml_q, ml_k, ml_v, ml_o, ml_if, mla_q, mla_kva, s5_u, df_q, df_k, df_v, gate) = jnp.split(w, splits, axis=1)
    mq = mla_q.reshape(D_MODEL, MLA_HEADS, MLA_QK)
    qn = mq[:, :, :MLA_NOPE].reshape(D_MODEL, MLA_HEADS * MLA_NOPE)
    qr = mq[:, :, MLA_NOPE:].reshape(D_MODEL, MLA_HEADS * MLA_ROPE)
    zeros = lambda width: jnp.zeros((D_MODEL, width), BF16)
    cols = [ml_q, ml_k, ml_v, ml_o, s5_u, df_q, df_k, df_v, gate, qn, qr,
            mla_kva[:, :MLA_KV_RANK], mla_kva[:, MLA_KV_RANK:], zeros(LANES - MLA_ROPE),
            ml_if, zeros(LANES - 4 * ML_HEADS)]
    cols.append(zeros(PROJ_COLS - sum(c.shape[1] for c in cols)))
    return jnp.concatenate([c.astype(BF16) for c in cols], axis=1)


def _seq_mixers(proj, row0, bsz, seq, lw, lam_init, state, cache, rope):
    n = bsz * seq
    tm = min(TM_SEQ, seq)
    ml_c0, ml_n0, ml_m0, s5_h0r, s5_h0i = state
    r8 = 2 * ML_HEADS

    m0 = jnp.broadcast_to(ml_m0.reshape(bsz, r8, 1), (bsz, r8, LANES))
    hf, hb, c_new, n_new, m_new = _mlstm(proj, row0, bsz, seq, lw['ml_bias'],
                                         ml_c0.reshape(bsz, r8, ML_DH, ML_DH), ml_n0.reshape(bsz, r8, ML_DH), m0)
    y_ml = _ml_post(hf, hb, proj, row0, lw['ml_norm'], tm)

    cos, sin = rope if rope is not None else (None, None)
    k_new, v_new, q_mla, ckv = _mla_prep(proj, proj, COL_MLA_CKV // MLA_KV_RANK, COL_MLA_KR // LANES, row0, n,
                                         lw['mla_kv_norm'], lw['mla_w_kvb'], tm, q_src=proj, cos=cos, sin=sin)
    k_c = v_c = None
    if cache is not None:
        ckv_c, krope_c, dk_c, dv_c = cache
        past = ckv_c.shape[1]
        kr_pad = jnp.pad(krope_c.reshape(bsz * past, MLA_ROPE), ((0, 0), (0, LANES - MLA_ROPE)))
        k_c, v_c = _mla_prep(ckv_c.reshape(bsz * past, MLA_KV_RANK), kr_pad, 0, 0, 0, bsz * past,
                             lw['mla_kv_norm'], lw['mla_w_kvb'], min(tm, bsz * past), norm=False)
    y_mla = _mla_attn(q_mla, k_new, v_new, bsz, seq, k_c, v_c)

    proj3 = proj.reshape(proj.shape[0] // seq, seq, PROJ_COLS)
    ng = bsz // S5_SEQS

    def pack_state(hs):
        return hs.reshape(ng, S5_SEQS, 2, S5_STATE).transpose(0, 2, 1, 3).reshape(ng, 2 * S5_SEQS, S5_STATE)

    def unpack_state(hs):
        return hs.reshape(ng, 2, S5_SEQS, S5_STATE).transpose(0, 2, 1, 3).reshape(bsz, 2, S5_GROUPS, S5_P)

    yf, yb, hr_new, hi_new = _s5(proj3, row0 // (seq * S5_SEQS), bsz, seq, lw['s5_a_re8'], lw['s5_a_im8'],
                                 pack_state(s5_h0r), pack_state(s5_h0i),
                                 lw['s5_wb_re'], lw['s5_wb_im'], lw['s5_wc_re'], lw['s5_wc_im'])
    y_s5 = _s5_post(yf.reshape(n, S5_W), yb.reshape(n, S5_W), proj, row0, lw['s5_d'], lw['s5_w_glu'], tm)

    if cos is not None:
        dk, kcol = _rope_cast(proj, COL_DF_K // DF_W, n, tm, cos, sin), 0
    else:
        dk, kcol = proj, COL_DF_K // DF_DV
    dk_cache = dv_cache = None
    if cache is not None:
        dk_cache = dk_c.reshape(bsz * past, DF_W)
        dv_cache = dv_c.reshape(bsz * past, DF_W)
    y_df = _diff_attn(proj, dk, kcol, lw['df_lambda'], lw['df_norm'], lam_init, bsz, seq,
                      dk_cache, dv_cache, cos, sin)

    cols = lambda c0, w: lax.slice(proj, (row0, c0), (row0 + n, c0 + w))
    new_ctx = (ckv.reshape(bsz, seq, MLA_KV_RANK),
               cols(COL_MLA_KR, MLA_ROPE).reshape(bsz, seq, MLA_ROPE),
               cols(COL_DF_K, DF_W).reshape(bsz, seq, DF_HEADS, 2 * DF_DQK),
               cols(COL_DF_V, DF_W).reshape(bsz, seq, DF_HEADS, DF_DV),
               c_new.reshape(bsz, 2, ML_HEADS, ML_DH, ML_DH), n_new.reshape(bsz, 2, ML_HEADS, ML_DH),
               m_new[:, :, 0].reshape(bsz, 2, ML_HEADS), unpack_state(hr_new), unpack_state(hi_new))
    return (y_ml, y_mla, y_s5, y_df), new_ctx


def kernel(x_prompt, x_sample, cache_mla_ckv, cache_mla_krope, cache_diff_k, cache_diff_v,
           state_mlstm_c, state_mlstm_n, state_mlstm_m, state_s5_re, state_s5_im, c,
           c_ctx, w_ada, b_ada, norm_mix, norm_ffn, w_in, ml_if_bias, ml_norm, mla_kv_norm,
           mla_w_kvb, s5_a_re, s5_a_im, s5_log_dt, s5_b_re, s5_b_im, s5_c_re, s5_c_im, s5_d,
           s5_w_glu, df_lambda, df_norm, w_branch, w_o, w_ffn_in, w_ffn_out, final_norm):
    bp, sp, _ = x_prompt.shape
    bs, ss, _ = x_sample.shape
    n_p, n_s = bp * sp, bs * ss
    tm = TM_DENSE

    cond = jnp.concatenate([c, c_ctx[None, :], jnp.zeros((SUBLANES - 1 - bs, D_MODEL), F32)], axis=0)
    mods = _ada(cond, w_ada, b_ada).reshape(DEPTH, SUBLANES, 6, 1, D_MODEL)
    rope = _rope_tables(ss)
    latent_row = lambda i: (i * tm) // ss
    prompt_row = lambda i: bs

    xs = x_sample.reshape(n_s, D_MODEL)
    xp = x_prompt.reshape(n_p, D_MODEL)
    zeros_state = (jnp.zeros((bp, 2, ML_HEADS, ML_DH, ML_DH), F32), jnp.zeros((bp, 2, ML_HEADS, ML_DH), F32),
                   jnp.zeros((bp, 2, ML_HEADS), F32), jnp.zeros((bp, 2, S5_GROUPS, S5_P), F32),
                   jnp.zeros((bp, 2, S5_GROUPS, S5_P), F32))
    ctx_out = []
    for l in range(DEPTH):
        lam_init = 0.8 - 0.6 * math.exp(-0.3 * l)
        ab_re, ab_im, bb_re, bb_im = _s5_discretise(s5_a_re[l], s5_a_im[l], s5_log_dt[l], s5_b_re[l], s5_b_im[l])
        rep = lambda a: jnp.repeat(a.reshape(2, S5_STATE), S5_SEQS, axis=0)
        lw = {
            'ml_bias': jnp.pad(ml_if_bias[l].reshape(1, 4 * ML_HEADS), ((0, 0), (0, LANES - 4 * ML_HEADS))),
            'ml_norm': ml_norm[l].reshape(1, ML_W),
            'mla_kv_norm': mla_kv_norm[l].reshape(1, MLA_KV_RANK),
            'mla_w_kvb': mla_w_kvb[l].astype(BF16),
            's5_a_re8': rep(ab_re), 's5_a_im8': rep(ab_im),
            's5_wb_re': _block_diag(bb_re.transpose(0, 1, 3, 2)).astype(BF16),
            's5_wb_im': _block_diag(bb_im.transpose(0, 1, 3, 2)).astype(BF16),
            's5_wc_re': _block_diag(s5_c_re[l].transpose(0, 1, 3, 2)).astype(BF16),
            's5_wc_im': _block_diag(s5_c_im[l].transpose(0, 1, 3, 2)).astype(BF16),
            's5_d': s5_d[l].reshape(1, S5_W),
            's5_w_glu': s5_w_glu[l].astype(BF16),
            'df_lambda': df_lambda[l],
            'df_norm': df_norm[l].reshape(1, DF_DV),
        }
        mod = mods[l]
        w_in_l = _permute_w_in(w_in[l])
        w_branch_l, w_o_l = w_branch[l].astype(BF16), w_o[l].astype(BF16)
        w_ffn_in_l, w_ffn_out_l = w_ffn_in[l].astype(BF16), w_ffn_out[l].astype(BF16)
        g_mix, g_ffn = norm_mix[l].reshape(1, D_MODEL), norm_ffn[l].reshape(1, D_MODEL)

        def layer(x, row_of_tile, bsz, seq, state, cache, rope_tabs):
            proj = _proj_in(x, g_mix, mod, row_of_tile, w_in_l, tm)
            ys, new_ctx = _seq_mixers(proj, 0, bsz, seq, lw, lam_init, state, cache, rope_tabs)
            merged = _merge(list(ys), proj, w_branch_l, tm)
            x = _matmul_resid(merged, w_o_l, x, mod, 2, row_of_tile, tm)
            act = _ffn_in(x, g_ffn, mod, row_of_tile, w_ffn_in_l, tm)
            return _matmul_resid(act, w_ffn_out_l, x, mod, 5, row_of_tile, tm), new_ctx

        xp, new_ctx = layer(xp, prompt_row, bp, sp, zeros_state, None, None)
        ctx_out.append(new_ctx)
        state = (state_mlstm_c[:, l], state_mlstm_n[:, l], state_mlstm_m[:, l], state_s5_re[:, l], state_s5_im[:, l])
        cache = (cache_mla_ckv[:, l], cache_mla_krope[:, l], cache_diff_k[:, l], cache_diff_v[:, l])
        xs, _ = layer(xs, latent_row, bs, ss, state, cache, rope)

    g_fin = final_norm.reshape(1, D_MODEL)
    y_prompt = _final_norm(xp, g_fin, tm).reshape(bp, sp, D_MODEL)
    y_sample = _final_norm(xs, g_fin, tm).reshape(bs, ss, D_MODEL)
    stacked = tuple(jnp.stack([ctx[k] for ctx in ctx_out], axis=1) for k in range(9))
    return (y_prompt, y_sample) + stacked
```

```python
import functools
import math

import jax
import jax.numpy as jnp
import numpy as np
from jax import lax
from jax.experimental import pallas as pl
from jax.experimental.pallas import tpu as pltpu

F32 = jnp.float32
BF16 = jnp.bfloat16

D_MODEL = 2048
DEPTH = 2
GRID_W = 64
ROPE_DIM = 64
ROPE_BASE = 10000.0
RMS_EPS = 1e-6
ML_HEADS = 4
ML_DH = 256
ML_W = ML_HEADS * ML_DH
MLA_HEADS = 8
MLA_NOPE = 128
MLA_ROPE = ROPE_DIM
MLA_V = 128
MLA_KV_RANK = 512
MLA_QK = MLA_NOPE + MLA_ROPE
S5_GROUP = 16
S5_GROUPS = 64
S5_W = S5_GROUPS * S5_GROUP
S5_P = 64
S5_STATE = S5_GROUPS * S5_P
DF_HEADS = 8
DF_DQK = ROPE_DIM
DF_DV = 2 * DF_DQK
DF_W = DF_HEADS * DF_DV
N_BRANCH = 4
BRANCH_W = 1024
FFN_HIDDEN = (8 * D_MODEL + 3 * 256 - 1) // (3 * 256) * 256

LANES = 128
SUBLANES = 8
VMEM_LIMIT_BYTES = 56 * 1024 * 1024

COL_ML_Q = 0
COL_ML_K = 1024
COL_ML_V = 2048
COL_ML_O = 3072
COL_S5_U = 4096
COL_DF_Q = 5120
COL_DF_K = 6144
COL_DF_V = 7168
COL_GATE = 8192
COL_MLA_QN = 16384
COL_MLA_QR = 17408
COL_MLA_CKV = 17920
COL_MLA_KR = 18432
COL_ML_IF = 18560
PROJ_TN = 1024
PROJ_COLS = 19456

LOG2E = math.log2(math.e)
VT_ROWS = 128 + 16

TM_DENSE = 1024
DENSE_TN = 512
TM_SEQ = 512
ATTN_STEP_Q = 1024
ML_CHUNK = 256
S5_CHUNK = 64
S5_SEQS = 4
S5_LANE_BLK = 512
S5_GROUPS_PER_TILE = 16
ATTN_TQ = 256


def _cparams(*sem):
    return pltpu.CompilerParams(dimension_semantics=sem, vmem_limit_bytes=VMEM_LIMIT_BYTES)


def _dot(a, b):
    return jnp.dot(a, b, preferred_element_type=F32)


def _dot_nt(a, b):
    return lax.dot_general(a, b, (((1,), (1,)), ((), ())), preferred_element_type=F32)


def _dot_exact(a, b):
    return jnp.dot(a, b, preferred_element_type=F32, precision=lax.Precision.HIGHEST)


def _rms(x):
    return x * lax.rsqrt(jnp.mean(x * x, axis=-1, keepdims=True) + RMS_EPS)


def _ones_row_block(width):
    row = lax.broadcasted_iota(jnp.int32, (VT_ROWS - 128, width), 0)
    return jnp.where(row == 0, 1.0, 0.0).astype(BF16)


def _rope_slab(x, cos, sin):
    quarter = ROPE_DIM // 4
    lane = lax.broadcasted_iota(jnp.int32, x.shape, 1)
    partner = jnp.where((lane % (2 * quarter)) < quarter, pltpu.roll(x, LANES - quarter, 1),
                        pltpu.roll(x, quarter, 1))
    return x * cos + partner * sin


def _ada_kernel(c_ref, w_ref, b_ref, o_ref):
    c = c_ref[...]
    s = c * jax.nn.sigmoid(c)
    o_ref[...] = _dot(s.astype(BF16), w_ref[...].astype(BF16)) + b_ref[...]


def _ada(cond, w_ada, b_ada):
    rows = cond.shape[0]
    tn = PROJ_TN
    return pl.pallas_call(
        _ada_kernel,
        grid=(DEPTH, 6 * D_MODEL // tn),
        in_specs=[pl.BlockSpec((rows, D_MODEL), lambda l, j: (0, 0)),
                  pl.BlockSpec((None, D_MODEL, tn), lambda l, j: (l, 0, j)),
                  pl.BlockSpec((None, 1, tn), lambda l, j: (l, 0, j))],
        out_specs=pl.BlockSpec((None, rows, tn), lambda l, j: (l, 0, j)),
        out_shape=jax.ShapeDtypeStruct((DEPTH, rows, 6 * D_MODEL), F32),
        compiler_params=_cparams("parallel", "parallel"),
        name="ada",
    )(cond, w_ada, b_ada.reshape(DEPTH, 1, 6 * D_MODEL))


def _mod_spec(which, row_of_tile):
    return pl.BlockSpec((None, None, 1, D_MODEL), lambda i, j: (row_of_tile(i), which, 0, 0))


def _norm_mod(x_ref, g_ref, sc_ref, sh_ref):
    return (_rms(x_ref[...]) * g_ref[...]) * (1.0 + sc_ref[...]) + sh_ref[...]


def _proj_in_kernel(x_ref, g_ref, sc_ref, sh_ref, w_ref, o_ref, h_ref):
    @pl.when(pl.program_id(1) == 0)
    def _():
        h_ref[...] = _norm_mod(x_ref, g_ref, sc_ref, sh_ref).astype(BF16)

    o_ref[...] = _dot(h_ref[...], w_ref[...])


def _proj_in(x, g, mod, row_of_tile, w, tm):
    n = x.shape[0]
    ncol = w.shape[1]
    return pl.pallas_call(
        _proj_in_kernel,
        grid=(n // tm, ncol // PROJ_TN),
        in_specs=[pl.BlockSpec((tm, D_MODEL), lambda i, j: (i, 0), pipeline_mode=pl.Buffered(1)),
                  pl.BlockSpec((1, D_MODEL), lambda i, j: (0, 0)),
                  _mod_spec(1, row_of_tile), _mod_spec(0, row_of_tile),
                  pl.BlockSpec((D_MODEL, PROJ_TN), lambda i, j: (0, j))],
        out_specs=pl.BlockSpec((tm, PROJ_TN), lambda i, j: (i, j)),
        out_shape=jax.ShapeDtypeStruct((n, ncol), F32),
        scratch_shapes=[pltpu.VMEM((tm, D_MODEL), BF16)],
        compiler_params=_cparams("parallel", "arbitrary"),
        name="proj_in",
    )(x, g, mod, mod, w)


def _ffn_in_kernel(x_ref, g_ref, sc_ref, sh_ref, wa_ref, wb_ref, o_ref, h_ref):
    @pl.when(pl.program_id(1) == 0)
    def _():
        h_ref[...] = _norm_mod(x_ref, g_ref, sc_ref, sh_ref).astype(BF16)

    h = h_ref[...]
    a = _dot(h, wa_ref[...])
    b = _dot(h, wb_ref[...])
    o_ref[...] = (a * jax.nn.sigmoid(a) * b).astype(BF16)


def _ffn_in(x, g, mod, row_of_tile, w, tm):
    n = x.shape[0]
    tn = DENSE_TN
    nj = FFN_HIDDEN // tn
    return pl.pallas_call(
        _ffn_in_kernel,
        grid=(n // tm, nj),
        in_specs=[pl.BlockSpec((tm, D_MODEL), lambda i, j: (i, 0)),
                  pl.BlockSpec((1, D_MODEL), lambda i, j: (0, 0)),
                  _mod_spec(4, row_of_tile), _mod_spec(3, row_of_tile),
                  pl.BlockSpec((D_MODEL, tn), lambda i, j: (0, j)),
                  pl.BlockSpec((D_MODEL, tn), lambda i, j: (0, nj + j))],
        out_specs=pl.BlockSpec((tm, tn), lambda i, j: (i, j)),
        out_shape=jax.ShapeDtypeStruct((n, FFN_HIDDEN), BF16),
        scratch_shapes=[pltpu.VMEM((tm, D_MODEL), BF16)],
        compiler_params=_cparams("parallel", "arbitrary"),
        name="ffn_in",
    )(x, g, mod, mod, w, w)


def _resid_kernel(a_ref, w_ref, x_ref, gate_ref, o_ref):
    o_ref[...] = x_ref[...] + gate_ref[...] * _dot(a_ref[...], w_ref[...])


def _matmul_resid(a, w, x, mod, which, row_of_tile, tm):
    n, kdim = a.shape
    tn = DENSE_TN
    return pl.pallas_call(
        _resid_kernel,
        grid=(n // tm, D_MODEL // tn),
        in_specs=[pl.BlockSpec((tm, kdim), lambda i, j: (i, 0)),
                  pl.BlockSpec((kdim, tn), lambda i, j: (0, j)),
                  pl.BlockSpec((tm, tn), lambda i, j: (i, j)),
                  pl.BlockSpec((None, None, 1, tn), lambda i, j: (row_of_tile(i), which, 0, j))],
        out_specs=pl.BlockSpec((tm, tn), lambda i, j: (i, j)),
        out_shape=jax.ShapeDtypeStruct((n, D_MODEL), F32),
        compiler_params=_cparams("parallel", "parallel"),
        name="matmul_resid",
    )(a, w, x, mod)


def _final_norm_kernel(x_ref, g_ref, o_ref):
    o_ref[...] = _rms(x_ref[...]) * g_ref[...]


def _final_norm(x, g, tm):
    n = x.shape[0]
    return pl.pallas_call(
        _final_norm_kernel,
        grid=(n // tm,),
        in_specs=[pl.BlockSpec((tm, D_MODEL), lambda i: (i, 0)),
                  pl.BlockSpec((1, D_MODEL), lambda i: (0, 0))],
        out_specs=pl.BlockSpec((tm, D_MODEL), lambda i: (i, 0)),
        out_shape=jax.ShapeDtypeStruct((n, D_MODEL), F32),
        compiler_params=_cparams("parallel"),
        name="final_norm",
    )(x, g)


def _mlstm_kernel(qf_ref, kf_ref, vf_ref, gf_ref, qb_ref, kb_ref, vb_ref, gb_ref, bias_ref,
                  c0_ref, n0_ref, m0_ref,
                  hf_ref, hb_ref, c_out_ref, n_out_ref, m_out_ref,
                  c_scr, n_scr, m_scr):
    ci = pl.program_id(1)
    t = qf_ref.shape[0]

    @pl.when(ci == 0)
    def _():
        c_scr[...] = c0_ref[...]
        n_scr[...] = n0_ref[...]
        m_scr[...] = m0_ref[...]

    row = lax.broadcasted_iota(jnp.int32, (t, t), 0)
    col = lax.broadcasted_iota(jnp.int32, (t, t), 1)
    refs = ((qf_ref, kf_ref, vf_ref, gf_ref, hf_ref), (qb_ref, kb_ref, vb_ref, gb_ref, hb_ref))
    for d in range(2):
        q_ref, k_ref, v_ref, g_ref, h_ref = refs[d]
        keep = (col <= row) if d == 0 else (col >= row)
        cum = jnp.where(keep, 1.0, 0.0).astype(F32)
        gates = g_ref[...] + bias_ref[...]
        csum = _dot_exact(cum, jax.nn.log_sigmoid(gates))
        gates_t = gates.T
        csum_t = csum.T
        last = t - 1 if d == 0 else 0
        for hd in range(ML_HEADS):
            r = d * ML_HEADS + hd
            i_col = d * 2 * ML_HEADS + hd
            f_col = i_col + ML_HEADS
            b_c = csum[:, f_col:f_col + 1]
            b_r = csum_t[f_col:f_col + 1, :]
            li_c = gates[:, i_col:i_col + 1]
            li_r = gates_t[i_col:i_col + 1, :]
            m_st = m_scr[r:r + 1, 0:1]
            c_st = c_scr[r]
            n_st = n_scr[r:r + 1, :]
            sl = slice(hd * ML_DH, (hd + 1) * ML_DH)
            q = q_ref[:, sl]
            k = k_ref[:, sl] * (ML_DH ** -0.5)
            v = v_ref[:, sl]
            qb16 = q.astype(BF16)
            kb16 = k.astype(BF16)

            log_d = jnp.where(keep, b_c - b_r + li_r, -jnp.inf)
            log_inter = b_c + m_st
            m_t = jnp.maximum(log_inter, jnp.max(log_d, axis=1, keepdims=True))
            w_d = jnp.exp(log_d - m_t)
            w_inter = jnp.exp(log_inter - m_t)
            s = _dot_nt(qb16, kb16) * w_d
            num = _dot(s.astype(BF16), v.astype(BF16)) + w_inter * _dot_nt(qb16, c_st.astype(BF16))
            den = jnp.sum(s, axis=1, keepdims=True) + w_inter * jnp.sum(q * n_st, axis=1, keepdims=True)
            h_ref[:, sl] = num / jnp.maximum(jnp.abs(den), jnp.exp(-m_t))

            b_last = b_c[last:last + 1, :]
            log_w = b_last - b_c + li_c
            m_new = jnp.maximum(b_last + m_st, jnp.max(log_w, axis=0, keepdims=True))
            w_s = jnp.exp(log_w - m_new)
            w_c = jnp.exp(b_last + m_st - m_new)
            vw_t = (v * w_s).T.astype(BF16)
            c_scr[r] = w_c * c_st + _dot(vw_t, kb16)
            n_scr[r:r + 1, :] = w_c * n_st + jnp.sum(k * w_s, axis=0, keepdims=True)
            m_scr[r:r + 1, :] = jnp.broadcast_to(m_new, (1, LANES))

    @pl.when(ci == pl.num_programs(1) - 1)
    def _():
        c_out_ref[...] = c_scr[...]
        n_out_ref[...] = n_scr[...]
        m_out_ref[...] = m_scr[...]


def _mlstm(proj, bsz, seq, bias, c0, n0, m0):
    tc = min(ML_CHUNK, seq)
    nc = seq // tc
    r8 = 2 * ML_HEADS

    def fwd(colblk):
        return lambda b, c: (b * nc + c, colblk)

    def bwd(colblk):
        return lambda b, c: (b * nc + nc - 1 - c, colblk)

    def seqspecs(mk):
        return [pl.BlockSpec((tc, ML_W), mk(COL_ML_Q // ML_W)),
                pl.BlockSpec((tc, ML_W), mk(COL_ML_K // ML_W)),
                pl.BlockSpec((tc, ML_W), mk(COL_ML_V // ML_W)),
                pl.BlockSpec((tc, LANES), mk(COL_ML_IF // LANES))]

    state_specs = [pl.BlockSpec((None, r8, ML_DH, ML_DH), lambda b, c: (b, 0, 0, 0)),
                   pl.BlockSpec((None, r8, ML_DH), lambda b, c: (b, 0, 0)),
                   pl.BlockSpec((None, r8, LANES), lambda b, c: (b, 0, 0))]
    return pl.pallas_call(
        _mlstm_kernel,
        grid=(bsz, nc),
        in_specs=seqspecs(fwd) + seqspecs(bwd) + [pl.BlockSpec((1, LANES), lambda b, c: (0, 0))] + state_specs,
        out_specs=[pl.BlockSpec((tc, ML_W), lambda b, c: (b * nc + c, 0)),
                   pl.BlockSpec((tc, ML_W), lambda b, c: (b * nc + nc - 1 - c, 0))] + state_specs,
        out_shape=[jax.ShapeDtypeStruct((bsz * seq, ML_W), F32),
                   jax.ShapeDtypeStruct((bsz * seq, ML_W), F32),
                   jax.ShapeDtypeStruct((bsz, r8, ML_DH, ML_DH), F32),
                   jax.ShapeDtypeStruct((bsz, r8, ML_DH), F32),
                   jax.ShapeDtypeStruct((bsz, r8, LANES), F32)],
        scratch_shapes=[pltpu.VMEM((r8, ML_DH, ML_DH), F32),
                        pltpu.VMEM((r8, ML_DH), F32),
                        pltpu.VMEM((r8, LANES), F32)],
        compiler_params=_cparams("parallel", "arbitrary"),
        name="mlstm",
    )(proj, proj, proj, proj, proj, proj, proj, proj, bias, c0, n0, m0)


def _ml_post_kernel(hf_ref, hb_ref, o_ref, g_ref, y_ref):
    h = hf_ref[...] + hb_ref[...]
    for hd in range(ML_HEADS):
        sl = slice(hd * ML_DH, (hd + 1) * ML_DH)
        y_ref[:, sl] = (_rms(h[:, sl]) * g_ref[:, sl] * jax.nn.sigmoid(o_ref[:, sl])).astype(BF16)


def _ml_post(hf, hb, proj, g, tm):
    n = hf.shape[0]
    return pl.pallas_call(
        _ml_post_kernel,
        grid=(n // tm,),
        in_specs=[pl.BlockSpec((tm, ML_W), lambda i: (i, 0)),
                  pl.BlockSpec((tm, ML_W), lambda i: (i, 0)),
                  pl.BlockSpec((tm, ML_W), lambda i: (i, COL_ML_O // ML_W)),
                  pl.BlockSpec((1, ML_W), lambda i: (0, 0))],
        out_specs=pl.BlockSpec((tm, ML_W), lambda i: (i, 0)),
        out_shape=jax.ShapeDtypeStruct((n, ML_W), BF16),
        compiler_params=_cparams("parallel"),
        name="ml_post",
    )(hf, hb, proj, g)


def _mla_prep_kernel(*refs, norm, rope, with_q):
    it = iter(refs)
    ckv_ref, kr_ref, g_ref, wkvb_ref = next(it), next(it), next(it), next(it)
    if with_q:
        qn_ref, qr_ref = next(it), next(it)
    if rope:
        cos_ref, sin_ref = next(it), next(it)
    k_ref, v_ref = next(it), next(it)
    if with_q:
        q_ref = next(it)
    if norm:
        ckv_out_ref = next(it)

    ckv = ckv_ref[...]
    if norm:
        ckv = _rms(ckv) * g_ref[...]
        ckv_out_ref[...] = ckv
    kv = _dot(ckv.astype(BF16), wkvb_ref[...])
    kr = kr_ref[...]
    if rope:
        kr = _rope_slab(kr, cos_ref[...], sin_ref[...])
    kr16 = kr[:, :MLA_ROPE].astype(BF16)
    hw = MLA_NOPE + MLA_V
    for hd in range(MLA_HEADS):
        k_ref[hd, :, 0:MLA_NOPE] = kv[:, hd * hw:hd * hw + MLA_NOPE].astype(BF16)
        k_ref[hd, :, MLA_NOPE:MLA_QK] = kr16
        v_ref[hd, 0:MLA_V, :] = kv[:, hd * hw + MLA_NOPE:(hd + 1) * hw].T.astype(BF16)
        v_ref[hd, MLA_V:, :] = _ones_row_block(k_ref.shape[1])
    if with_q:
        scale = MLA_QK ** -0.5 * LOG2E
        qn = qn_ref[...]
        for sb in range(MLA_HEADS * MLA_ROPE // LANES):
            qr = qr_ref[:, sb * LANES:(sb + 1) * LANES]
            if rope:
                qr = _rope_slab(qr, cos_ref[...], sin_ref[...])
            for half in range(LANES // MLA_ROPE):
                hd = sb * (LANES // MLA_ROPE) + half
                q_ref[hd, :, MLA_NOPE:MLA_QK] = (qr[:, half * MLA_ROPE:(half + 1) * MLA_ROPE] * scale).astype(BF16)
        for hd in range(MLA_HEADS):
            q_ref[hd, :, 0:MLA_NOPE] = (qn[:, hd * MLA_NOPE:(hd + 1) * MLA_NOPE] * scale).astype(BF16)


def _mla_prep(ckv_src, kr_src, ckv_col, kr_col, n, g, wkvb, tm, q_src=None, cos=None, sin=None, norm=True):
    rope = cos is not None
    with_q = q_src is not None
    ins = [ckv_src, kr_src, g, wkvb]
    in_specs = [pl.BlockSpec((tm, MLA_KV_RANK), lambda i: (i, ckv_col)),
                pl.BlockSpec((tm, LANES), lambda i: (i, kr_col)),
                pl.BlockSpec((1, MLA_KV_RANK), lambda i: (0, 0)),
                pl.BlockSpec(wkvb.shape, lambda i: (0, 0))]
    if with_q:
        ins += [q_src, q_src]
        in_specs += [pl.BlockSpec((tm, MLA_HEADS * MLA_NOPE), lambda i: (i, COL_MLA_QN // (MLA_HEADS * MLA_NOPE))),
                     pl.BlockSpec((tm, MLA_HEADS * MLA_ROPE), lambda i: (i, COL_MLA_QR // (MLA_HEADS * MLA_ROPE)))]
    if rope:
        nt = cos.shape[0] // tm
        ins += [cos, sin]
        in_specs += [pl.BlockSpec((tm, LANES), lambda i: (i % nt, 0))] * 2
    out_shape = [jax.ShapeDtypeStruct((MLA_HEADS, n, MLA_QK), BF16),
                 jax.ShapeDtypeStruct((MLA_HEADS, VT_ROWS, n), BF16)]
    out_specs = [pl.BlockSpec((MLA_HEADS, tm, MLA_QK), lambda i: (0, i, 0)),
                 pl.BlockSpec((MLA_HEADS, VT_ROWS, tm), lambda i: (0, 0, i))]
    if with_q:
        out_shape.append(jax.ShapeDtypeStruct((MLA_HEADS, n, MLA_QK), BF16))
        out_specs.append(pl.BlockSpec((MLA_HEADS, tm, MLA_QK), lambda i: (0, i, 0)))
    if norm:
        out_shape.append(jax.ShapeDtypeStruct((n, MLA_KV_RANK), F32))
        out_specs.append(pl.BlockSpec((tm, MLA_KV_RANK), lambda i: (i, 0)))
    return pl.pallas_call(
        functools.partial(_mla_prep_kernel, norm=norm, rope=rope, with_q=with_q),
        grid=(n // tm,),
        in_specs=in_specs, out_specs=out_specs, out_shape=out_shape,
        compiler_params=_cparams("parallel"),
        name="mla_prep",
    )(*ins)


def _query_halves(q_ref):
    tq = min(ATTN_TQ, q_ref.shape[0])
    return [slice(i * tq, (i + 1) * tq) for i in range(q_ref.shape[0] // tq)]


def _mla_attn_kernel(*refs, cached):
    if cached:
        q_ref, k_ref, v_ref, kc_ref, vc_ref, o_ref, s_scr, p_scr = refs
        sources = ((k_ref, v_ref), (kc_ref, vc_ref))
    else:
        q_ref, k_ref, v_ref, o_ref, s_scr, p_scr = refs
        sources = ((k_ref, v_ref),)
    halves = _query_halves(q_ref)
    spans, off = [], 0
    for kk_ref, _ in sources:
        spans.append(slice(off, off + kk_ref.shape[0]))
        off += kk_ref.shape[0]
    for i, rows in enumerate(halves):
        for (kk_ref, _), span in zip(sources, spans):
            s_scr[i, span, :] = _dot_nt(kk_ref[...], q_ref[rows, :])
    for i, rows in enumerate(halves):
        s = s_scr[i]
        p_scr[i] = jnp.exp2(s - jnp.max(s, axis=0, keepdims=True)).astype(BF16)
        oe = functools.reduce(jnp.add, [_dot(vv_ref[...], p_scr[i, span, :]) for (_, vv_ref), span in zip(sources, spans)])
        o = oe[0:MLA_V] / oe[MLA_V:MLA_V + 1]
        o_ref[rows, :] = o.T.astype(BF16)


def _mla_attn(q, k, v, bsz, seq, k_c=None, v_c=None):
    tq = min(ATTN_STEP_Q, seq)
    nq = seq // tq
    cached = k_c is not None
    keys = seq
    ins = [q, k, v]
    in_specs = [pl.BlockSpec((None, tq, MLA_QK), lambda h, b, i: (h, b * nq + i, 0)),
                pl.BlockSpec((None, seq, MLA_QK), lambda h, b, i: (h, b, 0)),
                pl.BlockSpec((None, VT_ROWS, seq), lambda h, b, i: (h, 0, b))]
    if cached:
        past = k_c.shape[1] // bsz
        keys += past
        ins += [k_c, v_c]
        in_specs += [pl.BlockSpec((None, past, MLA_QK), lambda h, b, i: (h, b, 0)),
                     pl.BlockSpec((None, VT_ROWS, past), lambda h, b, i: (h, 0, b))]
    nsub = tq // min(ATTN_TQ, tq)
    return pl.pallas_call(
        functools.partial(_mla_attn_kernel, cached=cached),
        grid=(MLA_HEADS, bsz, nq),
        in_specs=in_specs,
        out_specs=pl.BlockSpec((tq, MLA_V), lambda h, b, i: (b * nq + i, h)),
        out_shape=jax.ShapeDtypeStruct((bsz * seq, MLA_HEADS * MLA_V), BF16),
        scratch_shapes=[pltpu.VMEM((nsub, keys, tq // nsub), F32),
                        pltpu.VMEM((nsub, keys, tq // nsub), BF16)],
        compiler_params=_cparams("parallel", "parallel", "parallel"),
        name="mla_attn",
    )(*ins)


def _rope_cast_kernel(x_ref, cos_ref, sin_ref, o_ref):
    for sb in range(x_ref.shape[1] // LANES):
        sl = slice(sb * LANES, (sb + 1) * LANES)
        o_ref[:, sl] = _rope_slab(x_ref[:, sl], cos_ref[...], sin_ref[...]).astype(BF16)


def _rope_cast(src, colblk, n, tm, cos, sin):
    nt = cos.shape[0] // tm
    return pl.pallas_call(
        _rope_cast_kernel,
        grid=(n // tm,),
        in_specs=[pl.BlockSpec((tm, DF_W), lambda i: (i, colblk))]
                 + [pl.BlockSpec((tm, LANES), lambda i: (i % nt, 0))] * 2,
        out_specs=pl.BlockSpec((tm, DF_W), lambda i: (i, 0)),
        out_shape=jax.ShapeDtypeStruct((n, DF_W), BF16),
        compiler_params=_cparams("parallel"),
        name="rope_cast",
    )(src, cos, sin)


def _diff_attn_kernel(*refs, lam_init, cached, rope):
    it = iter(refs)
    q_ref, k_ref, v_ref = next(it), next(it), next(it)
    sources = [(k_ref, v_ref)]
    if cached:
        sources.append((next(it), next(it)))
    if rope:
        cos_ref, sin_ref = next(it), next(it)
    lam_ref, g_ref, o_ref, s_scr, p_scr = next(it), next(it), next(it), next(it), next(it)
    lp = lam_ref[...]
    lam = (jnp.exp(jnp.sum(lp[0:1] * lp[1:2], axis=-1, keepdims=True))
           - jnp.exp(jnp.sum(lp[2:3] * lp[3:4], axis=-1, keepdims=True)) + lam_init)
    halves = _query_halves(q_ref)
    ks = [kk_ref[...].astype(BF16) for kk_ref, _ in sources]
    vs = [vv_ref[...].astype(BF16) for _, vv_ref in sources]
    spans, off = [], 0
    for kk_ref, _ in sources:
        spans.append(slice(off, off + kk_ref.shape[0]))
        off += kk_ref.shape[0]
    for i, rows in enumerate(halves):
        q = q_ref[rows, :]
        if rope:
            q = _rope_slab(q, cos_ref[rows, :], sin_ref[rows, :])
        q = (q * (DF_DQK ** -0.5 * LOG2E)).astype(BF16)
        lane = lax.broadcasted_iota(jnp.int32, q.shape, 1)
        zero = jnp.zeros_like(q)
        for comp, qc in enumerate((jnp.where(lane < DF_DQK, q, zero), jnp.where(lane >= DF_DQK, q, zero))):
            for k, span in zip(ks, spans):
                s_scr[comp, i, :, span] = _dot_nt(qc, k)
    for i, rows in enumerate(halves):
        r = []
        for comp in range(2):
            s = s_scr[comp, i]
            p = jnp.exp2(s - jnp.max(s, axis=-1, keepdims=True))
            r.append(1.0 / jnp.sum(p, axis=-1, keepdims=True))
            s_scr[comp, i] = p
        c = lam * r[1] / r[0]
        p_scr[i] = (s_scr[0, i] - c * s_scr[1, i]).astype(BF16)
        o = functools.reduce(jnp.add, [_dot(p_scr[i, :, span], v) for v, span in zip(vs, spans)]) * r[0]
        o_ref[rows, :] = (_rms(o) * g_ref[...] * (1.0 - lam_init)).astype(BF16)


def _diff_attn(proj, k, kcol, lam_p, g, lam_init, bsz, seq, k_c=None, v_c=None, cos=None, sin=None):
    tq = min(ATTN_STEP_Q, seq)
    nq = seq // tq
    nsub = tq // min(ATTN_TQ, tq)
    cached = k_c is not None
    rope = cos is not None
    keys = seq
    qcol, vcol = COL_DF_Q // DF_DV, COL_DF_V // DF_DV
    ins = [proj, k, proj]
    in_specs = [pl.BlockSpec((tq, DF_DV), lambda b, h, i: (b * nq + i, qcol + h)),
                pl.BlockSpec((seq, DF_DV), lambda b, h, i: (b, kcol + h)),
                pl.BlockSpec((seq, DF_DV), lambda b, h, i: (b, vcol + h))]
    if cached:
        past = k_c.shape[0] // bsz
        keys += past
        ins += [k_c, v_c]
        in_specs += [pl.BlockSpec((past, DF_DV), lambda b, h, i: (b, h)),
                     pl.BlockSpec((past, DF_DV), lambda b, h, i: (b, h))]
    if rope:
        ins += [cos, sin]
        in_specs += [pl.BlockSpec((tq, LANES), lambda b, h, i: (i, 0))] * 2
    ins += [lam_p, g]
    in_specs += [pl.BlockSpec((4, DF_DQK), lambda b, h, i: (0, 0)),
                 pl.BlockSpec((1, DF_DV), lambda b, h, i: (0, 0))]
    return pl.pallas_call(
        functools.partial(_diff_attn_kernel, lam_init=lam_init, cached=cached, rope=rope),
        grid=(bsz, DF_HEADS, nq),
        in_specs=in_specs,
        out_specs=pl.BlockSpec((tq, DF_DV), lambda b, h, i: (b * nq + i, h)),
        out_shape=jax.ShapeDtypeStruct((bsz * seq, DF_W), BF16),
        scratch_shapes=[pltpu.VMEM((2, nsub, tq // nsub, keys), F32),
                        pltpu.VMEM((nsub, tq // nsub, keys), BF16)],
        compiler_params=_cparams("parallel", "parallel", "parallel"),
        name="diff_attn",
    )(*ins)


def _s5_kernel(uf_ref, ub_ref, a_re_ref, a_im_ref, h0_re_ref, h0_im_ref,
               wb_re_ref, wb_im_ref, wc_re_ref, wc_im_ref,
               yf_ref, yb_ref, hr_out_ref, hi_out_ref,
               bu_re, bu_im, h_re, h_im):
    ci = pl.program_id(1)
    tc = uf_ref.shape[1]
    nseq = 2 * S5_SEQS
    gblk, sblk = wb_re_ref.shape[2:]
    ngb = S5_W // gblk

    @pl.when(ci == 0)
    def _():
        h_re[...] = h0_re_ref[...]
        h_im[...] = h0_im_ref[...]

    ri = lax.broadcasted_iota(jnp.int32, (tc, tc), 0)
    cj = lax.broadcasted_iota(jnp.int32, (tc, tc), 1)
    rev = jnp.where(ri + cj == tc - 1, 1.0, 0.0).astype(BF16)

    for d, u_ref in enumerate((uf_ref, ub_ref)):
        us = []
        for s in range(S5_SEQS):
            u = u_ref[s].astype(BF16)
            if d == 1:
                u = _dot(rev, u).astype(BF16)
            us.append(u)
        u_all = jnp.concatenate(us, axis=0)
        for gb in range(ngb):
            ug = u_all[:, gb * gblk:(gb + 1) * gblk]
            for w_ref, dst in ((wb_re_ref, bu_re), (wb_im_ref, bu_im)):
                bu = _dot(ug, w_ref[d, gb])
                for s in range(S5_SEQS):
                    for lk in range(sblk // LANES):
                        dst[gb * (sblk // LANES) + lk, pl.ds(d * S5_SEQS + s, tc, stride=nseq), :] = (
                            bu[s * tc:(s + 1) * tc, lk * LANES:(lk + 1) * LANES])

    nlk = S5_LANE_BLK // LANES
    for lb in range(S5_STATE // S5_LANE_BLK):
        lks = tuple(range(lb * nlk, (lb + 1) * nlk))
        ar = [a_re_ref[:, k * LANES:(k + 1) * LANES] for k in lks]
        ai = [a_im_ref[:, k * LANES:(k + 1) * LANES] for k in lks]

        def step(j, carry):
            r0 = pl.multiple_of(j * nseq, nseq)
            out = []
            for i, k in enumerate(lks):
                hr, hi = carry[2 * i], carry[2 * i + 1]
                nhr = ar[i] * hr - ai[i] * hi + bu_re[k, pl.ds(r0, nseq), :]
                nhi = ar[i] * hi + ai[i] * hr + bu_im[k, pl.ds(r0, nseq), :]
                bu_re[k, pl.ds(r0, nseq), :] = nhr
                bu_im[k, pl.ds(r0, nseq), :] = nhi
                out += [nhr, nhi]
            return tuple(out)

        init = []
        for k in lks:
            init += [h_re[:, k * LANES:(k + 1) * LANES], h_im[:, k * LANES:(k + 1) * LANES]]
        fin = lax.fori_loop(0, tc, step, tuple(init), unroll=8)
        for i, k in enumerate(lks):
            h_re[:, k * LANES:(k + 1) * LANES] = fin[2 * i]
            h_im[:, k * LANES:(k + 1) * LANES] = fin[2 * i + 1]

    def seq_states(src, s):
        parts = [src[k, pl.ds(s, tc, stride=nseq), :] for k in range(S5_STATE // LANES)]
        return jnp.concatenate(parts, axis=1).astype(BF16)

    def reverse_rows(y):
        hi = y.astype(BF16)
        r1 = y - hi.astype(F32)
        mid = r1.astype(BF16)
        lo = (r1 - mid.astype(F32)).astype(BF16)
        return _dot(rev, hi) + _dot(rev, mid) + _dot(rev, lo)

    for d, y_ref in enumerate((yf_ref, yb_ref)):
        hr_all = jnp.concatenate([seq_states(bu_re, d * S5_SEQS + s) for s in range(S5_SEQS)], axis=0)
        hi_all = jnp.concatenate([seq_states(bu_im, d * S5_SEQS + s) for s in range(S5_SEQS)], axis=0)
        for gb in range(ngb):
            ssl = slice(gb * sblk, (gb + 1) * sblk)
            y = _dot(hr_all[:, ssl], wc_re_ref[d, gb]) - _dot(hi_all[:, ssl], wc_im_ref[d, gb])
            for s in range(S5_SEQS):
                ys = y[s * tc:(s + 1) * tc]
                y_ref[s, :, gb * gblk:(gb + 1) * gblk] = reverse_rows(ys) if d == 1 else ys

    @pl.when(ci == pl.num_programs(1) - 1)
    def _():
        hr_out_ref[...] = h_re[...]
        hi_out_ref[...] = h_im[...]


def _s5(proj3, bsz, seq, a_re, a_im, h0_re, h0_im, wb_re, wb_im, wc_re, wc_im):
    tc = min(S5_CHUNK, seq)
    nc = seq // tc
    ng = bsz // S5_SEQS
    nseq = 2 * S5_SEQS
    ucol = COL_S5_U // S5_W
    const4 = lambda g, c: (0, 0, 0, 0)
    wspec = lambda w: pl.BlockSpec(w.shape, const4, pipeline_mode=pl.Buffered(1))
    hspec = pl.BlockSpec((None, nseq, S5_STATE), lambda g, c: (g, 0, 0))
    return pl.pallas_call(
        _s5_kernel,
        grid=(ng, nc),
        in_specs=[pl.BlockSpec((S5_SEQS, tc, S5_W), lambda g, c: (g, c, ucol)),
                  pl.BlockSpec((S5_SEQS, tc, S5_W), lambda g, c: (g, nc - 1 - c, ucol)),
                  pl.BlockSpec((nseq, S5_STATE), lambda g, c: (0, 0)),
                  pl.BlockSpec((nseq, S5_STATE), lambda g, c: (0, 0)),
                  hspec, hspec, wspec(wb_re), wspec(wb_im), wspec(wc_re), wspec(wc_im)],
        out_specs=[pl.BlockSpec((S5_SEQS, tc, S5_W), lambda g, c: (g, c, 0)),
                   pl.BlockSpec((S5_SEQS, tc, S5_W), lambda g, c: (g, nc - 1 - c, 0)),
                   hspec, hspec],
        out_shape=[jax.ShapeDtypeStruct((bsz, seq, S5_W), F32),
                   jax.ShapeDtypeStruct((bsz, seq, S5_W), F32),
                   jax.ShapeDtypeStruct((ng, nseq, S5_STATE), F32),
                   jax.ShapeDtypeStruct((ng, nseq, S5_STATE), F32)],
        scratch_shapes=[pltpu.VMEM((S5_STATE // LANES, tc * nseq, LANES), F32),
                        pltpu.VMEM((S5_STATE // LANES, tc * nseq, LANES), F32),
                        pltpu.VMEM((nseq, S5_STATE), F32),
                        pltpu.VMEM((nseq, S5_STATE), F32)],
        compiler_params=_cparams("parallel", "arbitrary"),
        name="s5",
    )(proj3, proj3, a_re, a_im, h0_re, h0_im, wb_re, wb_im, wc_re, wc_im)


def _s5_post_kernel(yf_ref, yb_ref, u_ref, d_ref, w_ref, o_ref):
    y = (yf_ref[...] + yb_ref[...]) + d_ref[...] * u_ref[...]
    g = jax.nn.gelu(y)
    o_ref[...] = (g * jax.nn.sigmoid(_dot(g.astype(BF16), w_ref[...]))).astype(BF16)


def _s5_post(yf, yb, proj, d, w, tm):
    n = yf.shape[0]
    return pl.pallas_call(
        _s5_post_kernel,
        grid=(n // tm,),
        in_specs=[pl.BlockSpec((tm, S5_W), lambda i: (i, 0)),
                  pl.BlockSpec((tm, S5_W), lambda i: (i, 0)),
                  pl.BlockSpec((tm, S5_W), lambda i: (i, COL_S5_U // S5_W)),
                  pl.BlockSpec((1, S5_W), lambda i: (0, 0)),
                  pl.BlockSpec((S5_W, S5_W), lambda i: (0, 0))],
        out_specs=pl.BlockSpec((tm, S5_W), lambda i: (i, 0)),
        out_shape=jax.ShapeDtypeStruct((n, S5_W), BF16),
        compiler_params=_cparams("parallel"),
        name="s5_post",
    )(yf, yb, proj, d, w)


def _s5_discretise(a_re, a_im, log_dt, b_re, b_im):
    lr = jnp.minimum(a_re, -1e-4)
    li = a_im
    dt = jnp.exp(log_dt)[..., None]
    mag = jnp.exp(dt * lr)
    ab_re, ab_im = mag * jnp.cos(dt * li), mag * jnp.sin(dt * li)
    den = lr * lr + li * li
    nr, ni = ab_re - 1.0, ab_im
    qr = (nr * lr + ni * li) / den
    qi = (ni * lr - nr * li) / den
    bb_re = qr[..., None] * b_re - qi[..., None] * b_im
    bb_im = qr[..., None] * b_im + qi[..., None] * b_re
    return ab_re, ab_im, bb_re, bb_im


def _block_diag(w):
    ngrp = S5_GROUPS_PER_TILE
    d, g, r, c = w.shape
    wg = w.reshape(d, g // ngrp, ngrp, r, c)
    eye = jnp.eye(ngrp, dtype=w.dtype)
    out = jnp.einsum('dbgrc,gh->dbgrhc', wg, eye)
    return out.reshape(d, g // ngrp, ngrp * r, ngrp * c)


def _merge_kernel(y0_ref, y1_ref, y2_ref, y3_ref, g0_ref, g1_ref, g2_ref, g3_ref, w_ref, o_ref):
    acc = None
    for b, (y_ref, g_ref) in enumerate(((y0_ref, g0_ref), (y1_ref, g1_ref), (y2_ref, g2_ref), (y3_ref, g3_ref))):
        term = jax.nn.sigmoid(g_ref[...]) * _dot(y_ref[...], w_ref[b])
        acc = term if acc is None else acc + term
    o_ref[...] = acc.astype(BF16)


def _merge(ys, proj, wb, tm):
    n = ys[0].shape[0]
    tn = DENSE_TN
    gate0 = COL_GATE // tn
    per = D_MODEL // tn

    def gspec(b):
        return pl.BlockSpec((tm, tn), lambda i, j: (i, gate0 + b * per + j))

    return pl.pallas_call(
        _merge_kernel,
        grid=(n // tm, per),
        in_specs=[pl.BlockSpec((tm, BRANCH_W), lambda i, j: (i, 0))] * N_BRANCH
                 + [gspec(b) for b in range(N_BRANCH)]
                 + [pl.BlockSpec((N_BRANCH, BRANCH_W, tn), lambda i, j: (0, 0, j))],
        out_specs=pl.BlockSpec((tm, tn), lambda i, j: (i, j)),
        out_shape=jax.ShapeDtypeStruct((n, D_MODEL), BF16),
        compiler_params=_cparams("parallel", "parallel"),
        name="merge",
    )(*ys, proj, proj, proj, proj, wb)


def _rope_tables(n_tok):
    grid_rows = n_tok // GRID_W
    rows, cols = jnp.meshgrid(jnp.arange(grid_rows, dtype=F32), jnp.arange(GRID_W, dtype=F32), indexing='ij')
    quarter = ROPE_DIM // 4
    inv = ROPE_BASE ** (-jnp.arange(quarter, dtype=F32) / quarter)
    ang_r = rows.reshape(-1, 1) * inv
    ang_c = cols.reshape(-1, 1) * inv
    cos = jnp.concatenate([jnp.cos(ang_r)] * 2 + [jnp.cos(ang_c)] * 2, axis=-1)
    sin = jnp.concatenate([-jnp.sin(ang_r), jnp.sin(ang_r), -jnp.sin(ang_c), jnp.sin(ang_c)], axis=-1)
    return jnp.tile(cos, (1, LANES // ROPE_DIM)), jnp.tile(sin, (1, LANES // ROPE_DIM))


def _permute_w_in(w):
    sizes = (ML_W, ML_W, ML_W, ML_W, 4 * ML_HEADS, MLA_HEADS * MLA_QK, MLA_KV_RANK + MLA_ROPE, S5_W,
             DF_W, DF_W, DF_W, N_BRANCH * D_MODEL)
    splits = tuple(int(s) for s in np.cumsum(sizes)[:-1])
    (ml_q, ml_k, ml_v, ml_o, ml_if, mla_q, mla_kva, s5_u, df_q, df_k, df_v, gate) = jnp.split(w, splits, axis=1)
    mq = mla_q.reshape(D_MODEL, MLA_HEADS, MLA_QK)
    qn = mq[:, :, :MLA_NOPE].reshape(D_MODEL, MLA_HEADS * MLA_NOPE)
    qr = mq[:, :, MLA_NOPE:].reshape(D_MODEL, MLA_HEADS * MLA_ROPE)
    pad = lambda a, width: jnp.pad(a, ((0, 0), (0, width - a.shape[1])))
    cols = [ml_q, ml_k, ml_v, ml_o, s5_u, df_q, df_k, df_v, gate, qn, qr,
            mla_kva[:, :MLA_KV_RANK], pad(mla_kva[:, MLA_KV_RANK:], LANES), pad(ml_if, LANES)]
    out = jnp.concatenate(cols, axis=1)
    return pad(out, PROJ_COLS).astype(BF16)


def _seq_mixers(proj, bsz, seq, lw, lam_init, state, cache, rope):
    n = bsz * seq
    tm = min(TM_SEQ, seq)
    ml_c0, ml_n0, ml_m0, s5_h0r, s5_h0i = state
    r8 = 2 * ML_HEADS

    m0 = jnp.broadcast_to(ml_m0.reshape(bsz, r8, 1), (bsz, r8, LANES))
    hf, hb, c_new, n_new, m_new = _mlstm(proj, bsz, seq, lw['ml_bias'],
                                         ml_c0.reshape(bsz, r8, ML_DH, ML_DH), ml_n0.reshape(bsz, r8, ML_DH), m0)
    y_ml = _ml_post(hf, hb, proj, lw['ml_norm'], tm)

    cos, sin = rope if rope is not None else (None, None)
    k_new, v_new, q_mla, ckv = _mla_prep(proj, proj, COL_MLA_CKV // MLA_KV_RANK, COL_MLA_KR // LANES, n,
                                         lw['mla_kv_norm'], lw['mla_w_kvb'], tm, q_src=proj, cos=cos, sin=sin)
    k_c = v_c = None
    if cache is not None:
        ckv_c, krope_c, dk_c, dv_c = cache
        past = ckv_c.shape[1]
        kr_pad = jnp.pad(krope_c.reshape(bsz * past, MLA_ROPE), ((0, 0), (0, LANES - MLA_ROPE)))
        k_c, v_c = _mla_prep(ckv_c.reshape(bsz * past, MLA_KV_RANK), kr_pad, 0, 0, bsz * past,
                             lw['mla_kv_norm'], lw['mla_w_kvb'], min(tm, bsz * past), norm=False)
    y_mla = _mla_attn(q_mla, k_new, v_new, bsz, seq, k_c, v_c)

    proj3 = proj.reshape(bsz, seq, PROJ_COLS)
    ng = bsz // S5_SEQS

    def pack_state(hs):
        return hs.reshape(ng, S5_SEQS, 2, S5_STATE).transpose(0, 2, 1, 3).reshape(ng, 2 * S5_SEQS, S5_STATE)

    def unpack_state(hs):
        return hs.reshape(ng, 2, S5_SEQS, S5_STATE).transpose(0, 2, 1, 3).reshape(bsz, 2, S5_GROUPS, S5_P)

    yf, yb, hr_new, hi_new = _s5(proj3, bsz, seq, lw['s5_a_re8'], lw['s5_a_im8'],
                                 pack_state(s5_h0r), pack_state(s5_h0i),
                                 lw['s5_wb_re'], lw['s5_wb_im'], lw['s5_wc_re'], lw['s5_wc_im'])
    y_s5 = _s5_post(yf.reshape(n, S5_W), yb.reshape(n, S5_W), proj, lw['s5_d'], lw['s5_w_glu'], tm)

    if cos is not None:
        dk, kcol = _rope_cast(proj, COL_DF_K // DF_W, n, tm, cos, sin), 0
    else:
        dk, kcol = proj, COL_DF_K // DF_DV
    dk_cache = dv_cache = None
    if cache is not None:
        dk_cache = dk_c.reshape(bsz * past, DF_W)
        dv_cache = dv_c.reshape(bsz * past, DF_W)
    y_df = _diff_attn(proj, dk, kcol, lw['df_lambda'], lw['df_norm'], lam_init, bsz, seq,
                      dk_cache, dv_cache, cos, sin)

    cols = lambda c0, w: lax.slice(proj, (0, c0), (n, c0 + w))
    new_ctx = (ckv.reshape(bsz, seq, MLA_KV_RANK),
               cols(COL_MLA_KR, MLA_ROPE).reshape(bsz, seq, MLA_ROPE),
               cols(COL_DF_K, DF_W).reshape(bsz, seq, DF_HEADS, 2 * DF_DQK),
               cols(COL_DF_V, DF_W).reshape(bsz, seq, DF_HEADS, DF_DV),
               c_new.reshape(bsz, 2, ML_HEADS, ML_DH, ML_DH), n_new.reshape(bsz, 2, ML_HEADS, ML_DH),
               m_new[:, :, 0].reshape(bsz, 2, ML_HEADS), unpack_state(hr_new), unpack_state(hi_new))
    return (y_ml, y_mla, y_s5, y_df), new_ctx


def kernel(x_prompt, x_sample, cache_mla_ckv, cache_mla_krope, cache_diff_k, cache_diff_v,
           state_mlstm_c, state_mlstm_n, state_mlstm_m, state_s5_re, state_s5_im, c,
           c_ctx, w_ada, b_ada, norm_mix, norm_ffn, w_in, ml_if_bias, ml_norm, mla_kv_norm,
           mla_w_kvb, s5_a_re, s5_a_im, s5_log_dt, s5_b_re, s5_b_im, s5_c_re, s5_c_im, s5_d,
           s5_w_glu, df_lambda, df_norm, w_branch, w_o, w_ffn_in, w_ffn_out, final_norm):
    bp, sp, _ = x_prompt.shape
    bs, ss, _ = x_sample.shape
    n_p, n_s = bp * sp, bs * ss
    tm = TM_DENSE

    cond = jnp.concatenate([c, c_ctx[None, :], jnp.zeros((SUBLANES - 1 - bs, D_MODEL), F32)], axis=0)
    mods = _ada(cond, w_ada, b_ada).reshape(DEPTH, SUBLANES, 6, 1, D_MODEL)
    rope = _rope_tables(ss)
    latent_row = lambda i: (i * tm) // ss
    prompt_row = lambda i: bs

    xs = x_sample.reshape(n_s, D_MODEL)
    xp = x_prompt.reshape(n_p, D_MODEL)
    zeros_state = (jnp.zeros((bp, 2, ML_HEADS, ML_DH, ML_DH), F32), jnp.zeros((bp, 2, ML_HEADS, ML_DH), F32),
                   jnp.zeros((bp, 2, ML_HEADS), F32), jnp.zeros((bp, 2, S5_GROUPS, S5_P), F32),
                   jnp.zeros((bp, 2, S5_GROUPS, S5_P), F32))
    ctx_out = []
    for l in range(DEPTH):
        lam_init = 0.8 - 0.6 * math.exp(-0.3 * l)
        ab_re, ab_im, bb_re, bb_im = _s5_discretise(s5_a_re[l], s5_a_im[l], s5_log_dt[l], s5_b_re[l], s5_b_im[l])
        rep = lambda a: jnp.repeat(a.reshape(2, S5_STATE), S5_SEQS, axis=0)
        lw = {
            'ml_bias': jnp.pad(ml_if_bias[l].reshape(1, 4 * ML_HEADS), ((0, 0), (0, LANES - 4 * ML_HEADS))),
            'ml_norm': ml_norm[l].reshape(1, ML_W),
            'mla_kv_norm': mla_kv_norm[l].reshape(1, MLA_KV_RANK),
            'mla_w_kvb': mla_w_kvb[l].astype(BF16),
            's5_a_re8': rep(ab_re), 's5_a_im8': rep(ab_im),
            's5_wb_re': _block_diag(bb_re.transpose(0, 1, 3, 2)).astype(BF16),
            's5_wb_im': _block_diag(bb_im.transpose(0, 1, 3, 2)).astype(BF16),
            's5_wc_re': _block_diag(s5_c_re[l].transpose(0, 1, 3, 2)).astype(BF16),
            's5_wc_im': _block_diag(s5_c_im[l].transpose(0, 1, 3, 2)).astype(BF16),
            's5_d': s5_d[l].reshape(1, S5_W),
            's5_w_glu': s5_w_glu[l].astype(BF16),
            'df_lambda': df_lambda[l],
            'df_norm': df_norm[l].reshape(1, DF_DV),
        }
        mod = mods[l]
        w_in_l = _permute_w_in(w_in[l])
        w_branch_l, w_o_l = w_branch[l].astype(BF16), w_o[l].astype(BF16)
        w_ffn_in_l, w_ffn_out_l = w_ffn_in[l].astype(BF16), w_ffn_out[l].astype(BF16)
        g_mix, g_ffn = norm_mix[l].reshape(1, D_MODEL), norm_ffn[l].reshape(1, D_MODEL)

        def layer(x, row_of_tile, bsz, seq, state, cache, rope_tabs):
            proj = _proj_in(x, g_mix, mod, row_of_tile, w_in_l, tm)
            ys, new_ctx = _seq_mixers(proj, bsz, seq, lw, lam_init, state, cache, rope_tabs)
            merged = _merge(list(ys), proj, w_branch_l, tm)
            x = _matmul_resid(merged, w_o_l, x, mod, 2, row_of_tile, tm)
            act = _ffn_in(x, g_ffn, mod, row_of_tile, w_ffn_in_l, tm)
            return _matmul_resid(act, w_ffn_out_l, x, mod, 5, row_of_tile, tm), new_ctx

        xp, new_ctx = layer(xp, prompt_row, bp, sp, zeros_state, None, None)
        ctx_out.append(new_ctx)
        state = (state_mlstm_c[:, l], state_mlstm_n[:, l], state_mlstm_m[:, l], state_s5_re[:, l], state_s5_im[:, l])
        cache = (cache_mla_ckv[:, l], cache_mla_krope[:, l], cache_diff_k[:, l], cache_diff_v[:, l])
        xs, _ = layer(xs, latent_row, bs, ss, state, cache, rope)

    g_fin = final_norm.reshape(1, D_MODEL)
    y_prompt = _final_norm(xp, g_fin, tm).reshape(bp, sp, D_MODEL)
    y_sample = _final_norm(xs, g_fin, tm).reshape(bs, ss, D_MODEL)
    stacked = tuple(jnp.stack([ctx[k] for ctx in ctx_out], axis=1) for k in range(9))
    return (y_prompt, y_sample) + stacked
```

```python
import functools
import math

import jax
import jax.numpy as jnp
import numpy as np
from jax import lax
from jax.experimental import pallas as pl
from jax.experimental.pallas import tpu as pltpu

F32 = jnp.float32
BF16 = jnp.bfloat16

D_MODEL = 2048
DEPTH = 2
GRID_W = 64
ROPE_DIM = 64
ROPE_BASE = 10000.0
RMS_EPS = 1e-6
ML_HEADS = 4
ML_DH = 256
ML_W = ML_HEADS * ML_DH
MLA_HEADS = 8
MLA_NOPE = 128
MLA_ROPE = ROPE_DIM
MLA_V = 128
MLA_KV_RANK = 512
MLA_QK = MLA_NOPE + MLA_ROPE
S5_GROUP = 16
S5_GROUPS = 64
S5_W = S5_GROUPS * S5_GROUP
S5_P = 64
S5_STATE = S5_GROUPS * S5_P
DF_HEADS = 8
DF_DQK = ROPE_DIM
DF_DV = 2 * DF_DQK
DF_W = DF_HEADS * DF_DV
N_BRANCH = 4
BRANCH_W = 1024
FFN_HIDDEN = (8 * D_MODEL + 3 * 256 - 1) // (3 * 256) * 256

LANES = 128
SUBLANES = 8
VMEM_LIMIT_BYTES = 56 * 1024 * 1024

COL_ML_Q = 0
COL_ML_K = 1024
COL_ML_V = 2048
COL_ML_O = 3072
COL_S5_U = 4096
COL_DF_Q = 5120
COL_DF_K = 6144
COL_DF_V = 7168
COL_MLA_QN = 8192
COL_MLA_QR = 9216
COL_MLA_CKV = 9728
COL_MLA_KR = 10240
COL_ML_IF = 10368
PROJ_TN = 1024
PROJ_COLS = 11264
GATE_COLS = N_BRANCH * D_MODEL

LOG2E = math.log2(math.e)
VT_ROWS = 128 + 16

TM_DENSE = 1024
DENSE_TN = 512
TM_SEQ = 512
ATTN_STEP_Q = 1024
ML_CHUNK = 256
S5_CHUNK = 64
S5_SEQS = 4
S5_LANE_BLK = 512
S5_GROUPS_PER_TILE = 16
ATTN_TQ = 256


def _cparams(*sem):
    return pltpu.CompilerParams(dimension_semantics=sem, vmem_limit_bytes=VMEM_LIMIT_BYTES)


def _dot(a, b):
    return jnp.dot(a, b, preferred_element_type=F32)


def _dot_nt(a, b):
    return lax.dot_general(a, b, (((1,), (1,)), ((), ())), preferred_element_type=F32)


def _dot_exact(a, b):
    return jnp.dot(a, b, preferred_element_type=F32, precision=lax.Precision.HIGHEST)


def _rms(x):
    return x * lax.rsqrt(jnp.mean(x * x, axis=-1, keepdims=True) + RMS_EPS)


def _ones_row_block(width):
    row = lax.broadcasted_iota(jnp.int32, (VT_ROWS - 128, width), 0)
    return jnp.where(row == 0, 1.0, 0.0).astype(BF16)


def _rope_slab(x, cos, sin):
    quarter = ROPE_DIM // 4
    lane = lax.broadcasted_iota(jnp.int32, x.shape, 1)
    partner = jnp.where((lane % (2 * quarter)) < quarter, pltpu.roll(x, LANES - quarter, 1),
                        pltpu.roll(x, quarter, 1))
    return x * cos + partner * sin


def _ada_kernel(c_ref, w_ref, b_ref, o_ref):
    c = c_ref[...]
    s = c * jax.nn.sigmoid(c)
    o_ref[...] = _dot(s.astype(BF16), w_ref[...].astype(BF16)) + b_ref[...]


def _ada(cond, w_ada, b_ada):
    rows = cond.shape[0]
    tn = PROJ_TN
    return pl.pallas_call(
        _ada_kernel,
        grid=(DEPTH, 6 * D_MODEL // tn),
        in_specs=[pl.BlockSpec((rows, D_MODEL), lambda l, j: (0, 0)),
                  pl.BlockSpec((None, D_MODEL, tn), lambda l, j: (l, 0, j)),
                  pl.BlockSpec((None, 1, tn), lambda l, j: (l, 0, j))],
        out_specs=pl.BlockSpec((None, rows, tn), lambda l, j: (l, 0, j)),
        out_shape=jax.ShapeDtypeStruct((DEPTH, rows, 6 * D_MODEL), F32),
        compiler_params=_cparams("parallel", "parallel"),
        name="ada",
    )(cond, w_ada, b_ada.reshape(DEPTH, 1, 6 * D_MODEL))


def _mod_spec(which, row_of_tile):
    return pl.BlockSpec((None, None, 1, D_MODEL), lambda i, j: (row_of_tile(i), which, 0, 0))


def _norm_mod(x_ref, g_ref, sc_ref, sh_ref):
    return (_rms(x_ref[...]) * g_ref[...]) * (1.0 + sc_ref[...]) + sh_ref[...]


def _proj_in_kernel(x_ref, g_ref, sc_ref, sh_ref, w_ref, o_ref, og_ref, h_ref):
    j = pl.program_id(1)
    main_tiles = PROJ_COLS // PROJ_TN

    @pl.when(j == 0)
    def _():
        h_ref[...] = _norm_mod(x_ref, g_ref, sc_ref, sh_ref).astype(BF16)

    acc = _dot(h_ref[...], w_ref[...])

    @pl.when(j < main_tiles)
    def _():
        o_ref[...] = acc

    @pl.when(j >= main_tiles)
    def _():
        og_ref[...] = acc.astype(BF16)


def _proj_in(x, g, mod, row_of_tile, w, tm):
    n = x.shape[0]
    main_tiles = PROJ_COLS // PROJ_TN
    return pl.pallas_call(
        _proj_in_kernel,
        grid=(n // tm, (PROJ_COLS + GATE_COLS) // PROJ_TN),
        in_specs=[pl.BlockSpec((tm, D_MODEL), lambda i, j: (i, 0), pipeline_mode=pl.Buffered(1)),
                  pl.BlockSpec((1, D_MODEL), lambda i, j: (0, 0)),
                  _mod_spec(1, row_of_tile), _mod_spec(0, row_of_tile),
                  pl.BlockSpec((D_MODEL, PROJ_TN), lambda i, j: (0, j))],
        out_specs=[pl.BlockSpec((tm, PROJ_TN), lambda i, j: (i, jnp.minimum(j, main_tiles - 1))),
                   pl.BlockSpec((tm, PROJ_TN), lambda i, j: (i, jnp.maximum(j - main_tiles, 0)))],
        out_shape=[jax.ShapeDtypeStruct((n, PROJ_COLS), F32),
                   jax.ShapeDtypeStruct((n, GATE_COLS), BF16)],
        scratch_shapes=[pltpu.VMEM((tm, D_MODEL), BF16)],
        compiler_params=_cparams("parallel", "arbitrary"),
        name="proj_in",
    )(x, g, mod, mod, w)


def _ffn_in_kernel(x_ref, g_ref, sc_ref, sh_ref, wa_ref, wb_ref, o_ref, h_ref):
    @pl.when(pl.program_id(1) == 0)
    def _():
        h_ref[...] = _norm_mod(x_ref, g_ref, sc_ref, sh_ref).astype(BF16)

    h = h_ref[...]
    a = _dot(h, wa_ref[...])
    b = _dot(h, wb_ref[...])
    o_ref[...] = (a * jax.nn.sigmoid(a) * b).astype(BF16)


def _ffn_in(x, g, mod, row_of_tile, w, tm):
    n = x.shape[0]
    tn = DENSE_TN
    nj = FFN_HIDDEN // tn
    return pl.pallas_call(
        _ffn_in_kernel,
        grid=(n // tm, nj),
        in_specs=[pl.BlockSpec((tm, D_MODEL), lambda i, j: (i, 0)),
                  pl.BlockSpec((1, D_MODEL), lambda i, j: (0, 0)),
                  _mod_spec(4, row_of_tile), _mod_spec(3, row_of_tile),
                  pl.BlockSpec((D_MODEL, tn), lambda i, j: (0, j)),
                  pl.BlockSpec((D_MODEL, tn), lambda i, j: (0, nj + j))],
        out_specs=pl.BlockSpec((tm, tn), lambda i, j: (i, j)),
        out_shape=jax.ShapeDtypeStruct((n, FFN_HIDDEN), BF16),
        scratch_shapes=[pltpu.VMEM((tm, D_MODEL), BF16)],
        compiler_params=_cparams("parallel", "arbitrary"),
        name="ffn_in",
    )(x, g, mod, mod, w, w)


def _resid_kernel(a_ref, w_ref, x_ref, gate_ref, o_ref):
    o_ref[...] = x_ref[...] + gate_ref[...] * _dot(a_ref[...], w_ref[...])


def _matmul_resid(a, w, x, mod, which, row_of_tile, tm):
    n, kdim = a.shape
    tn = DENSE_TN
    return pl.pallas_call(
        _resid_kernel,
        grid=(n // tm, D_MODEL // tn),
        in_specs=[pl.BlockSpec((tm, kdim), lambda i, j: (i, 0)),
                  pl.BlockSpec((kdim, tn), lambda i, j: (0, j)),
                  pl.BlockSpec((tm, tn), lambda i, j: (i, j)),
                  pl.BlockSpec((None, None, 1, tn), lambda i, j: (row_of_tile(i), which, 0, j))],
        out_specs=pl.BlockSpec((tm, tn), lambda i, j: (i, j)),
        out_shape=jax.ShapeDtypeStruct((n, D_MODEL), F32),
        compiler_params=_cparams("parallel", "parallel"),
        name="matmul_resid",
    )(a, w, x, mod)


def _final_norm_kernel(x_ref, g_ref, o_ref):
    o_ref[...] = _rms(x_ref[...]) * g_ref[...]


def _final_norm(x, g, tm):
    n = x.shape[0]
    return pl.pallas_call(
        _final_norm_kernel,
        grid=(n // tm,),
        in_specs=[pl.BlockSpec((tm, D_MODEL), lambda i: (i, 0)),
                  pl.BlockSpec((1, D_MODEL), lambda i: (0, 0))],
        out_specs=pl.BlockSpec((tm, D_MODEL), lambda i: (i, 0)),
        out_shape=jax.ShapeDtypeStruct((n, D_MODEL), F32),
        compiler_params=_cparams("parallel"),
        name="final_norm",
    )(x, g)


def _mlstm_kernel(qf_ref, kf_ref, vf_ref, gf_ref, qb_ref, kb_ref, vb_ref, gb_ref, bias_ref,
                  c0_ref, n0_ref, m0_ref,
                  hf_ref, hb_ref, c_out_ref, n_out_ref, m_out_ref,
                  c_scr, n_scr, m_scr):
    ci = pl.program_id(1)
    t = qf_ref.shape[0]

    @pl.when(ci == 0)
    def _():
        c_scr[...] = c0_ref[...]
        n_scr[...] = n0_ref[...]
        m_scr[...] = m0_ref[...]

    row = lax.broadcasted_iota(jnp.int32, (t, t), 0)
    col = lax.broadcasted_iota(jnp.int32, (t, t), 1)
    refs = ((qf_ref, kf_ref, vf_ref, gf_ref, hf_ref), (qb_ref, kb_ref, vb_ref, gb_ref, hb_ref))
    for d in range(2):
        q_ref, k_ref, v_ref, g_ref, h_ref = refs[d]
        keep = (col <= row) if d == 0 else (col >= row)
        cum = jnp.where(keep, 1.0, 0.0).astype(F32)
        gates = g_ref[...] + bias_ref[...]
        csum = _dot_exact(cum, jax.nn.log_sigmoid(gates))
        gates_t = gates.T
        csum_t = csum.T
        last = t - 1 if d == 0 else 0
        for hd in range(ML_HEADS):
            r = d * ML_HEADS + hd
            i_col = d * 2 * ML_HEADS + hd
            f_col = i_col + ML_HEADS
            b_c = csum[:, f_col:f_col + 1]
            b_r = csum_t[f_col:f_col + 1, :]
            li_c = gates[:, i_col:i_col + 1]
            li_r = gates_t[i_col:i_col + 1, :]
            m_st = m_scr[r:r + 1, 0:1]
            c_st = c_scr[r]
            n_st = n_scr[r:r + 1, :]
            sl = slice(hd * ML_DH, (hd + 1) * ML_DH)
            q = q_ref[:, sl]
            k = k_ref[:, sl] * (ML_DH ** -0.5)
            v = v_ref[:, sl]
            qb16 = q.astype(BF16)
            kb16 = k.astype(BF16)

            log_d = jnp.where(keep, b_c - b_r + li_r, -jnp.inf)
            log_inter = b_c + m_st
            m_t = jnp.maximum(log_inter, jnp.max(log_d, axis=1, keepdims=True))
            w_d = jnp.exp(log_d - m_t)
            w_inter = jnp.exp(log_inter - m_t)
            s = _dot_nt(qb16, kb16) * w_d
            num = _dot(s.astype(BF16), v.astype(BF16)) + w_inter * _dot_nt(qb16, c_st.astype(BF16))
            den = jnp.sum(s, axis=1, keepdims=True) + w_inter * jnp.sum(q * n_st, axis=1, keepdims=True)
            h_ref[:, sl] = num / jnp.maximum(jnp.abs(den), jnp.exp(-m_t))

            b_last = b_c[last:last + 1, :]
            log_w = b_last - b_c + li_c
            m_new = jnp.maximum(b_last + m_st, jnp.max(log_w, axis=0, keepdims=True))
            w_s = jnp.exp(log_w - m_new)
            w_c = jnp.exp(b_last + m_st - m_new)
            vw_t = (v * w_s).T.astype(BF16)
            c_scr[r] = w_c * c_st + _dot(vw_t, kb16)
            n_scr[r:r + 1, :] = w_c * n_st + jnp.sum(k * w_s, axis=0, keepdims=True)
            m_scr[r:r + 1, :] = jnp.broadcast_to(m_new, (1, LANES))

    @pl.when(ci == pl.num_programs(1) - 1)
    def _():
        c_out_ref[...] = c_scr[...]
        n_out_ref[...] = n_scr[...]
        m_out_ref[...] = m_scr[...]


def _mlstm(proj, bsz, seq, bias, c0, n0, m0):
    tc = min(ML_CHUNK, seq)
    nc = seq // tc
    r8 = 2 * ML_HEADS

    def fwd(colblk):
        return lambda b, c: (b * nc + c, colblk)

    def bwd(colblk):
        return lambda b, c: (b * nc + nc - 1 - c, colblk)

    def seqspecs(mk):
        return [pl.BlockSpec((tc, ML_W), mk(COL_ML_Q // ML_W)),
                pl.BlockSpec((tc, ML_W), mk(COL_ML_K // ML_W)),
                pl.BlockSpec((tc, ML_W), mk(COL_ML_V // ML_W)),
                pl.BlockSpec((tc, LANES), mk(COL_ML_IF // LANES))]

    state_specs = [pl.BlockSpec((None, r8, ML_DH, ML_DH), lambda b, c: (b, 0, 0, 0)),
                   pl.BlockSpec((None, r8, ML_DH), lambda b, c: (b, 0, 0)),
                   pl.BlockSpec((None, r8, LANES), lambda b, c: (b, 0, 0))]
    return pl.pallas_call(
        _mlstm_kernel,
        grid=(bsz, nc),
        in_specs=seqspecs(fwd) + seqspecs(bwd) + [pl.BlockSpec((1, LANES), lambda b, c: (0, 0))] + state_specs,
        out_specs=[pl.BlockSpec((tc, ML_W), lambda b, c: (b * nc + c, 0)),
                   pl.BlockSpec((tc, ML_W), lambda b, c: (b * nc + nc - 1 - c, 0))] + state_specs,
        out_shape=[jax.ShapeDtypeStruct((bsz * seq, ML_W), F32),
                   jax.ShapeDtypeStruct((bsz * seq, ML_W), F32),
                   jax.ShapeDtypeStruct((bsz, r8, ML_DH, ML_DH), F32),
                   jax.ShapeDtypeStruct((bsz, r8, ML_DH), F32),
                   jax.ShapeDtypeStruct((bsz, r8, LANES), F32)],
        scratch_shapes=[pltpu.VMEM((r8, ML_DH, ML_DH), F32),
                        pltpu.VMEM((r8, ML_DH), F32),
                        pltpu.VMEM((r8, LANES), F32)],
        compiler_params=_cparams("parallel", "arbitrary"),
        name="mlstm",
    )(proj, proj, proj, proj, proj, proj, proj, proj, bias, c0, n0, m0)


def _ml_post_kernel(hf_ref, hb_ref, o_ref, g_ref, y_ref):
    h = hf_ref[...] + hb_ref[...]
    for hd in range(ML_HEADS):
        sl = slice(hd * ML_DH, (hd + 1) * ML_DH)
        y_ref[:, sl] = (_rms(h[:, sl]) * g_ref[:, sl] * jax.nn.sigmoid(o_ref[:, sl])).astype(BF16)


def _ml_post(hf, hb, proj, g, tm):
    n = hf.shape[0]
    return pl.pallas_call(
        _ml_post_kernel,
        grid=(n // tm,),
        in_specs=[pl.BlockSpec((tm, ML_W), lambda i: (i, 0)),
                  pl.BlockSpec((tm, ML_W), lambda i: (i, 0)),
                  pl.BlockSpec((tm, ML_W), lambda i: (i, COL_ML_O // ML_W)),
                  pl.BlockSpec((1, ML_W), lambda i: (0, 0))],
        out_specs=pl.BlockSpec((tm, ML_W), lambda i: (i, 0)),
        out_shape=jax.ShapeDtypeStruct((n, ML_W), BF16),
        compiler_params=_cparams("parallel"),
        name="ml_post",
    )(hf, hb, proj, g)


def _mla_prep_kernel(*refs, norm, rope, with_q):
    it = iter(refs)
    ckv_ref, kr_ref, g_ref, wkvb_ref = next(it), next(it), next(it), next(it)
    if with_q:
        qn_ref, qr_ref = next(it), next(it)
    if rope:
        cos_ref, sin_ref = next(it), next(it)
    k_ref, v_ref = next(it), next(it)
    if with_q:
        q_ref = next(it)
    if norm:
        ckv_out_ref = next(it)

    ckv = ckv_ref[...]
    if norm:
        ckv = _rms(ckv) * g_ref[...]
        ckv_out_ref[...] = ckv
    kv = _dot(ckv.astype(BF16), wkvb_ref[...])
    kr = kr_ref[...]
    if rope:
        kr = _rope_slab(kr, cos_ref[...], sin_ref[...])
    kr16 = kr[:, :MLA_ROPE].astype(BF16)
    hw = MLA_NOPE + MLA_V
    for hd in range(MLA_HEADS):
        k_ref[hd, :, 0:MLA_NOPE] = kv[:, hd * hw:hd * hw + MLA_NOPE].astype(BF16)
        k_ref[hd, :, MLA_NOPE:MLA_QK] = kr16
        v_ref[hd, 0:MLA_V, :] = kv[:, hd * hw + MLA_NOPE:(hd + 1) * hw].T.astype(BF16)
        v_ref[hd, MLA_V:, :] = _ones_row_block(k_ref.shape[1])
    if with_q:
        scale = MLA_QK ** -0.5 * LOG2E
        qn = qn_ref[...]
        for sb in range(MLA_HEADS * MLA_ROPE // LANES):
            qr = qr_ref[:, sb * LANES:(sb + 1) * LANES]
            if rope:
                qr = _rope_slab(qr, cos_ref[...], sin_ref[...])
            for half in range(LANES // MLA_ROPE):
                hd = sb * (LANES // MLA_ROPE) + half
                q_ref[hd, :, MLA_NOPE:MLA_QK] = (qr[:, half * MLA_ROPE:(half + 1) * MLA_ROPE] * scale).astype(BF16)
        for hd in range(MLA_HEADS):
            q_ref[hd, :, 0:MLA_NOPE] = (qn[:, hd * MLA_NOPE:(hd + 1) * MLA_NOPE] * scale).astype(BF16)


def _mla_prep(ckv_src, kr_src, ckv_col, kr_col, n, g, wkvb, tm, q_src=None, cos=None, sin=None, norm=True):
    rope = cos is not None
    with_q = q_src is not None
    ins = [ckv_src, kr_src, g, wkvb]
    in_specs = [pl.BlockSpec((tm, MLA_KV_RANK), lambda i: (i, ckv_col)),
                pl.BlockSpec((tm, LANES), lambda i: (i, kr_col)),
                pl.BlockSpec((1, MLA_KV_RANK), lambda i: (0, 0)),
                pl.BlockSpec(wkvb.shape, lambda i: (0, 0))]
    if with_q:
        ins += [q_src, q_src]
        in_specs += [pl.BlockSpec((tm, MLA_HEADS * MLA_NOPE), lambda i: (i, COL_MLA_QN // (MLA_HEADS * MLA_NOPE))),
                     pl.BlockSpec((tm, MLA_HEADS * MLA_ROPE), lambda i: (i, COL_MLA_QR // (MLA_HEADS * MLA_ROPE)))]
    if rope:
        nt = cos.shape[0] // tm
        ins += [cos, sin]
        in_specs += [pl.BlockSpec((tm, LANES), lambda i: (i % nt, 0))] * 2
    out_shape = [jax.ShapeDtypeStruct((MLA_HEADS, n, MLA_QK), BF16),
                 jax.ShapeDtypeStruct((MLA_HEADS, VT_ROWS, n), BF16)]
    out_specs = [pl.BlockSpec((MLA_HEADS, tm, MLA_QK), lambda i: (0, i, 0)),
                 pl.BlockSpec((MLA_HEADS, VT_ROWS, tm), lambda i: (0, 0, i))]
    if with_q:
        out_shape.append(jax.ShapeDtypeStruct((MLA_HEADS, n, MLA_QK), BF16))
        out_specs.append(pl.BlockSpec((MLA_HEADS, tm, MLA_QK), lambda i: (0, i, 0)))
    if norm:
        out_shape.append(jax.ShapeDtypeStruct((n, MLA_KV_RANK), F32))
        out_specs.append(pl.BlockSpec((tm, MLA_KV_RANK), lambda i: (i, 0)))
    return pl.pallas_call(
        functools.partial(_mla_prep_kernel, norm=norm, rope=rope, with_q=with_q),
        grid=(n // tm,),
        in_specs=in_specs, out_specs=out_specs, out_shape=out_shape,
        compiler_params=_cparams("parallel"),
        name="mla_prep",
    )(*ins)


def _query_halves(q_ref):
    tq = min(ATTN_TQ, q_ref.shape[0])
    return [slice(i * tq, (i + 1) * tq) for i in range(q_ref.shape[0] // tq)]


def _mla_attn_kernel(*refs, cached):
    if cached:
        q_ref, k_ref, v_ref, kc_ref, vc_ref, o_ref, s_scr, p_scr = refs
        sources = ((k_ref, v_ref), (kc_ref, vc_ref))
    else:
        q_ref, k_ref, v_ref, o_ref, s_scr, p_scr = refs
        sources = ((k_ref, v_ref),)
    halves = _query_halves(q_ref)
    spans, off = [], 0
    for kk_ref, _ in sources:
        spans.append(slice(off, off + kk_ref.shape[0]))
        off += kk_ref.shape[0]
    for i, rows in enumerate(halves):
        for (kk_ref, _), span in zip(sources, spans):
            s_scr[i, span, :] = _dot_nt(kk_ref[...], q_ref[rows, :])
    for i, rows in enumerate(halves):
        s = s_scr[i]
        p_scr[i] = jnp.exp2(s - jnp.max(s, axis=0, keepdims=True)).astype(BF16)
        oe = functools.reduce(jnp.add, [_dot(vv_ref[...], p_scr[i, span, :]) for (_, vv_ref), span in zip(sources, spans)])
        o = oe[0:MLA_V] / oe[MLA_V:MLA_V + 1]
        o_ref[rows, :] = o.T.astype(BF16)


def _mla_attn(q, k, v, bsz, seq, k_c=None, v_c=None):
    tq = min(ATTN_STEP_Q, seq)
    nq = seq // tq
    cached = k_c is not None
    keys = seq
    ins = [q, k, v]
    in_specs = [pl.BlockSpec((None, tq, MLA_QK), lambda h, b, i: (h, b * nq + i, 0)),
                pl.BlockSpec((None, seq, MLA_QK), lambda h, b, i: (h, b, 0)),
                pl.BlockSpec((None, VT_ROWS, seq), lambda h, b, i: (h, 0, b))]
    if cached:
        past = k_c.shape[1] // bsz
        keys += past
        ins += [k_c, v_c]
        in_specs += [pl.BlockSpec((None, past, MLA_QK), lambda h, b, i: (h, b, 0)),
                     pl.BlockSpec((None, VT_ROWS, past), lambda h, b, i: (h, 0, b))]
    nsub = tq // min(ATTN_TQ, tq)
    return pl.pallas_call(
        functools.partial(_mla_attn_kernel, cached=cached),
        grid=(MLA_HEADS, bsz, nq),
        in_specs=in_specs,
        out_specs=pl.BlockSpec((tq, MLA_V), lambda h, b, i: (b * nq + i, h)),
        out_shape=jax.ShapeDtypeStruct((bsz * seq, MLA_HEADS * MLA_V), BF16),
        scratch_shapes=[pltpu.VMEM((nsub, keys, tq // nsub), F32),
                        pltpu.VMEM((nsub, keys, tq // nsub), BF16)],
        compiler_params=_cparams("parallel", "parallel", "parallel"),
        name="mla_attn",
    )(*ins)


def _rope_cast_kernel(x_ref, cos_ref, sin_ref, o_ref):
    for sb in range(x_ref.shape[1] // LANES):
        sl = slice(sb * LANES, (sb + 1) * LANES)
        o_ref[:, sl] = _rope_slab(x_ref[:, sl], cos_ref[...], sin_ref[...]).astype(BF16)


def _rope_cast(src, colblk, n, tm, cos, sin):
    nt = cos.shape[0] // tm
    return pl.pallas_call(
        _rope_cast_kernel,
        grid=(n // tm,),
        in_specs=[pl.BlockSpec((tm, DF_W), lambda i: (i, colblk))]
                 + [pl.BlockSpec((tm, LANES), lambda i: (i % nt, 0))] * 2,
        out_specs=pl.BlockSpec((tm, DF_W), lambda i: (i, 0)),
        out_shape=jax.ShapeDtypeStruct((n, DF_W), BF16),
        compiler_params=_cparams("parallel"),
        name="rope_cast",
    )(src, cos, sin)


def _diff_attn_kernel(*refs, lam_init, cached, rope):
    it = iter(refs)
    q_ref, k_ref, v_ref = next(it), next(it), next(it)
    sources = [(k_ref, v_ref)]
    if cached:
        sources.append((next(it), next(it)))
    if rope:
        cos_ref, sin_ref = next(it), next(it)
    lam_ref, g_ref, o_ref, s_scr, p_scr = next(it), next(it), next(it), next(it), next(it)
    lp = lam_ref[...]
    lam = (jnp.exp(jnp.sum(lp[0:1] * lp[1:2], axis=-1, keepdims=True))
           - jnp.exp(jnp.sum(lp[2:3] * lp[3:4], axis=-1, keepdims=True)) + lam_init)
    halves = _query_halves(q_ref)
    ks = [kk_ref[...].astype(BF16) for kk_ref, _ in sources]
    vs = [vv_ref[...].astype(BF16) for _, vv_ref in sources]
    spans, off = [], 0
    for kk_ref, _ in sources:
        spans.append(slice(off, off + kk_ref.shape[0]))
        off += kk_ref.shape[0]
    for i, rows in enumerate(halves):
        q = q_ref[rows, :]
        if rope:
            q = _rope_slab(q, cos_ref[rows, :], sin_ref[rows, :])
        q = (q * (DF_DQK ** -0.5 * LOG2E)).astype(BF16)
        lane = lax.broadcasted_iota(jnp.int32, q.shape, 1)
        zero = jnp.zeros_like(q)
        for comp, qc in enumerate((jnp.where(lane < DF_DQK, q, zero), jnp.where(lane >= DF_DQK, q, zero))):
            for k, span in zip(ks, spans):
                s_scr[comp, i, :, span] = _dot_nt(qc, k)
    for i, rows in enumerate(halves):
        r = []
        for comp in range(2):
            s = s_scr[comp, i]
            p = jnp.exp2(s - jnp.max(s, axis=-1, keepdims=True))
            r.append(1.0 / jnp.sum(p, axis=-1, keepdims=True))
            s_scr[comp, i] = p
        c = lam * r[1] / r[0]
        p_scr[i] = (s_scr[0, i] - c * s_scr[1, i]).astype(BF16)
        o = functools.reduce(jnp.add, [_dot(p_scr[i, :, span], v) for v, span in zip(vs, spans)]) * r[0]
        o_ref[rows, :] = (_rms(o) * g_ref[...] * (1.0 - lam_init)).astype(BF16)


def _diff_attn(proj, k, kcol, lam_p, g, lam_init, bsz, seq, k_c=None, v_c=None, cos=None, sin=None):
    tq = min(ATTN_STEP_Q, seq)
    nq = seq // tq
    nsub = tq // min(ATTN_TQ, tq)
    cached = k_c is not None
    rope = cos is not None
    keys = seq
    qcol, vcol = COL_DF_Q // DF_DV, COL_DF_V // DF_DV
    ins = [proj, k, proj]
    in_specs = [pl.BlockSpec((tq, DF_DV), lambda b, h, i: (b * nq + i, qcol + h)),
                pl.BlockSpec((seq, DF_DV), lambda b, h, i: (b, kcol + h)),
                pl.BlockSpec((seq, DF_DV), lambda b, h, i: (b, vcol + h))]
    if cached:
        past = k_c.shape[0] // bsz
        keys += past
        ins += [k_c, v_c]
        in_specs += [pl.BlockSpec((past, DF_DV), lambda b, h, i: (b, h)),
                     pl.BlockSpec((past, DF_DV), lambda b, h, i: (b, h))]
    if rope:
        ins += [cos, sin]
        in_specs += [pl.BlockSpec((tq, LANES), lambda b, h, i: (i, 0))] * 2
    ins += [lam_p, g]
    in_specs += [pl.BlockSpec((4, DF_DQK), lambda b, h, i: (0, 0)),
                 pl.BlockSpec((1, DF_DV), lambda b, h, i: (0, 0))]
    return pl.pallas_call(
        functools.partial(_diff_attn_kernel, lam_init=lam_init, cached=cached, rope=rope),
        grid=(bsz, DF_HEADS, nq),
        in_specs=in_specs,
        out_specs=pl.BlockSpec((tq, DF_DV), lambda b, h, i: (b * nq + i, h)),
        out_shape=jax.ShapeDtypeStruct((bsz * seq, DF_W), BF16),
        scratch_shapes=[pltpu.VMEM((2, nsub, tq // nsub, keys), F32),
                        pltpu.VMEM((nsub, tq // nsub, keys), BF16)],
        compiler_params=_cparams("parallel", "parallel", "parallel"),
        name="diff_attn",
    )(*ins)


def _s5_kernel(uf_ref, ub_ref, a_re_ref, a_im_ref, h0_re_ref, h0_im_ref,
               wb_re_ref, wb_im_ref, wc_re_ref, wc_im_ref,
               yf_ref, yb_ref, hr_out_ref, hi_out_ref,
               bu_re, bu_im, h_re, h_im):
    ci = pl.program_id(1)
    tc = uf_ref.shape[1]
    nseq = 2 * S5_SEQS
    gblk, sblk = wb_re_ref.shape[2:]
    ngb = S5_W // gblk

    @pl.when(ci == 0)
    def _():
        h_re[...] = h0_re_ref[...]
        h_im[...] = h0_im_ref[...]

    ri = lax.broadcasted_iota(jnp.int32, (tc, tc), 0)
    cj = lax.broadcasted_iota(jnp.int32, (tc, tc), 1)
    rev = jnp.where(ri + cj == tc - 1, 1.0, 0.0).astype(BF16)

    for d, u_ref in enumerate((uf_ref, ub_ref)):
        us = []
        for s in range(S5_SEQS):
            u = u_ref[s].astype(BF16)
            if d == 1:
                u = _dot(rev, u).astype(BF16)
            us.append(u)
        u_all = jnp.concatenate(us, axis=0)
        for gb in range(ngb):
            ug = u_all[:, gb * gblk:(gb + 1) * gblk]
            for w_ref, dst in ((wb_re_ref, bu_re), (wb_im_ref, bu_im)):
                bu = _dot(ug, w_ref[d, gb])
                for s in range(S5_SEQS):
                    for lk in range(sblk // LANES):
                        dst[gb * (sblk // LANES) + lk, pl.ds(d * S5_SEQS + s, tc, stride=nseq), :] = (
                            bu[s * tc:(s + 1) * tc, lk * LANES:(lk + 1) * LANES])

    nlk = S5_LANE_BLK // LANES
    for lb in range(S5_STATE // S5_LANE_BLK):
        lks = tuple(range(lb * nlk, (lb + 1) * nlk))
        ar = [a_re_ref[:, k * LANES:(k + 1) * LANES] for k in lks]
        ai = [a_im_ref[:, k * LANES:(k + 1) * LANES] for k in lks]

        def step(j, carry):
            r0 = pl.multiple_of(j * nseq, nseq)
            out = []
            for i, k in enumerate(lks):
                hr, hi = carry[2 * i], carry[2 * i + 1]
                nhr = ar[i] * hr - ai[i] * hi + bu_re[k, pl.ds(r0, nseq), :]
                nhi = ar[i] * hi + ai[i] * hr + bu_im[k, pl.ds(r0, nseq), :]
                bu_re[k, pl.ds(r0, nseq), :] = nhr
                bu_im[k, pl.ds(r0, nseq), :] = nhi
                out += [nhr, nhi]
            return tuple(out)

        init = []
        for k in lks:
            init += [h_re[:, k * LANES:(k + 1) * LANES], h_im[:, k * LANES:(k + 1) * LANES]]
        fin = lax.fori_loop(0, tc, step, tuple(init), unroll=8)
        for i, k in enumerate(lks):
            h_re[:, k * LANES:(k + 1) * LANES] = fin[2 * i]
            h_im[:, k * LANES:(k + 1) * LANES] = fin[2 * i + 1]

    def seq_states(src, s):
        parts = [src[k, pl.ds(s, tc, stride=nseq), :] for k in range(S5_STATE // LANES)]
        return jnp.concatenate(parts, axis=1).astype(BF16)

    def reverse_rows(y):
        hi = y.astype(BF16)
        r1 = y - hi.astype(F32)
        mid = r1.astype(BF16)
        lo = (r1 - mid.astype(F32)).astype(BF16)
        return _dot(rev, hi) + _dot(rev, mid) + _dot(rev, lo)

    for d, y_ref in enumerate((yf_ref, yb_ref)):
        hr_all = jnp.concatenate([seq_states(bu_re, d * S5_SEQS + s) for s in range(S5_SEQS)], axis=0)
        hi_all = jnp.concatenate([seq_states(bu_im, d * S5_SEQS + s) for s in range(S5_SEQS)], axis=0)
        for gb in range(ngb):
            ssl = slice(gb * sblk, (gb + 1) * sblk)
            y = _dot(hr_all[:, ssl], wc_re_ref[d, gb]) - _dot(hi_all[:, ssl], wc_im_ref[d, gb])
            for s in range(S5_SEQS):
                ys = y[s * tc:(s + 1) * tc]
                y_ref[s, :, gb * gblk:(gb + 1) * gblk] = reverse_rows(ys) if d == 1 else ys

    @pl.when(ci == pl.num_programs(1) - 1)
    def _():
        hr_out_ref[...] = h_re[...]
        hi_out_ref[...] = h_im[...]


def _s5(proj3, bsz, seq, a_re, a_im, h0_re, h0_im, wb_re, wb_im, wc_re, wc_im):
    tc = min(S5_CHUNK, seq)
    nc = seq // tc
    ng = bsz // S5_SEQS
    nseq = 2 * S5_SEQS
    ucol = COL_S5_U // S5_W
    const4 = lambda g, c: (0, 0, 0, 0)
    wspec = lambda w: pl.BlockSpec(w.shape, const4, pipeline_mode=pl.Buffered(1))
    hspec = pl.BlockSpec((None, nseq, S5_STATE), lambda g, c: (g, 0, 0))
    return pl.pallas_call(
        _s5_kernel,
        grid=(ng, nc),
        in_specs=[pl.BlockSpec((S5_SEQS, tc, S5_W), lambda g, c: (g, c, ucol)),
                  pl.BlockSpec((S5_SEQS, tc, S5_W), lambda g, c: (g, nc - 1 - c, ucol)),
                  pl.BlockSpec((nseq, S5_STATE), lambda g, c: (0, 0)),
                  pl.BlockSpec((nseq, S5_STATE), lambda g, c: (0, 0)),
                  hspec, hspec, wspec(wb_re), wspec(wb_im), wspec(wc_re), wspec(wc_im)],
        out_specs=[pl.BlockSpec((S5_SEQS, tc, S5_W), lambda g, c: (g, c, 0)),
                   pl.BlockSpec((S5_SEQS, tc, S5_W), lambda g, c: (g, nc - 1 - c, 0)),
                   hspec, hspec],
        out_shape=[jax.ShapeDtypeStruct((bsz, seq, S5_W), F32),
                   jax.ShapeDtypeStruct((bsz, seq, S5_W), F32),
                   jax.ShapeDtypeStruct((ng, nseq, S5_STATE), F32),
                   jax.ShapeDtypeStruct((ng, nseq, S5_STATE), F32)],
        scratch_shapes=[pltpu.VMEM((S5_STATE // LANES, tc * nseq, LANES), F32),
                        pltpu.VMEM((S5_STATE // LANES, tc * nseq, LANES), F32),
                        pltpu.VMEM((nseq, S5_STATE), F32),
                        pltpu.VMEM((nseq, S5_STATE), F32)],
        compiler_params=_cparams("parallel", "arbitrary"),
        name="s5",
    )(proj3, proj3, a_re, a_im, h0_re, h0_im, wb_re, wb_im, wc_re, wc_im)


def _s5_post_kernel(yf_ref, yb_ref, u_ref, d_ref, w_ref, o_ref):
    y = (yf_ref[...] + yb_ref[...]) + d_ref[...] * u_ref[...]
    g = jax.nn.gelu(y)
    o_ref[...] = (g * jax.nn.sigmoid(_dot(g.astype(BF16), w_ref[...]))).astype(BF16)


def _s5_post(yf, yb, proj, d, w, tm):
    n = yf.shape[0]
    return pl.pallas_call(
        _s5_post_kernel,
        grid=(n // tm,),
        in_specs=[pl.BlockSpec((tm, S5_W), lambda i: (i, 0)),
                  pl.BlockSpec((tm, S5_W), lambda i: (i, 0)),
                  pl.BlockSpec((tm, S5_W), lambda i: (i, COL_S5_U // S5_W)),
                  pl.BlockSpec((1, S5_W), lambda i: (0, 0)),
                  pl.BlockSpec((S5_W, S5_W), lambda i: (0, 0))],
        out_specs=pl.BlockSpec((tm, S5_W), lambda i: (i, 0)),
        out_shape=jax.ShapeDtypeStruct((n, S5_W), BF16),
        compiler_params=_cparams("parallel"),
        name="s5_post",
    )(yf, yb, proj, d, w)


def _s5_discretise(a_re, a_im, log_dt, b_re, b_im):
    lr = jnp.minimum(a_re, -1e-4)
    li = a_im
    dt = jnp.exp(log_dt)[..., None]
    mag = jnp.exp(dt * lr)
    ab_re, ab_im = mag * jnp.cos(dt * li), mag * jnp.sin(dt * li)
    den = lr * lr + li * li
    nr, ni = ab_re - 1.0, ab_im
    qr = (nr * lr + ni * li) / den
    qi = (ni * lr - nr * li) / den
    bb_re = qr[..., None] * b_re - qi[..., None] * b_im
    bb_im = qr[..., None] * b_im + qi[..., None] * b_re
    return ab_re, ab_im, bb_re, bb_im


def _block_diag(w):
    ngrp = S5_GROUPS_PER_TILE
    d, g, r, c = w.shape
    wg = w.reshape(d, g // ngrp, ngrp, r, c)
    eye = jnp.eye(ngrp, dtype=w.dtype)
    out = jnp.einsum('dbgrc,gh->dbgrhc', wg, eye)
    return out.reshape(d, g // ngrp, ngrp * r, ngrp * c)


def _merge_kernel(y0_ref, y1_ref, y2_ref, y3_ref, g0_ref, g1_ref, g2_ref, g3_ref, w_ref, o_ref):
    acc = None
    for b, (y_ref, g_ref) in enumerate(((y0_ref, g0_ref), (y1_ref, g1_ref), (y2_ref, g2_ref), (y3_ref, g3_ref))):
        term = jax.nn.sigmoid(g_ref[...].astype(F32)) * _dot(y_ref[...], w_ref[b])
        acc = term if acc is None else acc + term
    o_ref[...] = acc.astype(BF16)


def _merge(ys, gates, wb, tm):
    n = ys[0].shape[0]
    tn = DENSE_TN
    per = D_MODEL // tn

    def gspec(b):
        return pl.BlockSpec((tm, tn), lambda i, j: (i, b * per + j))

    return pl.pallas_call(
        _merge_kernel,
        grid=(n // tm, per),
        in_specs=[pl.BlockSpec((tm, BRANCH_W), lambda i, j: (i, 0))] * N_BRANCH
                 + [gspec(b) for b in range(N_BRANCH)]
                 + [pl.BlockSpec((N_BRANCH, BRANCH_W, tn), lambda i, j: (0, 0, j))],
        out_specs=pl.BlockSpec((tm, tn), lambda i, j: (i, j)),
        out_shape=jax.ShapeDtypeStruct((n, D_MODEL), BF16),
        compiler_params=_cparams("parallel", "parallel"),
        name="merge",
    )(*ys, gates, gates, gates, gates, wb)


def _rope_tables(n_tok):
    grid_rows = n_tok // GRID_W
    rows, cols = jnp.meshgrid(jnp.arange(grid_rows, dtype=F32), jnp.arange(GRID_W, dtype=F32), indexing='ij')
    quarter = ROPE_DIM // 4
    inv = ROPE_BASE ** (-jnp.arange(quarter, dtype=F32) / quarter)
    ang_r = rows.reshape(-1, 1) * inv
    ang_c = cols.reshape(-1, 1) * inv
    cos = jnp.concatenate([jnp.cos(ang_r)] * 2 + [jnp.cos(ang_c)] * 2, axis=-1)
    sin = jnp.concatenate([-jnp.sin(ang_r), jnp.sin(ang_r), -jnp.sin(ang_c), jnp.sin(ang_c)], axis=-1)
    return jnp.tile(cos, (1, LANES // ROPE_DIM)), jnp.tile(sin, (1, LANES // ROPE_DIM))


def _permute_w_in(w):
    sizes = (ML_W, ML_W, ML_W, ML_W, 4 * ML_HEADS, MLA_HEADS * MLA_QK, MLA_KV_RANK + MLA_ROPE, S5_W,
             DF_W, DF_W, DF_W, N_BRANCH * D_MODEL)
    splits = tuple(int(s) for s in np.cumsum(sizes)[:-1])
    (ml_q, ml_k, ml_v, ml_o, ml_if, mla_q, mla_kva, s5_u, df_q, df_k, df_v, gate) = jnp.split(w, splits, axis=1)
    mq = mla_q.reshape(D_MODEL, MLA_HEADS, MLA_QK)
    qn = mq[:, :, :MLA_NOPE].reshape(D_MODEL, MLA_HEADS * MLA_NOPE)
    qr = mq[:, :, MLA_NOPE:].reshape(D_MODEL, MLA_HEADS * MLA_ROPE)
    pad = lambda a, width: jnp.pad(a, ((0, 0), (0, width - a.shape[1])))
    cols = [ml_q, ml_k, ml_v, ml_o, s5_u, df_q, df_k, df_v, qn, qr,
            mla_kva[:, :MLA_KV_RANK], pad(mla_kva[:, MLA_KV_RANK:], LANES), pad(ml_if, LANES)]
    main = pad(jnp.concatenate(cols, axis=1), PROJ_COLS)
    return jnp.concatenate([main, gate], axis=1).astype(BF16)


def _seq_mixers(proj, bsz, seq, lw, lam_init, state, cache, rope):
    n = bsz * seq
    tm = min(TM_SEQ, seq)
    ml_c0, ml_n0, ml_m0, s5_h0r, s5_h0i = state
    r8 = 2 * ML_HEADS

    m0 = jnp.broadcast_to(ml_m0.reshape(bsz, r8, 1), (bsz, r8, LANES))
    hf, hb, c_new, n_new, m_new = _mlstm(proj, bsz, seq, lw['ml_bias'],
                                         ml_c0.reshape(bsz, r8, ML_DH, ML_DH), ml_n0.reshape(bsz, r8, ML_DH), m0)
    y_ml = _ml_post(hf, hb, proj, lw['ml_norm'], tm)

    cos, sin = rope if rope is not None else (None, None)
    k_new, v_new, q_mla, ckv = _mla_prep(proj, proj, COL_MLA_CKV // MLA_KV_RANK, COL_MLA_KR // LANES, n,
                                         lw['mla_kv_norm'], lw['mla_w_kvb'], tm, q_src=proj, cos=cos, sin=sin)
    k_c = v_c = None
    if cache is not None:
        ckv_c, krope_c, dk_c, dv_c = cache
        past = ckv_c.shape[1]
        kr_pad = jnp.pad(krope_c.reshape(bsz * past, MLA_ROPE), ((0, 0), (0, LANES - MLA_ROPE)))
        k_c, v_c = _mla_prep(ckv_c.reshape(bsz * past, MLA_KV_RANK), kr_pad, 0, 0, bsz * past,
                             lw['mla_kv_norm'], lw['mla_w_kvb'], min(tm, bsz * past), norm=False)
    y_mla = _mla_attn(q_mla, k_new, v_new, bsz, seq, k_c, v_c)

    proj3 = proj.reshape(bsz, seq, PROJ_COLS)
    ng = bsz // S5_SEQS

    def pack_state(hs):
        return hs.reshape(ng, S5_SEQS, 2, S5_STATE).transpose(0, 2, 1, 3).reshape(ng, 2 * S5_SEQS, S5_STATE)

    def unpack_state(hs):
        return hs.reshape(ng, 2, S5_SEQS, S5_STATE).transpose(0, 2, 1, 3).reshape(bsz, 2, S5_GROUPS, S5_P)

    yf, yb, hr_new, hi_new = _s5(proj3, bsz, seq, lw['s5_a_re8'], lw['s5_a_im8'],
                                 pack_state(s5_h0r), pack_state(s5_h0i),
                                 lw['s5_wb_re'], lw['s5_wb_im'], lw['s5_wc_re'], lw['s5_wc_im'])
    y_s5 = _s5_post(yf.reshape(n, S5_W), yb.reshape(n, S5_W), proj, lw['s5_d'], lw['s5_w_glu'], tm)

    if cos is not None:
        dk, kcol = _rope_cast(proj, COL_DF_K // DF_W, n, tm, cos, sin), 0
    else:
        dk, kcol = proj, COL_DF_K // DF_DV
    dk_cache = dv_cache = None
    if cache is not None:
        dk_cache = dk_c.reshape(bsz * past, DF_W)
        dv_cache = dv_c.reshape(bsz * past, DF_W)
    y_df = _diff_attn(proj, dk, kcol, lw['df_lambda'], lw['df_norm'], lam_init, bsz, seq,
                      dk_cache, dv_cache, cos, sin)

    cols = lambda c0, w: lax.slice(proj, (0, c0), (n, c0 + w))
    new_ctx = (ckv.reshape(bsz, seq, MLA_KV_RANK),
               cols(COL_MLA_KR, MLA_ROPE).reshape(bsz, seq, MLA_ROPE),
               cols(COL_DF_K, DF_W).reshape(bsz, seq, DF_HEADS, 2 * DF_DQK),
               cols(COL_DF_V, DF_W).reshape(bsz, seq, DF_HEADS, DF_DV),
               c_new.reshape(bsz, 2, ML_HEADS, ML_DH, ML_DH), n_new.reshape(bsz, 2, ML_HEADS, ML_DH),
               m_new[:, :, 0].reshape(bsz, 2, ML_HEADS), unpack_state(hr_new), unpack_state(hi_new))
    return (y_ml, y_mla, y_s5, y_df), new_ctx


def kernel(x_prompt, x_sample, cache_mla_ckv, cache_mla_krope, cache_diff_k, cache_diff_v,
           state_mlstm_c, state_mlstm_n, state_mlstm_m, state_s5_re, state_s5_im, c,
           c_ctx, w_ada, b_ada, norm_mix, norm_ffn, w_in, ml_if_bias, ml_norm, mla_kv_norm,
           mla_w_kvb, s5_a_re, s5_a_im, s5_log_dt, s5_b_re, s5_b_im, s5_c_re, s5_c_im, s5_d,
           s5_w_glu, df_lambda, df_norm, w_branch, w_o, w_ffn_in, w_ffn_out, final_norm):
    bp, sp, _ = x_prompt.shape
    bs, ss, _ = x_sample.shape
    n_p, n_s = bp * sp, bs * ss
    tm = TM_DENSE

    cond = jnp.concatenate([c, c_ctx[None, :], jnp.zeros((SUBLANES - 1 - bs, D_MODEL), F32)], axis=0)
    mods = _ada(cond, w_ada, b_ada).reshape(DEPTH, SUBLANES, 6, 1, D_MODEL)
    rope = _rope_tables(ss)
    latent_row = lambda i: (i * tm) // ss
    prompt_row = lambda i: bs

    xs = x_sample.reshape(n_s, D_MODEL)
    xp = x_prompt.reshape(n_p, D_MODEL)
    zeros_state = (jnp.zeros((bp, 2, ML_HEADS, ML_DH, ML_DH), F32), jnp.zeros((bp, 2, ML_HEADS, ML_DH), F32),
                   jnp.zeros((bp, 2, ML_HEADS), F32), jnp.zeros((bp, 2, S5_GROUPS, S5_P), F32),
                   jnp.zeros((bp, 2, S5_GROUPS, S5_P), F32))
    ctx_out = []
    for l in range(DEPTH):
        lam_init = 0.8 - 0.6 * math.exp(-0.3 * l)
        ab_re, ab_im, bb_re, bb_im = _s5_discretise(s5_a_re[l], s5_a_im[l], s5_log_dt[l], s5_b_re[l], s5_b_im[l])
        rep = lambda a: jnp.repeat(a.reshape(2, S5_STATE), S5_SEQS, axis=0)
        lw = {
            'ml_bias': jnp.pad(ml_if_bias[l].reshape(1, 4 * ML_HEADS), ((0, 0), (0, LANES - 4 * ML_HEADS))),
            'ml_norm': ml_norm[l].reshape(1, ML_W),
            'mla_kv_norm': mla_kv_norm[l].reshape(1, MLA_KV_RANK),
            'mla_w_kvb': mla_w_kvb[l].astype(BF16),
            's5_a_re8': rep(ab_re), 's5_a_im8': rep(ab_im),
            's5_wb_re': _block_diag(bb_re.transpose(0, 1, 3, 2)).astype(BF16),
            's5_wb_im': _block_diag(bb_im.transpose(0, 1, 3, 2)).astype(BF16),
            's5_wc_re': _block_diag(s5_c_re[l].transpose(0, 1, 3, 2)).astype(BF16),
            's5_wc_im': _block_diag(s5_c_im[l].transpose(0, 1, 3, 2)).astype(BF16),
            's5_d': s5_d[l].reshape(1, S5_W),
            's5_w_glu': s5_w_glu[l].astype(BF16),
            'df_lambda': df_lambda[l],
            'df_norm': df_norm[l].reshape(1, DF_DV),
        }
        mod = mods[l]
        w_in_l = _permute_w_in(w_in[l])
        w_branch_l, w_o_l = w_branch[l].astype(BF16), w_o[l].astype(BF16)
        w_ffn_in_l, w_ffn_out_l = w_ffn_in[l].astype(BF16), w_ffn_out[l].astype(BF16)
        g_mix, g_ffn = norm_mix[l].reshape(1, D_MODEL), norm_ffn[l].reshape(1, D_MODEL)

        def layer(x, row_of_tile, bsz, seq, state, cache, rope_tabs):
            proj, gates = _proj_in(x, g_mix, mod, row_of_tile, w_in_l, tm)
            ys, new_ctx = _seq_mixers(proj, bsz, seq, lw, lam_init, state, cache, rope_tabs)
            merged = _merge(list(ys), gates, w_branch_l, tm)
            x = _matmul_resid(merged, w_o_l, x, mod, 2, row_of_tile, tm)
            act = _ffn_in(x, g_ffn, mod, row_of_tile, w_ffn_in_l, tm)
            return _matmul_resid(act, w_ffn_out_l, x, mod, 5, row_of_tile, tm), new_ctx

        xp, new_ctx = layer(xp, prompt_row, bp, sp, zeros_state, None, None)
        ctx_out.append(new_ctx)
        state = (state_mlstm_c[:, l], state_mlstm_n[:, l], state_mlstm_m[:, l], state_s5_re[:, l], state_s5_im[:, l])
        cache = (cache_mla_ckv[:, l], cache_mla_krope[:, l], cache_diff_k[:, l], cache_diff_v[:, l])
        xs, _ = layer(xs, latent_row, bs, ss, state, cache, rope)

    g_fin = final_norm.reshape(1, D_MODEL)
    y_prompt = _final_norm(xp, g_fin, tm).reshape(bp, sp, D_MODEL)
    y_sample = _final_norm(xs, g_fin, tm).reshape(bs, ss, D_MODEL)
    stacked = tuple(jnp.stack([ctx[k] for ctx in ctx_out], axis=1) for k in range(9))
    return (y_prompt, y_sample) + stacked
```

```python
import functools
import math

import jax
import jax.numpy as jnp
import numpy as np
from jax import lax
from jax.experimental import pallas as pl
from jax.experimental.pallas import tpu as pltpu

F32 = jnp.float32
BF16 = jnp.bfloat16

D_MODEL = 2048
DEPTH = 2
GRID_W = 64
ROPE_DIM = 64
ROPE_BASE = 10000.0
RMS_EPS = 1e-6
ML_HEADS = 4
ML_DH = 256
ML_W = ML_HEADS * ML_DH
MLA_HEADS = 8
MLA_NOPE = 128
MLA_ROPE = ROPE_DIM
MLA_V = 128
MLA_KV_RANK = 512
MLA_QK = MLA_NOPE + MLA_ROPE
S5_GROUP = 16
S5_GROUPS = 64
S5_W = S5_GROUPS * S5_GROUP
S5_P = 64
S5_STATE = S5_GROUPS * S5_P
DF_HEADS = 8
DF_DQK = ROPE_DIM
DF_DV = 2 * DF_DQK
DF_W = DF_HEADS * DF_DV
N_BRANCH = 4
BRANCH_W = 1024
FFN_HIDDEN = (8 * D_MODEL + 3 * 256 - 1) // (3 * 256) * 256

LANES = 128
SUBLANES = 8
VMEM_LIMIT_BYTES = 56 * 1024 * 1024

COL_ML_Q = 0
COL_ML_K = 1024
COL_ML_V = 2048
COL_ML_O = 3072
COL_S5_U = 4096
COL_DF_Q = 5120
COL_DF_K = 6144
COL_DF_V = 7168
COL_GATE = 8192
COL_MLA_QN = 16384
COL_MLA_QR = 17408
COL_MLA_CKV = 17920
COL_MLA_KR = 18432
COL_ML_IF = 18560
PROJ_TN = 512
PROJ_COLS = 19456

LOG2E = math.log2(math.e)
VT_ROWS = 128 + 16

TM_DENSE = 1024
DENSE_TN = 512
TM_SEQ = 512
ATTN_STEP_Q = 1024
ML_CHUNK = 256
S5_CHUNK = 64
S5_SEQS = 4
S5_LANE_BLK = 512
S5_GROUPS_PER_TILE = 16
ATTN_TQ = 256


def _cparams(*sem):
    return pltpu.CompilerParams(dimension_semantics=sem, vmem_limit_bytes=VMEM_LIMIT_BYTES)


def _dot(a, b):
    return jnp.dot(a, b, preferred_element_type=F32)


def _dot_nt(a, b):
    return lax.dot_general(a, b, (((1,), (1,)), ((), ())), preferred_element_type=F32)


def _dot_exact(a, b):
    return jnp.dot(a, b, preferred_element_type=F32, precision=lax.Precision.HIGHEST)


def _rms(x):
    return x * lax.rsqrt(jnp.mean(x * x, axis=-1, keepdims=True) + RMS_EPS)


def _ones_row_block(width):
    row = lax.broadcasted_iota(jnp.int32, (VT_ROWS - 128, width), 0)
    return jnp.where(row == 0, 1.0, 0.0).astype(BF16)


def _rope_slab(x, cos, sin):
    quarter = ROPE_DIM // 4
    lane = lax.broadcasted_iota(jnp.int32, x.shape, 1)
    partner = jnp.where((lane % (2 * quarter)) < quarter, pltpu.roll(x, LANES - quarter, 1),
                        pltpu.roll(x, quarter, 1))
    return x * cos + partner * sin


def _ada_kernel(c_ref, w_ref, b_ref, o_ref):
    c = c_ref[...]
    s = c * jax.nn.sigmoid(c)
    o_ref[...] = _dot(s.astype(BF16), w_ref[...].astype(BF16)) + b_ref[...]


def _ada(cond, w_ada, b_ada):
    rows = cond.shape[0]
    tn = PROJ_TN
    return pl.pallas_call(
        _ada_kernel,
        grid=(DEPTH, 6 * D_MODEL // tn),
        in_specs=[pl.BlockSpec((rows, D_MODEL), lambda l, j: (0, 0)),
                  pl.BlockSpec((None, D_MODEL, tn), lambda l, j: (l, 0, j)),
                  pl.BlockSpec((None, 1, tn), lambda l, j: (l, 0, j))],
        out_specs=pl.BlockSpec((None, rows, tn), lambda l, j: (l, 0, j)),
        out_shape=jax.ShapeDtypeStruct((DEPTH, rows, 6 * D_MODEL), F32),
        compiler_params=_cparams("parallel", "parallel"),
        name="ada",
    )(cond, w_ada, b_ada.reshape(DEPTH, 1, 6 * D_MODEL))


def _mod_spec(which, row_of_tile, tm=None):
    if tm is not None:
        base = row_of_tile
        row_of_tile = lambda i: base((i * tm) // TM_DENSE)
    return pl.BlockSpec((None, None, 1, D_MODEL), lambda i, j: (row_of_tile(i), which, 0, 0))


def _norm_mod(x_ref, g_ref, sc_ref, sh_ref):
    return (_rms(x_ref[...]) * g_ref[...]) * (1.0 + sc_ref[...]) + sh_ref[...]


def _proj_in_kernel(x_ref, g_ref, sc_ref, sh_ref, w_ref, o_ref, h_ref):
    @pl.when(pl.program_id(1) == 0)
    def _():
        h_ref[...] = _norm_mod(x_ref, g_ref, sc_ref, sh_ref).astype(BF16)

    o_ref[...] = _dot(h_ref[...], w_ref[...])


def _proj_in(x, g, mod, row_of_tile, w, tm):
    n = x.shape[0]
    ncol = w.shape[1]
    return pl.pallas_call(
        _proj_in_kernel,
        grid=(n // tm, ncol // PROJ_TN),
        in_specs=[pl.BlockSpec((tm, D_MODEL), lambda i, j: (i, 0), pipeline_mode=pl.Buffered(1)),
                  pl.BlockSpec((1, D_MODEL), lambda i, j: (0, 0)),
                  _mod_spec(1, row_of_tile, tm), _mod_spec(0, row_of_tile, tm),
                  pl.BlockSpec((D_MODEL, PROJ_TN), lambda i, j: (0, j))],
        out_specs=pl.BlockSpec((tm, PROJ_TN), lambda i, j: (i, j)),
        out_shape=jax.ShapeDtypeStruct((n, ncol), F32),
        scratch_shapes=[pltpu.VMEM((tm, D_MODEL), BF16)],
        compiler_params=_cparams("parallel", "arbitrary"),
        name="proj_in",
    )(x, g, mod, mod, w)


def _ffn_in_kernel(x_ref, g_ref, sc_ref, sh_ref, wa_ref, wb_ref, o_ref, h_ref):
    @pl.when(pl.program_id(1) == 0)
    def _():
        h_ref[...] = _norm_mod(x_ref, g_ref, sc_ref, sh_ref).astype(BF16)

    h = h_ref[...]
    a = _dot(h, wa_ref[...])
    b = _dot(h, wb_ref[...])
    o_ref[...] = (a * jax.nn.sigmoid(a) * b).astype(BF16)


def _ffn_in(x, g, mod, row_of_tile, w, tm):
    n = x.shape[0]
    tn = DENSE_TN
    nj = FFN_HIDDEN // tn
    return pl.pallas_call(
        _ffn_in_kernel,
        grid=(n // tm, nj),
        in_specs=[pl.BlockSpec((tm, D_MODEL), lambda i, j: (i, 0)),
                  pl.BlockSpec((1, D_MODEL), lambda i, j: (0, 0)),
                  _mod_spec(4, row_of_tile), _mod_spec(3, row_of_tile),
                  pl.BlockSpec((D_MODEL, tn), lambda i, j: (0, j)),
                  pl.BlockSpec((D_MODEL, tn), lambda i, j: (0, nj + j))],
        out_specs=pl.BlockSpec((tm, tn), lambda i, j: (i, j)),
        out_shape=jax.ShapeDtypeStruct((n, FFN_HIDDEN), BF16),
        scratch_shapes=[pltpu.VMEM((tm, D_MODEL), BF16)],
        compiler_params=_cparams("parallel", "arbitrary"),
        name="ffn_in",
    )(x, g, mod, mod, w, w)


def _resid_kernel(a_ref, w_ref, x_ref, gate_ref, o_ref):
    o_ref[...] = x_ref[...] + gate_ref[...] * _dot(a_ref[...], w_ref[...])


def _matmul_resid(a, w, x, mod, which, row_of_tile, tm):
    n, kdim = a.shape
    tn = DENSE_TN
    return pl.pallas_call(
        _resid_kernel,
        grid=(n // tm, D_MODEL // tn),
        in_specs=[pl.BlockSpec((tm, kdim), lambda i, j: (i, 0)),
                  pl.BlockSpec((kdim, tn), lambda i, j: (0, j)),
                  pl.BlockSpec((tm, tn), lambda i, j: (i, j)),
                  pl.BlockSpec((None, None, 1, tn), lambda i, j: (row_of_tile(i), which, 0, j))],
        out_specs=pl.BlockSpec((tm, tn), lambda i, j: (i, j)),
        out_shape=jax.ShapeDtypeStruct((n, D_MODEL), F32),
        compiler_params=_cparams("parallel", "parallel"),
        name="matmul_resid",
    )(a, w, x, mod)


def _final_norm_kernel(x_ref, g_ref, o_ref):
    o_ref[...] = _rms(x_ref[...]) * g_ref[...]


def _final_norm(x, g, tm):
    n = x.shape[0]
    return pl.pallas_call(
        _final_norm_kernel,
        grid=(n // tm,),
        in_specs=[pl.BlockSpec((tm, D_MODEL), lambda i: (i, 0)),
                  pl.BlockSpec((1, D_MODEL), lambda i: (0, 0))],
        out_specs=pl.BlockSpec((tm, D_MODEL), lambda i: (i, 0)),
        out_shape=jax.ShapeDtypeStruct((n, D_MODEL), F32),
        compiler_params=_cparams("parallel"),
        name="final_norm",
    )(x, g)


def _mlstm_kernel(qf_ref, kf_ref, vf_ref, gf_ref, qb_ref, kb_ref, vb_ref, gb_ref, bias_ref,
                  c0_ref, n0_ref, m0_ref,
                  hf_ref, hb_ref, c_out_ref, n_out_ref, m_out_ref,
                  c_scr, n_scr, m_scr):
    ci = pl.program_id(1)
    t = qf_ref.shape[0]

    @pl.when(ci == 0)
    def _():
        c_scr[...] = c0_ref[...]
        n_scr[...] = n0_ref[...]
        m_scr[...] = m0_ref[...]

    row = lax.broadcasted_iota(jnp.int32, (t, t), 0)
    col = lax.broadcasted_iota(jnp.int32, (t, t), 1)
    refs = ((qf_ref, kf_ref, vf_ref, gf_ref, hf_ref), (qb_ref, kb_ref, vb_ref, gb_ref, hb_ref))
    for d in range(2):
        q_ref, k_ref, v_ref, g_ref, h_ref = refs[d]
        keep = (col <= row) if d == 0 else (col >= row)
        cum = jnp.where(keep, 1.0, 0.0).astype(F32)
        gates = g_ref[...] + bias_ref[...]
        csum = _dot_exact(cum, jax.nn.log_sigmoid(gates))
        gates_t = gates.T
        csum_t = csum.T
        last = t - 1 if d == 0 else 0
        for hd in range(ML_HEADS):
            r = d * ML_HEADS + hd
            i_col = d * 2 * ML_HEADS + hd
            f_col = i_col + ML_HEADS
            b_c = csum[:, f_col:f_col + 1]
            b_r = csum_t[f_col:f_col + 1, :]
            li_c = gates[:, i_col:i_col + 1]
            li_r = gates_t[i_col:i_col + 1, :]
            m_st = m_scr[r:r + 1, 0:1]
            c_st = c_scr[r]
            n_st = n_scr[r:r + 1, :]
            sl = slice(hd * ML_DH, (hd + 1) * ML_DH)
            q = q_ref[:, sl]
            k = k_ref[:, sl] * (ML_DH ** -0.5)
            v = v_ref[:, sl]
            qb16 = q.astype(BF16)
            kb16 = k.astype(BF16)

            log_d = jnp.where(keep, b_c - b_r + li_r, -jnp.inf)
            log_inter = b_c + m_st
            m_t = jnp.maximum(log_inter, jnp.max(log_d, axis=1, keepdims=True))
            w_d = jnp.exp(log_d - m_t)
            w_inter = jnp.exp(log_inter - m_t)
            s = _dot_nt(qb16, kb16) * w_d
            num = _dot(s.astype(BF16), v.astype(BF16)) + w_inter * _dot_nt(qb16, c_st.astype(BF16))
            den = jnp.sum(s, axis=1, keepdims=True) + w_inter * jnp.sum(q * n_st, axis=1, keepdims=True)
            h_ref[:, sl] = num / jnp.maximum(jnp.abs(den), jnp.exp(-m_t))

            b_last = b_c[last:last + 1, :]
            log_w = b_last - b_c + li_c
            m_new = jnp.maximum(b_last + m_st, jnp.max(log_w, axis=0, keepdims=True))
            w_s = jnp.exp(log_w - m_new)
            w_c = jnp.exp(b_last + m_st - m_new)
            vw_t = (v * w_s).T.astype(BF16)
            c_scr[r] = w_c * c_st + _dot(vw_t, kb16)
            n_scr[r:r + 1, :] = w_c * n_st + jnp.sum(k * w_s, axis=0, keepdims=True)
            m_scr[r:r + 1, :] = jnp.broadcast_to(m_new, (1, LANES))

    @pl.when(ci == pl.num_programs(1) - 1)
    def _():
        c_out_ref[...] = c_scr[...]
        n_out_ref[...] = n_scr[...]
        m_out_ref[...] = m_scr[...]


def _mlstm(proj, bsz, seq, bias, c0, n0, m0):
    tc = min(ML_CHUNK, seq)
    nc = seq // tc
    r8 = 2 * ML_HEADS

    def fwd(colblk):
        return lambda b, c: (b * nc + c, colblk)

    def bwd(colblk):
        return lambda b, c: (b * nc + nc - 1 - c, colblk)

    def seqspecs(mk):
        return [pl.BlockSpec((tc, ML_W), mk(COL_ML_Q // ML_W)),
                pl.BlockSpec((tc, ML_W), mk(COL_ML_K // ML_W)),
                pl.BlockSpec((tc, ML_W), mk(COL_ML_V // ML_W)),
                pl.BlockSpec((tc, LANES), mk(COL_ML_IF // LANES))]

    state_specs = [pl.BlockSpec((None, r8, ML_DH, ML_DH), lambda b, c: (b, 0, 0, 0)),
                   pl.BlockSpec((None, r8, ML_DH), lambda b, c: (b, 0, 0)),
                   pl.BlockSpec((None, r8, LANES), lambda b, c: (b, 0, 0))]
    return pl.pallas_call(
        _mlstm_kernel,
        grid=(bsz, nc),
        in_specs=seqspecs(fwd) + seqspecs(bwd) + [pl.BlockSpec((1, LANES), lambda b, c: (0, 0))] + state_specs,
        out_specs=[pl.BlockSpec((tc, ML_W), lambda b, c: (b * nc + c, 0)),
                   pl.BlockSpec((tc, ML_W), lambda b, c: (b * nc + nc - 1 - c, 0))] + state_specs,
        out_shape=[jax.ShapeDtypeStruct((bsz * seq, ML_W), F32),
                   jax.ShapeDtypeStruct((bsz * seq, ML_W), F32),
                   jax.ShapeDtypeStruct((bsz, r8, ML_DH, ML_DH), F32),
                   jax.ShapeDtypeStruct((bsz, r8, ML_DH), F32),
                   jax.ShapeDtypeStruct((bsz, r8, LANES), F32)],
        scratch_shapes=[pltpu.VMEM((r8, ML_DH, ML_DH), F32),
                        pltpu.VMEM((r8, ML_DH), F32),
                        pltpu.VMEM((r8, LANES), F32)],
        compiler_params=_cparams("parallel", "arbitrary"),
        name="mlstm",
    )(proj, proj, proj, proj, proj, proj, proj, proj, bias, c0, n0, m0)


def _ml_post_kernel(hf_ref, hb_ref, o_ref, g_ref, y_ref):
    h = hf_ref[...] + hb_ref[...]
    for hd in range(ML_HEADS):
        sl = slice(hd * ML_DH, (hd + 1) * ML_DH)
        y_ref[:, sl] = (_rms(h[:, sl]) * g_ref[:, sl] * jax.nn.sigmoid(o_ref[:, sl])).astype(BF16)


def _ml_post(hf, hb, proj, g, tm):
    n = hf.shape[0]
    return pl.pallas_call(
        _ml_post_kernel,
        grid=(n // tm,),
        in_specs=[pl.BlockSpec((tm, ML_W), lambda i: (i, 0)),
                  pl.BlockSpec((tm, ML_W), lambda i: (i, 0)),
                  pl.BlockSpec((tm, ML_W), lambda i: (i, COL_ML_O // ML_W)),
                  pl.BlockSpec((1, ML_W), lambda i: (0, 0))],
        out_specs=pl.BlockSpec((tm, ML_W), lambda i: (i, 0)),
        out_shape=jax.ShapeDtypeStruct((n, ML_W), BF16),
        compiler_params=_cparams("parallel"),
        name="ml_post",
    )(hf, hb, proj, g)


def _mla_prep_kernel(*refs, norm, rope, with_q):
    it = iter(refs)
    ckv_ref, kr_ref, g_ref, wkvb_ref = next(it), next(it), next(it), next(it)
    if with_q:
        qn_ref, qr_ref = next(it), next(it)
    if rope:
        cos_ref, sin_ref = next(it), next(it)
    k_ref, v_ref = next(it), next(it)
    if with_q:
        q_ref = next(it)
    if norm:
        ckv_out_ref = next(it)

    ckv = ckv_ref[...]
    if norm:
        ckv = _rms(ckv) * g_ref[...]
        ckv_out_ref[...] = ckv
    kv = _dot(ckv.astype(BF16), wkvb_ref[...])
    kr = kr_ref[...]
    if rope:
        kr = _rope_slab(kr, cos_ref[...], sin_ref[...])
    kr16 = kr[:, :MLA_ROPE].astype(BF16)
    hw = MLA_NOPE + MLA_V
    for hd in range(MLA_HEADS):
        k_ref[hd, :, 0:MLA_NOPE] = kv[:, hd * hw:hd * hw + MLA_NOPE].astype(BF16)
        k_ref[hd, :, MLA_NOPE:MLA_QK] = kr16
        v_ref[hd, 0:MLA_V, :] = kv[:, hd * hw + MLA_NOPE:(hd + 1) * hw].T.astype(BF16)
        v_ref[hd, MLA_V:, :] = _ones_row_block(k_ref.shape[1])
    if with_q:
        scale = MLA_QK ** -0.5 * LOG2E
        qn = qn_ref[...]
        for sb in range(MLA_HEADS * MLA_ROPE // LANES):
            qr = qr_ref[:, sb * LANES:(sb + 1) * LANES]
            if rope:
                qr = _rope_slab(qr, cos_ref[...], sin_ref[...])
            for half in range(LANES // MLA_ROPE):
                hd = sb * (LANES // MLA_ROPE) + half
                q_ref[hd, :, MLA_NOPE:MLA_QK] = (qr[:, half * MLA_ROPE:(half + 1) * MLA_ROPE] * scale).astype(BF16)
        for hd in range(MLA_HEADS):
            q_ref[hd, :, 0:MLA_NOPE] = (qn[:, hd * MLA_NOPE:(hd + 1) * MLA_NOPE] * scale).astype(BF16)


def _mla_prep(ckv_src, kr_src, ckv_col, kr_col, n, g, wkvb, tm, q_src=None, cos=None, sin=None, norm=True):
    rope = cos is not None
    with_q = q_src is not None
    ins = [ckv_src, kr_src, g, wkvb]
    in_specs = [pl.BlockSpec((tm, MLA_KV_RANK), lambda i: (i, ckv_col)),
                pl.BlockSpec((tm, LANES), lambda i: (i, kr_col)),
                pl.BlockSpec((1, MLA_KV_RANK), lambda i: (0, 0)),
                pl.BlockSpec(wkvb.shape, lambda i: (0, 0))]
    if with_q:
        ins += [q_src, q_src]
        in_specs += [pl.BlockSpec((tm, MLA_HEADS * MLA_NOPE), lambda i: (i, COL_MLA_QN // (MLA_HEADS * MLA_NOPE))),
                     pl.BlockSpec((tm, MLA_HEADS * MLA_ROPE), lambda i: (i, COL_MLA_QR // (MLA_HEADS * MLA_ROPE)))]
    if rope:
        nt = cos.shape[0] // tm
        ins += [cos, sin]
        in_specs += [pl.BlockSpec((tm, LANES), lambda i: (i % nt, 0))] * 2
    out_shape = [jax.ShapeDtypeStruct((MLA_HEADS, n, MLA_QK), BF16),
                 jax.ShapeDtypeStruct((MLA_HEADS, VT_ROWS, n), BF16)]
    out_specs = [pl.BlockSpec((MLA_HEADS, tm, MLA_QK), lambda i: (0, i, 0)),
                 pl.BlockSpec((MLA_HEADS, VT_ROWS, tm), lambda i: (0, 0, i))]
    if with_q:
        out_shape.append(jax.ShapeDtypeStruct((MLA_HEADS, n, MLA_QK), BF16))
        out_specs.append(pl.BlockSpec((MLA_HEADS, tm, MLA_QK), lambda i: (0, i, 0)))
    if norm:
        out_shape.append(jax.ShapeDtypeStruct((n, MLA_KV_RANK), F32))
        out_specs.append(pl.BlockSpec((tm, MLA_KV_RANK), lambda i: (i, 0)))
    return pl.pallas_call(
        functools.partial(_mla_prep_kernel, norm=norm, rope=rope, with_q=with_q),
        grid=(n // tm,),
        in_specs=in_specs, out_specs=out_specs, out_shape=out_shape,
        compiler_params=_cparams("parallel"),
        name="mla_prep",
    )(*ins)


def _query_halves(q_ref):
    tq = min(ATTN_TQ, q_ref.shape[0])
    return [slice(i * tq, (i + 1) * tq) for i in range(q_ref.shape[0] // tq)]


def _mla_attn_kernel(*refs, cached):
    if cached:
        q_ref, k_ref, v_ref, kc_ref, vc_ref, o_ref, s_scr, p_scr = refs
        sources = ((k_ref, v_ref), (kc_ref, vc_ref))
    else:
        q_ref, k_ref, v_ref, o_ref, s_scr, p_scr = refs
        sources = ((k_ref, v_ref),)
    halves = _query_halves(q_ref)
    spans, off = [], 0
    for kk_ref, _ in sources:
        spans.append(slice(off, off + kk_ref.shape[0]))
        off += kk_ref.shape[0]
    for i, rows in enumerate(halves):
        for (kk_ref, _), span in zip(sources, spans):
            s_scr[i, span, :] = _dot_nt(kk_ref[...], q_ref[rows, :])
    for i, rows in enumerate(halves):
        s = s_scr[i]
        p_scr[i] = jnp.exp2(s - jnp.max(s, axis=0, keepdims=True)).astype(BF16)
        oe = functools.reduce(jnp.add, [_dot(vv_ref[...], p_scr[i, span, :]) for (_, vv_ref), span in zip(sources, spans)])
        o = oe[0:MLA_V] / oe[MLA_V:MLA_V + 1]
        o_ref[rows, :] = o.T.astype(BF16)


def _mla_attn(q, k, v, bsz, seq, k_c=None, v_c=None):
    tq = min(ATTN_STEP_Q, seq)
    nq = seq // tq
    cached = k_c is not None
    keys = seq
    ins = [q, k, v]
    in_specs = [pl.BlockSpec((None, tq, MLA_QK), lambda h, b, i: (h, b * nq + i, 0)),
                pl.BlockSpec((None, seq, MLA_QK), lambda h, b, i: (h, b, 0)),
                pl.BlockSpec((None, VT_ROWS, seq), lambda h, b, i: (h, 0, b))]
    if cached:
        past = k_c.shape[1] // bsz
        keys += past
        ins += [k_c, v_c]
        in_specs += [pl.BlockSpec((None, past, MLA_QK), lambda h, b, i: (h, b, 0)),
                     pl.BlockSpec((None, VT_ROWS, past), lambda h, b, i: (h, 0, b))]
    nsub = tq // min(ATTN_TQ, tq)
    return pl.pallas_call(
        functools.partial(_mla_attn_kernel, cached=cached),
        grid=(MLA_HEADS, bsz, nq),
        in_specs=in_specs,
        out_specs=pl.BlockSpec((tq, MLA_V), lambda h, b, i: (b * nq + i, h)),
        out_shape=jax.ShapeDtypeStruct((bsz * seq, MLA_HEADS * MLA_V), BF16),
        scratch_shapes=[pltpu.VMEM((nsub, keys, tq // nsub), F32),
                        pltpu.VMEM((nsub, keys, tq // nsub), BF16)],
        compiler_params=_cparams("parallel", "parallel", "parallel"),
        name="mla_attn",
    )(*ins)


def _rope_cast_kernel(x_ref, cos_ref, sin_ref, o_ref):
    for sb in range(x_ref.shape[1] // LANES):
        sl = slice(sb * LANES, (sb + 1) * LANES)
        o_ref[:, sl] = _rope_slab(x_ref[:, sl], cos_ref[...], sin_ref[...]).astype(BF16)


def _rope_cast(src, colblk, n, tm, cos, sin):
    nt = cos.shape[0] // tm
    return pl.pallas_call(
        _rope_cast_kernel,
        grid=(n // tm,),
        in_specs=[pl.BlockSpec((tm, DF_W), lambda i: (i, colblk))]
                 + [pl.BlockSpec((tm, LANES), lambda i: (i % nt, 0))] * 2,
        out_specs=pl.BlockSpec((tm, DF_W), lambda i: (i, 0)),
        out_shape=jax.ShapeDtypeStruct((n, DF_W), BF16),
        compiler_params=_cparams("parallel"),
        name="rope_cast",
    )(src, cos, sin)


def _diff_attn_kernel(*refs, lam_init, cached, rope):
    it = iter(refs)
    q_ref, k_ref, v_ref = next(it), next(it), next(it)
    sources = [(k_ref, v_ref)]
    if cached:
        sources.append((next(it), next(it)))
    if rope:
        cos_ref, sin_ref = next(it), next(it)
    lam_ref, g_ref, o_ref, s_scr, p_scr = next(it), next(it), next(it), next(it), next(it)
    lp = lam_ref[...]
    lam = (jnp.exp(jnp.sum(lp[0:1] * lp[1:2], axis=-1, keepdims=True))
           - jnp.exp(jnp.sum(lp[2:3] * lp[3:4], axis=-1, keepdims=True)) + lam_init)
    halves = _query_halves(q_ref)
    ks = [kk_ref[...].astype(BF16) for kk_ref, _ in sources]
    vs = [vv_ref[...].astype(BF16) for _, vv_ref in sources]
    spans, off = [], 0
    for kk_ref, _ in sources:
        spans.append(slice(off, off + kk_ref.shape[0]))
        off += kk_ref.shape[0]
    for i, rows in enumerate(halves):
        q = q_ref[rows, :]
        if rope:
            q = _rope_slab(q, cos_ref[rows, :], sin_ref[rows, :])
        q = (q * (DF_DQK ** -0.5 * LOG2E)).astype(BF16)
        lane = lax.broadcasted_iota(jnp.int32, q.shape, 1)
        zero = jnp.zeros_like(q)
        for comp, qc in enumerate((jnp.where(lane < DF_DQK, q, zero), jnp.where(lane >= DF_DQK, q, zero))):
            for k, span in zip(ks, spans):
                s_scr[comp, i, :, span] = _dot_nt(qc, k)
    for i, rows in enumerate(halves):
        r = []
        for comp in range(2):
            s = s_scr[comp, i]
            p = jnp.exp2(s - jnp.max(s, axis=-1, keepdims=True))
            r.append(1.0 / jnp.sum(p, axis=-1, keepdims=True))
            s_scr[comp, i] = p
        c = lam * r[1] / r[0]
        p_scr[i] = (s_scr[0, i] - c * s_scr[1, i]).astype(BF16)
        o = functools.reduce(jnp.add, [_dot(p_scr[i, :, span], v) for v, span in zip(vs, spans)]) * r[0]
        o_ref[rows, :] = (_rms(o) * g_ref[...] * (1.0 - lam_init)).astype(BF16)


def _diff_attn(proj, k, kcol, lam_p, g, lam_init, bsz, seq, k_c=None, v_c=None, cos=None, sin=None):
    tq = min(ATTN_STEP_Q, seq)
    nq = seq // tq
    nsub = tq // min(ATTN_TQ, tq)
    cached = k_c is not None
    rope = cos is not None
    keys = seq
    qcol, vcol = COL_DF_Q // DF_DV, COL_DF_V // DF_DV
    ins = [proj, k, proj]
    in_specs = [pl.BlockSpec((tq, DF_DV), lambda b, h, i: (b * nq + i, qcol + h)),
                pl.BlockSpec((seq, DF_DV), lambda b, h, i: (b, kcol + h)),
                pl.BlockSpec((seq, DF_DV), lambda b, h, i: (b, vcol + h))]
    if cached:
        past = k_c.shape[0] // bsz
        keys += past
        ins += [k_c, v_c]
        in_specs += [pl.BlockSpec((past, DF_DV), lambda b, h, i: (b, h)),
                     pl.BlockSpec((past, DF_DV), lambda b, h, i: (b, h))]
    if rope:
        ins += [cos, sin]
        in_specs += [pl.BlockSpec((tq, LANES), lambda b, h, i: (i, 0))] * 2
    ins += [lam_p, g]
    in_specs += [pl.BlockSpec((4, DF_DQK), lambda b, h, i: (0, 0)),
                 pl.BlockSpec((1, DF_DV), lambda b, h, i: (0, 0))]
    return pl.pallas_call(
        functools.partial(_diff_attn_kernel, lam_init=lam_init, cached=cached, rope=rope),
        grid=(bsz, DF_HEADS, nq),
        in_specs=in_specs,
        out_specs=pl.BlockSpec((tq, DF_DV), lambda b, h, i: (b * nq + i, h)),
        out_shape=jax.ShapeDtypeStruct((bsz * seq, DF_W), BF16),
        scratch_shapes=[pltpu.VMEM((2, nsub, tq // nsub, keys), F32),
                        pltpu.VMEM((nsub, tq // nsub, keys), BF16)],
        compiler_params=_cparams("parallel", "parallel", "parallel"),
        name="diff_attn",
    )(*ins)


def _s5_kernel(uf_ref, ub_ref, a_re_ref, a_im_ref, h0_re_ref, h0_im_ref,
               wb_re_ref, wb_im_ref, wc_re_ref, wc_im_ref,
               yf_ref, yb_ref, hr_out_ref, hi_out_ref,
               bu_re, bu_im, h_re, h_im):
    ci = pl.program_id(1)
    tc = uf_ref.shape[1]
    nseq = 2 * S5_SEQS
    gblk, sblk = wb_re_ref.shape[2:]
    ngb = S5_W // gblk

    @pl.when(ci == 0)
    def _():
        h_re[...] = h0_re_ref[...]
        h_im[...] = h0_im_ref[...]

    ri = lax.broadcasted_iota(jnp.int32, (tc, tc), 0)
    cj = lax.broadcasted_iota(jnp.int32, (tc, tc), 1)
    rev = jnp.where(ri + cj == tc - 1, 1.0, 0.0).astype(BF16)

    for d, u_ref in enumerate((uf_ref, ub_ref)):
        us = []
        for s in range(S5_SEQS):
            u = u_ref[s].astype(BF16)
            if d == 1:
                u = _dot(rev, u).astype(BF16)
            us.append(u)
        u_all = jnp.concatenate(us, axis=0)
        for gb in range(ngb):
            ug = u_all[:, gb * gblk:(gb + 1) * gblk]
            for w_ref, dst in ((wb_re_ref, bu_re), (wb_im_ref, bu_im)):
                bu = _dot(ug, w_ref[d, gb])
                for s in range(S5_SEQS):
                    for lk in range(sblk // LANES):
                        dst[gb * (sblk // LANES) + lk, pl.ds(d * S5_SEQS + s, tc, stride=nseq), :] = (
                            bu[s * tc:(s + 1) * tc, lk * LANES:(lk + 1) * LANES])

    nlk = S5_LANE_BLK // LANES
    for lb in range(S5_STATE // S5_LANE_BLK):
        lks = tuple(range(lb * nlk, (lb + 1) * nlk))
        ar = [a_re_ref[:, k * LANES:(k + 1) * LANES] for k in lks]
        ai = [a_im_ref[:, k * LANES:(k + 1) * LANES] for k in lks]

        def step(j, carry):
            r0 = pl.multiple_of(j * nseq, nseq)
            out = []
            for i, k in enumerate(lks):
                hr, hi = carry[2 * i], carry[2 * i + 1]
                nhr = ar[i] * hr - ai[i] * hi + bu_re[k, pl.ds(r0, nseq), :]
                nhi = ar[i] * hi + ai[i] * hr + bu_im[k, pl.ds(r0, nseq), :]
                bu_re[k, pl.ds(r0, nseq), :] = nhr
                bu_im[k, pl.ds(r0, nseq), :] = nhi
                out += [nhr, nhi]
            return tuple(out)

        init = []
        for k in lks:
            init += [h_re[:, k * LANES:(k + 1) * LANES], h_im[:, k * LANES:(k + 1) * LANES]]
        fin = lax.fori_loop(0, tc, step, tuple(init), unroll=8)
        for i, k in enumerate(lks):
            h_re[:, k * LANES:(k + 1) * LANES] = fin[2 * i]
            h_im[:, k * LANES:(k + 1) * LANES] = fin[2 * i + 1]

    def seq_states(src, s):
        parts = [src[k, pl.ds(s, tc, stride=nseq), :] for k in range(S5_STATE // LANES)]
        return jnp.concatenate(parts, axis=1).astype(BF16)

    def reverse_rows(y):
        hi = y.astype(BF16)
        r1 = y - hi.astype(F32)
        mid = r1.astype(BF16)
        lo = (r1 - mid.astype(F32)).astype(BF16)
        return _dot(rev, hi) + _dot(rev, mid) + _dot(rev, lo)

    for d, y_ref in enumerate((yf_ref, yb_ref)):
        hr_all = jnp.concatenate([seq_states(bu_re, d * S5_SEQS + s) for s in range(S5_SEQS)], axis=0)
        hi_all = jnp.concatenate([seq_states(bu_im, d * S5_SEQS + s) for s in range(S5_SEQS)], axis=0)
        for gb in range(ngb):
            ssl = slice(gb * sblk, (gb + 1) * sblk)
            y = _dot(hr_all[:, ssl], wc_re_ref[d, gb]) - _dot(hi_all[:, ssl], wc_im_ref[d, gb])
            for s in range(S5_SEQS):
                ys = y[s * tc:(s + 1) * tc]
                y_ref[s, :, gb * gblk:(gb + 1) * gblk] = reverse_rows(ys) if d == 1 else ys

    @pl.when(ci == pl.num_programs(1) - 1)
    def _():
        hr_out_ref[...] = h_re[...]
        hi_out_ref[...] = h_im[...]


def _s5(proj3, bsz, seq, a_re, a_im, h0_re, h0_im, wb_re, wb_im, wc_re, wc_im):
    tc = min(S5_CHUNK, seq)
    nc = seq // tc
    ng = bsz // S5_SEQS
    nseq = 2 * S5_SEQS
    ucol = COL_S5_U // S5_W
    const4 = lambda g, c: (0, 0, 0, 0)
    wspec = lambda w: pl.BlockSpec(w.shape, const4, pipeline_mode=pl.Buffered(1))
    hspec = pl.BlockSpec((None, nseq, S5_STATE), lambda g, c: (g, 0, 0))
    return pl.pallas_call(
        _s5_kernel,
        grid=(ng, nc),
        in_specs=[pl.BlockSpec((S5_SEQS, tc, S5_W), lambda g, c: (g, c, ucol)),
                  pl.BlockSpec((S5_SEQS, tc, S5_W), lambda g, c: (g, nc - 1 - c, ucol)),
                  pl.BlockSpec((nseq, S5_STATE), lambda g, c: (0, 0)),
                  pl.BlockSpec((nseq, S5_STATE), lambda g, c: (0, 0)),
                  hspec, hspec, wspec(wb_re), wspec(wb_im), wspec(wc_re), wspec(wc_im)],
        out_specs=[pl.BlockSpec((S5_SEQS, tc, S5_W), lambda g, c: (g, c, 0)),
                   pl.BlockSpec((S5_SEQS, tc, S5_W), lambda g, c: (g, nc - 1 - c, 0)),
                   hspec, hspec],
        out_shape=[jax.ShapeDtypeStruct((bsz, seq, S5_W), F32),
                   jax.ShapeDtypeStruct((bsz, seq, S5_W), F32),
                   jax.ShapeDtypeStruct((ng, nseq, S5_STATE), F32),
                   jax.ShapeDtypeStruct((ng, nseq, S5_STATE), F32)],
        scratch_shapes=[pltpu.VMEM((S5_STATE // LANES, tc * nseq, LANES), F32),
                        pltpu.VMEM((S5_STATE // LANES, tc * nseq, LANES), F32),
                        pltpu.VMEM((nseq, S5_STATE), F32),
                        pltpu.VMEM((nseq, S5_STATE), F32)],
        compiler_params=_cparams("parallel", "arbitrary"),
        name="s5",
    )(proj3, proj3, a_re, a_im, h0_re, h0_im, wb_re, wb_im, wc_re, wc_im)


def _s5_post_kernel(yf_ref, yb_ref, u_ref, d_ref, w_ref, o_ref):
    y = (yf_ref[...] + yb_ref[...]) + d_ref[...] * u_ref[...]
    g = jax.nn.gelu(y)
    o_ref[...] = (g * jax.nn.sigmoid(_dot(g.astype(BF16), w_ref[...]))).astype(BF16)


def _s5_post(yf, yb, proj, d, w, tm):
    n = yf.shape[0]
    return pl.pallas_call(
        _s5_post_kernel,
        grid=(n // tm,),
        in_specs=[pl.BlockSpec((tm, S5_W), lambda i: (i, 0)),
                  pl.BlockSpec((tm, S5_W), lambda i: (i, 0)),
                  pl.BlockSpec((tm, S5_W), lambda i: (i, COL_S5_U // S5_W)),
                  pl.BlockSpec((1, S5_W), lambda i: (0, 0)),
                  pl.BlockSpec((S5_W, S5_W), lambda i: (0, 0))],
        out_specs=pl.BlockSpec((tm, S5_W), lambda i: (i, 0)),
        out_shape=jax.ShapeDtypeStruct((n, S5_W), BF16),
        compiler_params=_cparams("parallel"),
        name="s5_post",
    )(yf, yb, proj, d, w)


def _s5_discretise(a_re, a_im, log_dt, b_re, b_im):
    lr = jnp.minimum(a_re, -1e-4)
    li = a_im
    dt = jnp.exp(log_dt)[..., None]
    mag = jnp.exp(dt * lr)
    ab_re, ab_im = mag * jnp.cos(dt * li), mag * jnp.sin(dt * li)
    den = lr * lr + li * li
    nr, ni = ab_re - 1.0, ab_im
    qr = (nr * lr + ni * li) / den
    qi = (ni * lr - nr * li) / den
    bb_re = qr[..., None] * b_re - qi[..., None] * b_im
    bb_im = qr[..., None] * b_im + qi[..., None] * b_re
    return ab_re, ab_im, bb_re, bb_im


def _block_diag(w):
    ngrp = S5_GROUPS_PER_TILE
    d, g, r, c = w.shape
    wg = w.reshape(d, g // ngrp, ngrp, r, c)
    eye = jnp.eye(ngrp, dtype=w.dtype)
    out = jnp.einsum('dbgrc,gh->dbgrhc', wg, eye)
    return out.reshape(d, g // ngrp, ngrp * r, ngrp * c)


def _merge_kernel(y0_ref, y1_ref, y2_ref, y3_ref, g0_ref, g1_ref, g2_ref, g3_ref, w_ref, o_ref):
    acc = None
    for b, (y_ref, g_ref) in enumerate(((y0_ref, g0_ref), (y1_ref, g1_ref), (y2_ref, g2_ref), (y3_ref, g3_ref))):
        term = jax.nn.sigmoid(g_ref[...]) * _dot(y_ref[...], w_ref[b])
        acc = term if acc is None else acc + term
    o_ref[...] = acc.astype(BF16)


def _merge(ys, proj, wb, tm):
    n = ys[0].shape[0]
    tn = DENSE_TN
    gate0 = COL_GATE // tn
    per = D_MODEL // tn

    def gspec(b):
        return pl.BlockSpec((tm, tn), lambda i, j: (i, gate0 + b * per + j))

    return pl.pallas_call(
        _merge_kernel,
        grid=(n // tm, per),
        in_specs=[pl.BlockSpec((tm, BRANCH_W), lambda i, j: (i, 0))] * N_BRANCH
                 + [gspec(b) for b in range(N_BRANCH)]
                 + [pl.BlockSpec((N_BRANCH, BRANCH_W, tn), lambda i, j: (0, 0, j))],
        out_specs=pl.BlockSpec((tm, tn), lambda i, j: (i, j)),
        out_shape=jax.ShapeDtypeStruct((n, D_MODEL), BF16),
        compiler_params=_cparams("parallel", "parallel"),
        name="merge",
    )(*ys, proj, proj, proj, proj, wb)


def _rope_tables(n_tok):
    grid_rows = n_tok // GRID_W
    rows, cols = jnp.meshgrid(jnp.arange(grid_rows, dtype=F32), jnp.arange(GRID_W, dtype=F32), indexing='ij')
    quarter = ROPE_DIM // 4
    inv = ROPE_BASE ** (-jnp.arange(quarter, dtype=F32) / quarter)
    ang_r = rows.reshape(-1, 1) * inv
    ang_c = cols.reshape(-1, 1) * inv
    cos = jnp.concatenate([jnp.cos(ang_r)] * 2 + [jnp.cos(ang_c)] * 2, axis=-1)
    sin = jnp.concatenate([-jnp.sin(ang_r), jnp.sin(ang_r), -jnp.sin(ang_c), jnp.sin(ang_c)], axis=-1)
    return jnp.tile(cos, (1, LANES // ROPE_DIM)), jnp.tile(sin, (1, LANES // ROPE_DIM))


def _permute_w_in(w):
    sizes = (ML_W, ML_W, ML_W, ML_W, 4 * ML_HEADS, MLA_HEADS * MLA_QK, MLA_KV_RANK + MLA_ROPE, S5_W,
             DF_W, DF_W, DF_W, N_BRANCH * D_MODEL)
    splits = tuple(int(s) for s in np.cumsum(sizes)[:-1])
    (ml_q, ml_k, ml_v, ml_o, ml_if, mla_q, mla_kva, s5_u, df_q, df_k, df_v, gate) = jnp.split(w, splits, axis=1)
    mq = mla_q.reshape(D_MODEL, MLA_HEADS, MLA_QK)
    qn = mq[:, :, :MLA_NOPE].reshape(D_MODEL, MLA_HEADS * MLA_NOPE)
    qr = mq[:, :, MLA_NOPE:].reshape(D_MODEL, MLA_HEADS * MLA_ROPE)
    pad = lambda a, width: jnp.pad(a, ((0, 0), (0, width - a.shape[1])))
    cols = [ml_q, ml_k, ml_v, ml_o, s5_u, df_q, df_k, df_v, gate, qn, qr,
            mla_kva[:, :MLA_KV_RANK], pad(mla_kva[:, MLA_KV_RANK:], LANES), pad(ml_if, LANES)]
    out = jnp.concatenate(cols, axis=1)
    return pad(out, PROJ_COLS).astype(BF16)


def _seq_mixers(proj, bsz, seq, lw, lam_init, state, cache, rope):
    n = bsz * seq
    tm = min(TM_SEQ, seq)
    ml_c0, ml_n0, ml_m0, s5_h0r, s5_h0i = state
    r8 = 2 * ML_HEADS

    m0 = jnp.broadcast_to(ml_m0.reshape(bsz, r8, 1), (bsz, r8, LANES))
    hf, hb, c_new, n_new, m_new = _mlstm(proj, bsz, seq, lw['ml_bias'],
                                         ml_c0.reshape(bsz, r8, ML_DH, ML_DH), ml_n0.reshape(bsz, r8, ML_DH), m0)
    y_ml = _ml_post(hf, hb, proj, lw['ml_norm'], tm)

    cos, sin = rope if rope is not None else (None, None)
    k_new, v_new, q_mla, ckv = _mla_prep(proj, proj, COL_MLA_CKV // MLA_KV_RANK, COL_MLA_KR // LANES, n,
                                         lw['mla_kv_norm'], lw['mla_w_kvb'], tm, q_src=proj, cos=cos, sin=sin)
    k_c = v_c = None
    if cache is not None:
        ckv_c, krope_c, dk_c, dv_c = cache
        past = ckv_c.shape[1]
        kr_pad = jnp.pad(krope_c.reshape(bsz * past, MLA_ROPE), ((0, 0), (0, LANES - MLA_ROPE)))
        k_c, v_c = _mla_prep(ckv_c.reshape(bsz * past, MLA_KV_RANK), kr_pad, 0, 0, bsz * past,
                             lw['mla_kv_norm'], lw['mla_w_kvb'], min(tm, bsz * past), norm=False)
    y_mla = _mla_attn(q_mla, k_new, v_new, bsz, seq, k_c, v_c)

    proj3 = proj.reshape(bsz, seq, PROJ_COLS)
    ng = bsz // S5_SEQS

    def pack_state(hs):
        return hs.reshape(ng, S5_SEQS, 2, S5_STATE).transpose(0, 2, 1, 3).reshape(ng, 2 * S5_SEQS, S5_STATE)

    def unpack_state(hs):
        return hs.reshape(ng, 2, S5_SEQS, S5_STATE).transpose(0, 2, 1, 3).reshape(bsz, 2, S5_GROUPS, S5_P)

    yf, yb, hr_new, hi_new = _s5(proj3, bsz, seq, lw['s5_a_re8'], lw['s5_a_im8'],
                                 pack_state(s5_h0r), pack_state(s5_h0i),
                                 lw['s5_wb_re'], lw['s5_wb_im'], lw['s5_wc_re'], lw['s5_wc_im'])
    y_s5 = _s5_post(yf.reshape(n, S5_W), yb.reshape(n, S5_W), proj, lw['s5_d'], lw['s5_w_glu'], tm)

    if cos is not None:
        dk, kcol = _rope_cast(proj, COL_DF_K // DF_W, n, tm, cos, sin), 0
    else:
        dk, kcol = proj, COL_DF_K // DF_DV
    dk_cache = dv_cache = None
    if cache is not None:
        dk_cache = dk_c.reshape(bsz * past, DF_W)
        dv_cache = dv_c.reshape(bsz * past, DF_W)
    y_df = _diff_attn(proj, dk, kcol, lw['df_lambda'], lw['df_norm'], lam_init, bsz, seq,
                      dk_cache, dv_cache, cos, sin)

    cols = lambda c0, w: lax.slice(proj, (0, c0), (n, c0 + w))
    new_ctx = (ckv.reshape(bsz, seq, MLA_KV_RANK),
               cols(COL_MLA_KR, MLA_ROPE).reshape(bsz, seq, MLA_ROPE),
               cols(COL_DF_K, DF_W).reshape(bsz, seq, DF_HEADS, 2 * DF_DQK),
               cols(COL_DF_V, DF_W).reshape(bsz, seq, DF_HEADS, DF_DV),
               c_new.reshape(bsz, 2, ML_HEADS, ML_DH, ML_DH), n_new.reshape(bsz, 2, ML_HEADS, ML_DH),
               m_new[:, :, 0].reshape(bsz, 2, ML_HEADS), unpack_state(hr_new), unpack_state(hi_new))
    return (y_ml, y_mla, y_s5, y_df), new_ctx


def kernel(x_prompt, x_sample, cache_mla_ckv, cache_mla_krope, cache_diff_k, cache_diff_v,
           state_mlstm_c, state_mlstm_n, state_mlstm_m, state_s5_re, state_s5_im, c,
           c_ctx, w_ada, b_ada, norm_mix, norm_ffn, w_in, ml_if_bias, ml_norm, mla_kv_norm,
           mla_w_kvb, s5_a_re, s5_a_im, s5_log_dt, s5_b_re, s5_b_im, s5_c_re, s5_c_im, s5_d,
           s5_w_glu, df_lambda, df_norm, w_branch, w_o, w_ffn_in, w_ffn_out, final_norm):
    bp, sp, _ = x_prompt.shape
    bs, ss, _ = x_sample.shape
    n_p, n_s = bp * sp, bs * ss
    tm = TM_DENSE

    cond = jnp.concatenate([c, c_ctx[None, :], jnp.zeros((SUBLANES - 1 - bs, D_MODEL), F32)], axis=0)
    mods = _ada(cond, w_ada, b_ada).reshape(DEPTH, SUBLANES, 6, 1, D_MODEL)
    rope = _rope_tables(ss)
    latent_row = lambda i: (i * tm) // ss
    prompt_row = lambda i: bs

    xs = x_sample.reshape(n_s, D_MODEL)
    xp = x_prompt.reshape(n_p, D_MODEL)
    zeros_state = (jnp.zeros((bp, 2, ML_HEADS, ML_DH, ML_DH), F32), jnp.zeros((bp, 2, ML_HEADS, ML_DH), F32),
                   jnp.zeros((bp, 2, ML_HEADS), F32), jnp.zeros((bp, 2, S5_GROUPS, S5_P), F32),
                   jnp.zeros((bp, 2, S5_GROUPS, S5_P), F32))
    ctx_out = []
    for l in range(DEPTH):
        lam_init = 0.8 - 0.6 * math.exp(-0.3 * l)
        ab_re, ab_im, bb_re, bb_im = _s5_discretise(s5_a_re[l], s5_a_im[l], s5_log_dt[l], s5_b_re[l], s5_b_im[l])
        rep = lambda a: jnp.repeat(a.reshape(2, S5_STATE), S5_SEQS, axis=0)
        lw = {
            'ml_bias': jnp.pad(ml_if_bias[l].reshape(1, 4 * ML_HEADS), ((0, 0), (0, LANES - 4 * ML_HEADS))),
            'ml_norm': ml_norm[l].reshape(1, ML_W),
            'mla_kv_norm': mla_kv_norm[l].reshape(1, MLA_KV_RANK),
            'mla_w_kvb': mla_w_kvb[l].astype(BF16),
            's5_a_re8': rep(ab_re), 's5_a_im8': rep(ab_im),
            's5_wb_re': _block_diag(bb_re.transpose(0, 1, 3, 2)).astype(BF16),
            's5_wb_im': _block_diag(bb_im.transpose(0, 1, 3, 2)).astype(BF16),
            's5_wc_re': _block_diag(s5_c_re[l].transpose(0, 1, 3, 2)).astype(BF16),
            's5_wc_im': _block_diag(s5_c_im[l].transpose(0, 1, 3, 2)).astype(BF16),
            's5_d': s5_d[l].reshape(1, S5_W),
            's5_w_glu': s5_w_glu[l].astype(BF16),
            'df_lambda': df_lambda[l],
            'df_norm': df_norm[l].reshape(1, DF_DV),
        }
        mod = mods[l]
        w_in_l = _permute_w_in(w_in[l])
        w_branch_l, w_o_l = w_branch[l].astype(BF16), w_o[l].astype(BF16)
        w_ffn_in_l, w_ffn_out_l = w_ffn_in[l].astype(BF16), w_ffn_out[l].astype(BF16)
        g_mix, g_ffn = norm_mix[l].reshape(1, D_MODEL), norm_ffn[l].reshape(1, D_MODEL)

        def layer(x, row_of_tile, bsz, seq, state, cache, rope_tabs):
            proj = _proj_in(x, g_mix, mod, row_of_tile, w_in_l, 2 * tm)
            ys, new_ctx = _seq_mixers(proj, bsz, seq, lw, lam_init, state, cache, rope_tabs)
            merged = _merge(list(ys), proj, w_branch_l, tm)
            x = _matmul_resid(merged, w_o_l, x, mod, 2, row_of_tile, tm)
            act = _ffn_in(x, g_ffn, mod, row_of_tile, w_ffn_in_l, tm)
            return _matmul_resid(act, w_ffn_out_l, x, mod, 5, row_of_tile, tm), new_ctx

        xp, new_ctx = layer(xp, prompt_row, bp, sp, zeros_state, None, None)
        ctx_out.append(new_ctx)
        state = (state_mlstm_c[:, l], state_mlstm_n[:, l], state_mlstm_m[:, l], state_s5_re[:, l], state_s5_im[:, l])
        cache = (cache_mla_ckv[:, l], cache_mla_krope[:, l], cache_diff_k[:, l], cache_diff_v[:, l])
        xs, _ = layer(xs, latent_row, bs, ss, state, cache, rope)

    g_fin = final_norm.reshape(1, D_MODEL)
    y_prompt = _final_norm(xp, g_fin, tm).reshape(bp, sp, D_MODEL)
    y_sample = _final_norm(xs, g_fin, tm).reshape(bs, ss, D_MODEL)
    stacked = tuple(jnp.stack([ctx[k] for ctx in ctx_out], axis=1) for k in range(9))
    return (y_prompt, y_sample) + stacked
```

```python
import functools
import math

import jax
import jax.numpy as jnp
import numpy as np
from jax import lax
from jax.experimental import pallas as pl
from jax.experimental.pallas import tpu as pltpu

F32 = jnp.float32
BF16 = jnp.bfloat16

D_MODEL = 2048
DEPTH = 2
GRID_W = 64
ROPE_DIM = 64
ROPE_BASE = 10000.0
RMS_EPS = 1e-6
ML_HEADS = 4
ML_DH = 256
ML_W = ML_HEADS * ML_DH
MLA_HEADS = 8
MLA_NOPE = 128
MLA_ROPE = ROPE_DIM
MLA_V = 128
MLA_KV_RANK = 512
MLA_QK = MLA_NOPE + MLA_ROPE
S5_GROUP = 16
S5_GROUPS = 64
S5_W = S5_GROUPS * S5_GROUP
S5_P = 64
S5_STATE = S5_GROUPS * S5_P
DF_HEADS = 8
DF_DQK = ROPE_DIM
DF_DV = 2 * DF_DQK
DF_W = DF_HEADS * DF_DV
N_BRANCH = 4
BRANCH_W = 1024
FFN_HIDDEN = (8 * D_MODEL + 3 * 256 - 1) // (3 * 256) * 256

LANES = 128
SUBLANES = 8
VMEM_LIMIT_BYTES = 56 * 1024 * 1024

COL_ML_Q = 0
COL_ML_K = 1024
COL_ML_V = 2048
COL_ML_O = 3072
COL_S5_U = 4096
COL_DF_Q = 5120
COL_DF_K = 6144
COL_DF_V = 7168
COL_GATE = 8192
COL_MLA_QN = 16384
COL_MLA_QR = 17408
COL_MLA_CKV = 17920
COL_MLA_KR = 18432
COL_ML_IF = 18560
PROJ_TN = 1024
PROJ_COLS = 19456

LOG2E = math.log2(math.e)
VT_ROWS = 128 + 16

WEIGHT_BUFFERS = 3
TM_DENSE = 1024
DENSE_TN = 512
TM_SEQ = 512
ATTN_STEP_Q = 1024
ML_CHUNK = 256
S5_CHUNK = 64
S5_SEQS = 4
S5_LANE_BLK = 512
S5_GROUPS_PER_TILE = 16
ATTN_TQ = 256


def _cparams(*sem):
    return pltpu.CompilerParams(dimension_semantics=sem, vmem_limit_bytes=VMEM_LIMIT_BYTES)


def _dot(a, b):
    return jnp.dot(a, b, preferred_element_type=F32)


def _dot_nt(a, b):
    return lax.dot_general(a, b, (((1,), (1,)), ((), ())), preferred_element_type=F32)


def _dot_exact(a, b):
    return jnp.dot(a, b, preferred_element_type=F32, precision=lax.Precision.HIGHEST)


def _rms(x):
    return x * lax.rsqrt(jnp.mean(x * x, axis=-1, keepdims=True) + RMS_EPS)


def _ones_row_block(width):
    row = lax.broadcasted_iota(jnp.int32, (VT_ROWS - 128, width), 0)
    return jnp.where(row == 0, 1.0, 0.0).astype(BF16)


def _rope_slab(x, cos, sin):
    quarter = ROPE_DIM // 4
    lane = lax.broadcasted_iota(jnp.int32, x.shape, 1)
    partner = jnp.where((lane % (2 * quarter)) < quarter, pltpu.roll(x, LANES - quarter, 1),
                        pltpu.roll(x, quarter, 1))
    return x * cos + partner * sin


def _ada_kernel(c_ref, w_ref, b_ref, o_ref):
    c = c_ref[...]
    s = c * jax.nn.sigmoid(c)
    o_ref[...] = _dot(s.astype(BF16), w_ref[...].astype(BF16)) + b_ref[...]


def _ada(cond, w_ada, b_ada):
    rows = cond.shape[0]
    tn = PROJ_TN
    return pl.pallas_call(
        _ada_kernel,
        grid=(DEPTH, 6 * D_MODEL // tn),
        in_specs=[pl.BlockSpec((rows, D_MODEL), lambda l, j: (0, 0)),
                  pl.BlockSpec((None, D_MODEL, tn), lambda l, j: (l, 0, j)),
                  pl.BlockSpec((None, 1, tn), lambda l, j: (l, 0, j))],
        out_specs=pl.BlockSpec((None, rows, tn), lambda l, j: (l, 0, j)),
        out_shape=jax.ShapeDtypeStruct((DEPTH, rows, 6 * D_MODEL), F32),
        compiler_params=_cparams("parallel", "parallel"),
        name="ada",
    )(cond, w_ada, b_ada.reshape(DEPTH, 1, 6 * D_MODEL))


def _mod_spec(which, row_of_tile):
    return pl.BlockSpec((None, None, 1, D_MODEL), lambda i, j: (row_of_tile(i), which, 0, 0))


def _norm_mod(x_ref, g_ref, sc_ref, sh_ref):
    return (_rms(x_ref[...]) * g_ref[...]) * (1.0 + sc_ref[...]) + sh_ref[...]


def _proj_in_kernel(x_ref, g_ref, sc_ref, sh_ref, w_ref, o_ref, h_ref):
    @pl.when(pl.program_id(1) == 0)
    def _():
        h_ref[...] = _norm_mod(x_ref, g_ref, sc_ref, sh_ref).astype(BF16)

    o_ref[...] = _dot(h_ref[...], w_ref[...])


def _proj_in(x, g, mod, row_of_tile, w, tm):
    n = x.shape[0]
    ncol = w.shape[1]
    return pl.pallas_call(
        _proj_in_kernel,
        grid=(n // tm, ncol // PROJ_TN),
        in_specs=[pl.BlockSpec((tm, D_MODEL), lambda i, j: (i, 0), pipeline_mode=pl.Buffered(1)),
                  pl.BlockSpec((1, D_MODEL), lambda i, j: (0, 0)),
                  _mod_spec(1, row_of_tile), _mod_spec(0, row_of_tile),
                  pl.BlockSpec((D_MODEL, PROJ_TN), lambda i, j: (0, j))],
        out_specs=pl.BlockSpec((tm, PROJ_TN), lambda i, j: (i, j)),
        out_shape=jax.ShapeDtypeStruct((n, ncol), F32),
        scratch_shapes=[pltpu.VMEM((tm, D_MODEL), BF16)],
        compiler_params=_cparams("parallel", "arbitrary"),
        name="proj_in",
    )(x, g, mod, mod, w)


def _ffn_in_kernel(x_ref, g_ref, sc_ref, sh_ref, wa_ref, wb_ref, o_ref, h_ref):
    @pl.when(pl.program_id(1) == 0)
    def _():
        h_ref[...] = _norm_mod(x_ref, g_ref, sc_ref, sh_ref).astype(BF16)

    h = h_ref[...]
    a = _dot(h, wa_ref[...])
    b = _dot(h, wb_ref[...])
    o_ref[...] = (a * jax.nn.sigmoid(a) * b).astype(BF16)


def _ffn_in(x, g, mod, row_of_tile, w, tm):
    n = x.shape[0]
    tn = DENSE_TN
    nj = FFN_HIDDEN // tn
    return pl.pallas_call(
        _ffn_in_kernel,
        grid=(n // tm, nj),
        in_specs=[pl.BlockSpec((tm, D_MODEL), lambda i, j: (i, 0)),
                  pl.BlockSpec((1, D_MODEL), lambda i, j: (0, 0)),
                  _mod_spec(4, row_of_tile), _mod_spec(3, row_of_tile),
                  pl.BlockSpec((D_MODEL, tn), lambda i, j: (0, j)),
                  pl.BlockSpec((D_MODEL, tn), lambda i, j: (0, nj + j))],
        out_specs=pl.BlockSpec((tm, tn), lambda i, j: (i, j)),
        out_shape=jax.ShapeDtypeStruct((n, FFN_HIDDEN), BF16),
        scratch_shapes=[pltpu.VMEM((tm, D_MODEL), BF16)],
        compiler_params=_cparams("parallel", "arbitrary"),
        name="ffn_in",
    )(x, g, mod, mod, w, w)


def _resid_kernel(a_ref, w_ref, x_ref, gate_ref, o_ref):
    o_ref[...] = x_ref[...] + gate_ref[...] * _dot(a_ref[...], w_ref[...])


def _matmul_resid(a, w, x, mod, which, row_of_tile, tm):
    n, kdim = a.shape
    tn = DENSE_TN
    inner = pltpu.emit_pipeline(
        _resid_kernel,
        grid=(n // tm, D_MODEL // tn),
        in_specs=[pl.BlockSpec((tm, kdim), lambda i, j: (i, 0)),
                  pl.BlockSpec((kdim, tn), lambda i, j: (0, j), pipeline_mode=pl.Buffered(WEIGHT_BUFFERS)),
                  pl.BlockSpec((tm, tn), lambda i, j: (i, j)),
                  pl.BlockSpec((None, None, 1, tn), lambda i, j: (row_of_tile(i), which, 0, j))],
        out_specs=[pl.BlockSpec((tm, tn), lambda i, j: (i, j))],
    )

    def outer(a_hbm, w_hbm, x_hbm, mod_hbm, o_hbm):
        inner(a_hbm, w_hbm, x_hbm, mod_hbm, o_hbm)

    return pl.pallas_call(
        outer,
        in_specs=[pl.BlockSpec(memory_space=pl.ANY)] * 4,
        out_specs=pl.BlockSpec(memory_space=pl.ANY),
        out_shape=jax.ShapeDtypeStruct((n, D_MODEL), F32),
        compiler_params=pltpu.CompilerParams(vmem_limit_bytes=VMEM_LIMIT_BYTES),
        name="matmul_resid",
    )(a, w, x, mod)


def _final_norm_kernel(x_ref, g_ref, o_ref):
    o_ref[...] = _rms(x_ref[...]) * g_ref[...]


def _final_norm(x, g, tm):
    n = x.shape[0]
    return pl.pallas_call(
        _final_norm_kernel,
        grid=(n // tm,),
        in_specs=[pl.BlockSpec((tm, D_MODEL), lambda i: (i, 0)),
                  pl.BlockSpec((1, D_MODEL), lambda i: (0, 0))],
        out_specs=pl.BlockSpec((tm, D_MODEL), lambda i: (i, 0)),
        out_shape=jax.ShapeDtypeStruct((n, D_MODEL), F32),
        compiler_params=_cparams("parallel"),
        name="final_norm",
    )(x, g)


def _mlstm_kernel(qf_ref, kf_ref, vf_ref, gf_ref, qb_ref, kb_ref, vb_ref, gb_ref, bias_ref,
                  c0_ref, n0_ref, m0_ref,
                  hf_ref, hb_ref, c_out_ref, n_out_ref, m_out_ref,
                  c_scr, n_scr, m_scr):
    ci = pl.program_id(1)
    t = qf_ref.shape[0]

    @pl.when(ci == 0)
    def _():
        c_scr[...] = c0_ref[...]
        n_scr[...] = n0_ref[...]
        m_scr[...] = m0_ref[...]

    row = lax.broadcasted_iota(jnp.int32, (t, t), 0)
    col = lax.broadcasted_iota(jnp.int32, (t, t), 1)
    refs = ((qf_ref, kf_ref, vf_ref, gf_ref, hf_ref), (qb_ref, kb_ref, vb_ref, gb_ref, hb_ref))
    for d in range(2):
        q_ref, k_ref, v_ref, g_ref, h_ref = refs[d]
        keep = (col <= row) if d == 0 else (col >= row)
        cum = jnp.where(keep, 1.0, 0.0).astype(F32)
        gates = g_ref[...] + bias_ref[...]
        csum = _dot_exact(cum, jax.nn.log_sigmoid(gates))
        gates_t = gates.T
        csum_t = csum.T
        last = t - 1 if d == 0 else 0
        for hd in range(ML_HEADS):
            r = d * ML_HEADS + hd
            i_col = d * 2 * ML_HEADS + hd
            f_col = i_col + ML_HEADS
            b_c = csum[:, f_col:f_col + 1]
            b_r = csum_t[f_col:f_col + 1, :]
            li_c = gates[:, i_col:i_col + 1]
            li_r = gates_t[i_col:i_col + 1, :]
            m_st = m_scr[r:r + 1, 0:1]
            c_st = c_scr[r]
            n_st = n_scr[r:r + 1, :]
            sl = slice(hd * ML_DH, (hd + 1) * ML_DH)
            q = q_ref[:, sl]
            k = k_ref[:, sl] * (ML_DH ** -0.5)
            v = v_ref[:, sl]
            qb16 = q.astype(BF16)
            kb16 = k.astype(BF16)

            log_d = jnp.where(keep, b_c - b_r + li_r, -jnp.inf)
            log_inter = b_c + m_st
            m_t = jnp.maximum(log_inter, jnp.max(log_d, axis=1, keepdims=True))
            w_d = jnp.exp(log_d - m_t)
            w_inter = jnp.exp(log_inter - m_t)
            s = _dot_nt(qb16, kb16) * w_d
            num = _dot(s.astype(BF16), v.astype(BF16)) + w_inter * _dot_nt(qb16, c_st.astype(BF16))
            den = jnp.sum(s, axis=1, keepdims=True) + w_inter * jnp.sum(q * n_st, axis=1, keepdims=True)
            h_ref[:, sl] = num / jnp.maximum(jnp.abs(den), jnp.exp(-m_t))

            b_last = b_c[last:last + 1, :]
            log_w = b_last - b_c + li_c
            m_new = jnp.maximum(b_last + m_st, jnp.max(log_w, axis=0, keepdims=True))
            w_s = jnp.exp(log_w - m_new)
            w_c = jnp.exp(b_last + m_st - m_new)
            vw_t = (v * w_s).T.astype(BF16)
            c_scr[r] = w_c * c_st + _dot(vw_t, kb16)
            n_scr[r:r + 1, :] = w_c * n_st + jnp.sum(k * w_s, axis=0, keepdims=True)
            m_scr[r:r + 1, :] = jnp.broadcast_to(m_new, (1, LANES))

    @pl.when(ci == pl.num_programs(1) - 1)
    def _():
        c_out_ref[...] = c_scr[...]
        n_out_ref[...] = n_scr[...]
        m_out_ref[...] = m_scr[...]


def _mlstm(proj, bsz, seq, bias, c0, n0, m0):
    tc = min(ML_CHUNK, seq)
    nc = seq // tc
    r8 = 2 * ML_HEADS

    def fwd(colblk):
        return lambda b, c: (b * nc + c, colblk)

    def bwd(colblk):
        return lambda b, c: (b * nc + nc - 1 - c, colblk)

    def seqspecs(mk):
        return [pl.BlockSpec((tc, ML_W), mk(COL_ML_Q // ML_W)),
                pl.BlockSpec((tc, ML_W), mk(COL_ML_K // ML_W)),
                pl.BlockSpec((tc, ML_W), mk(COL_ML_V // ML_W)),
                pl.BlockSpec((tc, LANES), mk(COL_ML_IF // LANES))]

    state_specs = [pl.BlockSpec((None, r8, ML_DH, ML_DH), lambda b, c: (b, 0, 0, 0)),
                   pl.BlockSpec((None, r8, ML_DH), lambda b, c: (b, 0, 0)),
                   pl.BlockSpec((None, r8, LANES), lambda b, c: (b, 0, 0))]
    return pl.pallas_call(
        _mlstm_kernel,
        grid=(bsz, nc),
        in_specs=seqspecs(fwd) + seqspecs(bwd) + [pl.BlockSpec((1, LANES), lambda b, c: (0, 0))] + state_specs,
        out_specs=[pl.BlockSpec((tc, ML_W), lambda b, c: (b * nc + c, 0)),
                   pl.BlockSpec((tc, ML_W), lambda b, c: (b * nc + nc - 1 - c, 0))] + state_specs,
        out_shape=[jax.ShapeDtypeStruct((bsz * seq, ML_W), F32),
                   jax.ShapeDtypeStruct((bsz * seq, ML_W), F32),
                   jax.ShapeDtypeStruct((bsz, r8, ML_DH, ML_DH), F32),
                   jax.ShapeDtypeStruct((bsz, r8, ML_DH), F32),
                   jax.ShapeDtypeStruct((bsz, r8, LANES), F32)],
        scratch_shapes=[pltpu.VMEM((r8, ML_DH, ML_DH), F32),
                        pltpu.VMEM((r8, ML_DH), F32),
                        pltpu.VMEM((r8, LANES), F32)],
        compiler_params=_cparams("parallel", "arbitrary"),
        name="mlstm",
    )(proj, proj, proj, proj, proj, proj, proj, proj, bias, c0, n0, m0)


def _ml_post_kernel(hf_ref, hb_ref, o_ref, g_ref, y_ref):
    h = hf_ref[...] + hb_ref[...]
    for hd in range(ML_HEADS):
        sl = slice(hd * ML_DH, (hd + 1) * ML_DH)
        y_ref[:, sl] = (_rms(h[:, sl]) * g_ref[:, sl] * jax.nn.sigmoid(o_ref[:, sl])).astype(BF16)


def _ml_post(hf, hb, proj, g, tm):
    n = hf.shape[0]
    return pl.pallas_call(
        _ml_post_kernel,
        grid=(n // tm,),
        in_specs=[pl.BlockSpec((tm, ML_W), lambda i: (i, 0)),
                  pl.BlockSpec((tm, ML_W), lambda i: (i, 0)),
                  pl.BlockSpec((tm, ML_W), lambda i: (i, COL_ML_O // ML_W)),
                  pl.BlockSpec((1, ML_W), lambda i: (0, 0))],
        out_specs=pl.BlockSpec((tm, ML_W), lambda i: (i, 0)),
        out_shape=jax.ShapeDtypeStruct((n, ML_W), BF16),
        compiler_params=_cparams("parallel"),
        name="ml_post",
    )(hf, hb, proj, g)


def _mla_prep_kernel(*refs, norm, rope, with_q):
    it = iter(refs)
    ckv_ref, kr_ref, g_ref, wkvb_ref = next(it), next(it), next(it), next(it)
    if with_q:
        qn_ref, qr_ref = next(it), next(it)
    if rope:
        cos_ref, sin_ref = next(it), next(it)
    k_ref, v_ref = next(it), next(it)
    if with_q:
        q_ref = next(it)
    if norm:
        ckv_out_ref = next(it)

    ckv = ckv_ref[...]
    if norm:
        ckv = _rms(ckv) * g_ref[...]
        ckv_out_ref[...] = ckv
    kv = _dot(ckv.astype(BF16), wkvb_ref[...])
    kr = kr_ref[...]
    if rope:
        kr = _rope_slab(kr, cos_ref[...], sin_ref[...])
    kr16 = kr[:, :MLA_ROPE].astype(BF16)
    hw = MLA_NOPE + MLA_V
    for hd in range(MLA_HEADS):
        k_ref[hd, :, 0:MLA_NOPE] = kv[:, hd * hw:hd * hw + MLA_NOPE].astype(BF16)
        k_ref[hd, :, MLA_NOPE:MLA_QK] = kr16
        v_ref[hd, 0:MLA_V, :] = kv[:, hd * hw + MLA_NOPE:(hd + 1) * hw].T.astype(BF16)
        v_ref[hd, MLA_V:, :] = _ones_row_block(k_ref.shape[1])
    if with_q:
        scale = MLA_QK ** -0.5 * LOG2E
        qn = qn_ref[...]
        for sb in range(MLA_HEADS * MLA_ROPE // LANES):
            qr = qr_ref[:, sb * LANES:(sb + 1) * LANES]
            if rope:
                qr = _rope_slab(qr, cos_ref[...], sin_ref[...])
            for half in range(LANES // MLA_ROPE):
                hd = sb * (LANES // MLA_ROPE) + half
                q_ref[hd, :, MLA_NOPE:MLA_QK] = (qr[:, half * MLA_ROPE:(half + 1) * MLA_ROPE] * scale).astype(BF16)
        for hd in range(MLA_HEADS):
            q_ref[hd, :, 0:MLA_NOPE] = (qn[:, hd * MLA_NOPE:(hd + 1) * MLA_NOPE] * scale).astype(BF16)


def _mla_prep(ckv_src, kr_src, ckv_col, kr_col, n, g, wkvb, tm, q_src=None, cos=None, sin=None, norm=True):
    rope = cos is not None
    with_q = q_src is not None
    ins = [ckv_src, kr_src, g, wkvb]
    in_specs = [pl.BlockSpec((tm, MLA_KV_RANK), lambda i: (i, ckv_col)),
                pl.BlockSpec((tm, LANES), lambda i: (i, kr_col)),
                pl.BlockSpec((1, MLA_KV_RANK), lambda i: (0, 0)),
                pl.BlockSpec(wkvb.shape, lambda i: (0, 0))]
    if with_q:
        ins += [q_src, q_src]
        in_specs += [pl.BlockSpec((tm, MLA_HEADS * MLA_NOPE), lambda i: (i, COL_MLA_QN // (MLA_HEADS * MLA_NOPE))),
                     pl.BlockSpec((tm, MLA_HEADS * MLA_ROPE), lambda i: (i, COL_MLA_QR // (MLA_HEADS * MLA_ROPE)))]
    if rope:
        nt = cos.shape[0] // tm
        ins += [cos, sin]
        in_specs += [pl.BlockSpec((tm, LANES), lambda i: (i % nt, 0))] * 2
    out_shape = [jax.ShapeDtypeStruct((MLA_HEADS, n, MLA_QK), BF16),
                 jax.ShapeDtypeStruct((MLA_HEADS, VT_ROWS, n), BF16)]
    out_specs = [pl.BlockSpec((MLA_HEADS, tm, MLA_QK), lambda i: (0, i, 0)),
                 pl.BlockSpec((MLA_HEADS, VT_ROWS, tm), lambda i: (0, 0, i))]
    if with_q:
        out_shape.append(jax.ShapeDtypeStruct((MLA_HEADS, n, MLA_QK), BF16))
        out_specs.append(pl.BlockSpec((MLA_HEADS, tm, MLA_QK), lambda i: (0, i, 0)))
    if norm:
        out_shape.append(jax.ShapeDtypeStruct((n, MLA_KV_RANK), F32))
        out_specs.append(pl.BlockSpec((tm, MLA_KV_RANK), lambda i: (i, 0)))
    return pl.pallas_call(
        functools.partial(_mla_prep_kernel, norm=norm, rope=rope, with_q=with_q),
        grid=(n // tm,),
        in_specs=in_specs, out_specs=out_specs, out_shape=out_shape,
        compiler_params=_cparams("parallel"),
        name="mla_prep",
    )(*ins)


def _query_halves(q_ref):
    tq = min(ATTN_TQ, q_ref.shape[0])
    return [slice(i * tq, (i + 1) * tq) for i in range(q_ref.shape[0] // tq)]


def _mla_attn_kernel(*refs, cached):
    if cached:
        q_ref, k_ref, v_ref, kc_ref, vc_ref, o_ref, s_scr, p_scr = refs
        sources = ((k_ref, v_ref), (kc_ref, vc_ref))
    else:
        q_ref, k_ref, v_ref, o_ref, s_scr, p_scr = refs
        sources = ((k_ref, v_ref),)
    halves = _query_halves(q_ref)
    spans, off = [], 0
    for kk_ref, _ in sources:
        spans.append(slice(off, off + kk_ref.shape[0]))
        off += kk_ref.shape[0]
    for i, rows in enumerate(halves):
        for (kk_ref, _), span in zip(sources, spans):
            s_scr[i, span, :] = _dot_nt(kk_ref[...], q_ref[rows, :])
    for i, rows in enumerate(halves):
        s = s_scr[i]
        p_scr[i] = jnp.exp2(s - jnp.max(s, axis=0, keepdims=True)).astype(BF16)
        oe = functools.reduce(jnp.add, [_dot(vv_ref[...], p_scr[i, span, :]) for (_, vv_ref), span in zip(sources, spans)])
        o = oe[0:MLA_V] / oe[MLA_V:MLA_V + 1]
        o_ref[rows, :] = o.T.astype(BF16)


def _mla_attn(q, k, v, bsz, seq, k_c=None, v_c=None):
    tq = min(ATTN_STEP_Q, seq)
    nq = seq // tq
    cached = k_c is not None
    keys = seq
    ins = [q, k, v]
    in_specs = [pl.BlockSpec((None, tq, MLA_QK), lambda h, b, i: (h, b * nq + i, 0)),
                pl.BlockSpec((None, seq, MLA_QK), lambda h, b, i: (h, b, 0)),
                pl.BlockSpec((None, VT_ROWS, seq), lambda h, b, i: (h, 0, b))]
    if cached:
        past = k_c.shape[1] // bsz
        keys += past
        ins += [k_c, v_c]
        in_specs += [pl.BlockSpec((None, past, MLA_QK), lambda h, b, i: (h, b, 0)),
                     pl.BlockSpec((None, VT_ROWS, past), lambda h, b, i: (h, 0, b))]
    nsub = tq // min(ATTN_TQ, tq)
    return pl.pallas_call(
        functools.partial(_mla_attn_kernel, cached=cached),
        grid=(MLA_HEADS, bsz, nq),
        in_specs=in_specs,
        out_specs=pl.BlockSpec((tq, MLA_V), lambda h, b, i: (b * nq + i, h)),
        out_shape=jax.ShapeDtypeStruct((bsz * seq, MLA_HEADS * MLA_V), BF16),
        scratch_shapes=[pltpu.VMEM((nsub, keys, tq // nsub), F32),
                        pltpu.VMEM((nsub, keys, tq // nsub), BF16)],
        compiler_params=_cparams("parallel", "parallel", "parallel"),
        name="mla_attn",
    )(*ins)


def _rope_cast_kernel(x_ref, cos_ref, sin_ref, o_ref):
    for sb in range(x_ref.shape[1] // LANES):
        sl = slice(sb * LANES, (sb + 1) * LANES)
        o_ref[:, sl] = _rope_slab(x_ref[:, sl], cos_ref[...], sin_ref[...]).astype(BF16)


def _rope_cast(src, colblk, n, tm, cos, sin):
    nt = cos.shape[0] // tm
    return pl.pallas_call(
        _rope_cast_kernel,
        grid=(n // tm,),
        in_specs=[pl.BlockSpec((tm, DF_W), lambda i: (i, colblk))]
                 + [pl.BlockSpec((tm, LANES), lambda i: (i % nt, 0))] * 2,
        out_specs=pl.BlockSpec((tm, DF_W), lambda i: (i, 0)),
        out_shape=jax.ShapeDtypeStruct((n, DF_W), BF16),
        compiler_params=_cparams("parallel"),
        name="rope_cast",
    )(src, cos, sin)


def _diff_attn_kernel(*refs, lam_init, cached, rope):
    it = iter(refs)
    q_ref, k_ref, v_ref = next(it), next(it), next(it)
    sources = [(k_ref, v_ref)]
    if cached:
        sources.append((next(it), next(it)))
    if rope:
        cos_ref, sin_ref = next(it), next(it)
    lam_ref, g_ref, o_ref, s_scr, p_scr = next(it), next(it), next(it), next(it), next(it)
    lp = lam_ref[...]
    lam = (jnp.exp(jnp.sum(lp[0:1] * lp[1:2], axis=-1, keepdims=True))
           - jnp.exp(jnp.sum(lp[2:3] * lp[3:4], axis=-1, keepdims=True)) + lam_init)
    halves = _query_halves(q_ref)
    ks = [kk_ref[...].astype(BF16) for kk_ref, _ in sources]
    vs = [vv_ref[...].astype(BF16) for _, vv_ref in sources]
    spans, off = [], 0
    for kk_ref, _ in sources:
        spans.append(slice(off, off + kk_ref.shape[0]))
        off += kk_ref.shape[0]
    for i, rows in enumerate(halves):
        q = q_ref[rows, :]
        if rope:
            q = _rope_slab(q, cos_ref[rows, :], sin_ref[rows, :])
        q = (q * (DF_DQK ** -0.5 * LOG2E)).astype(BF16)
        lane = lax.broadcasted_iota(jnp.int32, q.shape, 1)
        zero = jnp.zeros_like(q)
        for comp, qc in enumerate((jnp.where(lane < DF_DQK, q, zero), jnp.where(lane >= DF_DQK, q, zero))):
            for k, span in zip(ks, spans):
                s_scr[comp, i, :, span] = _dot_nt(qc, k)
    for i, rows in enumerate(halves):
        r = []
        for comp in range(2):
            s = s_scr[comp, i]
            p = jnp.exp2(s - jnp.max(s, axis=-1, keepdims=True))
            r.append(1.0 / jnp.sum(p, axis=-1, keepdims=True))
            s_scr[comp, i] = p
        c = lam * r[1] / r[0]
        p_scr[i] = (s_scr[0, i] - c * s_scr[1, i]).astype(BF16)
        o = functools.reduce(jnp.add, [_dot(p_scr[i, :, span], v) for v, span in zip(vs, spans)]) * r[0]
        o_ref[rows, :] = (_rms(o) * g_ref[...] * (1.0 - lam_init)).astype(BF16)


def _diff_attn(proj, k, kcol, lam_p, g, lam_init, bsz, seq, k_c=None, v_c=None, cos=None, sin=None):
    tq = min(ATTN_STEP_Q, seq)
    nq = seq // tq
    nsub = tq // min(ATTN_TQ, tq)
    cached = k_c is not None
    rope = cos is not None
    keys = seq
    qcol, vcol = COL_DF_Q // DF_DV, COL_DF_V // DF_DV
    ins = [proj, k, proj]
    in_specs = [pl.BlockSpec((tq, DF_DV), lambda b, h, i: (b * nq + i, qcol + h)),
                pl.BlockSpec((seq, DF_DV), lambda b, h, i: (b, kcol + h)),
                pl.BlockSpec((seq, DF_DV), lambda b, h, i: (b, vcol + h))]
    if cached:
        past = k_c.shape[0] // bsz
        keys += past
        ins += [k_c, v_c]
        in_specs += [pl.BlockSpec((past, DF_DV), lambda b, h, i: (b, h)),
                     pl.BlockSpec((past, DF_DV), lambda b, h, i: (b, h))]
    if rope:
        ins += [cos, sin]
        in_specs += [pl.BlockSpec((tq, LANES), lambda b, h, i: (i, 0))] * 2
    ins += [lam_p, g]
    in_specs += [pl.BlockSpec((4, DF_DQK), lambda b, h, i: (0, 0)),
                 pl.BlockSpec((1, DF_DV), lambda b, h, i: (0, 0))]
    return pl.pallas_call(
        functools.partial(_diff_attn_kernel, lam_init=lam_init, cached=cached, rope=rope),
        grid=(bsz, DF_HEADS, nq),
        in_specs=in_specs,
        out_specs=pl.BlockSpec((tq, DF_DV), lambda b, h, i: (b * nq + i, h)),
        out_shape=jax.ShapeDtypeStruct((bsz * seq, DF_W), BF16),
        scratch_shapes=[pltpu.VMEM((2, nsub, tq // nsub, keys), F32),
                        pltpu.VMEM((nsub, tq // nsub, keys), BF16)],
        compiler_params=_cparams("parallel", "parallel", "parallel"),
        name="diff_attn",
    )(*ins)


def _s5_kernel(uf_ref, ub_ref, a_re_ref, a_im_ref, h0_re_ref, h0_im_ref,
               wb_re_ref, wb_im_ref, wc_re_ref, wc_im_ref,
               yf_ref, yb_ref, hr_out_ref, hi_out_ref,
               bu_re, bu_im, h_re, h_im):
    ci = pl.program_id(1)
    tc = uf_ref.shape[1]
    nseq = 2 * S5_SEQS
    gblk, sblk = wb_re_ref.shape[2:]
    ngb = S5_W // gblk

    @pl.when(ci == 0)
    def _():
        h_re[...] = h0_re_ref[...]
        h_im[...] = h0_im_ref[...]

    ri = lax.broadcasted_iota(jnp.int32, (tc, tc), 0)
    cj = lax.broadcasted_iota(jnp.int32, (tc, tc), 1)
    rev = jnp.where(ri + cj == tc - 1, 1.0, 0.0).astype(BF16)

    for d, u_ref in enumerate((uf_ref, ub_ref)):
        us = []
        for s in range(S5_SEQS):
            u = u_ref[s].astype(BF16)
            if d == 1:
                u = _dot(rev, u).astype(BF16)
            us.append(u)
        u_all = jnp.concatenate(us, axis=0)
        for gb in range(ngb):
            ug = u_all[:, gb * gblk:(gb + 1) * gblk]
            for w_ref, dst in ((wb_re_ref, bu_re), (wb_im_ref, bu_im)):
                bu = _dot(ug, w_ref[d, gb])
                for s in range(S5_SEQS):
                    for lk in range(sblk // LANES):
                        dst[gb * (sblk // LANES) + lk, pl.ds(d * S5_SEQS + s, tc, stride=nseq), :] = (
                            bu[s * tc:(s + 1) * tc, lk * LANES:(lk + 1) * LANES])

    nlk = S5_LANE_BLK // LANES
    for lb in range(S5_STATE // S5_LANE_BLK):
        lks = tuple(range(lb * nlk, (lb + 1) * nlk))
        ar = [a_re_ref[:, k * LANES:(k + 1) * LANES] for k in lks]
        ai = [a_im_ref[:, k * LANES:(k + 1) * LANES] for k in lks]

        def step(j, carry):
            r0 = pl.multiple_of(j * nseq, nseq)
            out = []
            for i, k in enumerate(lks):
                hr, hi = carry[2 * i], carry[2 * i + 1]
                nhr = ar[i] * hr - ai[i] * hi + bu_re[k, pl.ds(r0, nseq), :]
                nhi = ar[i] * hi + ai[i] * hr + bu_im[k, pl.ds(r0, nseq), :]
                bu_re[k, pl.ds(r0, nseq), :] = nhr
                bu_im[k, pl.ds(r0, nseq), :] = nhi
                out += [nhr, nhi]
            return tuple(out)

        init = []
        for k in lks:
            init += [h_re[:, k * LANES:(k + 1) * LANES], h_im[:, k * LANES:(k + 1) * LANES]]
        fin = lax.fori_loop(0, tc, step, tuple(init), unroll=8)
        for i, k in enumerate(lks):
            h_re[:, k * LANES:(k + 1) * LANES] = fin[2 * i]
            h_im[:, k * LANES:(k + 1) * LANES] = fin[2 * i + 1]

    def seq_states(src, s):
        parts = [src[k, pl.ds(s, tc, stride=nseq), :] for k in range(S5_STATE // LANES)]
        return jnp.concatenate(parts, axis=1).astype(BF16)

    def reverse_rows(y):
        hi = y.astype(BF16)
        r1 = y - hi.astype(F32)
        mid = r1.astype(BF16)
        lo = (r1 - mid.astype(F32)).astype(BF16)
        return _dot(rev, hi) + _dot(rev, mid) + _dot(rev, lo)

    for d, y_ref in enumerate((yf_ref, yb_ref)):
        hr_all = jnp.concatenate([seq_states(bu_re, d * S5_SEQS + s) for s in range(S5_SEQS)], axis=0)
        hi_all = jnp.concatenate([seq_states(bu_im, d * S5_SEQS + s) for s in range(S5_SEQS)], axis=0)
        for gb in range(ngb):
            ssl = slice(gb * sblk, (gb + 1) * sblk)
            y = _dot(hr_all[:, ssl], wc_re_ref[d, gb]) - _dot(hi_all[:, ssl], wc_im_ref[d, gb])
            for s in range(S5_SEQS):
                ys = y[s * tc:(s + 1) * tc]
                y_ref[s, :, gb * gblk:(gb + 1) * gblk] = reverse_rows(ys) if d == 1 else ys

    @pl.when(ci == pl.num_programs(1) - 1)
    def _():
        hr_out_ref[...] = h_re[...]
        hi_out_ref[...] = h_im[...]


def _s5(proj3, bsz, seq, a_re, a_im, h0_re, h0_im, wb_re, wb_im, wc_re, wc_im):
    tc = min(S5_CHUNK, seq)
    nc = seq // tc
    ng = bsz // S5_SEQS
    nseq = 2 * S5_SEQS
    ucol = COL_S5_U // S5_W
    const4 = lambda g, c: (0, 0, 0, 0)
    wspec = lambda w: pl.BlockSpec(w.shape, const4, pipeline_mode=pl.Buffered(1))
    hspec = pl.BlockSpec((None, nseq, S5_STATE), lambda g, c: (g, 0, 0))
    return pl.pallas_call(
        _s5_kernel,
        grid=(ng, nc),
        in_specs=[pl.BlockSpec((S5_SEQS, tc, S5_W), lambda g, c: (g, c, ucol)),
                  pl.BlockSpec((S5_SEQS, tc, S5_W), lambda g, c: (g, nc - 1 - c, ucol)),
                  pl.BlockSpec((nseq, S5_STATE), lambda g, c: (0, 0)),
                  pl.BlockSpec((nseq, S5_STATE), lambda g, c: (0, 0)),
                  hspec, hspec, wspec(wb_re), wspec(wb_im), wspec(wc_re), wspec(wc_im)],
        out_specs=[pl.BlockSpec((S5_SEQS, tc, S5_W), lambda g, c: (g, c, 0)),
                   pl.BlockSpec((S5_SEQS, tc, S5_W), lambda g, c: (g, nc - 1 - c, 0)),
                   hspec, hspec],
        out_shape=[jax.ShapeDtypeStruct((bsz, seq, S5_W), F32),
                   jax.ShapeDtypeStruct((bsz, seq, S5_W), F32),
                   jax.ShapeDtypeStruct((ng, nseq, S5_STATE), F32),
                   jax.ShapeDtypeStruct((ng, nseq, S5_STATE), F32)],
        scratch_shapes=[pltpu.VMEM((S5_STATE // LANES, tc * nseq, LANES), F32),
                        pltpu.VMEM((S5_STATE // LANES, tc * nseq, LANES), F32),
                        pltpu.VMEM((nseq, S5_STATE), F32),
                        pltpu.VMEM((nseq, S5_STATE), F32)],
        compiler_params=_cparams("parallel", "arbitrary"),
        name="s5",
    )(proj3, proj3, a_re, a_im, h0_re, h0_im, wb_re, wb_im, wc_re, wc_im)


def _s5_post_kernel(yf_ref, yb_ref, u_ref, d_ref, w_ref, o_ref):
    y = (yf_ref[...] + yb_ref[...]) + d_ref[...] * u_ref[...]
    g = jax.nn.gelu(y)
    o_ref[...] = (g * jax.nn.sigmoid(_dot(g.astype(BF16), w_ref[...]))).astype(BF16)


def _s5_post(yf, yb, proj, d, w, tm):
    n = yf.shape[0]
    return pl.pallas_call(
        _s5_post_kernel,
        grid=(n // tm,),
        in_specs=[pl.BlockSpec((tm, S5_W), lambda i: (i, 0)),
                  pl.BlockSpec((tm, S5_W), lambda i: (i, 0)),
                  pl.BlockSpec((tm, S5_W), lambda i: (i, COL_S5_U // S5_W)),
                  pl.BlockSpec((1, S5_W), lambda i: (0, 0)),
                  pl.BlockSpec((S5_W, S5_W), lambda i: (0, 0))],
        out_specs=pl.BlockSpec((tm, S5_W), lambda i: (i, 0)),
        out_shape=jax.ShapeDtypeStruct((n, S5_W), BF16),
        compiler_params=_cparams("parallel"),
        name="s5_post",
    )(yf, yb, proj, d, w)


def _s5_discretise(a_re, a_im, log_dt, b_re, b_im):
    lr = jnp.minimum(a_re, -1e-4)
    li = a_im
    dt = jnp.exp(log_dt)[..., None]
    mag = jnp.exp(dt * lr)
    ab_re, ab_im = mag * jnp.cos(dt * li), mag * jnp.sin(dt * li)
    den = lr * lr + li * li
    nr, ni = ab_re - 1.0, ab_im
    qr = (nr * lr + ni * li) / den
    qi = (ni * lr - nr * li) / den
    bb_re = qr[..., None] * b_re - qi[..., None] * b_im
    bb_im = qr[..., None] * b_im + qi[..., None] * b_re
    return ab_re, ab_im, bb_re, bb_im


def _block_diag(w):
    ngrp = S5_GROUPS_PER_TILE
    d, g, r, c = w.shape
    wg = w.reshape(d, g // ngrp, ngrp, r, c)
    eye = jnp.eye(ngrp, dtype=w.dtype)
    out = jnp.einsum('dbgrc,gh->dbgrhc', wg, eye)
    return out.reshape(d, g // ngrp, ngrp * r, ngrp * c)


def _merge_kernel(y0_ref, y1_ref, y2_ref, y3_ref, g0_ref, g1_ref, g2_ref, g3_ref, w_ref, o_ref):
    acc = None
    for b, (y_ref, g_ref) in enumerate(((y0_ref, g0_ref), (y1_ref, g1_ref), (y2_ref, g2_ref), (y3_ref, g3_ref))):
        term = jax.nn.sigmoid(g_ref[...]) * _dot(y_ref[...], w_ref[b])
        acc = term if acc is None else acc + term
    o_ref[...] = acc.astype(BF16)


def _merge(ys, proj, wb, tm):
    n = ys[0].shape[0]
    tn = DENSE_TN
    gate0 = COL_GATE // tn
    per = D_MODEL // tn

    def gspec(b):
        return pl.BlockSpec((tm, tn), lambda i, j: (i, gate0 + b * per + j))

    return pl.pallas_call(
        _merge_kernel,
        grid=(n // tm, per),
        in_specs=[pl.BlockSpec((tm, BRANCH_W), lambda i, j: (i, 0))] * N_BRANCH
                 + [gspec(b) for b in range(N_BRANCH)]
                 + [pl.BlockSpec((N_BRANCH, BRANCH_W, tn), lambda i, j: (0, 0, j))],
        out_specs=pl.BlockSpec((tm, tn), lambda i, j: (i, j)),
        out_shape=jax.ShapeDtypeStruct((n, D_MODEL), BF16),
        compiler_params=_cparams("parallel", "parallel"),
        name="merge",
    )(*ys, proj, proj, proj, proj, wb)


def _rope_tables(n_tok):
    grid_rows = n_tok // GRID_W
    rows, cols = jnp.meshgrid(jnp.arange(grid_rows, dtype=F32), jnp.arange(GRID_W, dtype=F32), indexing='ij')
    quarter = ROPE_DIM // 4
    inv = ROPE_BASE ** (-jnp.arange(quarter, dtype=F32) / quarter)
    ang_r = rows.reshape(-1, 1) * inv
    ang_c = cols.reshape(-1, 1) * inv
    cos = jnp.concatenate([jnp.cos(ang_r)] * 2 + [jnp.cos(ang_c)] * 2, axis=-1)
    sin = jnp.concatenate([-jnp.sin(ang_r), jnp.sin(ang_r), -jnp.sin(ang_c), jnp.sin(ang_c)], axis=-1)
    return jnp.tile(cos, (1, LANES // ROPE_DIM)), jnp.tile(sin, (1, LANES // ROPE_DIM))


def _permute_w_in(w):
    sizes = (ML_W, ML_W, ML_W, ML_W, 4 * ML_HEADS, MLA_HEADS * MLA_QK, MLA_KV_RANK + MLA_ROPE, S5_W,
             DF_W, DF_W, DF_W, N_BRANCH * D_MODEL)
    splits = tuple(int(s) for s in np.cumsum(sizes)[:-1])
    (ml_q, ml_k, ml_v, ml_o, ml_if, mla_q, mla_kva, s5_u, df_q, df_k, df_v, gate) = jnp.split(w, splits, axis=1)
    mq = mla_q.reshape(D_MODEL, MLA_HEADS, MLA_QK)
    qn = mq[:, :, :MLA_NOPE].reshape(D_MODEL, MLA_HEADS * MLA_NOPE)
    qr = mq[:, :, MLA_NOPE:].reshape(D_MODEL, MLA_HEADS * MLA_ROPE)
    pad = lambda a, width: jnp.pad(a, ((0, 0), (0, width - a.shape[1])))
    cols = [ml_q, ml_k, ml_v, ml_o, s5_u, df_q, df_k, df_v, gate, qn, qr,
            mla_kva[:, :MLA_KV_RANK], pad(mla_kva[:, MLA_KV_RANK:], LANES), pad(ml_if, LANES)]
    out = jnp.concatenate(cols, axis=1)
    return pad(out, PROJ_COLS).astype(BF16)


def _seq_mixers(proj, bsz, seq, lw, lam_init, state, cache, rope):
    n = bsz * seq
    tm = min(TM_SEQ, seq)
    ml_c0, ml_n0, ml_m0, s5_h0r, s5_h0i = state
    r8 = 2 * ML_HEADS

    m0 = jnp.broadcast_to(ml_m0.reshape(bsz, r8, 1), (bsz, r8, LANES))
    hf, hb, c_new, n_new, m_new = _mlstm(proj, bsz, seq, lw['ml_bias'],
                                         ml_c0.reshape(bsz, r8, ML_DH, ML_DH), ml_n0.reshape(bsz, r8, ML_DH), m0)
    y_ml = _ml_post(hf, hb, proj, lw['ml_norm'], tm)

    cos, sin = rope if rope is not None else (None, None)
    k_new, v_new, q_mla, ckv = _mla_prep(proj, proj, COL_MLA_CKV // MLA_KV_RANK, COL_MLA_KR // LANES, n,
                                         lw['mla_kv_norm'], lw['mla_w_kvb'], tm, q_src=proj, cos=cos, sin=sin)
    k_c = v_c = None
    if cache is not None:
        ckv_c, krope_c, dk_c, dv_c = cache
        past = ckv_c.shape[1]
        kr_pad = jnp.pad(krope_c.reshape(bsz * past, MLA_ROPE), ((0, 0), (0, LANES - MLA_ROPE)))
        k_c, v_c = _mla_prep(ckv_c.reshape(bsz * past, MLA_KV_RANK), kr_pad, 0, 0, bsz * past,
                             lw['mla_kv_norm'], lw['mla_w_kvb'], min(tm, bsz * past), norm=False)
    y_mla = _mla_attn(q_mla, k_new, v_new, bsz, seq, k_c, v_c)

    proj3 = proj.reshape(bsz, seq, PROJ_COLS)
    ng = bsz // S5_SEQS

    def pack_state(hs):
        return hs.reshape(ng, S5_SEQS, 2, S5_STATE).transpose(0, 2, 1, 3).reshape(ng, 2 * S5_SEQS, S5_STATE)

    def unpack_state(hs):
        return hs.reshape(ng, 2, S5_SEQS, S5_STATE).transpose(0, 2, 1, 3).reshape(bsz, 2, S5_GROUPS, S5_P)

    yf, yb, hr_new, hi_new = _s5(proj3, bsz, seq, lw['s5_a_re8'], lw['s5_a_im8'],
                                 pack_state(s5_h0r), pack_state(s5_h0i),
                                 lw['s5_wb_re'], lw['s5_wb_im'], lw['s5_wc_re'], lw['s5_wc_im'])
    y_s5 = _s5_post(yf.reshape(n, S5_W), yb.reshape(n, S5_W), proj, lw['s5_d'], lw['s5_w_glu'], tm)

    if cos is not None:
        dk, kcol = _rope_cast(proj, COL_DF_K // DF_W, n, tm, cos, sin), 0
    else:
        dk, kcol = proj, COL_DF_K // DF_DV
    dk_cache = dv_cache = None
    if cache is not None:
        dk_cache = dk_c.reshape(bsz * past, DF_W)
        dv_cache = dv_c.reshape(bsz * past, DF_W)
    y_df = _diff_attn(proj, dk, kcol, lw['df_lambda'], lw['df_norm'], lam_init, bsz, seq,
                      dk_cache, dv_cache, cos, sin)

    cols = lambda c0, w: lax.slice(proj, (0, c0), (n, c0 + w))
    new_ctx = (ckv.reshape(bsz, seq, MLA_KV_RANK),
               cols(COL_MLA_KR, MLA_ROPE).reshape(bsz, seq, MLA_ROPE),
               cols(COL_DF_K, DF_W).reshape(bsz, seq, DF_HEADS, 2 * DF_DQK),
               cols(COL_DF_V, DF_W).reshape(bsz, seq, DF_HEADS, DF_DV),
               c_new.reshape(bsz, 2, ML_HEADS, ML_DH, ML_DH), n_new.reshape(bsz, 2, ML_HEADS, ML_DH),
               m_new[:, :, 0].reshape(bsz, 2, ML_HEADS), unpack_state(hr_new), unpack_state(hi_new))
    return (y_ml, y_mla, y_s5, y_df), new_ctx


def kernel(x_prompt, x_sample, cache_mla_ckv, cache_mla_krope, cache_diff_k, cache_diff_v,
           state_mlstm_c, state_mlstm_n, state_mlstm_m, state_s5_re, state_s5_im, c,
           c_ctx, w_ada, b_ada, norm_mix, norm_ffn, w_in, ml_if_bias, ml_norm, mla_kv_norm,
           mla_w_kvb, s5_a_re, s5_a_im, s5_log_dt, s5_b_re, s5_b_im, s5_c_re, s5_c_im, s5_d,
           s5_w_glu, df_lambda, df_norm, w_branch, w_o, w_ffn_in, w_ffn_out, final_norm):
    bp, sp, _ = x_prompt.shape
    bs, ss, _ = x_sample.shape
    n_p, n_s = bp * sp, bs * ss
    tm = TM_DENSE

    cond = jnp.concatenate([c, c_ctx[None, :], jnp.zeros((SUBLANES - 1 - bs, D_MODEL), F32)], axis=0)
    mods = _ada(cond, w_ada, b_ada).reshape(DEPTH, SUBLANES, 6, 1, D_MODEL)
    rope = _rope_tables(ss)
    latent_row = lambda i: (i * tm) // ss
    prompt_row = lambda i: bs

    xs = x_sample.reshape(n_s, D_MODEL)
    xp = x_prompt.reshape(n_p, D_MODEL)
    zeros_state = (jnp.zeros((bp, 2, ML_HEADS, ML_DH, ML_DH), F32), jnp.zeros((bp, 2, ML_HEADS, ML_DH), F32),
                   jnp.zeros((bp, 2, ML_HEADS), F32), jnp.zeros((bp, 2, S5_GROUPS, S5_P), F32),
                   jnp.zeros((bp, 2, S5_GROUPS, S5_P), F32))
    ctx_out = []
    for l in range(DEPTH):
        lam_init = 0.8 - 0.6 * math.exp(-0.3 * l)
        ab_re, ab_im, bb_re, bb_im = _s5_discretise(s5_a_re[l], s5_a_im[l], s5_log_dt[l], s5_b_re[l], s5_b_im[l])
        rep = lambda a: jnp.repeat(a.reshape(2, S5_STATE), S5_SEQS, axis=0)
        lw = {
            'ml_bias': jnp.pad(ml_if_bias[l].reshape(1, 4 * ML_HEADS), ((0, 0), (0, LANES - 4 * ML_HEADS))),
            'ml_norm': ml_norm[l].reshape(1, ML_W),
            'mla_kv_norm': mla_kv_norm[l].reshape(1, MLA_KV_RANK),
            'mla_w_kvb': mla_w_kvb[l].astype(BF16),
            's5_a_re8': rep(ab_re), 's5_a_im8': rep(ab_im),
            's5_wb_re': _block_diag(bb_re.transpose(0, 1, 3, 2)).astype(BF16),
            's5_wb_im': _block_diag(bb_im.transpose(0, 1, 3, 2)).astype(BF16),
            's5_wc_re': _block_diag(s5_c_re[l].transpose(0, 1, 3, 2)).astype(BF16),
            's5_wc_im': _block_diag(s5_c_im[l].transpose(0, 1, 3, 2)).astype(BF16),
            's5_d': s5_d[l].reshape(1, S5_W),
            's5_w_glu': s5_w_glu[l].astype(BF16),
            'df_lambda': df_lambda[l],
            'df_norm': df_norm[l].reshape(1, DF_DV),
        }
        mod = mods[l]
        w_in_l = _permute_w_in(w_in[l])
        w_branch_l, w_o_l = w_branch[l].astype(BF16), w_o[l].astype(BF16)
        w_ffn_in_l, w_ffn_out_l = w_ffn_in[l].astype(BF16), w_ffn_out[l].astype(BF16)
        g_mix, g_ffn = norm_mix[l].reshape(1, D_MODEL), norm_ffn[l].reshape(1, D_MODEL)

        def layer(x, row_of_tile, bsz, seq, state, cache, rope_tabs):
            proj = _proj_in(x, g_mix, mod, row_of_tile, w_in_l, tm)
            ys, new_ctx = _seq_mixers(proj, bsz, seq, lw, lam_init, state, cache, rope_tabs)
            merged = _merge(list(ys), proj, w_branch_l, tm)
            x = _matmul_resid(merged, w_o_l, x, mod, 2, row_of_tile, tm)
            act = _ffn_in(x, g_ffn, mod, row_of_tile, w_ffn_in_l, tm)
            return _matmul_resid(act, w_ffn_out_l, x, mod, 5, row_of_tile, tm), new_ctx

        xp, new_ctx = layer(xp, prompt_row, bp, sp, zeros_state, None, None)
        ctx_out.append(new_ctx)
        state = (state_mlstm_c[:, l], state_mlstm_n[:, l], state_mlstm_m[:, l], state_s5_re[:, l], state_s5_im[:, l])
        cache = (cache_mla_ckv[:, l], cache_mla_krope[:, l], cache_diff_k[:, l], cache_diff_v[:, l])
        xs, _ = layer(xs, latent_row, bs, ss, state, cache, rope)

    g_fin = final_norm.reshape(1, D_MODEL)
    y_prompt = _final_norm(xp, g_fin, tm).reshape(bp, sp, D_MODEL)
    y_sample = _final_norm(xs, g_fin, tm).reshape(bs, ss, D_MODEL)
    stacked = tuple(jnp.stack([ctx[k] for ctx in ctx_out], axis=1) for k in range(9))
    return (y_prompt, y_sample) + stacked
```

```python
import functools
import math

import jax
import jax.numpy as jnp
import numpy as np
from jax import lax
from jax.experimental import pallas as pl
from jax.experimental.pallas import tpu as pltpu

F32 = jnp.float32
BF16 = jnp.bfloat16

D_MODEL = 2048
DEPTH = 2
GRID_W = 64
ROPE_DIM = 64
ROPE_BASE = 10000.0
RMS_EPS = 1e-6
ML_HEADS = 4
ML_DH = 256
ML_W = ML_HEADS * ML_DH
MLA_HEADS = 8
MLA_NOPE = 128
MLA_ROPE = ROPE_DIM
MLA_V = 128
MLA_KV_RANK = 512
MLA_QK = MLA_NOPE + MLA_ROPE
S5_GROUP = 16
S5_GROUPS = 64
S5_W = S5_GROUPS * S5_GROUP
S5_P = 64
S5_STATE = S5_GROUPS * S5_P
DF_HEADS = 8
DF_DQK = ROPE_DIM
DF_DV = 2 * DF_DQK
DF_W = DF_HEADS * DF_DV
N_BRANCH = 4
BRANCH_W = 1024
FFN_HIDDEN = (8 * D_MODEL + 3 * 256 - 1) // (3 * 256) * 256

LANES = 128
SUBLANES = 8
VMEM_LIMIT_BYTES = 56 * 1024 * 1024

COL_ML_Q = 0
COL_ML_K = 1024
COL_ML_V = 2048
COL_ML_O = 3072
COL_S5_U = 4096
COL_DF_Q = 5120
COL_DF_K = 6144
COL_DF_V = 7168
COL_GATE = 8192
COL_MLA_QN = 16384
COL_MLA_QR = 17408
COL_MLA_CKV = 17920
COL_MLA_KR = 18432
COL_ML_IF = 18560
PROJ_TN = 1024
PROJ_COLS = 19456

LOG2E = math.log2(math.e)
VT_ROWS = 128 + 16

TM_DENSE = 1024
DENSE_TN = 512
TM_SEQ = 512
ATTN_STEP_Q = 1024
ML_CHUNK = 256
S5_CHUNK = 64
S5_SEQS = 4
S5_LANE_BLK = 512
S5_GROUPS_PER_TILE = 16
ATTN_TQ = 256


def _cparams(*sem):
    return pltpu.CompilerParams(dimension_semantics=sem, vmem_limit_bytes=VMEM_LIMIT_BYTES)


def _dot(a, b):
    return jnp.dot(a, b, preferred_element_type=F32)


def _dot_nt(a, b):
    return lax.dot_general(a, b, (((1,), (1,)), ((), ())), preferred_element_type=F32)


def _dot_exact(a, b):
    return jnp.dot(a, b, preferred_element_type=F32, precision=lax.Precision.HIGHEST)


def _rms(x):
    return x * lax.rsqrt(jnp.mean(x * x, axis=-1, keepdims=True) + RMS_EPS)


def _ones_row_block(width):
    row = lax.broadcasted_iota(jnp.int32, (VT_ROWS - 128, width), 0)
    return jnp.where(row == 0, 1.0, 0.0).astype(BF16)


def _rope_slab(x, cos, sin):
    quarter = ROPE_DIM // 4
    lane = lax.broadcasted_iota(jnp.int32, x.shape, 1)
    partner = jnp.where((lane % (2 * quarter)) < quarter, pltpu.roll(x, LANES - quarter, 1),
                        pltpu.roll(x, quarter, 1))
    return x * cos + partner * sin


def _ada_kernel(c_ref, w_ref, b_ref, o_ref):
    c = c_ref[...]
    s = c * jax.nn.sigmoid(c)
    o_ref[...] = _dot(s.astype(BF16), w_ref[...].astype(BF16)) + b_ref[...]


def _ada(cond, w_ada, b_ada):
    rows = cond.shape[0]
    tn = PROJ_TN
    return pl.pallas_call(
        _ada_kernel,
        grid=(DEPTH, 6 * D_MODEL // tn),
        in_specs=[pl.BlockSpec((rows, D_MODEL), lambda l, j: (0, 0)),
                  pl.BlockSpec((None, D_MODEL, tn), lambda l, j: (l, 0, j)),
                  pl.BlockSpec((None, 1, tn), lambda l, j: (l, 0, j))],
        out_specs=pl.BlockSpec((None, rows, tn), lambda l, j: (l, 0, j)),
        out_shape=jax.ShapeDtypeStruct((DEPTH, rows, 6 * D_MODEL), F32),
        compiler_params=_cparams("parallel", "parallel"),
        name="ada",
    )(cond, w_ada, b_ada.reshape(DEPTH, 1, 6 * D_MODEL))


def _mod_spec(which, row_of_tile):
    return pl.BlockSpec((None, None, 1, D_MODEL), lambda i, j: (row_of_tile(i), which, 0, 0))


def _norm_mod(x_ref, g_ref, sc_ref, sh_ref):
    return (_rms(x_ref[...]) * g_ref[...]) * (1.0 + sc_ref[...]) + sh_ref[...]


def _proj_in_kernel(x_ref, g_ref, sc_ref, sh_ref, w_ref, o_ref, h_ref):
    @pl.when(pl.program_id(1) == 0)
    def _():
        h_ref[...] = _norm_mod(x_ref, g_ref, sc_ref, sh_ref).astype(BF16)

    o_ref[...] = _dot(h_ref[...], w_ref[...])


def _proj_in(x, g, mod, row_of_tile, w, tm):
    n = x.shape[0]
    ncol = w.shape[1]
    return pl.pallas_call(
        _proj_in_kernel,
        grid=(n // tm, ncol // PROJ_TN),
        in_specs=[pl.BlockSpec((tm, D_MODEL), lambda i, j: (i, 0)),
                  pl.BlockSpec((1, D_MODEL), lambda i, j: (0, 0)),
                  _mod_spec(1, row_of_tile), _mod_spec(0, row_of_tile),
                  pl.BlockSpec((D_MODEL, PROJ_TN), lambda i, j: (0, j))],
        out_specs=pl.BlockSpec((tm, PROJ_TN), lambda i, j: (i, j)),
        out_shape=jax.ShapeDtypeStruct((n, ncol), F32),
        scratch_shapes=[pltpu.VMEM((tm, D_MODEL), BF16)],
        compiler_params=_cparams("parallel", "arbitrary"),
        name="proj_in",
    )(x, g, mod, mod, w)


def _ffn_in_kernel(x_ref, g_ref, sc_ref, sh_ref, wa_ref, wb_ref, o_ref, h_ref):
    @pl.when(pl.program_id(1) == 0)
    def _():
        h_ref[...] = _norm_mod(x_ref, g_ref, sc_ref, sh_ref).astype(BF16)

    h = h_ref[...]
    a = _dot(h, wa_ref[...])
    b = _dot(h, wb_ref[...])
    o_ref[...] = (a * jax.nn.sigmoid(a) * b).astype(BF16)


def _ffn_in(x, g, mod, row_of_tile, w, tm):
    n = x.shape[0]
    tn = DENSE_TN
    nj = FFN_HIDDEN // tn
    return pl.pallas_call(
        _ffn_in_kernel,
        grid=(n // tm, nj),
        in_specs=[pl.BlockSpec((tm, D_MODEL), lambda i, j: (i, 0)),
                  pl.BlockSpec((1, D_MODEL), lambda i, j: (0, 0)),
                  _mod_spec(4, row_of_tile), _mod_spec(3, row_of_tile),
                  pl.BlockSpec((D_MODEL, tn), lambda i, j: (0, j)),
                  pl.BlockSpec((D_MODEL, tn), lambda i, j: (0, nj + j))],
        out_specs=pl.BlockSpec((tm, tn), lambda i, j: (i, j)),
        out_shape=jax.ShapeDtypeStruct((n, FFN_HIDDEN), BF16),
        scratch_shapes=[pltpu.VMEM((tm, D_MODEL), BF16)],
        compiler_params=_cparams("parallel", "arbitrary"),
        name="ffn_in",
    )(x, g, mod, mod, w, w)


def _resid_kernel(a_ref, w_ref, x_ref, gate_ref, o_ref):
    o_ref[...] = x_ref[...] + gate_ref[...] * _dot(a_ref[...], w_ref[...])


def _matmul_resid(a, w, x, mod, which, row_of_tile, tm):
    n, kdim = a.shape
    tn = DENSE_TN
    return pl.pallas_call(
        _resid_kernel,
        grid=(n // tm, D_MODEL // tn),
        in_specs=[pl.BlockSpec((tm, kdim), lambda i, j: (i, 0)),
                  pl.BlockSpec((kdim, tn), lambda i, j: (0, j)),
                  pl.BlockSpec((tm, tn), lambda i, j: (i, j)),
                  pl.BlockSpec((None, None, 1, tn), lambda i, j: (row_of_tile(i), which, 0, j))],
        out_specs=pl.BlockSpec((tm, tn), lambda i, j: (i, j)),
        out_shape=jax.ShapeDtypeStruct((n, D_MODEL), F32),
        compiler_params=_cparams("parallel", "parallel"),
        name="matmul_resid",
    )(a, w, x, mod)


def _final_norm_kernel(x_ref, g_ref, o_ref):
    o_ref[...] = _rms(x_ref[...]) * g_ref[...]


def _final_norm(x, g, tm):
    n = x.shape[0]
    return pl.pallas_call(
        _final_norm_kernel,
        grid=(n // tm,),
        in_specs=[pl.BlockSpec((tm, D_MODEL), lambda i: (i, 0)),
                  pl.BlockSpec((1, D_MODEL), lambda i: (0, 0))],
        out_specs=pl.BlockSpec((tm, D_MODEL), lambda i: (i, 0)),
        out_shape=jax.ShapeDtypeStruct((n, D_MODEL), F32),
        compiler_params=_cparams("parallel"),
        name="final_norm",
    )(x, g)


def _mlstm_kernel(qf_ref, kf_ref, vf_ref, gf_ref, qb_ref, kb_ref, vb_ref, gb_ref, bias_ref,
                  c0_ref, n0_ref, m0_ref,
                  hf_ref, hb_ref, c_out_ref, n_out_ref, m_out_ref,
                  c_scr, n_scr, m_scr):
    ci = pl.program_id(1)
    t = qf_ref.shape[0]

    @pl.when(ci == 0)
    def _():
        c_scr[...] = c0_ref[...]
        n_scr[...] = n0_ref[...]
        m_scr[...] = m0_ref[...]

    row = lax.broadcasted_iota(jnp.int32, (t, t), 0)
    col = lax.broadcasted_iota(jnp.int32, (t, t), 1)
    refs = ((qf_ref, kf_ref, vf_ref, gf_ref, hf_ref), (qb_ref, kb_ref, vb_ref, gb_ref, hb_ref))
    for d in range(2):
        q_ref, k_ref, v_ref, g_ref, h_ref = refs[d]
        keep = (col <= row) if d == 0 else (col >= row)
        cum = jnp.where(keep, 1.0, 0.0).astype(F32)
        gates = g_ref[...] + bias_ref[...]
        csum = _dot_exact(cum, jax.nn.log_sigmoid(gates))
        gates_t = gates.T
        csum_t = csum.T
        last = t - 1 if d == 0 else 0
        for hd in range(ML_HEADS):
            r = d * ML_HEADS + hd
            i_col = d * 2 * ML_HEADS + hd
            f_col = i_col + ML_HEADS
            b_c = csum[:, f_col:f_col + 1]
            b_r = csum_t[f_col:f_col + 1, :]
            li_c = gates[:, i_col:i_col + 1]
            li_r = gates_t[i_col:i_col + 1, :]
            m_st = m_scr[r:r + 1, 0:1]
            c_st = c_scr[r]
            n_st = n_scr[r:r + 1, :]
            sl = slice(hd * ML_DH, (hd + 1) * ML_DH)
            q = q_ref[:, sl]
            k = k_ref[:, sl] * (ML_DH ** -0.5)
            v = v_ref[:, sl]
            qb16 = q.astype(BF16)
            kb16 = k.astype(BF16)

            log_d = jnp.where(keep, b_c - b_r + li_r, -jnp.inf)
            log_inter = b_c + m_st
            m_t = jnp.maximum(log_inter, jnp.max(log_d, axis=1, keepdims=True))
            w_d = jnp.exp(log_d - m_t)
            w_inter = jnp.exp(log_inter - m_t)
            s = _dot_nt(qb16, kb16) * w_d
            num = _dot(s.astype(BF16), v.astype(BF16)) + w_inter * _dot_nt(qb16, c_st.astype(BF16))
            den = jnp.sum(s, axis=1, keepdims=True) + w_inter * jnp.sum(q * n_st, axis=1, keepdims=True)
            h_ref[:, sl] = num / jnp.maximum(jnp.abs(den), jnp.exp(-m_t))

            b_last = b_c[last:last + 1, :]
            log_w = b_last - b_c + li_c
            m_new = jnp.maximum(b_last + m_st, jnp.max(log_w, axis=0, keepdims=True))
            w_s = jnp.exp(log_w - m_new)
            w_c = jnp.exp(b_last + m_st - m_new)
            vw_t = (v * w_s).T.astype(BF16)
            c_scr[r] = w_c * c_st + _dot(vw_t, kb16)
            n_scr[r:r + 1, :] = w_c * n_st + jnp.sum(k * w_s, axis=0, keepdims=True)
            m_scr[r:r + 1, :] = jnp.broadcast_to(m_new, (1, LANES))

    @pl.when(ci == pl.num_programs(1) - 1)
    def _():
        c_out_ref[...] = c_scr[...]
        n_out_ref[...] = n_scr[...]
        m_out_ref[...] = m_scr[...]


def _mlstm(proj, bsz, seq, bias, c0, n0, m0):
    tc = min(ML_CHUNK, seq)
    nc = seq // tc
    r8 = 2 * ML_HEADS

    def fwd(colblk):
        return lambda b, c: (b * nc + c, colblk)

    def bwd(colblk):
        return lambda b, c: (b * nc + nc - 1 - c, colblk)

    def seqspecs(mk):
        return [pl.BlockSpec((tc, ML_W), mk(COL_ML_Q // ML_W)),
                pl.BlockSpec((tc, ML_W), mk(COL_ML_K // ML_W)),
                pl.BlockSpec((tc, ML_W), mk(COL_ML_V // ML_W)),
                pl.BlockSpec((tc, LANES), mk(COL_ML_IF // LANES))]

    state_specs = [pl.BlockSpec((None, r8, ML_DH, ML_DH), lambda b, c: (b, 0, 0, 0)),
                   pl.BlockSpec((None, r8, ML_DH), lambda b, c: (b, 0, 0)),
                   pl.BlockSpec((None, r8, LANES), lambda b, c: (b, 0, 0))]
    return pl.pallas_call(
        _mlstm_kernel,
        grid=(bsz, nc),
        in_specs=seqspecs(fwd) + seqspecs(bwd) + [pl.BlockSpec((1, LANES), lambda b, c: (0, 0))] + state_specs,
        out_specs=[pl.BlockSpec((tc, ML_W), lambda b, c: (b * nc + c, 0)),
                   pl.BlockSpec((tc, ML_W), lambda b, c: (b * nc + nc - 1 - c, 0))] + state_specs,
        out_shape=[jax.ShapeDtypeStruct((bsz * seq, ML_W), F32),
                   jax.ShapeDtypeStruct((bsz * seq, ML_W), F32),
                   jax.ShapeDtypeStruct((bsz, r8, ML_DH, ML_DH), F32),
                   jax.ShapeDtypeStruct((bsz, r8, ML_DH), F32),
                   jax.ShapeDtypeStruct((bsz, r8, LANES), F32)],
        scratch_shapes=[pltpu.VMEM((r8, ML_DH, ML_DH), F32),
                        pltpu.VMEM((r8, ML_DH), F32),
                        pltpu.VMEM((r8, LANES), F32)],
        compiler_params=_cparams("parallel", "arbitrary"),
        name="mlstm",
    )(proj, proj, proj, proj, proj, proj, proj, proj, bias, c0, n0, m0)


def _ml_post_kernel(hf_ref, hb_ref, o_ref, g_ref, y_ref):
    h = hf_ref[...] + hb_ref[...]
    for hd in range(ML_HEADS):
        sl = slice(hd * ML_DH, (hd + 1) * ML_DH)
        y_ref[:, sl] = (_rms(h[:, sl]) * g_ref[:, sl] * jax.nn.sigmoid(o_ref[:, sl])).astype(BF16)


def _ml_post(hf, hb, proj, g, tm):
    n = hf.shape[0]
    return pl.pallas_call(
        _ml_post_kernel,
        grid=(n // tm,),
        in_specs=[pl.BlockSpec((tm, ML_W), lambda i: (i, 0)),
                  pl.BlockSpec((tm, ML_W), lambda i: (i, 0)),
                  pl.BlockSpec((tm, ML_W), lambda i: (i, COL_ML_O // ML_W)),
                  pl.BlockSpec((1, ML_W), lambda i: (0, 0))],
        out_specs=pl.BlockSpec((tm, ML_W), lambda i: (i, 0)),
        out_shape=jax.ShapeDtypeStruct((n, ML_W), BF16),
        compiler_params=_cparams("parallel"),
        name="ml_post",
    )(hf, hb, proj, g)


def _mla_prep_kernel(*refs, norm, rope, with_q):
    it = iter(refs)
    ckv_ref, kr_ref, g_ref, wkvb_ref = next(it), next(it), next(it), next(it)
    if with_q:
        qn_ref, qr_ref = next(it), next(it)
    if rope:
        cos_ref, sin_ref = next(it), next(it)
    k_ref, v_ref = next(it), next(it)
    if with_q:
        q_ref = next(it)
    if norm:
        ckv_out_ref = next(it)

    ckv = ckv_ref[...]
    if norm:
        ckv = _rms(ckv) * g_ref[...]
        ckv_out_ref[...] = ckv
    kv = _dot(ckv.astype(BF16), wkvb_ref[...])
    kr = kr_ref[...]
    if rope:
        kr = _rope_slab(kr, cos_ref[...], sin_ref[...])
    kr16 = kr[:, :MLA_ROPE].astype(BF16)
    hw = MLA_NOPE + MLA_V
    for hd in range(MLA_HEADS):
        k_ref[hd, :, 0:MLA_NOPE] = kv[:, hd * hw:hd * hw + MLA_NOPE].astype(BF16)
        k_ref[hd, :, MLA_NOPE:MLA_QK] = kr16
        v_ref[hd, 0:MLA_V, :] = kv[:, hd * hw + MLA_NOPE:(hd + 1) * hw].T.astype(BF16)
        v_ref[hd, MLA_V:, :] = _ones_row_block(k_ref.shape[1])
    if with_q:
        scale = MLA_QK ** -0.5 * LOG2E
        qn = qn_ref[...]
        for sb in range(MLA_HEADS * MLA_ROPE // LANES):
            qr = qr_ref[:, sb * LANES:(sb + 1) * LANES]
            if rope:
                qr = _rope_slab(qr, cos_ref[...], sin_ref[...])
            for half in range(LANES // MLA_ROPE):
                hd = sb * (LANES // MLA_ROPE) + half
                q_ref[hd, :, MLA_NOPE:MLA_QK] = (qr[:, half * MLA_ROPE:(half + 1) * MLA_ROPE] * scale).astype(BF16)
        for hd in range(MLA_HEADS):
            q_ref[hd, :, 0:MLA_NOPE] = (qn[:, hd * MLA_NOPE:(hd + 1) * MLA_NOPE] * scale).astype(BF16)


def _mla_prep(ckv_src, kr_src, ckv_col, kr_col, n, g, wkvb, tm, q_src=None, cos=None, sin=None, norm=True):
    rope = cos is not None
    with_q = q_src is not None
    ins = [ckv_src, kr_src, g, wkvb]
    in_specs = [pl.BlockSpec((tm, MLA_KV_RANK), lambda i: (i, ckv_col)),
                pl.BlockSpec((tm, LANES), lambda i: (i, kr_col)),
                pl.BlockSpec((1, MLA_KV_RANK), lambda i: (0, 0)),
                pl.BlockSpec(wkvb.shape, lambda i: (0, 0))]
    if with_q:
        ins += [q_src, q_src]
        in_specs += [pl.BlockSpec((tm, MLA_HEADS * MLA_NOPE), lambda i: (i, COL_MLA_QN // (MLA_HEADS * MLA_NOPE))),
                     pl.BlockSpec((tm, MLA_HEADS * MLA_ROPE), lambda i: (i, COL_MLA_QR // (MLA_HEADS * MLA_ROPE)))]
    if rope:
        nt = cos.shape[0] // tm
        ins += [cos, sin]
        in_specs += [pl.BlockSpec((tm, LANES), lambda i: (i % nt, 0))] * 2
    out_shape = [jax.ShapeDtypeStruct((MLA_HEADS, n, MLA_QK), BF16),
                 jax.ShapeDtypeStruct((MLA_HEADS, VT_ROWS, n), BF16)]
    out_specs = [pl.BlockSpec((MLA_HEADS, tm, MLA_QK), lambda i: (0, i, 0)),
                 pl.BlockSpec((MLA_HEADS, VT_ROWS, tm), lambda i: (0, 0, i))]
    if with_q:
        out_shape.append(jax.ShapeDtypeStruct((MLA_HEADS, n, MLA_QK), BF16))
        out_specs.append(pl.BlockSpec((MLA_HEADS, tm, MLA_QK), lambda i: (0, i, 0)))
    if norm:
        out_shape.append(jax.ShapeDtypeStruct((n, MLA_KV_RANK), F32))
        out_specs.append(pl.BlockSpec((tm, MLA_KV_RANK), lambda i: (i, 0)))
    return pl.pallas_call(
        functools.partial(_mla_prep_kernel, norm=norm, rope=rope, with_q=with_q),
        grid=(n // tm,),
        in_specs=in_specs, out_specs=out_specs, out_shape=out_shape,
        compiler_params=_cparams("parallel"),
        name="mla_prep",
    )(*ins)


def _query_halves(q_ref):
    tq = min(ATTN_TQ, q_ref.shape[0])
    return [slice(i * tq, (i + 1) * tq) for i in range(q_ref.shape[0] // tq)]


def _mla_attn_kernel(*refs, cached):
    if cached:
        q_ref, k_ref, v_ref, kc_ref, vc_ref, o_ref, s_scr, p_scr = refs
        sources = ((k_ref, v_ref), (kc_ref, vc_ref))
    else:
        q_ref, k_ref, v_ref, o_ref, s_scr, p_scr = refs
        sources = ((k_ref, v_ref),)
    halves = _query_halves(q_ref)
    spans, off = [], 0
    for kk_ref, _ in sources:
        spans.append(slice(off, off + kk_ref.shape[0]))
        off += kk_ref.shape[0]
    for i, rows in enumerate(halves):
        for (kk_ref, _), span in zip(sources, spans):
            s_scr[i, span, :] = _dot_nt(kk_ref[...], q_ref[rows, :])
    for i, rows in enumerate(halves):
        s = s_scr[i]
        p_scr[i] = jnp.exp2(s - jnp.max(s, axis=0, keepdims=True)).astype(BF16)
        oe = functools.reduce(jnp.add, [_dot(vv_ref[...], p_scr[i, span, :]) for (_, vv_ref), span in zip(sources, spans)])
        o = oe[0:MLA_V] / oe[MLA_V:MLA_V + 1]
        o_ref[rows, :] = o.T.astype(BF16)


def _mla_attn(q, k, v, bsz, seq, k_c=None, v_c=None):
    tq = min(ATTN_STEP_Q, seq)
    nq = seq // tq
    cached = k_c is not None
    keys = seq
    ins = [q, k, v]
    in_specs = [pl.BlockSpec((None, tq, MLA_QK), lambda h, b, i: (h, b * nq + i, 0)),
                pl.BlockSpec((None, seq, MLA_QK), lambda h, b, i: (h, b, 0)),
                pl.BlockSpec((None, VT_ROWS, seq), lambda h, b, i: (h, 0, b))]
    if cached:
        past = k_c.shape[1] // bsz
        keys += past
        ins += [k_c, v_c]
        in_specs += [pl.BlockSpec((None, past, MLA_QK), lambda h, b, i: (h, b, 0)),
                     pl.BlockSpec((None, VT_ROWS, past), lambda h, b, i: (h, 0, b))]
    nsub = tq // min(ATTN_TQ, tq)
    return pl.pallas_call(
        functools.partial(_mla_attn_kernel, cached=cached),
        grid=(MLA_HEADS, bsz, nq),
        in_specs=in_specs,
        out_specs=pl.BlockSpec((tq, MLA_V), lambda h, b, i: (b * nq + i, h)),
        out_shape=jax.ShapeDtypeStruct((bsz * seq, MLA_HEADS * MLA_V), BF16),
        scratch_shapes=[pltpu.VMEM((nsub, keys, tq // nsub), F32),
                        pltpu.VMEM((nsub, keys, tq // nsub), BF16)],
        compiler_params=_cparams("parallel", "parallel", "parallel"),
        name="mla_attn",
    )(*ins)


def _rope_cast_kernel(x_ref, cos_ref, sin_ref, o_ref):
    for sb in range(x_ref.shape[1] // LANES):
        sl = slice(sb * LANES, (sb + 1) * LANES)
        o_ref[:, sl] = _rope_slab(x_ref[:, sl], cos_ref[...], sin_ref[...]).astype(BF16)


def _rope_cast(src, colblk, n, tm, cos, sin):
    nt = cos.shape[0] // tm
    return pl.pallas_call(
        _rope_cast_kernel,
        grid=(n // tm,),
        in_specs=[pl.BlockSpec((tm, DF_W), lambda i: (i, colblk))]
                 + [pl.BlockSpec((tm, LANES), lambda i: (i % nt, 0))] * 2,
        out_specs=pl.BlockSpec((tm, DF_W), lambda i: (i, 0)),
        out_shape=jax.ShapeDtypeStruct((n, DF_W), BF16),
        compiler_params=_cparams("parallel"),
        name="rope_cast",
    )(src, cos, sin)


def _diff_attn_kernel(*refs, lam_init, cached, rope):
    it = iter(refs)
    q_ref, k_ref, v_ref = next(it), next(it), next(it)
    sources = [(k_ref, v_ref)]
    if cached:
        sources.append((next(it), next(it)))
    if rope:
        cos_ref, sin_ref = next(it), next(it)
    lam_ref, g_ref, o_ref, s_scr, p_scr = next(it), next(it), next(it), next(it), next(it)
    lp = lam_ref[...]
    lam = (jnp.exp(jnp.sum(lp[0:1] * lp[1:2], axis=-1, keepdims=True))
           - jnp.exp(jnp.sum(lp[2:3] * lp[3:4], axis=-1, keepdims=True)) + lam_init)
    halves = _query_halves(q_ref)
    ks = [kk_ref[...].astype(BF16) for kk_ref, _ in sources]
    vs = [vv_ref[...].astype(BF16) for _, vv_ref in sources]
    spans, off = [], 0
    for kk_ref, _ in sources:
        spans.append(slice(off, off + kk_ref.shape[0]))
        off += kk_ref.shape[0]
    for i, rows in enumerate(halves):
        q = q_ref[rows, :]
        if rope:
            q = _rope_slab(q, cos_ref[rows, :], sin_ref[rows, :])
        q = (q * (DF_DQK ** -0.5 * LOG2E)).astype(BF16)
        lane = lax.broadcasted_iota(jnp.int32, q.shape, 1)
        zero = jnp.zeros_like(q)
        for comp, qc in enumerate((jnp.where(lane < DF_DQK, q, zero), jnp.where(lane >= DF_DQK, q, zero))):
            for k, span in zip(ks, spans):
                s_scr[comp, i, :, span] = _dot_nt(qc, k)
    for i, rows in enumerate(halves):
        r = []
        for comp in range(2):
            s = s_scr[comp, i]
            p = jnp.exp2(s - jnp.max(s, axis=-1, keepdims=True))
            r.append(1.0 / jnp.sum(p, axis=-1, keepdims=True))
            s_scr[comp, i] = p
        c = lam * r[1] / r[0]
        p_scr[i] = (s_scr[0, i] - c * s_scr[1, i]).astype(BF16)
        o = functools.reduce(jnp.add, [_dot(p_scr[i, :, span], v) for v, span in zip(vs, spans)]) * r[0]
        o_ref[rows, :] = (_rms(o) * g_ref[...] * (1.0 - lam_init)).astype(BF16)


def _diff_attn(proj, k, kcol, lam_p, g, lam_init, bsz, seq, k_c=None, v_c=None, cos=None, sin=None):
    tq = min(ATTN_STEP_Q, seq)
    nq = seq // tq
    nsub = tq // min(ATTN_TQ, tq)
    cached = k_c is not None
    rope = cos is not None
    keys = seq
    qcol, vcol = COL_DF_Q // DF_DV, COL_DF_V // DF_DV
    ins = [proj, k, proj]
    in_specs = [pl.BlockSpec((tq, DF_DV), lambda b, h, i: (b * nq + i, qcol + h)),
                pl.BlockSpec((seq, DF_DV), lambda b, h, i: (b, kcol + h)),
                pl.BlockSpec((seq, DF_DV), lambda b, h, i: (b, vcol + h))]
    if cached:
        past = k_c.shape[0] // bsz
        keys += past
        ins += [k_c, v_c]
        in_specs += [pl.BlockSpec((past, DF_DV), lambda b, h, i: (b, h)),
                     pl.BlockSpec((past, DF_DV), lambda b, h, i: (b, h))]
    if rope:
        ins += [cos, sin]
        in_specs += [pl.BlockSpec((tq, LANES), lambda b, h, i: (i, 0))] * 2
    ins += [lam_p, g]
    in_specs += [pl.BlockSpec((4, DF_DQK), lambda b, h, i: (0, 0)),
                 pl.BlockSpec((1, DF_DV), lambda b, h, i: (0, 0))]
    return pl.pallas_call(
        functools.partial(_diff_attn_kernel, lam_init=lam_init, cached=cached, rope=rope),
        grid=(bsz, DF_HEADS, nq),
        in_specs=in_specs,
        out_specs=pl.BlockSpec((tq, DF_DV), lambda b, h, i: (b * nq + i, h)),
        out_shape=jax.ShapeDtypeStruct((bsz * seq, DF_W), BF16),
        scratch_shapes=[pltpu.VMEM((2, nsub, tq // nsub, keys), F32),
                        pltpu.VMEM((nsub, tq // nsub, keys), BF16)],
        compiler_params=_cparams("parallel", "parallel", "parallel"),
        name="diff_attn",
    )(*ins)


def _s5_kernel(uf_ref, ub_ref, a_re_ref, a_im_ref, h0_re_ref, h0_im_ref,
               wb_re_ref, wb_im_ref, wc_re_ref, wc_im_ref,
               yf_ref, yb_ref, hr_out_ref, hi_out_ref,
               bu_re, bu_im, h_re, h_im):
    ci = pl.program_id(1)
    tc = uf_ref.shape[1]
    nseq = 2 * S5_SEQS
    gblk, sblk = wb_re_ref.shape[2:]
    ngb = S5_W // gblk

    @pl.when(ci == 0)
    def _():
        h_re[...] = h0_re_ref[...]
        h_im[...] = h0_im_ref[...]

    ri = lax.broadcasted_iota(jnp.int32, (tc, tc), 0)
    cj = lax.broadcasted_iota(jnp.int32, (tc, tc), 1)
    rev = jnp.where(ri + cj == tc - 1, 1.0, 0.0).astype(BF16)

    for d, u_ref in enumerate((uf_ref, ub_ref)):
        us = []
        for s in range(S5_SEQS):
            u = u_ref[s].astype(BF16)
            if d == 1:
                u = _dot(rev, u).astype(BF16)
            us.append(u)
        u_all = jnp.concatenate(us, axis=0)
        for gb in range(ngb):
            ug = u_all[:, gb * gblk:(gb + 1) * gblk]
            for w_ref, dst in ((wb_re_ref, bu_re), (wb_im_ref, bu_im)):
                bu = _dot(ug, w_ref[d, gb])
                for s in range(S5_SEQS):
                    for lk in range(sblk // LANES):
                        dst[gb * (sblk // LANES) + lk, pl.ds(d * S5_SEQS + s, tc, stride=nseq), :] = (
                            bu[s * tc:(s + 1) * tc, lk * LANES:(lk + 1) * LANES])

    nlk = S5_LANE_BLK // LANES
    for lb in range(S5_STATE // S5_LANE_BLK):
        lks = tuple(range(lb * nlk, (lb + 1) * nlk))
        ar = [a_re_ref[:, k * LANES:(k + 1) * LANES] for k in lks]
        ai = [a_im_ref[:, k * LANES:(k + 1) * LANES] for k in lks]

        def step(j, carry):
            r0 = pl.multiple_of(j * nseq, nseq)
            out = []
            for i, k in enumerate(lks):
                hr, hi = carry[2 * i], carry[2 * i + 1]
                nhr = ar[i] * hr - ai[i] * hi + bu_re[k, pl.ds(r0, nseq), :]
                nhi = ar[i] * hi + ai[i] * hr + bu_im[k, pl.ds(r0, nseq), :]
                bu_re[k, pl.ds(r0, nseq), :] = nhr
                bu_im[k, pl.ds(r0, nseq), :] = nhi
                out += [nhr, nhi]
            return tuple(out)

        init = []
        for k in lks:
            init += [h_re[:, k * LANES:(k + 1) * LANES], h_im[:, k * LANES:(k + 1) * LANES]]
        fin = lax.fori_loop(0, tc, step, tuple(init), unroll=8)
        for i, k in enumerate(lks):
            h_re[:, k * LANES:(k + 1) * LANES] = fin[2 * i]
            h_im[:, k * LANES:(k + 1) * LANES] = fin[2 * i + 1]

    def seq_states(src, s):
        parts = [src[k, pl.ds(s, tc, stride=nseq), :] for k in range(S5_STATE // LANES)]
        return jnp.concatenate(parts, axis=1).astype(BF16)

    def reverse_rows(y):
        hi = y.astype(BF16)
        r1 = y - hi.astype(F32)
        mid = r1.astype(BF16)
        lo = (r1 - mid.astype(F32)).astype(BF16)
        return _dot(rev, hi) + _dot(rev, mid) + _dot(rev, lo)

    for d, y_ref in enumerate((yf_ref, yb_ref)):
        hr_all = jnp.concatenate([seq_states(bu_re, d * S5_SEQS + s) for s in range(S5_SEQS)], axis=0)
        hi_all = jnp.concatenate([seq_states(bu_im, d * S5_SEQS + s) for s in range(S5_SEQS)], axis=0)
        for gb in range(ngb):
            ssl = slice(gb * sblk, (gb + 1) * sblk)
            y = _dot(hr_all[:, ssl], wc_re_ref[d, gb]) - _dot(hi_all[:, ssl], wc_im_ref[d, gb])
            for s in range(S5_SEQS):
                ys = y[s * tc:(s + 1) * tc]
                y_ref[s, :, gb * gblk:(gb + 1) * gblk] = reverse_rows(ys) if d == 1 else ys

    @pl.when(ci == pl.num_programs(1) - 1)
    def _():
        hr_out_ref[...] = h_re[...]
        hi_out_ref[...] = h_im[...]


def _s5(proj3, bsz, seq, a_re, a_im, h0_re, h0_im, wb_re, wb_im, wc_re, wc_im):
    tc = min(S5_CHUNK, seq)
    nc = seq // tc
    ng = bsz // S5_SEQS
    nseq = 2 * S5_SEQS
    ucol = COL_S5_U // S5_W
    const4 = lambda g, c: (0, 0, 0, 0)
    wspec = lambda w: pl.BlockSpec(w.shape, const4, pipeline_mode=pl.Buffered(1))
    hspec = pl.BlockSpec((None, nseq, S5_STATE), lambda g, c: (g, 0, 0))
    return pl.pallas_call(
        _s5_kernel,
        grid=(ng, nc),
        in_specs=[pl.BlockSpec((S5_SEQS, tc, S5_W), lambda g, c: (g, c, ucol)),
                  pl.BlockSpec((S5_SEQS, tc, S5_W), lambda g, c: (g, nc - 1 - c, ucol)),
                  pl.BlockSpec((nseq, S5_STATE), lambda g, c: (0, 0)),
                  pl.BlockSpec((nseq, S5_STATE), lambda g, c: (0, 0)),
                  hspec, hspec, wspec(wb_re), wspec(wb_im), wspec(wc_re), wspec(wc_im)],
        out_specs=[pl.BlockSpec((S5_SEQS, tc, S5_W), lambda g, c: (g, c, 0)),
                   pl.BlockSpec((S5_SEQS, tc, S5_W), lambda g, c: (g, nc - 1 - c, 0)),
                   hspec, hspec],
        out_shape=[jax.ShapeDtypeStruct((bsz, seq, S5_W), F32),
                   jax.ShapeDtypeStruct((bsz, seq, S5_W), F32),
                   jax.ShapeDtypeStruct((ng, nseq, S5_STATE), F32),
                   jax.ShapeDtypeStruct((ng, nseq, S5_STATE), F32)],
        scratch_shapes=[pltpu.VMEM((S5_STATE // LANES, tc * nseq, LANES), F32),
                        pltpu.VMEM((S5_STATE // LANES, tc * nseq, LANES), F32),
                        pltpu.VMEM((nseq, S5_STATE), F32),
                        pltpu.VMEM((nseq, S5_STATE), F32)],
        compiler_params=_cparams("parallel", "arbitrary"),
        name="s5",
    )(proj3, proj3, a_re, a_im, h0_re, h0_im, wb_re, wb_im, wc_re, wc_im)


def _s5_post_kernel(yf_ref, yb_ref, u_ref, d_ref, w_ref, o_ref):
    y = (yf_ref[...] + yb_ref[...]) + d_ref[...] * u_ref[...]
    g = jax.nn.gelu(y)
    o_ref[...] = (g * jax.nn.sigmoid(_dot(g.astype(BF16), w_ref[...]))).astype(BF16)


def _s5_post(yf, yb, proj, d, w, tm):
    n = yf.shape[0]
    return pl.pallas_call(
        _s5_post_kernel,
        grid=(n // tm,),
        in_specs=[pl.BlockSpec((tm, S5_W), lambda i: (i, 0)),
                  pl.BlockSpec((tm, S5_W), lambda i: (i, 0)),
                  pl.BlockSpec((tm, S5_W), lambda i: (i, COL_S5_U // S5_W)),
                  pl.BlockSpec((1, S5_W), lambda i: (0, 0)),
                  pl.BlockSpec((S5_W, S5_W), lambda i: (0, 0))],
        out_specs=pl.BlockSpec((tm, S5_W), lambda i: (i, 0)),
        out_shape=jax.ShapeDtypeStruct((n, S5_W), BF16),
        compiler_params=_cparams("parallel"),
        name="s5_post",
    )(yf, yb, proj, d, w)


def _s5_discretise(a_re, a_im, log_dt, b_re, b_im):
    lr = jnp.minimum(a_re, -1e-4)
    li = a_im
    dt = jnp.exp(log_dt)[..., None]
    mag = jnp.exp(dt * lr)
    ab_re, ab_im = mag * jnp.cos(dt * li), mag * jnp.sin(dt * li)
    den = lr * lr + li * li
    nr, ni = ab_re - 1.0, ab_im
    qr = (nr * lr + ni * li) / den
    qi = (ni * lr - nr * li) / den
    bb_re = qr[..., None] * b_re - qi[..., None] * b_im
    bb_im = qr[..., None] * b_im + qi[..., None] * b_re
    return ab_re, ab_im, bb_re, bb_im


def _block_diag(w):
    ngrp = S5_GROUPS_PER_TILE
    d, g, r, c = w.shape
    wg = w.reshape(d, g // ngrp, ngrp, r, c)
    eye = jnp.eye(ngrp, dtype=w.dtype)
    out = jnp.einsum('dbgrc,gh->dbgrhc', wg, eye)
    return out.reshape(d, g // ngrp, ngrp * r, ngrp * c)


def _merge_kernel(y0_ref, y1_ref, y2_ref, y3_ref, g0_ref, g1_ref, g2_ref, g3_ref, w_ref, o_ref):
    acc = None
    for b, (y_ref, g_ref) in enumerate(((y0_ref, g0_ref), (y1_ref, g1_ref), (y2_ref, g2_ref), (y3_ref, g3_ref))):
        term = jax.nn.sigmoid(g_ref[...]) * _dot(y_ref[...], w_ref[b])
        acc = term if acc is None else acc + term
    o_ref[...] = acc.astype(BF16)


def _merge(ys, proj, wb, tm):
    n = ys[0].shape[0]
    tn = DENSE_TN
    gate0 = COL_GATE // tn
    per = D_MODEL // tn

    def gspec(b):
        return pl.BlockSpec((tm, tn), lambda i, j: (i, gate0 + b * per + j))

    return pl.pallas_call(
        _merge_kernel,
        grid=(n // tm, per),
        in_specs=[pl.BlockSpec((tm, BRANCH_W), lambda i, j: (i, 0))] * N_BRANCH
                 + [gspec(b) for b in range(N_BRANCH)]
                 + [pl.BlockSpec((N_BRANCH, BRANCH_W, tn), lambda i, j: (0, 0, j))],
        out_specs=pl.BlockSpec((tm, tn), lambda i, j: (i, j)),
        out_shape=jax.ShapeDtypeStruct((n, D_MODEL), BF16),
        compiler_params=_cparams("parallel", "parallel"),
        name="merge",
    )(*ys, proj, proj, proj, proj, wb)


def _rope_tables(n_tok):
    grid_rows = n_tok // GRID_W
    rows, cols = jnp.meshgrid(jnp.arange(grid_rows, dtype=F32), jnp.arange(GRID_W, dtype=F32), indexing='ij')
    quarter = ROPE_DIM // 4
    inv = ROPE_BASE ** (-jnp.arange(quarter, dtype=F32) / quarter)
    ang_r = rows.reshape(-1, 1) * inv
    ang_c = cols.reshape(-1, 1) * inv
    cos = jnp.concatenate([jnp.cos(ang_r)] * 2 + [jnp.cos(ang_c)] * 2, axis=-1)
    sin = jnp.concatenate([-jnp.sin(ang_r), jnp.sin(ang_r), -jnp.sin(ang_c), jnp.sin(ang_c)], axis=-1)
    return jnp.tile(cos, (1, LANES // ROPE_DIM)), jnp.tile(sin, (1, LANES // ROPE_DIM))


def _permute_w_in(w):
    sizes = (ML_W, ML_W, ML_W, ML_W, 4 * ML_HEADS, MLA_HEADS * MLA_QK, MLA_KV_RANK + MLA_ROPE, S5_W,
             DF_W, DF_W, DF_W, N_BRANCH * D_MODEL)
    splits = tuple(int(s) for s in np.cumsum(sizes)[:-1])
    (ml_q, ml_k, ml_v, ml_o, ml_if, mla_q, mla_kva, s5_u, df_q, df_k, df_v, gate) = jnp.split(w, splits, axis=1)
    mq = mla_q.reshape(D_MODEL, MLA_HEADS, MLA_QK)
    qn = mq[:, :, :MLA_NOPE].reshape(D_MODEL, MLA_HEADS * MLA_NOPE)
    qr = mq[:, :, MLA_NOPE:].reshape(D_MODEL, MLA_HEADS * MLA_ROPE)
    pad = lambda a, width: jnp.pad(a, ((0, 0), (0, width - a.shape[1])))
    cols = [ml_q, ml_k, ml_v, ml_o, s5_u, df_q, df_k, df_v, gate, qn, qr,
            mla_kva[:, :MLA_KV_RANK], pad(mla_kva[:, MLA_KV_RANK:], LANES), pad(ml_if, LANES)]
    out = jnp.concatenate(cols, axis=1)
    return pad(out, PROJ_COLS).astype(BF16)


def _seq_mixers(proj, bsz, seq, lw, lam_init, state, cache, rope):
    n = bsz * seq
    tm = min(TM_SEQ, seq)
    ml_c0, ml_n0, ml_m0, s5_h0r, s5_h0i = state
    r8 = 2 * ML_HEADS

    m0 = jnp.broadcast_to(ml_m0.reshape(bsz, r8, 1), (bsz, r8, LANES))
    hf, hb, c_new, n_new, m_new = _mlstm(proj, bsz, seq, lw['ml_bias'],
                                         ml_c0.reshape(bsz, r8, ML_DH, ML_DH), ml_n0.reshape(bsz, r8, ML_DH), m0)
    y_ml = _ml_post(hf, hb, proj, lw['ml_norm'], tm)

    cos, sin = rope if rope is not None else (None, None)
    k_new, v_new, q_mla, ckv = _mla_prep(proj, proj, COL_MLA_CKV // MLA_KV_RANK, COL_MLA_KR // LANES, n,
                                         lw['mla_kv_norm'], lw['mla_w_kvb'], tm, q_src=proj, cos=cos, sin=sin)
    k_c = v_c = None
    if cache is not None:
        ckv_c, krope_c, dk_c, dv_c = cache
        past = ckv_c.shape[1]
        kr_pad = jnp.pad(krope_c.reshape(bsz * past, MLA_ROPE), ((0, 0), (0, LANES - MLA_ROPE)))
        k_c, v_c = _mla_prep(ckv_c.reshape(bsz * past, MLA_KV_RANK), kr_pad, 0, 0, bsz * past,
                             lw['mla_kv_norm'], lw['mla_w_kvb'], min(tm, bsz * past), norm=False)
    y_mla = _mla_attn(q_mla, k_new, v_new, bsz, seq, k_c, v_c)

    proj3 = proj.reshape(bsz, seq, PROJ_COLS)
    ng = bsz // S5_SEQS

    def pack_state(hs):
        return hs.reshape(ng, S5_SEQS, 2, S5_STATE).transpose(0, 2, 1, 3).reshape(ng, 2 * S5_SEQS, S5_STATE)

    def unpack_state(hs):
        return hs.reshape(ng, 2, S5_SEQS, S5_STATE).transpose(0, 2, 1, 3).reshape(bsz, 2, S5_GROUPS, S5_P)

    yf, yb, hr_new, hi_new = _s5(proj3, bsz, seq, lw['s5_a_re8'], lw['s5_a_im8'],
                                 pack_state(s5_h0r), pack_state(s5_h0i),
                                 lw['s5_wb_re'], lw['s5_wb_im'], lw['s5_wc_re'], lw['s5_wc_im'])
    y_s5 = _s5_post(yf.reshape(n, S5_W), yb.reshape(n, S5_W), proj, lw['s5_d'], lw['s5_w_glu'], tm)

    if cos is not None:
        dk, kcol = _rope_cast(proj, COL_DF_K // DF_W, n, tm, cos, sin), 0
    else:
        dk, kcol = proj, COL_DF_K // DF_DV
    dk_cache = dv_cache = None
    if cache is not None:
        dk_cache = dk_c.reshape(bsz * past, DF_W)
        dv_cache = dv_c.reshape(bsz * past, DF_W)
    y_df = _diff_attn(proj, dk, kcol, lw['df_lambda'], lw['df_norm'], lam_init, bsz, seq,
                      dk_cache, dv_cache, cos, sin)

    cols = lambda c0, w: lax.slice(proj, (0, c0), (n, c0 + w))
    new_ctx = (ckv.reshape(bsz, seq, MLA_KV_RANK),
               cols(COL_MLA_KR, MLA_ROPE).reshape(bsz, seq, MLA_ROPE),
               cols(COL_DF_K, DF_W).reshape(bsz, seq, DF_HEADS, 2 * DF_DQK),
               cols(COL_DF_V, DF_W).reshape(bsz, seq, DF_HEADS, DF_DV),
               c_new.reshape(bsz, 2, ML_HEADS, ML_DH, ML_DH), n_new.reshape(bsz, 2, ML_HEADS, ML_DH),
               m_new[:, :, 0].reshape(bsz, 2, ML_HEADS), unpack_state(hr_new), unpack_state(hi_new))
    return (y_ml, y_mla, y_s5, y_df), new_ctx


def kernel(x_prompt, x_sample, cache_mla_ckv, cache_mla_krope, cache_diff_k, cache_diff_v,
           state_mlstm_c, state_mlstm_n, state_mlstm_m, state_s5_re, state_s5_im, c,
           c_ctx, w_ada, b_ada, norm_mix, norm_ffn, w_in, ml_if_bias, ml_norm, mla_kv_norm,
           mla_w_kvb, s5_a_re, s5_a_im, s5_log_dt, s5_b_re, s5_b_im, s5_c_re, s5_c_im, s5_d,
           s5_w_glu, df_lambda, df_norm, w_branch, w_o, w_ffn_in, w_ffn_out, final_norm):
    bp, sp, _ = x_prompt.shape
    bs, ss, _ = x_sample.shape
    n_p, n_s = bp * sp, bs * ss
    tm = TM_DENSE

    cond = jnp.concatenate([c, c_ctx[None, :], jnp.zeros((SUBLANES - 1 - bs, D_MODEL), F32)], axis=0)
    mods = _ada(cond, w_ada, b_ada).reshape(DEPTH, SUBLANES, 6, 1, D_MODEL)
    rope = _rope_tables(ss)
    latent_row = lambda i: (i * tm) // ss
    prompt_row = lambda i: bs

    xs = x_sample.reshape(n_s, D_MODEL)
    xp = x_prompt.reshape(n_p, D_MODEL)
    zeros_state = (jnp.zeros((bp, 2, ML_HEADS, ML_DH, ML_DH), F32), jnp.zeros((bp, 2, ML_HEADS, ML_DH), F32),
                   jnp.zeros((bp, 2, ML_HEADS), F32), jnp.zeros((bp, 2, S5_GROUPS, S5_P), F32),
                   jnp.zeros((bp, 2, S5_GROUPS, S5_P), F32))
    ctx_out = []
    for l in range(DEPTH):
        lam_init = 0.8 - 0.6 * math.exp(-0.3 * l)
        ab_re, ab_im, bb_re, bb_im = _s5_discretise(s5_a_re[l], s5_a_im[l], s5_log_dt[l], s5_b_re[l], s5_b_im[l])
        rep = lambda a: jnp.repeat(a.reshape(2, S5_STATE), S5_SEQS, axis=0)
        lw = {
            'ml_bias': jnp.pad(ml_if_bias[l].reshape(1, 4 * ML_HEADS), ((0, 0), (0, LANES - 4 * ML_HEADS))),
            'ml_norm': ml_norm[l].reshape(1, ML_W),
            'mla_kv_norm': mla_kv_norm[l].reshape(1, MLA_KV_RANK),
            'mla_w_kvb': mla_w_kvb[l].astype(BF16),
            's5_a_re8': rep(ab_re), 's5_a_im8': rep(ab_im),
            's5_wb_re': _block_diag(bb_re.transpose(0, 1, 3, 2)).astype(BF16),
            's5_wb_im': _block_diag(bb_im.transpose(0, 1, 3, 2)).astype(BF16),
            's5_wc_re': _block_diag(s5_c_re[l].transpose(0, 1, 3, 2)).astype(BF16),
            's5_wc_im': _block_diag(s5_c_im[l].transpose(0, 1, 3, 2)).astype(BF16),
            's5_d': s5_d[l].reshape(1, S5_W),
            's5_w_glu': s5_w_glu[l].astype(BF16),
            'df_lambda': df_lambda[l],
            'df_norm': df_norm[l].reshape(1, DF_DV),
        }
        mod = mods[l]
        w_in_l = _permute_w_in(w_in[l])
        w_branch_l, w_o_l = w_branch[l].astype(BF16), w_o[l].astype(BF16)
        w_ffn_in_l, w_ffn_out_l = w_ffn_in[l].astype(BF16), w_ffn_out[l].astype(BF16)
        g_mix, g_ffn = norm_mix[l].reshape(1, D_MODEL), norm_ffn[l].reshape(1, D_MODEL)

        def layer(x, row_of_tile, bsz, seq, state, cache, rope_tabs):
            proj = _proj_in(x, g_mix, mod, row_of_tile, w_in_l, tm)
            ys, new_ctx = _seq_mixers(proj, bsz, seq, lw, lam_init, state, cache, rope_tabs)
            merged = _merge(list(ys), proj, w_branch_l, tm)
            x = _matmul_resid(merged, w_o_l, x, mod, 2, row_of_tile, tm)
            act = _ffn_in(x, g_ffn, mod, row_of_tile, w_ffn_in_l, tm)
            return _matmul_resid(act, w_ffn_out_l, x, mod, 5, row_of_tile, tm), new_ctx

        xp, new_ctx = layer(xp, prompt_row, bp, sp, zeros_state, None, None)
        ctx_out.append(new_ctx)
        state = (state_mlstm_c[:, l], state_mlstm_n[:, l], state_mlstm_m[:, l], state_s5_re[:, l], state_s5_im[:, l])
        cache = (cache_mla_ckv[:, l], cache_mla_krope[:, l], cache_diff_k[:, l], cache_diff_v[:, l])
        xs, _ = layer(xs, latent_row, bs, ss, state, cache, rope)

    g_fin = final_norm.reshape(1, D_MODEL)
    y_prompt = _final_norm(xp, g_fin, tm).reshape(bp, sp, D_MODEL)
    y_sample = _final_norm(xs, g_fin, tm).reshape(bs, ss, D_MODEL)
    stacked = tuple(jnp.stack([ctx[k] for ctx in ctx_out], axis=1) for k in range(9))
    return (y_prompt, y_sample) + stacked
```
